```python
import math
import jax
import jax.numpy as jnp
from jax import lax
import numpy as np

D_MODEL = 1024
BATCH = 8
SEQ = 2048
DEPTH = 4

CTX_LEN = 256
GRID_W = 64
HEAD_DIM = 64
Q_BLOCK = 128
ROPE_BASE = 10000.0
EPS = 1e-6
NEG_INF = -1e30

WIN = 128
A_HEADS = 8
A_KV_HEADS = 2
A_GROUP = A_HEADS // A_KV_HEADS
B_HEADS = 4
B_NOPE = 64
B_ROPE = 32
B_V = 64
B_Q_RANK = 256
B_KV_RANK = 128
C_HEADS = 4
C_DH = 32
C_V = 2 * C_DH

A_WIDTH = A_HEADS * HEAD_DIM
B_WIDTH = B_HEADS * B_V
C_WIDTH = C_HEADS * C_V
MIX_WIDTH = A_WIDTH + B_WIDTH + C_WIDTH

IN_SPLITS = (A_HEADS * HEAD_DIM, A_KV_HEADS * HEAD_DIM, A_KV_HEADS * HEAD_DIM,
             B_Q_RANK, B_KV_RANK, B_ROPE,
             C_HEADS * 2 * C_DH, C_HEADS * 2 * C_DH, C_HEADS * C_V)
IN_WIDTH = sum(IN_SPLITS)

N_EXPERTS = 16
EC_FACTOR = 2
D_EXPERT = 512

kernel_name = "hybrid_parallel_heads_diffusion_block"


def rmsnorm(x, g):
    xf = x.astype(jnp.float32)
    xf = xf * lax.rsqrt(jnp.mean(xf * xf, axis=-1, keepdims=True) + EPS)
    return xf.astype(x.dtype) * g


def modulate(h, shift, scale):
    return h * (1.0 + scale) + shift


def split_cols(p):
    offs = np.cumsum(IN_SPLITS)[:-1].tolist()
    return jnp.split(p, offs, axis=-1)


def axial_rope(T, rot_dim):
    rows = T // GRID_W
    row = jnp.repeat(jnp.arange(rows), GRID_W)
    col = jnp.tile(jnp.arange(GRID_W), rows)
    n_f = rot_dim // 4
    inv = ROPE_BASE ** (-jnp.arange(n_f, dtype=jnp.float32) / n_f)
    ang = jnp.stack([row[:, None] * inv, col[:, None] * inv], axis=1)
    return jnp.cos(ang), jnp.sin(ang)


def apply_rope(x, cos, sin):
    shape = x.shape
    n_f = shape[-1] // 4
    xr = x.reshape(*shape[:-1], 2, 2, n_f)
    x0, x1 = xr[..., 0, :], xr[..., 1, :]
    bshape = (shape[1],) + (1,) * (x.ndim - 3) + (2, n_f)
    cos = cos.reshape(bshape).astype(x.dtype)
    sin = sin.reshape(bshape).astype(x.dtype)
    return jnp.stack([x0 * cos - x1 * sin, x1 * cos + x0 * sin], axis=-2).reshape(shape)


def map_query_blocks(fn, qs):
    B, T = qs[0].shape[:2]
    nb = T // Q_BLOCK
    blocks = tuple(jnp.moveaxis(a.reshape(B, nb, Q_BLOCK, *a.shape[2:]), 1, 0) for a in qs)
    out = lax.map(fn, blocks)
    return jnp.moveaxis(out, 0, 1).reshape(B, T, *out.shape[3:])


def window_gqa_mixer(qa, ka, va, qa_c, ka_c, va_c, sink, rope, need_ctx):
    B, T = qa.shape[:2]
    L = ka_c.shape[1]
    nb = T // WIN
    scale = HEAD_DIM ** -0.5
    q = apply_rope(qa.reshape(B, T, A_HEADS, HEAD_DIM), *rope)
    k = apply_rope(ka.reshape(B, T, A_KV_HEADS, HEAD_DIM), *rope)
    v = va.reshape(B, T, A_KV_HEADS, HEAD_DIM)
    kc = ka_c.reshape(B, L, A_KV_HEADS, HEAD_DIM)
    vc = va_c.reshape(B, L, A_KV_HEADS, HEAD_DIM)
    sink_f = sink.astype(jnp.float32)

    qb = q.reshape(B, nb, WIN, A_KV_HEADS, A_GROUP, HEAD_DIM)

    def band(a):
        ap = jnp.pad(a, ((0, 0), (WIN, WIN), (0, 0), (0, 0))).reshape(B, nb + 2, WIN, A_KV_HEADS, HEAD_DIM)
        return jnp.concatenate([ap[:, :-2], ap[:, 1:-1], ap[:, 2:]], axis=2)

    kb, vb = band(k), band(v)
    qi = jnp.arange(WIN)[:, None]
    kj = jnp.arange(3 * WIN)[None, :]
    kpos = jnp.arange(nb)[:, None, None] * WIN - WIN + kj
    valid = (kj >= qi) & (kj <= qi + 2 * WIN) & (kpos >= 0) & (kpos < T)
    s_loc = jnp.einsum('bnqhgd,bnkhd->bhgnqk', qb, kb).astype(jnp.float32) * scale
    s_loc = jnp.where(valid, s_loc, NEG_INF)
    s_ctx = jnp.einsum('bnqhgd,bkhd->bhgnqk', qb, kc).astype(jnp.float32) * scale
    s_sink = jnp.broadcast_to(sink_f.reshape(1, A_KV_HEADS, A_GROUP, 1, 1, 1), s_ctx.shape[:-1] + (1,))
    p = jax.nn.softmax(jnp.concatenate([s_loc, s_ctx, s_sink], axis=-1), axis=-1).astype(v.dtype)
    o = (jnp.einsum('bhgnqk,bnkhd->bnqhgd', p[..., :3 * WIN], vb)
         + jnp.einsum('bhgnqk,bkhd->bnqhgd', p[..., 3 * WIN:3 * WIN + L], vc))
    o = o.reshape(B, T, A_WIDTH)

    o_c = None
    if need_ctx:
        qcg = qa_c.reshape(B, L, A_KV_HEADS, A_GROUP, HEAD_DIM)
        s = jnp.einsum('bqhgd,bkhd->bhgqk', qcg, kc).astype(jnp.float32) * scale
        s_sink_c = jnp.broadcast_to(sink_f.reshape(1, A_KV_HEADS, A_GROUP, 1, 1), s.shape[:-1] + (1,))
        pc = jax.nn.softmax(jnp.concatenate([s, s_sink_c], axis=-1), axis=-1)[..., :L].astype(vc.dtype)
        o_c = jnp.einsum('bhgqk,bkhd->bqhgd', pc, vc).reshape(B, L, A_WIDTH)
    return o, o_c


def mla_attend(qn, qr, kn, kr, v):
    s = (jnp.einsum('bqhd,bkhd->bhqk', qn, kn) + jnp.einsum('bqhr,bkr->bhqk', qr, kr)).astype(jnp.float32)
    p = jax.nn.softmax(s * (B_NOPE + B_ROPE) ** -0.5, axis=-1).astype(v.dtype)
    return jnp.einsum('bhqk,bkhd->bqhd', p, v)


def mla_mixer(cq, ckv, kr, cq_c, ckv_c, kr_c, q_norm, w_uq, kv_norm, w_ukv, rope, need_ctx):
    def project(cq_, ckv_, kr_, rope_):
        B_, N = cq_.shape[:2]
        q = (rmsnorm(cq_, q_norm) @ w_uq).reshape(B_, N, B_HEADS, B_NOPE + B_ROPE)
        kv = (rmsnorm(ckv_, kv_norm) @ w_ukv).reshape(B_, N, B_HEADS, B_NOPE + B_V)
        qn, qr = q[..., :B_NOPE], q[..., B_NOPE:]
        kn, v = kv[..., :B_NOPE], kv[..., B_NOPE:]
        if rope_ is not None:
            qr = apply_rope(qr, *rope_)
            kr_ = apply_rope(kr_, *rope_)
        return qn, qr, kn, kr_, v

    B, T = cq.shape[:2]
    qn, qr, kn, krl, v = project(cq, ckv, kr, rope)
    qn_c, qr_c, kn_c, kr_cc, v_c = project(cq_c, ckv_c, kr_c, None)
    kn_all = jnp.concatenate([kn, kn_c], axis=1)
    kr_all = jnp.concatenate([krl, kr_cc], axis=1)
    v_all = jnp.concatenate([v, v_c], axis=1)
    o = map_query_blocks(lambda a: mla_attend(a[0], a[1], kn_all, kr_all, v_all), (qn, qr))
    o = o.reshape(B, T, B_WIDTH)
    o_c = None
    if need_ctx:
        o_c = mla_attend(qn_c, qr_c, kn_c, kr_cc, v_c).reshape(B, -1, B_WIDTH)
    return o, o_c


def diff_attend(q, k, v, lam):
    s = jnp.einsum('bqhmd,bkhmd->bhmqk', q, k).astype(jnp.float32) * C_DH ** -0.5
    p = jax.nn.softmax(s, axis=-1)
    a = p[:, :, 0] - lam * p[:, :, 1]
    return jnp.einsum('bhqk,bkhd->bqhd', a.astype(v.dtype), v)


def diff_mixer(qd, kd, vd, qd_c, kd_c, vd_c, lam_q1, lam_k1, lam_q2, lam_k2, sub_norm, lam_init, rope, need_ctx):
    B, T = qd.shape[:2]
    L = qd_c.shape[1]
    lam = (jnp.exp(jnp.sum((lam_q1 * lam_k1).astype(jnp.float32)))
           - jnp.exp(jnp.sum((lam_q2 * lam_k2).astype(jnp.float32))) + lam_init)
    q = apply_rope(qd.reshape(B, T, C_HEADS, 2, C_DH), *rope)
    k = apply_rope(kd.reshape(B, T, C_HEADS, 2, C_DH), *rope)
    v = vd.reshape(B, T, C_HEADS, C_V)
    qc = qd_c.reshape(B, L, C_HEADS, 2, C_DH)
    kc = kd_c.reshape(B, L, C_HEADS, 2, C_DH)
    vc = vd_c.reshape(B, L, C_HEADS, C_V)
    k_all = jnp.concatenate([k, kc], axis=1)
    v_all = jnp.concatenate([v, vc], axis=1)
    o = map_query_blocks(lambda a: diff_attend(a[0], k_all, v_all, lam), (q,))
    o = (rmsnorm(o, sub_norm) * (1.0 - lam_init)).reshape(B, T, C_WIDTH)
    o_c = None
    if need_ctx:
        o_c = (rmsnorm(diff_attend(qc, kc, vc, lam), sub_norm) * (1.0 - lam_init)).reshape(B, L, C_WIDTH)
    return o, o_c


def expert_choice_ffn(h, w_router, w_gate, w_up, w_down):
    B, N, D = h.shape
    cap = EC_FACTOR * N // N_EXPERTS
    aff = jax.nn.softmax((h @ w_router).astype(jnp.float32), axis=-1)
    gate_val, tok = lax.top_k(jnp.swapaxes(aff, 1, 2), cap)
    xg = jax.vmap(lambda hb, ib: hb[ib])(h, tok)
    hid = jax.nn.silu(jnp.einsum('becd,edf->becf', xg, w_gate)) * jnp.einsum('becd,edf->becf', xg, w_up)
    y = jnp.einsum('becf,efd->becd', hid, w_down) * gate_val[..., None].astype(h.dtype)
    return jax.vmap(lambda ib, yb: jnp.zeros((N, D), yb.dtype).at[ib.reshape(-1)].add(yb.reshape(-1, D)))(tok, y)


def setup_inputs(seed: int = 0) -> dict:
    key = jax.random.key(seed)
    ks = jax.random.split(key, 25)

    def nrm(k, shape, s):
        return jax.random.normal(k, shape, jnp.float32) * s

    def gain(k, shape):
        return 1.0 + nrm(k, shape, 0.02)

    return {
        "x": nrm(ks[0], (BATCH, SEQ, D_MODEL), 1.0),
        "c": nrm(ks[1], (BATCH, D_MODEL), 1.0),
        "ctx": nrm(ks[2], (BATCH, CTX_LEN, D_MODEL), 1.0),
        "c_ctx": nrm(ks[3], (D_MODEL,), 1.0),
        "norm1": gain(ks[4], (DEPTH, D_MODEL)),
        "norm2": gain(ks[5], (DEPTH, D_MODEL)),
        "w_ada": nrm(ks[6], (DEPTH, D_MODEL, 6 * D_MODEL), 0.5 * D_MODEL ** -0.5),
        "b_ada": nrm(ks[7], (DEPTH, 6 * D_MODEL), 0.02),
        "w_in": nrm(ks[8], (DEPTH, D_MODEL, IN_WIDTH), D_MODEL ** -0.5),
        "sink": nrm(ks[9], (DEPTH, A_HEADS), 0.5),
        "mla_q_norm": gain(ks[10], (DEPTH, B_Q_RANK)),
        "w_uq": nrm(ks[11], (DEPTH, B_Q_RANK, B_HEADS * (B_NOPE + B_ROPE)), B_Q_RANK ** -0.5),
        "mla_kv_norm": gain(ks[12], (DEPTH, B_KV_RANK)),
        "w_ukv": nrm(ks[13], (DEPTH, B_KV_RANK, B_HEADS * (B_NOPE + B_V)), B_KV_RANK ** -0.5),
        "lam_q1": nrm(ks[14], (DEPTH, C_DH), 0.1),
        "lam_k1": nrm(ks[15], (DEPTH, C_DH), 0.1),
        "lam_q2": nrm(ks[16], (DEPTH, C_DH), 0.1),
        "lam_k2": nrm(ks[17], (DEPTH, C_DH), 0.1),
        "diff_norm": gain(ks[18], (DEPTH, C_V)),
        "w_out": nrm(ks[19], (DEPTH, MIX_WIDTH, D_MODEL), MIX_WIDTH ** -0.5),
        "w_router": nrm(ks[20], (DEPTH, D_MODEL, N_EXPERTS), D_MODEL ** -0.5),
        "w_gate": nrm(ks[21], (DEPTH, N_EXPERTS, D_MODEL, D_EXPERT), D_MODEL ** -0.5),
        "w_up": nrm(ks[22], (DEPTH, N_EXPERTS, D_MODEL, D_EXPERT), D_MODEL ** -0.5),
        "w_down": nrm(ks[23], (DEPTH, N_EXPERTS, D_EXPERT, D_MODEL), D_EXPERT ** -0.5),
        "final_norm": gain(ks[24], (D_MODEL,)),
    }


def reference(x, c, ctx, c_ctx, norm1, norm2, w_ada, b_ada, w_in, sink, mla_q_norm, w_uq, mla_kv_norm, w_ukv,
              lam_q1, lam_k1, lam_q2, lam_k2, diff_norm, w_out, w_router, w_gate, w_up, w_down, final_norm):
    T = x.shape[1]
    rope_a = axial_rope(T, HEAD_DIM)
    rope_bc = axial_rope(T, B_ROPE)
    xc = ctx
    for l in range(DEPTH):
        need_ctx = l < DEPTH - 1
        mod = jax.nn.silu(c) @ w_ada[l] + b_ada[l]
        mod_c = jax.nn.silu(c_ctx) @ w_ada[l] + b_ada[l]
        sh1, sc1, g1, sh2, sc2, g2 = jnp.split(mod[:, None, :], 6, axis=-1)
        csh1, csc1, cg1, csh2, csc2, cg2 = jnp.split(mod_c, 6, axis=-1)

        p = modulate(rmsnorm(x, norm1[l]), sh1, sc1) @ w_in[l]
        pc = modulate(rmsnorm(xc, norm1[l]), csh1, csc1) @ w_in[l]
        qa, ka, va, cq, ckv, kr, qd, kd, vd = split_cols(p)
        qa_c, ka_c, va_c, cq_c, ckv_c, kr_c, qd_c, kd_c, vd_c = split_cols(pc)
        oa, oa_c = window_gqa_mixer(qa, ka, va, qa_c, ka_c, va_c, sink[l], rope_a, need_ctx)
        ob, ob_c = mla_mixer(cq, ckv, kr, cq_c, ckv_c, kr_c, mla_q_norm[l], w_uq[l], mla_kv_norm[l], w_ukv[l],
                             rope_bc, need_ctx)
        lam_init = 0.8 - 0.6 * math.exp(-0.3 * l)
        od, od_c = diff_mixer(qd, kd, vd, qd_c, kd_c, vd_c, lam_q1[l], lam_k1[l], lam_q2[l], lam_k2[l],
                              diff_norm[l], lam_init, rope_bc, need_ctx)
        x = x + g1 * (jnp.concatenate([oa, ob, od], axis=-1) @ w_out[l])

        x = x + g2 * expert_choice_ffn(modulate(rmsnorm(x, norm2[l]), sh2, sc2),
                                       w_router[l], w_gate[l], w_up[l], w_down[l])
        if need_ctx:
            xc = xc + cg1 * (jnp.concatenate([oa_c, ob_c, od_c], axis=-1) @ w_out[l])
            xc = xc + cg2 * expert_choice_ffn(modulate(rmsnorm(xc, norm2[l]), csh2, csc2),
                                              w_router[l], w_gate[l], w_up[l], w_down[l])
    return rmsnorm(x, final_norm)
```

```python
import functools
import math

import numpy as np
import jax
import jax.numpy as jnp
from jax import lax
from jax.experimental import pallas as pl
from jax.experimental.pallas import tpu as pltpu

D_MODEL = 1024
BATCH = 8
SEQ = 2048
DEPTH = 4
CTX_LEN = 256
NT = SEQ + CTX_LEN
GRID_W = 64
HEAD_DIM = 64
ROPE_BASE = 10000.0
EPS = 1e-6
NEG_INF = -1e30

WIN = 128
A_HEADS = 8
A_KV_HEADS = 2
A_GROUP = A_HEADS // A_KV_HEADS
B_HEADS = 4
B_NOPE = 64
B_ROPE = 32
B_V = 64
B_Q_RANK = 256
B_KV_RANK = 128
C_HEADS = 4
C_DH = 32
C_V = 2 * C_DH
N_EXPERTS = 16
EC_FACTOR = 2
D_EXPERT = 512
CAP_LAT = EC_FACTOR * SEQ // N_EXPERTS
CAP_CTX = EC_FACTOR * CTX_LEN // N_EXPERTS

LANES = 128
TOK_TILE = 256
N_TILES = NT // TOK_TILE
N_LAT_TILES = SEQ // TOK_TILE
VMEM_LIMIT = 56 * 1024 * 1024

F32 = jnp.float32
BF16 = jnp.bfloat16

_NT_DIMS = (((1,), (1,)), ((), ()))


def _params(sem):
    return pltpu.CompilerParams(dimension_semantics=sem, vmem_limit_bytes=VMEM_LIMIT)


def _dot(a, b):
    return jnp.dot(a, b, preferred_element_type=F32)


def _dot_nt(a, b):
    return lax.dot_general(a, b, _NT_DIMS, preferred_element_type=F32)


def _split_bf16(v):
    hi = v.astype(BF16)
    lo = (v - hi.astype(F32)).astype(BF16)
    return hi, lo


def _dot3(a, b):
    a_hi, a_lo = _split_bf16(a)
    b_hi, b_lo = _split_bf16(b)
    return _dot(a_hi, b_hi) + _dot(a_lo, b_hi) + _dot(a_hi, b_lo)


def _indicator(cond):
    return jnp.where(cond, 1.0, 0.0).astype(BF16)


def _rms(v):
    return v * lax.rsqrt(jnp.mean(v * v, axis=-1, keepdims=True) + EPS)


def _softmax_rows(s):
    m = jnp.max(s, axis=-1, keepdims=True)
    e = jnp.exp(s - m)
    return e / jnp.sum(e, axis=-1, keepdims=True)


def _mod_kernel(c_ref, w_ref, b_ref, o_ref):
    c = c_ref[...]
    s = c / (1.0 + jnp.exp(-c))
    o_ref[0] = _dot3(s, w_ref[0]) + b_ref[0]


def _modulation(cc, w_ada, b_ada):
    tn = 1536
    return pl.pallas_call(
        _mod_kernel,
        out_shape=jax.ShapeDtypeStruct((DEPTH, 16, 6 * D_MODEL), F32),
        grid=(DEPTH, 6 * D_MODEL // tn),
        in_specs=[
            pl.BlockSpec((16, D_MODEL), lambda l, j: (0, 0)),
            pl.BlockSpec((1, D_MODEL, tn), lambda l, j: (l, 0, j)),
            pl.BlockSpec((1, 1, tn), lambda l, j: (l, 0, j)),
        ],
        out_specs=pl.BlockSpec((1, 16, tn), lambda l, j: (l, 0, j)),
        compiler_params=_params(("arbitrary", "arbitrary")),
        name="adaln_mod",
    )(cc, w_ada, b_ada.reshape(DEPTH, 1, 6 * D_MODEL))


def _rope(p, cos, sin, half):
    lane = lax.broadcasted_iota(jnp.int32, (1, LANES), 1)
    first = (lane // half) % 2 == 0
    outs = []
    for j in range(p.shape[1] // LANES):
        blk = p[:, j * LANES:(j + 1) * LANES]
        partner = jnp.where(first, pltpu.roll(blk, LANES - half, 1), pltpu.roll(blk, half, 1))
        outs.append(blk * cos + partner * sin)
    return outs[0] if len(outs) == 1 else jnp.concatenate(outs, axis=-1)


def _inproj_kernel(x_ref, mod_ref, g1_ref, w_ref, qn_ref, wuq_ref, kvn_ref, wukv_ref,
                   ca_ref, sa_ref, caq_ref, saq_ref, c8_ref, s8_ref, c8q_ref, s8q_ref,
                   cb_ref, sb_ref, cbq_ref, sbq_ref,
                   qa_ref, ka_ref, va_ref, qb_ref, kb_ref, vb_ref, qd_ref, kd_ref, vd_ref):
    x = x_ref[0]
    mod = mod_ref[0]
    sh1 = mod[:, 0:D_MODEL]
    sc1 = mod[:, D_MODEL:2 * D_MODEL]
    h = (_rms(x) * g1_ref[...]) * (1.0 + sc1) + sh1
    p = _dot(h.astype(BF16), w_ref[...])

    qa_ref[0] = _rope(p[:, 0:512], caq_ref[...], saq_ref[...], 16).astype(BF16)
    ka_ref[0] = _rope(p[:, 512:640], ca_ref[...], sa_ref[...], 16).astype(BF16)
    va_ref[0] = p[:, 640:768].astype(BF16)

    cq = (_rms(p[:, 768:1024]) * qn_ref[...]).astype(BF16)
    qb = _dot(cq, wuq_ref[...])
    qb_ref[0] = _rope(qb, cbq_ref[...], sbq_ref[...], 8).astype(BF16)
    ckv = (_rms(p[:, 1024:1152]) * kvn_ref[...]).astype(BF16)
    kv = _dot(ckv, wukv_ref[...])
    kr = _rope(p[:, 1152:1280], cb_ref[...], sb_ref[...], 8)
    kb_ref[0] = (kv[:, 0:512] + jnp.concatenate([kr] * B_HEADS, axis=-1)).astype(BF16)
    vb_ref[0] = kv[:, 512:768].astype(BF16)

    qd_ref[0] = _rope(p[:, 1280:1536], c8q_ref[...], s8q_ref[...], 8).astype(BF16)
    kd_ref[0] = _rope(p[:, 1536:1792], c8_ref[...], s8_ref[...], 8).astype(BF16)
    vd_ref[0] = p[:, 1792:2048].astype(BF16)


def _mod_row(b, t):
    return jnp.where(t == N_LAT_TILES, BATCH, b)


def _inproj(xall, mod_l, g1, w, qn, wuq, kvn, wukv, tabs):
    full = lambda shape: pl.BlockSpec(shape, lambda b, t: (0,) * len(shape))
    tab_spec = pl.BlockSpec((TOK_TILE, LANES), lambda b, t: (t, 0))
    widths = (512, 128, 128, 512, 512, 256, 256, 256, 256)
    return pl.pallas_call(
        _inproj_kernel,
        out_shape=[jax.ShapeDtypeStruct((BATCH, NT, n), BF16) for n in widths],
        grid=(BATCH, N_TILES),
        in_specs=[
            pl.BlockSpec((1, TOK_TILE, D_MODEL), lambda b, t: (b, t, 0)),
            pl.BlockSpec((1, 1, 6 * D_MODEL), lambda b, t: (_mod_row(b, t), 0, 0)),
            full((1, D_MODEL)),
            full((D_MODEL, 2048)),
            full((1, B_Q_RANK)),
            full((B_Q_RANK, 512)),
            full((1, B_KV_RANK)),
            full((B_KV_RANK, 768)),
        ] + [tab_spec] * 12,
        out_specs=[pl.BlockSpec((1, TOK_TILE, n), lambda b, t: (b, t, 0)) for n in widths],
        compiler_params=_params(("arbitrary", "arbitrary")),
        name="in_projection",
    )(xall, mod_l, g1, w, qn, wuq, kvn, wukv, *tabs)


def _attn_a_kernel(*refs, n_kv_blocks, has_bias, tq):
    sink_ref, q_ref = refs[0], refs[1]
    k_refs = refs[2:2 + n_kv_blocks]
    v_refs = refs[2 + n_kv_blocks:2 + 2 * n_kv_blocks]
    pos = 2 + 2 * n_kv_blocks
    bias_ref = refs[pos] if has_bias else None
    o_ref = refs[-1]

    q = q_ref[0]
    kcat = jnp.concatenate([r[0] for r in k_refs], axis=0) if n_kv_blocks > 1 else k_refs[0][0]
    vcat = jnp.concatenate([r[0] for r in v_refs], axis=0) if n_kv_blocks > 1 else v_refs[0][0]
    lane = lax.broadcasted_iota(jnp.int32, (1, LANES), 1)
    acc = [jnp.zeros((tq, LANES), F32) for _ in range(A_GROUP)]
    for g in range(A_KV_HEADS):
        in_half = (lane < HEAD_DIM) if g == 0 else (lane >= HEAD_DIM)
        keep = _indicator(in_half)
        qs = jnp.concatenate([q[:, j * LANES:(j + 1) * LANES] * keep for j in range(A_GROUP)], axis=0)
        s = _dot_nt(qs, kcat)
        if has_bias:
            s = s + jnp.concatenate([bias_ref[0]] * A_GROUP, axis=0)
        sink = jnp.concatenate(
            [jnp.full((tq, 1), sink_ref[g * A_GROUP + j], F32) for j in range(A_GROUP)], axis=0)
        m = jnp.maximum(jnp.max(s, axis=-1, keepdims=True), sink)
        e = jnp.exp(s - m)
        den = jnp.sum(e, axis=-1, keepdims=True) + jnp.exp(sink - m)
        o = _dot((e / den).astype(BF16), vcat)
        for j in range(A_GROUP):
            acc[j] = acc[j] + jnp.where(in_half, o[j * tq:(j + 1) * tq], 0.0)
    o_ref[0] = jnp.concatenate(acc, axis=-1).astype(BF16)


def _attn_a_latent(sink, qa, ka, va, bias):
    nb = SEQ // WIN
    last = nb - 1
    ctx_blk = SEQ // CTX_LEN
    kv_maps = [lambda b, n: (b, jnp.maximum(n - 1, 0), 0),
               lambda b, n: (b, n, 0),
               lambda b, n: (b, jnp.minimum(n + 1, last), 0)]
    kv_specs = [pl.BlockSpec((1, WIN, LANES), m) for m in kv_maps]
    kv_specs.append(pl.BlockSpec((1, CTX_LEN, LANES), lambda b, n: (b, ctx_blk, 0)))
    bias_spec = pl.BlockSpec(
        (1, WIN, 3 * WIN + CTX_LEN),
        lambda b, n: (jnp.where(n == 0, 0, jnp.where(n == last, 2, 1)), 0, 0))
    return pl.pallas_call(
        functools.partial(_attn_a_kernel, n_kv_blocks=4, has_bias=True, tq=WIN),
        out_shape=jax.ShapeDtypeStruct((BATCH, NT, 512), BF16),
        grid=(BATCH, nb),
        in_specs=[pl.BlockSpec(memory_space=pltpu.SMEM),
                  pl.BlockSpec((1, WIN, 512), lambda b, n: (b, n, 0))]
                 + kv_specs + kv_specs + [bias_spec],
        out_specs=pl.BlockSpec((1, WIN, 512), lambda b, n: (b, n, 0)),
        compiler_params=_params(("arbitrary", "arbitrary")),
        name="attn_window_latent",
    )(sink, qa, ka, ka, ka, ka, va, va, va, va, bias)


def _attn_a_ctx(sink, qa, ka, va, prev_out):
    ctx_blk = SEQ // CTX_LEN
    spec = lambda n: pl.BlockSpec((1, CTX_LEN, n), lambda b: (b, ctx_blk, 0))

    def body(sink_ref, q_ref, k_ref, v_ref, prev_ref, o_ref):
        del prev_ref
        _attn_a_kernel(sink_ref, q_ref, k_ref, v_ref, o_ref, n_kv_blocks=1, has_bias=False, tq=CTX_LEN)

    return pl.pallas_call(
        body,
        out_shape=jax.ShapeDtypeStruct((BATCH, NT, 512), BF16),
        grid=(BATCH,),
        in_specs=[pl.BlockSpec(memory_space=pltpu.SMEM), spec(512), spec(LANES), spec(LANES),
                  pl.BlockSpec(memory_space=pl.ANY)],
        out_specs=spec(512),
        input_output_aliases={4: 0},
        compiler_params=_params(("arbitrary",)),
        name="attn_window_ctx",
    )(sink, qa, ka, va, prev_out)


def _attn_b_kernel(q_ref, k_ref, v_ref, o_ref):
    q = q_ref[0]
    k = k_ref[0]
    v = v_ref[0]
    lane = lax.broadcasted_iota(jnp.int32, (1, B_HEADS * B_V), 1)
    acc = jnp.zeros((q.shape[0], B_HEADS * B_V), F32)
    for h in range(B_HEADS):
        s = _dot_nt(q[:, h * LANES:(h + 1) * LANES], k[:, h * LANES:(h + 1) * LANES])
        o = _dot(_softmax_rows(s).astype(BF16), v)
        acc = acc + jnp.where((lane >= h * B_V) & (lane < (h + 1) * B_V), o, 0.0)
    o_ref[0] = acc.astype(BF16)


def _full_attention(kernel, name, q, k, v, extra, extra_specs, out_width, q_width, k_width, v_width):
    out = pl.pallas_call(
        kernel,
        out_shape=jax.ShapeDtypeStruct((BATCH, NT, out_width), BF16),
        grid=(BATCH, N_LAT_TILES),
        in_specs=[pl.BlockSpec((1, TOK_TILE, q_width), lambda b, t: (b, t, 0)),
                  pl.BlockSpec((1, NT, k_width), lambda b, t: (b, 0, 0)),
                  pl.BlockSpec((1, NT, v_width), lambda b, t: (b, 0, 0))] + extra_specs(2),
        out_specs=pl.BlockSpec((1, TOK_TILE, out_width), lambda b, t: (b, t, 0)),
        compiler_params=_params(("arbitrary", "arbitrary")),
        name=name + "_latent",
    )(q, k, v, *extra)
    return out


def _full_attention_ctx(kernel, name, q, k, v, extra, extra_specs, out_width, q_width, k_width, v_width,
                        prev_out):
    ctx_blk = SEQ // CTX_LEN
    n_extra = len(extra)

    def body(*refs):
        ins = refs[:3 + n_extra]
        kernel(*ins, refs[-1])

    spec = lambda n: pl.BlockSpec((1, CTX_LEN, n), lambda b: (b, ctx_blk, 0))
    return pl.pallas_call(
        body,
        out_shape=jax.ShapeDtypeStruct((BATCH, NT, out_width), BF16),
        grid=(BATCH,),
        in_specs=[spec(q_width), spec(k_width), spec(v_width)] + extra_specs(1)
                 + [pl.BlockSpec(memory_space=pl.ANY)],
        out_specs=spec(out_width),
        input_output_aliases={3 + n_extra: 0},
        compiler_params=_params(("arbitrary",)),
        name=name + "_ctx",
    )(q, k, v, *extra, prev_out)


def _attn_c_kernel(q_ref, k_ref, v_ref, lq1_ref, lk1_ref, lq2_ref, lk2_ref, sn_ref, o_ref, *, lam_init):
    q = q_ref[0]
    k = k_ref[0]
    v = v_ref[0]
    tq = q.shape[0]
    lam = (jnp.exp(jnp.sum(lq1_ref[...] * lk1_ref[...], axis=-1, keepdims=True))
           - jnp.exp(jnp.sum(lq2_ref[...] * lk2_ref[...], axis=-1, keepdims=True)) + lam_init)
    lane = lax.broadcasted_iota(jnp.int32, (1, C_HEADS * C_V), 1)
    acc = jnp.zeros((tq, C_HEADS * C_V), F32)
    for h in range(C_HEADS):
        lo = h * C_V
        q1 = q * _indicator((lane >= lo) & (lane < lo + C_DH))
        q2 = q * _indicator((lane >= lo + C_DH) & (lane < lo + C_V))
        p = _softmax_rows(_dot_nt(jnp.concatenate([q1, q2], axis=0), k))
        a = (p[0:tq] - lam * p[tq:2 * tq]).astype(BF16)
        o = _dot(a, v)
        acc = acc + jnp.where((lane >= lo) & (lane < lo + C_V), o, 0.0)
    sq = acc * acc
    inv = jnp.zeros((tq, C_HEADS * C_V), F32)
    for h in range(C_HEADS):
        in_head = (lane >= h * C_V) & (lane < (h + 1) * C_V)
        ms = jnp.sum(jnp.where(in_head, sq, 0.0), axis=-1, keepdims=True) * (1.0 / C_V)
        inv = inv + jnp.where(in_head, lax.rsqrt(ms + EPS), 0.0)
    o_ref[0] = ((acc * inv) * sn_ref[...] * (1.0 - lam_init)).astype(BF16)


def _outproj_kernel(x_ref, oa_ref, ob_ref, od_ref, w_ref, mod_ref, g2_ref, wr_ref,
                    xo_ref, h2_ref, aff_ref):
    mod = mod_ref[0]
    g1 = mod[:, 2 * D_MODEL:3 * D_MODEL]
    sh2 = mod[:, 3 * D_MODEL:4 * D_MODEL]
    sc2 = mod[:, 4 * D_MODEL:5 * D_MODEL]
    mix = jnp.concatenate([oa_ref[0], ob_ref[0], od_ref[0]], axis=-1)
    x = x_ref[0] + g1 * _dot(mix, w_ref[...])
    xo_ref[0] = x
    h2 = (_rms(x) * g2_ref[...]) * (1.0 + sc2) + sh2
    h2_ref[0] = h2.astype(BF16)
    logits = _dot3(h2, wr_ref[...])
    lane = lax.broadcasted_iota(jnp.int32, (1, LANES), 1)
    logits = jnp.where(lane < N_EXPERTS, logits, NEG_INF)
    aff_ref[0] = _softmax_rows(logits)


def _outproj(xall, oa, ob, od, w, mod_l, g2, wr, n_tiles):
    full = lambda shape: pl.BlockSpec(shape, lambda b, t: (0,) * len(shape))
    tok = lambda n: pl.BlockSpec((1, TOK_TILE, n), lambda b, t: (b, t, 0))
    return pl.pallas_call(
        _outproj_kernel,
        out_shape=[jax.ShapeDtypeStruct((BATCH, NT, D_MODEL), F32),
                   jax.ShapeDtypeStruct((BATCH, NT, D_MODEL), BF16),
                   jax.ShapeDtypeStruct((BATCH, NT, LANES), F32)],
        grid=(BATCH, n_tiles),
        in_specs=[tok(D_MODEL), tok(512), tok(256), tok(256), full((D_MODEL, D_MODEL)),
                  pl.BlockSpec((1, 1, 6 * D_MODEL), lambda b, t: (_mod_row(b, t), 0, 0)),
                  full((1, D_MODEL)), full((D_MODEL, LANES))],
        out_specs=[tok(D_MODEL), tok(D_MODEL), tok(LANES)],
        compiler_params=_params(("arbitrary", "arbitrary")),
        name="out_projection",
    )(xall, oa, ob, od, w, mod_l, g2, wr)


def _prefix_count(flags_f32, tri_ref):
    rows, n = flags_f32.shape
    run = jnp.zeros((rows, 1), F32)
    outs = []
    for c in range(n // LANES):
        blk = flags_f32[:, c * LANES:(c + 1) * LANES]
        outs.append(_dot(blk.astype(BF16), tri_ref[...]) + run)
        run = run + jnp.sum(blk, axis=-1, keepdims=True)
    return jnp.concatenate(outs, axis=-1)


def _select_slots(a, cap, tri_ref):
    bits = pltpu.bitcast(a, jnp.int32)

    def step(i, t):
        cand = t | jnp.left_shift(jnp.int32(1), 30 - i)
        cnt = jnp.sum((bits >= cand).astype(jnp.int32), axis=-1, keepdims=True)
        return jnp.where(cnt >= cap, cand, t)

    thr = lax.fori_loop(0, 31, step, jnp.zeros((a.shape[0], 1), jnp.int32))
    above = bits > thr
    tied = bits == thr
    need = cap - jnp.sum(above.astype(F32), axis=-1, keepdims=True)
    tied_f = tied.astype(F32)
    sel = above | (tied & (_prefix_count(tied_f, tri_ref) < need))
    sel_f = sel.astype(F32)
    return jnp.where(sel, _prefix_count(sel_f, tri_ref), -1.0)


def _route_kernel(aff_ref, tri_ref, row_ref, col_ref, *, with_ctx):
    lat, ctx = [], []
    for b in range(BATCH):
        lat.append(aff_ref[b, 0:SEQ, :].T[0:N_EXPERTS])
        if with_ctx:
            ctx.append(aff_ref[b, SEQ:NT, :].T[0:N_EXPERTS])
    slot_lat = _select_slots(jnp.concatenate(lat, axis=0), CAP_LAT, tri_ref)
    if with_ctx:
        slot_ctx = _select_slots(jnp.concatenate(ctx, axis=0), CAP_CTX, tri_ref)
        slots = jnp.concatenate([slot_lat, slot_ctx], axis=-1)
    else:
        slots = slot_lat
    n = slots.shape[1]
    row_ref[:, 0:n] = slots
    pad = jnp.full((LANES - N_EXPERTS, n), -1.0, F32)
    for b in range(BATCH):
        blk = jnp.concatenate([slots[b * N_EXPERTS:(b + 1) * N_EXPERTS], pad], axis=0)
        col_ref[b, 0:n, :] = blk.T


def _route(aff, tri, with_ctx):
    vm = pl.BlockSpec(memory_space=pltpu.VMEM)
    return pl.pallas_call(
        functools.partial(_route_kernel, with_ctx=with_ctx),
        out_shape=[jax.ShapeDtypeStruct((BATCH * N_EXPERTS, NT), F32),
                   jax.ShapeDtypeStruct((BATCH, NT, LANES), F32)],
        in_specs=[vm, vm],
        out_specs=[vm, vm],
        compiler_params=pltpu.CompilerParams(vmem_limit_bytes=VMEM_LIMIT),
        name="route_topc",
    )(aff, tri)


def _gather_kernel(row_ref, h_ref, *out_refs, with_ctx):
    row_lat = row_ref[0, :, 0:SEQ]
    slot = lax.broadcasted_iota(jnp.int32, (CAP_LAT, 1), 0).astype(F32)
    p = _indicator(row_lat == slot)
    out_refs[0][0] = _dot(p, h_ref[0, 0:SEQ, :]).astype(BF16)
    if with_ctx:
        pc = _indicator(row_ref[0, :, SEQ:NT] == slot[0:CAP_CTX])
        out_refs[1][0] = _dot(pc, h_ref[0, SEQ:NT, :]).astype(BF16)


def _gather(slot_rows, h2, with_ctx):
    out_shape = [jax.ShapeDtypeStruct((N_EXPERTS, SEQ, D_MODEL), BF16)]
    out_specs = [pl.BlockSpec((1, CAP_LAT, D_MODEL), lambda b, e: (e, b, 0))]
    if with_ctx:
        out_shape.append(jax.ShapeDtypeStruct((N_EXPERTS, CTX_LEN, D_MODEL), BF16))
        out_specs.append(pl.BlockSpec((1, CAP_CTX, D_MODEL), lambda b, e: (e, b, 0)))
    return pl.pallas_call(
        functools.partial(_gather_kernel, with_ctx=with_ctx),
        out_shape=out_shape,
        grid=(BATCH, N_EXPERTS),
        in_specs=[pl.BlockSpec((1, 1, NT), lambda b, e: (b * N_EXPERTS + e, 0, 0)),
                  pl.BlockSpec((1, NT, D_MODEL), lambda b, e: (b, 0, 0))],
        out_specs=out_specs,
        compiler_params=_params(("arbitrary", "arbitrary")),
        name="expert_gather",
    )(slot_rows, h2)


def _ffn_kernel(*refs, with_ctx):
    if with_ctx:
        xl_ref, xc_ref, wg_ref, wu_ref, wd_ref, y_ref, wg_s, wu_s, wd_s = refs
    else:
        xl_ref, wg_ref, wu_ref, wd_ref, y_ref, wg_s, wu_s, wd_s = refs
    r = pl.program_id(1)

    @pl.when(r == 0)
    def _():
        wg_s[...] = wg_ref[0].astype(BF16)
        wu_s[...] = wu_ref[0].astype(BF16)
        wd_s[...] = wd_ref[0].astype(BF16)

    x = xl_ref[0]
    if with_ctx:
        x = jnp.where(r < N_LAT_TILES, x, xc_ref[0])
    gate = _dot(x, wg_s[...])
    up = _dot(x, wu_s[...])
    hid = (gate / (1.0 + jnp.exp(-gate))) * up
    y_ref[0] = _dot(hid.astype(BF16), wd_s[...]).astype(BF16)


def _ffn(xg_lat, xg_ctx, w_gate, w_up, w_down):
    with_ctx = xg_ctx is not None
    n_tiles = N_TILES if with_ctx else N_LAT_TILES
    last = N_LAT_TILES - 1
    ins = [xg_lat] + ([xg_ctx] if with_ctx else []) + [w_gate, w_up, w_down]
    in_specs = [pl.BlockSpec((1, TOK_TILE, D_MODEL), lambda e, r: (e, jnp.minimum(r, last), 0))]
    if with_ctx:
        in_specs.append(pl.BlockSpec((1, CTX_LEN, D_MODEL), lambda e, r: (e, 0, 0)))
    in_specs += [pl.BlockSpec((1, D_MODEL, D_EXPERT), lambda e, r: (e, 0, 0)),
                 pl.BlockSpec((1, D_MODEL, D_EXPERT), lambda e, r: (e, 0, 0)),
                 pl.BlockSpec((1, D_EXPERT, D_MODEL), lambda e, r: (e, 0, 0))]
    return pl.pallas_call(
        functools.partial(_ffn_kernel, with_ctx=with_ctx),
        out_shape=jax.ShapeDtypeStruct((N_EXPERTS, NT, D_MODEL), BF16),
        grid=(N_EXPERTS, n_tiles),
        in_specs=in_specs,
        out_specs=pl.BlockSpec((1, TOK_TILE, D_MODEL), lambda e, r: (e, r, 0)),
        scratch_shapes=[pltpu.VMEM((D_MODEL, D_EXPERT), BF16),
                        pltpu.VMEM((D_MODEL, D_EXPERT), BF16),
                        pltpu.VMEM((D_EXPERT, D_MODEL), BF16)],
        compiler_params=_params(("arbitrary", "arbitrary")),
        name="expert_ffn",
    )(*ins)


def _combine_kernel(*refs, with_ctx, final):
    if with_ctx:
        x_ref, col_ref, aff_ref, yl_ref, yc_ref, mod_ref, o_ref = refs
    elif final:
        x_ref, col_ref, aff_ref, yl_ref, mod_ref, fn_ref, o_ref = refs
    else:
        x_ref, col_ref, aff_ref, yl_ref, mod_ref, o_ref = refs
    g2 = mod_ref[0][:, 5 * D_MODEL:6 * D_MODEL]
    col = col_ref[0]
    aff = aff_ref[0]

    def finish(acc):
        x = x_ref[0] + g2 * acc
        if final:
            x = _rms(x) * fn_ref[...]
        o_ref[0] = x

    def latent():
        lane = lax.broadcasted_iota(jnp.int32, (1, CAP_LAT), 1).astype(F32)
        acc = jnp.zeros((TOK_TILE, D_MODEL), F32)
        for e in range(N_EXPERTS):
            pt = _indicator(col[:, e:e + 1] == lane)
            acc = acc + aff[:, e:e + 1] * _dot(pt, yl_ref[e])
        finish(acc)

    def context():
        lane = lax.broadcasted_iota(jnp.int32, (1, LANES), 1).astype(F32)
        pad = jnp.zeros((LANES - CAP_CTX, D_MODEL), BF16)
        acc = jnp.zeros((TOK_TILE, D_MODEL), F32)
        for e in range(N_EXPERTS):
            pt = _indicator(col[:, e:e + 1] == lane)
            acc = acc + aff[:, e:e + 1] * _dot(pt, jnp.concatenate([yc_ref[e], pad], axis=0))
        finish(acc)

    if with_ctx:
        t = pl.program_id(1)
        pl.when(t < N_LAT_TILES)(latent)
        pl.when(t == N_LAT_TILES)(context)
    else:
        latent()


def _combine(xall, slot_cols, aff, y, mod_l, with_ctx, final_norm=None):
    final = final_norm is not None
    n_tiles = N_TILES if with_ctx else N_LAT_TILES
    n_rows = SEQ if final else NT
    tok = lambda n: pl.BlockSpec((1, TOK_TILE, n), lambda b, t: (b, t, 0))
    ins = [xall, slot_cols, aff, y]
    in_specs = [tok(D_MODEL), tok(LANES), tok(LANES),
                pl.BlockSpec((N_EXPERTS, CAP_LAT, D_MODEL), lambda b, t: (0, b, 0))]
    if with_ctx:
        ins.append(y)
        in_specs.append(pl.BlockSpec((N_EXPERTS, CAP_CTX, D_MODEL), lambda b, t: (0, SEQ // CAP_CTX + b, 0)))
    ins.append(mod_l)
    in_specs.append(pl.BlockSpec((1, 1, 6 * D_MODEL), lambda b, t: (_mod_row(b, t), 0, 0)))
    if final:
        ins.append(final_norm)
        in_specs.append(pl.BlockSpec((1, D_MODEL), lambda b, t: (0, 0)))
    return pl.pallas_call(
        functools.partial(_combine_kernel, with_ctx=with_ctx, final=final),
        out_shape=jax.ShapeDtypeStruct((BATCH, n_rows, D_MODEL), F32),
        grid=(BATCH, n_tiles),
        in_specs=in_specs,
        out_specs=tok(D_MODEL),
        compiler_params=_params(("arbitrary", "arbitrary")),
        name="expert_combine",
    )(*ins)


def _rope_tables():
    t = jnp.arange(SEQ)
    pos = jnp.stack([t // GRID_W, t % GRID_W], axis=0)

    def table(n_f, lanes_per_unit):
        inv = ROPE_BASE ** (-jnp.arange(n_f, dtype=F32) / n_f)
        d = np.arange(lanes_per_unit)
        axis, second, f = d // (2 * n_f), (d // n_f) % 2, d % n_f
        ang = pos[axis].T.astype(F32) * inv[f][None, :]
        cos = jnp.cos(ang)
        sin = jnp.sin(ang) * jnp.where(second == 1, 1.0, -1.0)[None, :]
        return cos, sin

    def finish(cos, sin, scale):
        cos = jnp.concatenate([cos, jnp.ones((CTX_LEN, LANES), F32)], axis=0)
        sin = jnp.concatenate([sin, jnp.zeros((CTX_LEN, LANES), F32)], axis=0)
        return [cos, sin, cos * scale, sin * scale]

    cos_a, sin_a = table(HEAD_DIM // 4, HEAD_DIM)
    tabs_a = finish(jnp.tile(cos_a, (1, 2)), jnp.tile(sin_a, (1, 2)), HEAD_DIM ** -0.5)
    cos_8, sin_8 = table(B_ROPE // 4, B_ROPE)
    tabs_8 = finish(jnp.tile(cos_8, (1, 4)), jnp.tile(sin_8, (1, 4)), C_DH ** -0.5)
    ones = jnp.ones((SEQ, B_NOPE), F32)
    pad1 = jnp.ones((SEQ, LANES - B_NOPE - B_ROPE), F32)
    cos_b = jnp.concatenate([ones, cos_8, pad1], axis=1)
    sin_b = jnp.concatenate([0 * ones, sin_8, 0 * pad1], axis=1)
    tabs_b = finish(cos_b, sin_b, (B_NOPE + B_ROPE) ** -0.5)
    return tabs_a + tabs_8 + tabs_b


def _window_bias():
    qi = np.arange(WIN)[:, None]
    kj = np.arange(3 * WIN)[None, :]
    band = (kj >= qi) & (kj <= qi + 2 * WIN)
    out = np.zeros((3, WIN, 3 * WIN + CTX_LEN), np.float32)
    for v, (has_prev, has_next) in enumerate([(False, True), (True, True), (True, False)]):
        ok = band & ((kj >= WIN) | has_prev) & ((kj < 2 * WIN) | has_next)
        out[v, :, :3 * WIN] = np.where(ok, 0.0, NEG_INF)
    return jnp.asarray(out)


def _prep_weights(w_in, w_uq, w_ukv, w_out, w_router):
    offs = np.cumsum((0, 512, 128, 128, 256, 128, 32, 256, 256, 256))
    seg = lambda i: w_in[:, :, offs[i]:offs[i + 1]]
    head_order = np.array([[j, A_GROUP + j] for j in range(A_GROUP)]).reshape(-1)
    a_cols = (head_order[:, None] * HEAD_DIM + np.arange(HEAD_DIM)[None, :]).reshape(-1)
    kr = jnp.pad(seg(5), ((0, 0), (0, 0), (B_NOPE, LANES - B_NOPE - B_ROPE)))
    w = jnp.concatenate([seg(0)[:, :, a_cols], seg(1), seg(2), seg(3), seg(4), kr,
                         seg(6), seg(7), seg(8)], axis=-1).astype(BF16)
    wuq = w_uq.reshape(DEPTH, B_Q_RANK, B_HEADS, B_NOPE + B_ROPE)
    wuq = jnp.pad(wuq, ((0, 0), (0, 0), (0, 0), (0, LANES - B_NOPE - B_ROPE)))
    wuq = wuq.reshape(DEPTH, B_Q_RANK, B_HEADS * LANES).astype(BF16)
    wukv = w_ukv.reshape(DEPTH, B_KV_RANK, B_HEADS, B_NOPE + B_V)
    wk = jnp.pad(wukv[..., :B_NOPE], ((0, 0), (0, 0), (0, 0), (0, LANES - B_NOPE)))
    wukv = jnp.concatenate([wk.reshape(DEPTH, B_KV_RANK, B_HEADS * LANES),
                            wukv[..., B_NOPE:].reshape(DEPTH, B_KV_RANK, B_HEADS * B_V)],
                           axis=-1).astype(BF16)
    wo = jnp.concatenate([w_out[:, :512][:, a_cols], w_out[:, 512:]], axis=1).astype(BF16)
    wr = jnp.pad(w_router, ((0, 0), (0, 0), (0, LANES - N_EXPERTS)))
    return w, wuq, wukv, wo, wr


def kernel(x, c, ctx, c_ctx, norm1, norm2, w_ada, b_ada, w_in, sink, mla_q_norm, w_uq, mla_kv_norm, w_ukv,
           lam_q1, lam_k1, lam_q2, lam_k2, diff_norm, w_out, w_router, w_gate, w_up, w_down, final_norm):
    xall = jnp.concatenate([x, ctx], axis=1)
    cc = jnp.concatenate([c, c_ctx[None, :], jnp.zeros((16 - BATCH - 1, D_MODEL), F32)], axis=0)
    mod = _modulation(cc, w_ada, b_ada).reshape(DEPTH, 16, 1, 6 * D_MODEL)
    tabs = _rope_tables()
    tabs = [tabs[0], tabs[1], tabs[2], tabs[3], tabs[4], tabs[5], tabs[6], tabs[7],
            tabs[8], tabs[9], tabs[10], tabs[11]]
    bias = _window_bias()
    tri = jnp.asarray(np.triu(np.ones((LANES, LANES), np.float32), k=1), BF16)
    w, wuq, wukv, wo, wr = _prep_weights(w_in, w_uq, w_ukv, w_out, w_router)
    sub_norm = jnp.tile(diff_norm, (1, C_HEADS)).reshape(DEPTH, 1, C_HEADS * C_V)
    row = lambda a, l: a[l].reshape(1, -1)

    for l in range(DEPTH):
        need_ctx = l < DEPTH - 1
        qa, ka, va, qb, kb, vb, qd, kd, vd = _inproj(
            xall, mod[l], row(norm1, l), w[l], row(mla_q_norm, l), wuq[l], row(mla_kv_norm, l), wukv[l], tabs)

        oa = _attn_a_latent(sink[l], qa, ka, va, bias)
        ob = _full_attention(_attn_b_kernel, "attn_mla", qb, kb, vb, [], lambda n: [], 256, 512, 512, 256)
        lam_init = 0.8 - 0.6 * math.exp(-0.3 * l)
        c_kernel = functools.partial(_attn_c_kernel, lam_init=lam_init)
        c_extra = [row(lam_q1, l), row(lam_k1, l), row(lam_q2, l), row(lam_k2, l), sub_norm[l]]

        def c_specs(n_grid, c_extra=c_extra):
            zmap = (lambda b, t: (0, 0)) if n_grid == 2 else (lambda b: (0, 0))
            return [pl.BlockSpec(a.shape, zmap) for a in c_extra]

        od = _full_attention(c_kernel, "attn_diff", qd, kd, vd, c_extra, c_specs, 256, 256, 256, 256)
        if need_ctx:
            oa = _attn_a_ctx(sink[l], qa, ka, va, oa)
            ob = _full_attention_ctx(_attn_b_kernel, "attn_mla", qb, kb, vb, [], lambda n: [],
                                     256, 512, 512, 256, ob)
            od = _full_attention_ctx(c_kernel, "attn_diff", qd, kd, vd, c_extra, c_specs,
                                     256, 256, 256, 256, od)

        n_tiles = N_TILES if need_ctx else N_LAT_TILES
        xall, h2, aff = _outproj(xall, oa, ob, od, wo[l], mod[l], row(norm2, l), wr[l], n_tiles)
        slot_rows, slot_cols = _route(aff, tri, need_ctx)
        gathered = _gather(slot_rows.reshape(BATCH * N_EXPERTS, 1, NT), h2, need_ctx)
        y = _ffn(gathered[0], gathered[1] if need_ctx else None, w_gate[l], w_up[l], w_down[l])
        xall = _combine(xall, slot_cols, aff, y, mod[l], need_ctx,
                        final_norm=None if need_ctx else final_norm.reshape(1, D_MODEL))
    return xall
```

```python
import functools
import math

import numpy as np
import jax
import jax.numpy as jnp
from jax import lax
from jax.experimental import pallas as pl
from jax.experimental.pallas import tpu as pltpu

D_MODEL = 1024
BATCH = 8
SEQ = 2048
DEPTH = 4
CTX_LEN = 256
NT = SEQ + CTX_LEN
GRID_W = 64
HEAD_DIM = 64
ROPE_BASE = 10000.0
EPS = 1e-6
NEG_INF = -1e30
LOG2_E = math.log2(math.e)

WIN = 128
A_HEADS = 8
A_KV_HEADS = 2
A_GROUP = A_HEADS // A_KV_HEADS
B_HEADS = 4
B_NOPE = 64
B_ROPE = 32
B_V = 64
B_Q_RANK = 256
B_KV_RANK = 128
C_HEADS = 4
C_DH = 32
C_V = 2 * C_DH
N_EXPERTS = 16
EC_FACTOR = 2
D_EXPERT = 512
CAP_LAT = EC_FACTOR * SEQ // N_EXPERTS
CAP_CTX = EC_FACTOR * CTX_LEN // N_EXPERTS

LANES = 128
TOK_TILE = 256
N_TILES = NT // TOK_TILE
N_LAT_TILES = SEQ // TOK_TILE
VMEM_LIMIT = 56 * 1024 * 1024

F32 = jnp.float32
BF16 = jnp.bfloat16

_NT_DIMS = (((1,), (1,)), ((), ()))


def _params(sem):
    return pltpu.CompilerParams(dimension_semantics=sem, vmem_limit_bytes=VMEM_LIMIT)


def _dot(a, b):
    return jnp.dot(a, b, preferred_element_type=F32)


def _dot_nt(a, b):
    return lax.dot_general(a, b, _NT_DIMS, preferred_element_type=F32)


def _split_bf16(v):
    hi = v.astype(BF16)
    lo = (v - hi.astype(F32)).astype(BF16)
    return hi, lo


def _dot3(a, b):
    a_hi, a_lo = _split_bf16(a)
    b_hi, b_lo = _split_bf16(b)
    return _dot(a_hi, b_hi) + _dot(a_lo, b_hi) + _dot(a_hi, b_lo)


def _indicator(cond):
    return jnp.where(cond, 1.0, 0.0).astype(BF16)


def _rms(v):
    return v * lax.rsqrt(jnp.mean(v * v, axis=-1, keepdims=True) + EPS)


def _softmax_pv(s, v_blk, head):
    m = jnp.max(s, axis=-1, keepdims=True)
    ol = _dot(jnp.exp2(s - m).astype(BF16), v_blk)
    ones_lane = 64 if head % 2 == 0 else 0
    return ol / ol[:, ones_lane:ones_lane + 1]


def _merge_head_pairs(per_head):
    lane = lax.broadcasted_iota(jnp.int32, (1, LANES), 1)
    blocks = [jnp.where(lane < 64, per_head[j], per_head[j + 1]) for j in range(0, len(per_head), 2)]
    return jnp.concatenate(blocks, axis=-1)


def _softmax_rows(s):
    m = jnp.max(s, axis=-1, keepdims=True)
    e = jnp.exp(s - m)
    return e / jnp.sum(e, axis=-1, keepdims=True)


def _mod_kernel(c_ref, w_ref, b_ref, o_ref):
    c = c_ref[...]
    s = c / (1.0 + jnp.exp(-c))
    o_ref[0] = _dot3(s, w_ref[0]) + b_ref[0]


def _modulation(cc, w_ada, b_ada):
    tn = 1536
    return pl.pallas_call(
        _mod_kernel,
        out_shape=jax.ShapeDtypeStruct((DEPTH, 16, 6 * D_MODEL), F32),
        grid=(DEPTH, 6 * D_MODEL // tn),
        in_specs=[
            pl.BlockSpec((16, D_MODEL), lambda l, j: (0, 0)),
            pl.BlockSpec((1, D_MODEL, tn), lambda l, j: (l, 0, j)),
            pl.BlockSpec((1, 1, tn), lambda l, j: (l, 0, j)),
        ],
        out_specs=pl.BlockSpec((1, 16, tn), lambda l, j: (l, 0, j)),
        compiler_params=_params(("arbitrary", "arbitrary")),
        name="adaln_mod",
    )(cc, w_ada, b_ada.reshape(DEPTH, 1, 6 * D_MODEL))


def _rope(p, cos, sin, half):
    lane = lax.broadcasted_iota(jnp.int32, (1, LANES), 1)
    first = (lane // half) % 2 == 0
    outs = []
    for j in range(p.shape[1] // LANES):
        blk = p[:, j * LANES:(j + 1) * LANES]
        partner = jnp.where(first, pltpu.roll(blk, LANES - half, 1), pltpu.roll(blk, half, 1))
        outs.append(blk * cos + partner * sin)
    return outs[0] if len(outs) == 1 else jnp.concatenate(outs, axis=-1)


def _per_head_with_ones(v):
    lane = lax.broadcasted_iota(jnp.int32, (1, LANES), 1)
    outs = []
    for j in range(v.shape[1] // LANES):
        blk = v[:, j * LANES:(j + 1) * LANES]
        outs.append(jnp.where(lane < 64, blk, jnp.where(lane == 64, 1.0, 0.0)))
        outs.append(jnp.where(lane >= 64, blk, jnp.where(lane == 0, 1.0, 0.0)))
    return jnp.concatenate(outs, axis=-1)


def _inproj_kernel(x_ref, mod_ref, g1_ref, w_ref, qn_ref, wuq_ref, kvn_ref, wukv_ref,
                   ca_ref, sa_ref, caq_ref, saq_ref, c8_ref, s8_ref, c8q_ref, s8q_ref,
                   cb_ref, sb_ref, cbq_ref, sbq_ref,
                   qa_ref, ka_ref, va_ref, qb_ref, kb_ref, vb_ref, qd_ref, kd_ref, vd_ref):
    x = x_ref[0]
    mod = mod_ref[0]
    sh1 = mod[:, 0:D_MODEL]
    sc1 = mod[:, D_MODEL:2 * D_MODEL]
    h = (_rms(x) * g1_ref[...]) * (1.0 + sc1) + sh1
    p = _dot(h.astype(BF16), w_ref[...])

    qa_ref[0] = _rope(p[:, 0:512], caq_ref[...], saq_ref[...], 16).astype(BF16)
    ka_ref[0] = _rope(p[:, 512:640], ca_ref[...], sa_ref[...], 16).astype(BF16)
    lane = lax.broadcasted_iota(jnp.int32, (x.shape[0], LANES), 1)
    va_ref[0] = jnp.concatenate([p[:, 640:768], jnp.where(lane == 0, 1.0, 0.0)], axis=-1).astype(BF16)

    cq = (_rms(p[:, 768:1024]) * qn_ref[...]).astype(BF16)
    qb = _dot(cq, wuq_ref[...])
    qb_ref[0] = _rope(qb, cbq_ref[...], sbq_ref[...], 8).astype(BF16)
    ckv = (_rms(p[:, 1024:1152]) * kvn_ref[...]).astype(BF16)
    kv = _dot(ckv, wukv_ref[...])
    kr = _rope(p[:, 1152:1280], cb_ref[...], sb_ref[...], 8)
    kb_ref[0] = (kv[:, 0:512] + jnp.concatenate([kr] * B_HEADS, axis=-1)).astype(BF16)
    vb_ref[0] = _per_head_with_ones(kv[:, 512:768]).astype(BF16)

    qd_ref[0] = _rope(p[:, 1280:1536], c8q_ref[...], s8q_ref[...], 8).astype(BF16)
    kd_ref[0] = _rope(p[:, 1536:1792], c8_ref[...], s8_ref[...], 8).astype(BF16)
    vd_ref[0] = _per_head_with_ones(p[:, 1792:2048]).astype(BF16)


def _mod_row(b, t):
    return jnp.where(t == N_LAT_TILES, BATCH, b)


def _inproj(xall, mod_l, g1, w, qn, wuq, kvn, wukv, tabs):
    full = lambda shape: pl.BlockSpec(shape, lambda b, t: (0,) * len(shape))
    tab_spec = pl.BlockSpec((TOK_TILE, LANES), lambda b, t: (t, 0))
    widths = (512, 128, 256, 512, 512, 512, 256, 256, 512)
    return pl.pallas_call(
        _inproj_kernel,
        out_shape=[jax.ShapeDtypeStruct((BATCH, NT, n), BF16) for n in widths],
        grid=(BATCH, N_TILES),
        in_specs=[
            pl.BlockSpec((1, TOK_TILE, D_MODEL), lambda b, t: (b, t, 0)),
            pl.BlockSpec((1, 1, 6 * D_MODEL), lambda b, t: (_mod_row(b, t), 0, 0)),
            full((1, D_MODEL)),
            full((D_MODEL, 2048)),
            full((1, B_Q_RANK)),
            full((B_Q_RANK, 512)),
            full((1, B_KV_RANK)),
            full((B_KV_RANK, 768)),
        ] + [tab_spec] * 12,
        out_specs=[pl.BlockSpec((1, TOK_TILE, n), lambda b, t: (b, t, 0)) for n in widths],
        compiler_params=_params(("arbitrary", "arbitrary")),
        name="in_projection",
    )(xall, mod_l, g1, w, qn, wuq, kvn, wukv, *tabs)


def _attn_a_kernel(*refs, n_kv_blocks, has_bias, tq):
    sink_ref, q_ref = refs[0], refs[1]
    k_refs = refs[2:2 + n_kv_blocks]
    v_refs = refs[2 + n_kv_blocks:2 + 2 * n_kv_blocks]
    pos = 2 + 2 * n_kv_blocks
    bias_ref = refs[pos] if has_bias else None
    o_ref = refs[-1]

    q = q_ref[0]
    kcat = jnp.concatenate([r[0] for r in k_refs], axis=0) if n_kv_blocks > 1 else k_refs[0][0]
    vcat = jnp.concatenate([r[0] for r in v_refs], axis=0) if n_kv_blocks > 1 else v_refs[0][0]
    lane = lax.broadcasted_iota(jnp.int32, (1, LANES), 1)
    acc = [jnp.zeros((tq, LANES), F32) for _ in range(A_GROUP)]
    for g in range(A_KV_HEADS):
        in_half = (lane < HEAD_DIM) if g == 0 else (lane >= HEAD_DIM)
        keep = _indicator(in_half)
        qs = jnp.concatenate([q[:, j * LANES:(j + 1) * LANES] * keep for j in range(A_GROUP)], axis=0)
        s = _dot_nt(qs, kcat)
        if has_bias:
            s = s + jnp.concatenate([bias_ref[0]] * A_GROUP, axis=0)
        sink = jnp.concatenate(
            [jnp.full((tq, 1), sink_ref[g * A_GROUP + j] * LOG2_E, F32) for j in range(A_GROUP)], axis=0)
        m = jnp.maximum(jnp.max(s, axis=-1, keepdims=True), sink)
        ol = _dot(jnp.exp2(s - m).astype(BF16), vcat)
        o = ol[:, 0:LANES] / (ol[:, LANES:LANES + 1] + jnp.exp2(sink - m))
        for j in range(A_GROUP):
            acc[j] = acc[j] + jnp.where(in_half, o[j * tq:(j + 1) * tq], 0.0)
    o_ref[0] = jnp.concatenate(acc, axis=-1).astype(BF16)


def _attn_a_latent(sink, qa, ka, va, bias):
    nb = SEQ // WIN
    last = nb - 1
    ctx_blk = SEQ // CTX_LEN
    kv_maps = [lambda b, n: (b, jnp.maximum(n - 1, 0), 0),
               lambda b, n: (b, n, 0),
               lambda b, n: (b, jnp.minimum(n + 1, last), 0)]
    def kv_specs(width):
        return ([pl.BlockSpec((1, WIN, width), m) for m in kv_maps]
                + [pl.BlockSpec((1, CTX_LEN, width), lambda b, n: (b, ctx_blk, 0))])
    bias_spec = pl.BlockSpec(
        (1, WIN, 3 * WIN + CTX_LEN),
        lambda b, n: (jnp.where(n == 0, 0, jnp.where(n == last, 2, 1)), 0, 0))
    return pl.pallas_call(
        functools.partial(_attn_a_kernel, n_kv_blocks=4, has_bias=True, tq=WIN),
        out_shape=jax.ShapeDtypeStruct((BATCH, NT, 512), BF16),
        grid=(BATCH, nb),
        in_specs=[pl.BlockSpec(memory_space=pltpu.SMEM),
                  pl.BlockSpec((1, WIN, 512), lambda b, n: (b, n, 0))]
                 + kv_specs(LANES) + kv_specs(2 * LANES) + [bias_spec],
        out_specs=pl.BlockSpec((1, WIN, 512), lambda b, n: (b, n, 0)),
        compiler_params=_params(("arbitrary", "arbitrary")),
        name="attn_window_latent",
    )(sink, qa, ka, ka, ka, ka, va, va, va, va, bias)


def _attn_a_ctx(sink, qa, ka, va, prev_out):
    ctx_blk = SEQ // CTX_LEN
    spec = lambda n: pl.BlockSpec((1, CTX_LEN, n), lambda b: (b, ctx_blk, 0))

    def body(sink_ref, q_ref, k_ref, v_ref, prev_ref, o_ref):
        del prev_ref
        _attn_a_kernel(sink_ref, q_ref, k_ref, v_ref, o_ref, n_kv_blocks=1, has_bias=False, tq=CTX_LEN)

    return pl.pallas_call(
        body,
        out_shape=jax.ShapeDtypeStruct((BATCH, NT, 512), BF16),
        grid=(BATCH,),
        in_specs=[pl.BlockSpec(memory_space=pltpu.SMEM), spec(512), spec(LANES), spec(2 * LANES),
                  pl.BlockSpec(memory_space=pl.ANY)],
        out_specs=spec(512),
        input_output_aliases={4: 0},
        compiler_params=_params(("arbitrary",)),
        name="attn_window_ctx",
    )(sink, qa, ka, va, prev_out)


def _attn_b_kernel(q_ref, k_ref, v_ref, o_ref):
    q = q_ref[0]
    k = k_ref[0]
    v = v_ref[0]
    outs = []
    for h in range(B_HEADS):
        s = _dot_nt(q[:, h * LANES:(h + 1) * LANES], k[:, h * LANES:(h + 1) * LANES])
        outs.append(_softmax_pv(s, v[:, h * LANES:(h + 1) * LANES], h))
    o_ref[0] = _merge_head_pairs(outs).astype(BF16)


def _full_attention(kernel, name, q, k, v, extra, extra_specs, out_width, q_width, k_width, v_width):
    out = pl.pallas_call(
        kernel,
        out_shape=jax.ShapeDtypeStruct((BATCH, NT, out_width), BF16),
        grid=(BATCH, N_LAT_TILES),
        in_specs=[pl.BlockSpec((1, TOK_TILE, q_width), lambda b, t: (b, t, 0)),
                  pl.BlockSpec((1, NT, k_width), lambda b, t: (b, 0, 0)),
                  pl.BlockSpec((1, NT, v_width), lambda b, t: (b, 0, 0))] + extra_specs(2),
        out_specs=pl.BlockSpec((1, TOK_TILE, out_width), lambda b, t: (b, t, 0)),
        compiler_params=_params(("arbitrary", "arbitrary")),
        name=name + "_latent",
    )(q, k, v, *extra)
    return out


def _full_attention_ctx(kernel, name, q, k, v, extra, extra_specs, out_width, q_width, k_width, v_width,
                        prev_out):
    ctx_blk = SEQ // CTX_LEN
    n_extra = len(extra)

    def body(*refs):
        ins = refs[:3 + n_extra]
        kernel(*ins, refs[-1])

    spec = lambda n: pl.BlockSpec((1, CTX_LEN, n), lambda b: (b, ctx_blk, 0))
    return pl.pallas_call(
        body,
        out_shape=jax.ShapeDtypeStruct((BATCH, NT, out_width), BF16),
        grid=(BATCH,),
        in_specs=[spec(q_width), spec(k_width), spec(v_width)] + extra_specs(1)
                 + [pl.BlockSpec(memory_space=pl.ANY)],
        out_specs=spec(out_width),
        input_output_aliases={3 + n_extra: 0},
        compiler_params=_params(("arbitrary",)),
        name=name + "_ctx",
    )(q, k, v, *extra, prev_out)


def _attn_c_kernel(q_ref, k_ref, v_ref, lq1_ref, lk1_ref, lq2_ref, lk2_ref, sn_ref, o_ref, *, lam_init):
    q = q_ref[0]
    k = k_ref[0]
    v = v_ref[0]
    tq = q.shape[0]
    lam = (jnp.exp(jnp.sum(lq1_ref[...] * lk1_ref[...], axis=-1, keepdims=True))
           - jnp.exp(jnp.sum(lq2_ref[...] * lk2_ref[...], axis=-1, keepdims=True)) + lam_init)
    lane = lax.broadcasted_iota(jnp.int32, (1, C_HEADS * C_V), 1)
    outs = []
    for h in range(C_HEADS):
        lo = h * C_V
        q1 = q * _indicator((lane >= lo) & (lane < lo + C_DH))
        q2 = q * _indicator((lane >= lo + C_DH) & (lane < lo + C_V))
        s = _dot_nt(jnp.concatenate([q1, q2], axis=0), k)
        pv = _softmax_pv(s, v[:, h * LANES:(h + 1) * LANES], h)
        outs.append(pv[0:tq] - lam * pv[tq:2 * tq])
    acc = _merge_head_pairs(outs)
    sq = acc * acc
    inv = jnp.zeros((tq, C_HEADS * C_V), F32)
    for h in range(C_HEADS):
        in_head = (lane >= h * C_V) & (lane < (h + 1) * C_V)
        ms = jnp.sum(jnp.where(in_head, sq, 0.0), axis=-1, keepdims=True) * (1.0 / C_V)
        inv = inv + jnp.where(in_head, lax.rsqrt(ms + EPS), 0.0)
    o_ref[0] = ((acc * inv) * sn_ref[...] * (1.0 - lam_init)).astype(BF16)


def _outproj_kernel(x_ref, oa_ref, ob_ref, od_ref, w_ref, mod_ref, g2_ref, wr_ref,
                    xo_ref, h2_ref, aff_ref):
    mod = mod_ref[0]
    g1 = mod[:, 2 * D_MODEL:3 * D_MODEL]
    sh2 = mod[:, 3 * D_MODEL:4 * D_MODEL]
    sc2 = mod[:, 4 * D_MODEL:5 * D_MODEL]
    mix = jnp.concatenate([oa_ref[0], ob_ref[0], od_ref[0]], axis=-1)
    x = x_ref[0] + g1 * _dot(mix, w_ref[...])
    xo_ref[0] = x
    h2 = (_rms(x) * g2_ref[...]) * (1.0 + sc2) + sh2
    h2_ref[0] = h2.astype(BF16)
    logits = _dot3(h2, wr_ref[...])
    lane = lax.broadcasted_iota(jnp.int32, (1, LANES), 1)
    logits = jnp.where(lane < N_EXPERTS, logits, NEG_INF)
    aff_ref[0] = _softmax_rows(logits)


def _outproj(xall, oa, ob, od, w, mod_l, g2, wr, n_tiles):
    full = lambda shape: pl.BlockSpec(shape, lambda b, t: (0,) * len(shape))
    tok = lambda n: pl.BlockSpec((1, TOK_TILE, n), lambda b, t: (b, t, 0))
    return pl.pallas_call(
        _outproj_kernel,
        out_shape=[jax.ShapeDtypeStruct((BATCH, NT, D_MODEL), F32),
                   jax.ShapeDtypeStruct((BATCH, NT, D_MODEL), BF16),
                   jax.ShapeDtypeStruct((BATCH, NT, LANES), F32)],
        grid=(BATCH, n_tiles),
        in_specs=[tok(D_MODEL), tok(512), tok(256), tok(256), full((D_MODEL, D_MODEL)),
                  pl.BlockSpec((1, 1, 6 * D_MODEL), lambda b, t: (_mod_row(b, t), 0, 0)),
                  full((1, D_MODEL)), full((D_MODEL, LANES))],
        out_specs=[tok(D_MODEL), tok(D_MODEL), tok(LANES)],
        compiler_params=_params(("arbitrary", "arbitrary")),
        name="out_projection",
    )(xall, oa, ob, od, w, mod_l, g2, wr)


def _prefix_count(flags_f32, tri_ref):
    rows, n = flags_f32.shape
    run = jnp.zeros((rows, 1), F32)
    outs = []
    for c in range(n // LANES):
        blk = flags_f32[:, c * LANES:(c + 1) * LANES]
        outs.append(_dot(blk.astype(BF16), tri_ref[...]) + run)
        run = run + jnp.sum(blk, axis=-1, keepdims=True)
    return jnp.concatenate(outs, axis=-1)


def _select_slots(a, cap, tri_ref):
    bits = pltpu.bitcast(a, jnp.int32)

    def step(i, t):
        cand = t | jnp.left_shift(jnp.int32(1), 30 - i)
        cnt = jnp.sum((bits >= cand).astype(jnp.int32), axis=-1, keepdims=True)
        return jnp.where(cnt >= cap, cand, t)

    thr = lax.fori_loop(0, 31, step, jnp.zeros((a.shape[0], 1), jnp.int32))
    above = bits > thr
    tied = bits == thr
    need = cap - jnp.sum(above.astype(F32), axis=-1, keepdims=True)
    tied_f = tied.astype(F32)
    sel = above | (tied & (_prefix_count(tied_f, tri_ref) < need))
    sel_f = sel.astype(F32)
    return jnp.where(sel, _prefix_count(sel_f, tri_ref), -1.0)


def _route_kernel(aff_ref, tri_ref, row_ref, col_ref, *, with_ctx):
    lat, ctx = [], []
    for b in range(BATCH):
        lat.append(aff_ref[b, 0:SEQ, :].T[0:N_EXPERTS])
        if with_ctx:
            ctx.append(aff_ref[b, SEQ:NT, :].T[0:N_EXPERTS])
    slot_lat = _select_slots(jnp.concatenate(lat, axis=0), CAP_LAT, tri_ref)
    if with_ctx:
        slot_ctx = _select_slots(jnp.concatenate(ctx, axis=0), CAP_CTX, tri_ref)
        slots = jnp.concatenate([slot_lat, slot_ctx], axis=-1)
    else:
        slots = slot_lat
    n = slots.shape[1]
    row_ref[:, 0:n] = slots
    pad = jnp.full((LANES - N_EXPERTS, n), -1.0, F32)
    for b in range(BATCH):
        blk = jnp.concatenate([slots[b * N_EXPERTS:(b + 1) * N_EXPERTS], pad], axis=0)
        col_ref[b, 0:n, :] = blk.T


def _route(aff, tri, with_ctx):
    vm = pl.BlockSpec(memory_space=pltpu.VMEM)
    return pl.pallas_call(
        functools.partial(_route_kernel, with_ctx=with_ctx),
        out_shape=[jax.ShapeDtypeStruct((BATCH * N_EXPERTS, NT), F32),
                   jax.ShapeDtypeStruct((BATCH, NT, LANES), F32)],
        in_specs=[vm, vm],
        out_specs=[vm, vm],
        compiler_params=pltpu.CompilerParams(vmem_limit_bytes=VMEM_LIMIT),
        name="route_topc",
    )(aff, tri)


def _gather_kernel(row_ref, h_ref, *out_refs, with_ctx):
    row_lat = row_ref[0, :, 0:SEQ]
    slot = lax.broadcasted_iota(jnp.int32, (CAP_LAT, 1), 0).astype(F32)
    p = _indicator(row_lat == slot)
    out_refs[0][0] = _dot(p, h_ref[0, 0:SEQ, :]).astype(BF16)
    if with_ctx:
        pc = _indicator(row_ref[0, :, SEQ:NT] == slot[0:CAP_CTX])
        out_refs[1][0] = _dot(pc, h_ref[0, SEQ:NT, :]).astype(BF16)


def _gather(slot_rows, h2, with_ctx):
    out_shape = [jax.ShapeDtypeStruct((N_EXPERTS, SEQ, D_MODEL), BF16)]
    out_specs = [pl.BlockSpec((1, CAP_LAT, D_MODEL), lambda b, e: (e, b, 0))]
    if with_ctx:
        out_shape.append(jax.ShapeDtypeStruct((N_EXPERTS, CTX_LEN, D_MODEL), BF16))
        out_specs.append(pl.BlockSpec((1, CAP_CTX, D_MODEL), lambda b, e: (e, b, 0)))
    return pl.pallas_call(
        functools.partial(_gather_kernel, with_ctx=with_ctx),
        out_shape=out_shape,
        grid=(BATCH, N_EXPERTS),
        in_specs=[pl.BlockSpec((1, 1, NT), lambda b, e: (b * N_EXPERTS + e, 0, 0)),
                  pl.BlockSpec((1, NT, D_MODEL), lambda b, e: (b, 0, 0))],
        out_specs=out_specs,
        compiler_params=_params(("arbitrary", "arbitrary")),
        name="expert_gather",
    )(slot_rows, h2)


def _ffn_kernel(*refs, with_ctx):
    if with_ctx:
        xl_ref, xc_ref, wg_ref, wu_ref, wd_ref, y_ref, wg_s, wu_s, wd_s = refs
    else:
        xl_ref, wg_ref, wu_ref, wd_ref, y_ref, wg_s, wu_s, wd_s = refs
    r = pl.program_id(1)

    @pl.when(r == 0)
    def _():
        wg_s[...] = wg_ref[0, 0].astype(BF16)
        wu_s[...] = wu_ref[0, 0].astype(BF16)
        wd_s[...] = wd_ref[0, 0].astype(BF16)

    x = xl_ref[0]
    if with_ctx:
        x = jnp.where(r < N_LAT_TILES, x, xc_ref[0])
    gate = _dot(x, wg_s[...])
    up = _dot(x, wu_s[...])
    hid = (gate / (1.0 + jnp.exp(-gate))) * up
    y_ref[0] = _dot(hid.astype(BF16), wd_s[...]).astype(BF16)


def _ffn(xg_lat, xg_ctx, w_gate, w_up, w_down, layer):
    with_ctx = xg_ctx is not None
    n_tiles = N_TILES if with_ctx else N_LAT_TILES
    last = N_LAT_TILES - 1
    ins = [xg_lat] + ([xg_ctx] if with_ctx else []) + [w_gate, w_up, w_down]
    in_specs = [pl.BlockSpec((1, TOK_TILE, D_MODEL), lambda e, r: (e, jnp.minimum(r, last), 0))]
    if with_ctx:
        in_specs.append(pl.BlockSpec((1, CTX_LEN, D_MODEL), lambda e, r: (e, 0, 0)))
    in_specs += [pl.BlockSpec((1, 1, D_MODEL, D_EXPERT), lambda e, r: (layer, e, 0, 0)),
                 pl.BlockSpec((1, 1, D_MODEL, D_EXPERT), lambda e, r: (layer, e, 0, 0)),
                 pl.BlockSpec((1, 1, D_EXPERT, D_MODEL), lambda e, r: (layer, e, 0, 0))]
    return pl.pallas_call(
        functools.partial(_ffn_kernel, with_ctx=with_ctx),
        out_shape=jax.ShapeDtypeStruct((N_EXPERTS, NT, D_MODEL), BF16),
        grid=(N_EXPERTS, n_tiles),
        in_specs=in_specs,
        out_specs=pl.BlockSpec((1, TOK_TILE, D_MODEL), lambda e, r: (e, r, 0)),
        scratch_shapes=[pltpu.VMEM((D_MODEL, D_EXPERT), BF16),
                        pltpu.VMEM((D_MODEL, D_EXPERT), BF16),
                        pltpu.VMEM((D_EXPERT, D_MODEL), BF16)],
        compiler_params=_params(("arbitrary", "arbitrary")),
        name="expert_ffn",
    )(*ins)


def _combine_kernel(*refs, with_ctx, final):
    if with_ctx:
        x_ref, col_ref, aff_ref, yl_ref, yc_ref, mod_ref, o_ref = refs
    elif final:
        x_ref, col_ref, aff_ref, yl_ref, mod_ref, fn_ref, o_ref = refs
    else:
        x_ref, col_ref, aff_ref, yl_ref, mod_ref, o_ref = refs
    g2 = mod_ref[0][:, 5 * D_MODEL:6 * D_MODEL]
    col = col_ref[0]
    aff = aff_ref[0]

    def finish(acc):
        x = x_ref[0] + g2 * acc
        if final:
            x = _rms(x) * fn_ref[...]
        o_ref[0] = x

    def latent():
        lane = lax.broadcasted_iota(jnp.int32, (1, CAP_LAT), 1).astype(F32)
        acc = jnp.zeros((TOK_TILE, D_MODEL), F32)
        for e in range(N_EXPERTS):
            pt = _indicator(col[:, e:e + 1] == lane)
            acc = acc + aff[:, e:e + 1] * _dot(pt, yl_ref[e])
        finish(acc)

    def context():
        lane = lax.broadcasted_iota(jnp.int32, (1, LANES), 1).astype(F32)
        pad = jnp.zeros((LANES - CAP_CTX, D_MODEL), BF16)
        acc = jnp.zeros((TOK_TILE, D_MODEL), F32)
        for e in range(N_EXPERTS):
            pt = _indicator(col[:, e:e + 1] == lane)
            acc = acc + aff[:, e:e + 1] * _dot(pt, jnp.concatenate([yc_ref[e], pad], axis=0))
        finish(acc)

    if with_ctx:
        t = pl.program_id(1)
        pl.when(t < N_LAT_TILES)(latent)
        pl.when(t == N_LAT_TILES)(context)
    else:
        latent()


def _combine(xall, slot_cols, aff, y, mod_l, with_ctx, final_norm=None):
    final = final_norm is not None
    n_tiles = N_TILES if with_ctx else N_LAT_TILES
    n_rows = SEQ if final else NT
    tok = lambda n: pl.BlockSpec((1, TOK_TILE, n), lambda b, t: (b, t, 0))
    ins = [xall, slot_cols, aff, y]
    in_specs = [tok(D_MODEL), tok(LANES), tok(LANES),
                pl.BlockSpec((N_EXPERTS, CAP_LAT, D_MODEL), lambda b, t: (0, b, 0))]
    if with_ctx:
        ins.append(y)
        in_specs.append(pl.BlockSpec((N_EXPERTS, CAP_CTX, D_MODEL), lambda b, t: (0, SEQ // CAP_CTX + b, 0)))
    ins.append(mod_l)
    in_specs.append(pl.BlockSpec((1, 1, 6 * D_MODEL), lambda b, t: (_mod_row(b, t), 0, 0)))
    if final:
        ins.append(final_norm)
        in_specs.append(pl.BlockSpec((1, D_MODEL), lambda b, t: (0, 0)))
    return pl.pallas_call(
        functools.partial(_combine_kernel, with_ctx=with_ctx, final=final),
        out_shape=jax.ShapeDtypeStruct((BATCH, n_rows, D_MODEL), F32),
        grid=(BATCH, n_tiles),
        in_specs=in_specs,
        out_specs=tok(D_MODEL),
        compiler_params=_params(("arbitrary", "arbitrary")),
        name="expert_combine",
    )(*ins)


def _rope_tables():
    t = jnp.arange(SEQ)
    pos = jnp.stack([t // GRID_W, t % GRID_W], axis=0)

    def table(n_f, lanes_per_unit):
        inv = ROPE_BASE ** (-jnp.arange(n_f, dtype=F32) / n_f)
        d = np.arange(lanes_per_unit)
        axis, second, f = d // (2 * n_f), (d // n_f) % 2, d % n_f
        ang = pos[axis].T.astype(F32) * inv[f][None, :]
        cos = jnp.cos(ang)
        sin = jnp.sin(ang) * jnp.where(second == 1, 1.0, -1.0)[None, :]
        return cos, sin

    def finish(cos, sin, scale):
        cos = jnp.concatenate([cos, jnp.ones((CTX_LEN, LANES), F32)], axis=0)
        sin = jnp.concatenate([sin, jnp.zeros((CTX_LEN, LANES), F32)], axis=0)
        scale = scale * LOG2_E
        return [cos, sin, cos * scale, sin * scale]

    cos_a, sin_a = table(HEAD_DIM // 4, HEAD_DIM)
    tabs_a = finish(jnp.tile(cos_a, (1, 2)), jnp.tile(sin_a, (1, 2)), HEAD_DIM ** -0.5)
    cos_8, sin_8 = table(B_ROPE // 4, B_ROPE)
    tabs_8 = finish(jnp.tile(cos_8, (1, 4)), jnp.tile(sin_8, (1, 4)), C_DH ** -0.5)
    ones = jnp.ones((SEQ, B_NOPE), F32)
    pad1 = jnp.ones((SEQ, LANES - B_NOPE - B_ROPE), F32)
    cos_b = jnp.concatenate([ones, cos_8, pad1], axis=1)
    sin_b = jnp.concatenate([0 * ones, sin_8, 0 * pad1], axis=1)
    tabs_b = finish(cos_b, sin_b, (B_NOPE + B_ROPE) ** -0.5)
    return tabs_a + tabs_8 + tabs_b


def _window_bias():
    qi = np.arange(WIN)[:, None]
    kj = np.arange(3 * WIN)[None, :]
    band = (kj >= qi) & (kj <= qi + 2 * WIN)
    out = np.zeros((3, WIN, 3 * WIN + CTX_LEN), np.float32)
    for v, (has_prev, has_next) in enumerate([(False, True), (True, True), (True, False)]):
        ok = band & ((kj >= WIN) | has_prev) & ((kj < 2 * WIN) | has_next)
        out[v, :, :3 * WIN] = np.where(ok, 0.0, NEG_INF)
    return jnp.asarray(out)


def _prep_weights(w_in, w_uq, w_ukv, w_out, w_router):
    offs = np.cumsum((0, 512, 128, 128, 256, 128, 32, 256, 256, 256))
    seg = lambda i: w_in[:, :, offs[i]:offs[i + 1]]
    head_order = np.array([[j, A_GROUP + j] for j in range(A_GROUP)]).reshape(-1)
    a_cols = (head_order[:, None] * HEAD_DIM + np.arange(HEAD_DIM)[None, :]).reshape(-1)
    kr = jnp.pad(seg(5), ((0, 0), (0, 0), (B_NOPE, LANES - B_NOPE - B_ROPE)))
    w = jnp.concatenate([seg(0)[:, :, a_cols], seg(1), seg(2), seg(3), seg(4), kr,
                         seg(6), seg(7), seg(8)], axis=-1).astype(BF16)
    wuq = w_uq.reshape(DEPTH, B_Q_RANK, B_HEADS, B_NOPE + B_ROPE)
    wuq = jnp.pad(wuq, ((0, 0), (0, 0), (0, 0), (0, LANES - B_NOPE - B_ROPE)))
    wuq = wuq.reshape(DEPTH, B_Q_RANK, B_HEADS * LANES).astype(BF16)
    wukv = w_ukv.reshape(DEPTH, B_KV_RANK, B_HEADS, B_NOPE + B_V)
    wk = jnp.pad(wukv[..., :B_NOPE], ((0, 0), (0, 0), (0, 0), (0, LANES - B_NOPE)))
    wukv = jnp.concatenate([wk.reshape(DEPTH, B_KV_RANK, B_HEADS * LANES),
                            wukv[..., B_NOPE:].reshape(DEPTH, B_KV_RANK, B_HEADS * B_V)],
                           axis=-1).astype(BF16)
    wo = jnp.concatenate([w_out[:, :512][:, a_cols], w_out[:, 512:]], axis=1).astype(BF16)
    wr = jnp.pad(w_router, ((0, 0), (0, 0), (0, LANES - N_EXPERTS)))
    return w, wuq, wukv, wo, wr


def kernel(x, c, ctx, c_ctx, norm1, norm2, w_ada, b_ada, w_in, sink, mla_q_norm, w_uq, mla_kv_norm, w_ukv,
           lam_q1, lam_k1, lam_q2, lam_k2, diff_norm, w_out, w_router, w_gate, w_up, w_down, final_norm):
    xall = jnp.concatenate([x, ctx], axis=1)
    cc = jnp.concatenate([c, c_ctx[None, :], jnp.zeros((16 - BATCH - 1, D_MODEL), F32)], axis=0)
    mod = _modulation(cc, w_ada, b_ada).reshape(DEPTH, 16, 1, 6 * D_MODEL)
    tabs = _rope_tables()
    bias = _window_bias()
    tri = jnp.asarray(np.triu(np.ones((LANES, LANES), np.float32), k=1), BF16)
    w, wuq, wukv, wo, wr = _prep_weights(w_in, w_uq, w_ukv, w_out, w_router)
    sub_norm = jnp.tile(diff_norm, (1, C_HEADS)).reshape(DEPTH, 1, C_HEADS * C_V)
    row = lambda a, l: a[l].reshape(1, -1)

    for l in range(DEPTH):
        need_ctx = l < DEPTH - 1
        qa, ka, va, qb, kb, vb, qd, kd, vd = _inproj(
            xall, mod[l], row(norm1, l), w[l], row(mla_q_norm, l), wuq[l], row(mla_kv_norm, l), wukv[l], tabs)

        oa = _attn_a_latent(sink[l], qa, ka, va, bias)
        ob = _full_attention(_attn_b_kernel, "attn_mla", qb, kb, vb, [], lambda n: [], 256, 512, 512, 512)
        lam_init = 0.8 - 0.6 * math.exp(-0.3 * l)
        c_kernel = functools.partial(_attn_c_kernel, lam_init=lam_init)
        c_extra = [row(lam_q1, l), row(lam_k1, l), row(lam_q2, l), row(lam_k2, l), sub_norm[l]]

        def c_specs(n_grid, c_extra=c_extra):
            zmap = (lambda b, t: (0, 0)) if n_grid == 2 else (lambda b: (0, 0))
            return [pl.BlockSpec(a.shape, zmap) for a in c_extra]

        od = _full_attention(c_kernel, "attn_diff", qd, kd, vd, c_extra, c_specs, 256, 256, 256, 512)
        if need_ctx:
            oa = _attn_a_ctx(sink[l], qa, ka, va, oa)
            ob = _full_attention_ctx(_attn_b_kernel, "attn_mla", qb, kb, vb, [], lambda n: [],
                                     256, 512, 512, 512, ob)
            od = _full_attention_ctx(c_kernel, "attn_diff", qd, kd, vd, c_extra, c_specs,
                                     256, 256, 256, 512, od)

        n_tiles = N_TILES if need_ctx else N_LAT_TILES
        xall, h2, aff = _outproj(xall, oa, ob, od, wo[l], mod[l], row(norm2, l), wr[l], n_tiles)
        slot_rows, slot_cols = _route(aff, tri, need_ctx)
        gathered = _gather(slot_rows.reshape(BATCH * N_EXPERTS, 1, NT), h2, need_ctx)
        y = _ffn(gathered[0], gathered[1] if need_ctx else None, w_gate, w_up, w_down, l)
        xall = _combine(xall, slot_cols, aff, y, mod[l], need_ctx,
                        final_norm=None if need_ctx else final_norm.reshape(1, D_MODEL))
    return xall
```

```python
import functools
import math

import numpy as np
import jax
import jax.numpy as jnp
from jax import lax
from jax.experimental import pallas as pl
from jax.experimental.pallas import tpu as pltpu

D_MODEL = 1024
BATCH = 8
SEQ = 2048
DEPTH = 4
CTX_LEN = 256
NT = SEQ + CTX_LEN
GRID_W = 64
HEAD_DIM = 64
ROPE_BASE = 10000.0
EPS = 1e-6
NEG_INF = -1e30
LOG2_E = math.log2(math.e)

WIN = 128
A_HEADS = 8
A_KV_HEADS = 2
A_GROUP = A_HEADS // A_KV_HEADS
B_HEADS = 4
B_NOPE = 64
B_ROPE = 32
B_V = 64
B_Q_RANK = 256
B_KV_RANK = 128
C_HEADS = 4
C_DH = 32
C_V = 2 * C_DH
N_EXPERTS = 16
EC_FACTOR = 2
D_EXPERT = 512
CAP_LAT = EC_FACTOR * SEQ // N_EXPERTS
CAP_CTX = EC_FACTOR * CTX_LEN // N_EXPERTS

LANES = 128
TOK_TILE = 512
NT_PAD = SEQ + TOK_TILE
N_TILES = NT_PAD // TOK_TILE
N_LAT_TILES = SEQ // TOK_TILE
FFN_TILE = 256
FFN_LAT_TILES = SEQ // FFN_TILE
VMEM_LIMIT = 56 * 1024 * 1024

F32 = jnp.float32
BF16 = jnp.bfloat16

_NT_DIMS = (((1,), (1,)), ((), ()))


def _params(sem):
    return pltpu.CompilerParams(dimension_semantics=sem, vmem_limit_bytes=VMEM_LIMIT)


def _dot(a, b):
    return jnp.dot(a, b, preferred_element_type=F32)


def _dot_nt(a, b):
    return lax.dot_general(a, b, _NT_DIMS, preferred_element_type=F32)


def _split_bf16(v):
    hi = v.astype(BF16)
    lo = (v - hi.astype(F32)).astype(BF16)
    return hi, lo


def _dot3(a, b):
    a_hi, a_lo = _split_bf16(a)
    b_hi, b_lo = _split_bf16(b)
    return _dot(a_hi, b_hi) + _dot(a_lo, b_hi) + _dot(a_hi, b_lo)


def _indicator(cond):
    return jnp.where(cond, 1.0, 0.0).astype(BF16)


def _rms(v):
    return v * lax.rsqrt(jnp.mean(v * v, axis=-1, keepdims=True) + EPS)


def _softmax_pv(s, v_blk, head):
    m = jnp.max(s, axis=-1, keepdims=True)
    ol = _dot(jnp.exp2(s - m).astype(BF16), v_blk)
    ones_lane = 64 if head % 2 == 0 else 0
    return ol / ol[:, ones_lane:ones_lane + 1]


def _merge_head_pairs(per_head):
    lane = lax.broadcasted_iota(jnp.int32, (1, LANES), 1)
    blocks = [jnp.where(lane < 64, per_head[j], per_head[j + 1]) for j in range(0, len(per_head), 2)]
    return jnp.concatenate(blocks, axis=-1)


def _softmax_rows(s):
    m = jnp.max(s, axis=-1, keepdims=True)
    e = jnp.exp(s - m)
    return e / jnp.sum(e, axis=-1, keepdims=True)


def _mod_kernel(c_ref, w_ref, b_ref, o_ref):
    c = c_ref[...]
    s = c / (1.0 + jnp.exp(-c))
    o_ref[0] = _dot3(s, w_ref[0]) + b_ref[0]


def _modulation(cc, w_ada, b_ada):
    tn = 1536
    return pl.pallas_call(
        _mod_kernel,
        out_shape=jax.ShapeDtypeStruct((DEPTH, 16, 6 * D_MODEL), F32),
        grid=(DEPTH, 6 * D_MODEL // tn),
        in_specs=[
            pl.BlockSpec((16, D_MODEL), lambda l, j: (0, 0)),
            pl.BlockSpec((1, D_MODEL, tn), lambda l, j: (l, 0, j)),
            pl.BlockSpec((1, 1, tn), lambda l, j: (l, 0, j)),
        ],
        out_specs=pl.BlockSpec((1, 16, tn), lambda l, j: (l, 0, j)),
        compiler_params=_params(("arbitrary", "arbitrary")),
        name="adaln_mod",
    )(cc, w_ada, b_ada.reshape(DEPTH, 1, 6 * D_MODEL))


def _rope(p, cos, sin, half):
    lane = lax.broadcasted_iota(jnp.int32, (1, LANES), 1)
    first = (lane // half) % 2 == 0
    outs = []
    for j in range(p.shape[1] // LANES):
        blk = p[:, j * LANES:(j + 1) * LANES]
        partner = jnp.where(first, pltpu.roll(blk, LANES - half, 1), pltpu.roll(blk, half, 1))
        outs.append(blk * cos + partner * sin)
    return outs[0] if len(outs) == 1 else jnp.concatenate(outs, axis=-1)


def _per_head_with_ones(v):
    lane = lax.broadcasted_iota(jnp.int32, (1, LANES), 1)
    outs = []
    for j in range(v.shape[1] // LANES):
        blk = v[:, j * LANES:(j + 1) * LANES]
        outs.append(jnp.where(lane < 64, blk, jnp.where(lane == 64, 1.0, 0.0)))
        outs.append(jnp.where(lane >= 64, blk, jnp.where(lane == 0, 1.0, 0.0)))
    return jnp.concatenate(outs, axis=-1)


def _inproj_kernel(x_ref, mod_ref, g1_ref, w_ref, qn_ref, wuq_ref, kvn_ref, wukv_ref,
                   ca_ref, sa_ref, caq_ref, saq_ref, c8_ref, s8_ref, c8q_ref, s8q_ref,
                   cb_ref, sb_ref, cbq_ref, sbq_ref,
                   qa_ref, ka_ref, va_ref, qb_ref, kb_ref, vb_ref, qd_ref, kd_ref, vd_ref):
    x = x_ref[0]
    mod = mod_ref[0]
    sh1 = mod[:, 0:D_MODEL]
    sc1 = mod[:, D_MODEL:2 * D_MODEL]
    h = (_rms(x) * g1_ref[...]) * (1.0 + sc1) + sh1
    p = _dot(h.astype(BF16), w_ref[...])

    qa_ref[0] = _rope(p[:, 0:512], caq_ref[...], saq_ref[...], 16).astype(BF16)
    ka_ref[0] = _rope(p[:, 512:640], ca_ref[...], sa_ref[...], 16).astype(BF16)
    lane = lax.broadcasted_iota(jnp.int32, (x.shape[0], LANES), 1)
    va_ref[0] = jnp.concatenate([p[:, 640:768], jnp.where(lane == 0, 1.0, 0.0)], axis=-1).astype(BF16)

    cq = (_rms(p[:, 768:1024]) * qn_ref[...]).astype(BF16)
    qb = _dot(cq, wuq_ref[...])
    qb_ref[0] = _rope(qb, cbq_ref[...], sbq_ref[...], 8).astype(BF16)
    ckv = (_rms(p[:, 1024:1152]) * kvn_ref[...]).astype(BF16)
    kv = _dot(ckv, wukv_ref[...])
    kr = _rope(p[:, 1152:1280], cb_ref[...], sb_ref[...], 8)
    kb_ref[0] = (kv[:, 0:512] + jnp.concatenate([kr] * B_HEADS, axis=-1)).astype(BF16)
    vb_ref[0] = _per_head_with_ones(kv[:, 512:768]).astype(BF16)

    qd_ref[0] = _rope(p[:, 1280:1536], c8q_ref[...], s8q_ref[...], 8).astype(BF16)
    kd_ref[0] = _rope(p[:, 1536:1792], c8_ref[...], s8_ref[...], 8).astype(BF16)
    vd_ref[0] = _per_head_with_ones(p[:, 1792:2048]).astype(BF16)


def _mod_row(b, t):
    return jnp.where(t == N_LAT_TILES, BATCH, b)


def _inproj(xall, mod_l, g1, w, qn, wuq, kvn, wukv, tabs):
    full = lambda shape: pl.BlockSpec(shape, lambda b, t: (0,) * len(shape))
    tab_spec = pl.BlockSpec((TOK_TILE, LANES), lambda b, t: (t, 0))
    widths = (512, 128, 256, 512, 512, 512, 256, 256, 512)
    return pl.pallas_call(
        _inproj_kernel,
        out_shape=[jax.ShapeDtypeStruct((BATCH, NT_PAD, n), BF16) for n in widths],
        grid=(BATCH, N_TILES),
        in_specs=[
            pl.BlockSpec((1, TOK_TILE, D_MODEL), lambda b, t: (b, t, 0)),
            pl.BlockSpec((1, 1, 6 * D_MODEL), lambda b, t: (_mod_row(b, t), 0, 0)),
            full((1, D_MODEL)),
            full((D_MODEL, 2048)),
            full((1, B_Q_RANK)),
            full((B_Q_RANK, 512)),
            full((1, B_KV_RANK)),
            full((B_KV_RANK, 768)),
        ] + [tab_spec] * 12,
        out_specs=[pl.BlockSpec((1, TOK_TILE, n), lambda b, t: (b, t, 0)) for n in widths],
        compiler_params=_params(("arbitrary", "arbitrary")),
        name="in_projection",
    )(xall, mod_l, g1, w, qn, wuq, kvn, wukv, *tabs)


def _attn_a_kernel(*refs, n_kv_blocks, has_bias, tq):
    sink_ref, q_ref = refs[0], refs[1]
    k_refs = refs[2:2 + n_kv_blocks]
    v_refs = refs[2 + n_kv_blocks:2 + 2 * n_kv_blocks]
    pos = 2 + 2 * n_kv_blocks
    bias_ref = refs[pos] if has_bias else None
    o_ref = refs[-1]

    q = q_ref[0]
    kcat = jnp.concatenate([r[0] for r in k_refs], axis=0) if n_kv_blocks > 1 else k_refs[0][0]
    vcat = jnp.concatenate([r[0] for r in v_refs], axis=0) if n_kv_blocks > 1 else v_refs[0][0]
    lane = lax.broadcasted_iota(jnp.int32, (1, LANES), 1)
    acc = [jnp.zeros((tq, LANES), F32) for _ in range(A_GROUP)]
    for g in range(A_KV_HEADS):
        in_half = (lane < HEAD_DIM) if g == 0 else (lane >= HEAD_DIM)
        keep = _indicator(in_half)
        qs = jnp.concatenate([q[:, j * LANES:(j + 1) * LANES] * keep for j in range(A_GROUP)], axis=0)
        s = _dot_nt(qs, kcat)
        if has_bias:
            s = s + jnp.concatenate([bias_ref[0]] * A_GROUP, axis=0)
        sink = jnp.concatenate(
            [jnp.full((tq, 1), sink_ref[g * A_GROUP + j] * LOG2_E, F32) for j in range(A_GROUP)], axis=0)
        m = jnp.maximum(jnp.max(s, axis=-1, keepdims=True), sink)
        ol = _dot(jnp.exp2(s - m).astype(BF16), vcat)
        o = ol[:, 0:LANES] / (ol[:, LANES:LANES + 1] + jnp.exp2(sink - m))
        for j in range(A_GROUP):
            acc[j] = acc[j] + jnp.where(in_half, o[j * tq:(j + 1) * tq], 0.0)
    o_ref[0] = jnp.concatenate(acc, axis=-1).astype(BF16)


A_TQ = 2 * WIN


def _attn_a_latent(sink, qa, ka, va, bias):
    nq = SEQ // A_TQ
    last_blk = SEQ // WIN - 1
    ctx_blk = SEQ // CTX_LEN
    kv_maps = [lambda b, n: (b, jnp.maximum(2 * n - 1, 0), 0),
               lambda b, n: (b, 2 * n, 0),
               lambda b, n: (b, 2 * n + 1, 0),
               lambda b, n: (b, jnp.minimum(2 * n + 2, last_blk), 0)]

    def kv_specs(width):
        return ([pl.BlockSpec((1, WIN, width), m) for m in kv_maps]
                + [pl.BlockSpec((1, CTX_LEN, width), lambda b, n: (b, ctx_blk, 0))])

    bias_spec = pl.BlockSpec(
        (1, A_TQ, 4 * WIN + CTX_LEN),
        lambda b, n: (jnp.where(n == 0, 0, jnp.where(n == nq - 1, 2, 1)), 0, 0))
    return pl.pallas_call(
        functools.partial(_attn_a_kernel, n_kv_blocks=5, has_bias=True, tq=A_TQ),
        out_shape=jax.ShapeDtypeStruct((BATCH, NT_PAD, 512), BF16),
        grid=(BATCH, nq),
        in_specs=[pl.BlockSpec(memory_space=pltpu.SMEM),
                  pl.BlockSpec((1, A_TQ, 512), lambda b, n: (b, n, 0))]
                 + kv_specs(LANES) + kv_specs(2 * LANES) + [bias_spec],
        out_specs=pl.BlockSpec((1, A_TQ, 512), lambda b, n: (b, n, 0)),
        compiler_params=_params(("arbitrary", "arbitrary")),
        name="attn_window_latent",
    )(sink, qa, *([ka] * 5), *([va] * 5), bias)


def _attn_a_ctx(sink, qa, ka, va, prev_out):
    ctx_blk = SEQ // CTX_LEN
    spec = lambda n: pl.BlockSpec((1, CTX_LEN, n), lambda b: (b, ctx_blk, 0))
    tile = lambda n: pl.BlockSpec((1, TOK_TILE, n), lambda b: (b, N_LAT_TILES, 0))

    def body(sink_ref, q_ref, k_ref, v_ref, prev_ref, o_ref):
        del prev_ref
        _attn_a_kernel(sink_ref, q_ref, k_ref, v_ref, o_ref, n_kv_blocks=1, has_bias=False, tq=TOK_TILE)

    return pl.pallas_call(
        body,
        out_shape=jax.ShapeDtypeStruct((BATCH, NT_PAD, 512), BF16),
        grid=(BATCH,),
        in_specs=[pl.BlockSpec(memory_space=pltpu.SMEM), tile(512), spec(LANES), spec(2 * LANES),
                  pl.BlockSpec(memory_space=pl.ANY)],
        out_specs=tile(512),
        input_output_aliases={4: 0},
        compiler_params=_params(("arbitrary",)),
        name="attn_window_ctx",
    )(sink, qa, ka, va, prev_out)


def _key_rows(ref):
    return ref[0, 0:NT, :] if ref.shape[1] == NT_PAD else ref[0]


def _attn_b_kernel(q_ref, k_ref, v_ref, o_ref):
    q = q_ref[0]
    k = _key_rows(k_ref)
    v = _key_rows(v_ref)
    outs = []
    for h in range(B_HEADS):
        s = _dot_nt(q[:, h * LANES:(h + 1) * LANES], k[:, h * LANES:(h + 1) * LANES])
        outs.append(_softmax_pv(s, v[:, h * LANES:(h + 1) * LANES], h))
    o_ref[0] = _merge_head_pairs(outs).astype(BF16)


ATT_TQ = 512


def _full_attention(kernel, name, q, k, v, extra, extra_specs, out_width, q_width, k_width, v_width):
    out = pl.pallas_call(
        kernel,
        out_shape=jax.ShapeDtypeStruct((BATCH, NT_PAD, out_width), BF16),
        grid=(BATCH, SEQ // ATT_TQ),
        in_specs=[pl.BlockSpec((1, ATT_TQ, q_width), lambda b, t: (b, t, 0)),
                  pl.BlockSpec((1, NT_PAD, k_width), lambda b, t: (b, 0, 0)),
                  pl.BlockSpec((1, NT_PAD, v_width), lambda b, t: (b, 0, 0))] + extra_specs(2),
        out_specs=pl.BlockSpec((1, ATT_TQ, out_width), lambda b, t: (b, t, 0)),
        compiler_params=_params(("arbitrary", "arbitrary")),
        name=name + "_latent",
    )(q, k, v, *extra)
    return out


def _full_attention_ctx(kernel, name, q, k, v, extra, extra_specs, out_width, q_width, k_width, v_width,
                        prev_out):
    ctx_blk = SEQ // CTX_LEN
    n_extra = len(extra)

    def body(*refs):
        ins = refs[:3 + n_extra]
        kernel(*ins, refs[-1])

    spec = lambda n: pl.BlockSpec((1, CTX_LEN, n), lambda b: (b, ctx_blk, 0))
    tile = lambda n: pl.BlockSpec((1, TOK_TILE, n), lambda b: (b, N_LAT_TILES, 0))
    return pl.pallas_call(
        body,
        out_shape=jax.ShapeDtypeStruct((BATCH, NT_PAD, out_width), BF16),
        grid=(BATCH,),
        in_specs=[tile(q_width), spec(k_width), spec(v_width)] + extra_specs(1)
                 + [pl.BlockSpec(memory_space=pl.ANY)],
        out_specs=tile(out_width),
        input_output_aliases={3 + n_extra: 0},
        compiler_params=_params(("arbitrary",)),
        name=name + "_ctx",
    )(q, k, v, *extra, prev_out)


def _attn_c_kernel(q_ref, k_ref, v_ref, lq1_ref, lk1_ref, lq2_ref, lk2_ref, sn_ref, o_ref, *, lam_init):
    q = q_ref[0]
    k = _key_rows(k_ref)
    v = _key_rows(v_ref)
    tq = q.shape[0]
    lam = (jnp.exp(jnp.sum(lq1_ref[...] * lk1_ref[...], axis=-1, keepdims=True))
           - jnp.exp(jnp.sum(lq2_ref[...] * lk2_ref[...], axis=-1, keepdims=True)) + lam_init)
    lane = lax.broadcasted_iota(jnp.int32, (1, C_HEADS * C_V), 1)
    outs = []
    for h in range(C_HEADS):
        lo = h * C_V
        q1 = q * _indicator((lane >= lo) & (lane < lo + C_DH))
        q2 = q * _indicator((lane >= lo + C_DH) & (lane < lo + C_V))
        s = _dot_nt(jnp.concatenate([q1, q2], axis=0), k)
        pv = _softmax_pv(s, v[:, h * LANES:(h + 1) * LANES], h)
        outs.append(pv[0:tq] - lam * pv[tq:2 * tq])
    acc = _merge_head_pairs(outs)
    sq = acc * acc
    inv = jnp.zeros((tq, C_HEADS * C_V), F32)
    for h in range(C_HEADS):
        in_head = (lane >= h * C_V) & (lane < (h + 1) * C_V)
        ms = jnp.sum(jnp.where(in_head, sq, 0.0), axis=-1, keepdims=True) * (1.0 / C_V)
        inv = inv + jnp.where(in_head, lax.rsqrt(ms + EPS), 0.0)
    o_ref[0] = ((acc * inv) * sn_ref[...] * (1.0 - lam_init)).astype(BF16)


def _outproj_kernel(x_ref, oa_ref, ob_ref, od_ref, w_ref, mod_ref, g2_ref, wr_ref,
                    xo_ref, h2_ref, aff_ref):
    mod = mod_ref[0]
    g1 = mod[:, 2 * D_MODEL:3 * D_MODEL]
    sh2 = mod[:, 3 * D_MODEL:4 * D_MODEL]
    sc2 = mod[:, 4 * D_MODEL:5 * D_MODEL]
    mix = jnp.concatenate([oa_ref[0], ob_ref[0], od_ref[0]], axis=-1)
    x = x_ref[0] + g1 * _dot(mix, w_ref[...])
    xo_ref[0] = x
    h2 = (_rms(x) * g2_ref[...]) * (1.0 + sc2) + sh2
    h2_ref[0] = h2.astype(BF16)
    logits = _dot3(h2, wr_ref[...])
    lane = lax.broadcasted_iota(jnp.int32, (1, LANES), 1)
    logits = jnp.where(lane < N_EXPERTS, logits, NEG_INF)
    aff_ref[0] = _softmax_rows(logits)


def _outproj(xall, oa, ob, od, w, mod_l, g2, wr, n_tiles):
    full = lambda shape: pl.BlockSpec(shape, lambda b, t: (0,) * len(shape))
    tok = lambda n: pl.BlockSpec((1, TOK_TILE, n), lambda b, t: (b, t, 0))
    return pl.pallas_call(
        _outproj_kernel,
        out_shape=[jax.ShapeDtypeStruct((BATCH, NT_PAD, D_MODEL), F32),
                   jax.ShapeDtypeStruct((BATCH, NT_PAD, D_MODEL), BF16),
                   jax.ShapeDtypeStruct((BATCH, NT_PAD, LANES), F32)],
        grid=(BATCH, n_tiles),
        in_specs=[tok(D_MODEL), tok(512), tok(256), tok(256), full((D_MODEL, D_MODEL)),
                  pl.BlockSpec((1, 1, 6 * D_MODEL), lambda b, t: (_mod_row(b, t), 0, 0)),
                  full((1, D_MODEL)), full((D_MODEL, LANES))],
        out_specs=[tok(D_MODEL), tok(D_MODEL), tok(LANES)],
        compiler_params=_params(("arbitrary", "arbitrary")),
        name="out_projection",
    )(xall, oa, ob, od, w, mod_l, g2, wr)


def _prefix_count(flags_f32, tri_ref):
    rows, n = flags_f32.shape
    run = jnp.zeros((rows, 1), F32)
    outs = []
    for c in range(n // LANES):
        blk = flags_f32[:, c * LANES:(c + 1) * LANES]
        outs.append(_dot(blk.astype(BF16), tri_ref[...]) + run)
        run = run + jnp.sum(blk, axis=-1, keepdims=True)
    return jnp.concatenate(outs, axis=-1)


def _select_slots(a, cap, tri_ref):
    bits = pltpu.bitcast(a, jnp.int32)

    def step(i, t):
        cand = t | jnp.left_shift(jnp.int32(1), 30 - i)
        cnt = jnp.sum((bits >= cand).astype(jnp.int32), axis=-1, keepdims=True)
        return jnp.where(cnt >= cap, cand, t)

    thr = lax.fori_loop(0, 31, step, jnp.zeros((a.shape[0], 1), jnp.int32))
    above = bits > thr
    tied = bits == thr
    need = cap - jnp.sum(above.astype(F32), axis=-1, keepdims=True)
    tied_f = tied.astype(F32)
    sel = above | (tied & (_prefix_count(tied_f, tri_ref) < need))
    sel_f = sel.astype(F32)
    return jnp.where(sel, _prefix_count(sel_f, tri_ref), -1.0)


def _route_kernel(aff_ref, tri_ref, row_ref, col_ref, *, with_ctx):
    lat, ctx = [], []
    for b in range(BATCH):
        lat.append(aff_ref[b, 0:SEQ, :].T[0:N_EXPERTS])
        if with_ctx:
            ctx.append(aff_ref[b, SEQ:NT, :].T[0:N_EXPERTS])
    slot_lat = _select_slots(jnp.concatenate(lat, axis=0), CAP_LAT, tri_ref)
    if with_ctx:
        slot_ctx = _select_slots(jnp.concatenate(ctx, axis=0), CAP_CTX, tri_ref)
        slots = jnp.concatenate([slot_lat, slot_ctx], axis=-1)
    else:
        slots = slot_lat
    n = slots.shape[1]
    row_ref[:, 0:n] = slots
    pad = jnp.full((LANES - N_EXPERTS, n), -1.0, F32)
    for b in range(BATCH):
        blk = jnp.concatenate([slots[b * N_EXPERTS:(b + 1) * N_EXPERTS], pad], axis=0)
        col_ref[b, 0:n, :] = blk.T
        if with_ctx:
            col_ref[b, NT:NT_PAD, :] = jnp.full((NT_PAD - NT, LANES), -1.0, F32)


def _route(aff, tri, with_ctx):
    vm = pl.BlockSpec(memory_space=pltpu.VMEM)
    return pl.pallas_call(
        functools.partial(_route_kernel, with_ctx=with_ctx),
        out_shape=[jax.ShapeDtypeStruct((BATCH * N_EXPERTS, NT), F32),
                   jax.ShapeDtypeStruct((BATCH, NT_PAD, LANES), F32)],
        in_specs=[vm, vm],
        out_specs=[vm, vm],
        compiler_params=pltpu.CompilerParams(vmem_limit_bytes=VMEM_LIMIT),
        name="route_topc",
    )(aff, tri)


def _gather_kernel(row_ref, h_ref, *out_refs, with_ctx):
    row_lat = row_ref[0, :, 0:SEQ]
    slot = lax.broadcasted_iota(jnp.int32, (CAP_LAT, 1), 0).astype(F32)
    p = _indicator(row_lat == slot)
    out_refs[0][0] = _dot(p, h_ref[0, 0:SEQ, :]).astype(BF16)
    if with_ctx:
        pc = _indicator(row_ref[0, :, SEQ:NT] == slot[0:CAP_CTX])
        out_refs[1][0] = _dot(pc, h_ref[0, SEQ:NT, :]).astype(BF16)


def _gather(slot_rows, h2, with_ctx):
    out_shape = [jax.ShapeDtypeStruct((N_EXPERTS, SEQ, D_MODEL), BF16)]
    out_specs = [pl.BlockSpec((1, CAP_LAT, D_MODEL), lambda b, e: (e, b, 0))]
    if with_ctx:
        out_shape.append(jax.ShapeDtypeStruct((N_EXPERTS, CTX_LEN, D_MODEL), BF16))
        out_specs.append(pl.BlockSpec((1, CAP_CTX, D_MODEL), lambda b, e: (e, b, 0)))
    return pl.pallas_call(
        functools.partial(_gather_kernel, with_ctx=with_ctx),
        out_shape=out_shape,
        grid=(BATCH, N_EXPERTS),
        in_specs=[pl.BlockSpec((1, 1, NT), lambda b, e: (b * N_EXPERTS + e, 0, 0)),
                  pl.BlockSpec((1, NT_PAD, D_MODEL), lambda b, e: (b, 0, 0))],
        out_specs=out_specs,
        compiler_params=_params(("arbitrary", "arbitrary")),
        name="expert_gather",
    )(slot_rows, h2)


def _ffn_kernel(*refs, with_ctx):
    if with_ctx:
        xl_ref, xc_ref, wg_ref, wu_ref, wd_ref, y_ref, wg_s, wu_s, wd_s = refs
    else:
        xl_ref, wg_ref, wu_ref, wd_ref, y_ref, wg_s, wu_s, wd_s = refs
    r = pl.program_id(1)

    @pl.when(r == 0)
    def _():
        wg_s[...] = wg_ref[0, 0].astype(BF16)
        wu_s[...] = wu_ref[0, 0].astype(BF16)
        wd_s[...] = wd_ref[0, 0].astype(BF16)

    x = xl_ref[0]
    if with_ctx:
        x = jnp.where(r < FFN_LAT_TILES, x, xc_ref[0])
    gate = _dot(x, wg_s[...])
    up = _dot(x, wu_s[...])
    hid = (gate / (1.0 + jnp.exp(-gate))) * up
    y_ref[0] = _dot(hid.astype(BF16), wd_s[...]).astype(BF16)


def _ffn(xg_lat, xg_ctx, w_gate, w_up, w_down, layer):
    with_ctx = xg_ctx is not None
    n_tiles = FFN_LAT_TILES + 1 if with_ctx else FFN_LAT_TILES
    last = FFN_LAT_TILES - 1
    ins = [xg_lat] + ([xg_ctx] if with_ctx else []) + [w_gate, w_up, w_down]
    in_specs = [pl.BlockSpec((1, FFN_TILE, D_MODEL), lambda e, r: (e, jnp.minimum(r, last), 0))]
    if with_ctx:
        in_specs.append(pl.BlockSpec((1, CTX_LEN, D_MODEL), lambda e, r: (e, 0, 0)))
    in_specs += [pl.BlockSpec((1, 1, D_MODEL, D_EXPERT), lambda e, r: (layer, e, 0, 0)),
                 pl.BlockSpec((1, 1, D_MODEL, D_EXPERT), lambda e, r: (layer, e, 0, 0)),
                 pl.BlockSpec((1, 1, D_EXPERT, D_MODEL), lambda e, r: (layer, e, 0, 0))]
    return pl.pallas_call(
        functools.partial(_ffn_kernel, with_ctx=with_ctx),
        out_shape=jax.ShapeDtypeStruct((N_EXPERTS, NT, D_MODEL), BF16),
        grid=(N_EXPERTS, n_tiles),
        in_specs=in_specs,
        out_specs=pl.BlockSpec((1, FFN_TILE, D_MODEL), lambda e, r: (e, r, 0)),
        scratch_shapes=[pltpu.VMEM((D_MODEL, D_EXPERT), BF16),
                        pltpu.VMEM((D_MODEL, D_EXPERT), BF16),
                        pltpu.VMEM((D_EXPERT, D_MODEL), BF16)],
        compiler_params=_params(("arbitrary", "arbitrary")),
        name="expert_ffn",
    )(*ins)


def _combine_kernel(*refs, with_ctx, final):
    if with_ctx:
        x_ref, col_ref, aff_ref, yl_ref, yc_ref, mod_ref, o_ref = refs
    elif final:
        x_ref, col_ref, aff_ref, yl_ref, mod_ref, fn_ref, o_ref = refs
    else:
        x_ref, col_ref, aff_ref, yl_ref, mod_ref, o_ref = refs
    g2 = mod_ref[0][:, 5 * D_MODEL:6 * D_MODEL]
    col = col_ref[0]
    aff = aff_ref[0]

    def finish(acc):
        x = x_ref[0] + g2 * acc
        if final:
            x = _rms(x) * fn_ref[...]
        o_ref[0] = x

    def latent():
        lane = lax.broadcasted_iota(jnp.int32, (1, CAP_LAT), 1).astype(F32)
        acc = jnp.zeros((TOK_TILE, D_MODEL), F32)
        for e in range(N_EXPERTS):
            pt = _indicator(col[:, e:e + 1] == lane)
            acc = acc + aff[:, e:e + 1] * _dot(pt, yl_ref[e])
        finish(acc)

    def context():
        lane = lax.broadcasted_iota(jnp.int32, (1, LANES), 1).astype(F32)
        pad = jnp.zeros((LANES - CAP_CTX, D_MODEL), BF16)
        acc = jnp.zeros((TOK_TILE, D_MODEL), F32)
        for e in range(N_EXPERTS):
            pt = _indicator(col[:, e:e + 1] == lane)
            acc = acc + aff[:, e:e + 1] * _dot(pt, jnp.concatenate([yc_ref[e], pad], axis=0))
        finish(acc)

    if with_ctx:
        t = pl.program_id(1)
        pl.when(t < N_LAT_TILES)(latent)
        pl.when(t == N_LAT_TILES)(context)
    else:
        latent()


def _combine(xall, slot_cols, aff, y, mod_l, with_ctx, final_norm=None):
    final = final_norm is not None
    n_tiles = N_TILES if with_ctx else N_LAT_TILES
    n_rows = SEQ if final else NT_PAD
    tok = lambda n: pl.BlockSpec((1, TOK_TILE, n), lambda b, t: (b, t, 0))
    ins = [xall, slot_cols, aff, y]
    in_specs = [tok(D_MODEL), tok(LANES), tok(LANES),
                pl.BlockSpec((N_EXPERTS, CAP_LAT, D_MODEL), lambda b, t: (0, b, 0))]
    if with_ctx:
        ins.append(y)
        in_specs.append(pl.BlockSpec((N_EXPERTS, CAP_CTX, D_MODEL), lambda b, t: (0, SEQ // CAP_CTX + b, 0)))
    ins.append(mod_l)
    in_specs.append(pl.BlockSpec((1, 1, 6 * D_MODEL), lambda b, t: (_mod_row(b, t), 0, 0)))
    if final:
        ins.append(final_norm)
        in_specs.append(pl.BlockSpec((1, D_MODEL), lambda b, t: (0, 0)))
    return pl.pallas_call(
        functools.partial(_combine_kernel, with_ctx=with_ctx, final=final),
        out_shape=jax.ShapeDtypeStruct((BATCH, n_rows, D_MODEL), F32),
        grid=(BATCH, n_tiles),
        in_specs=in_specs,
        out_specs=tok(D_MODEL),
        compiler_params=_params(("arbitrary", "arbitrary")),
        name="expert_combine",
    )(*ins)


def _rope_tables():
    t = jnp.arange(SEQ)
    pos = jnp.stack([t // GRID_W, t % GRID_W], axis=0)

    def table(n_f, lanes_per_unit):
        inv = ROPE_BASE ** (-jnp.arange(n_f, dtype=F32) / n_f)
        d = np.arange(lanes_per_unit)
        axis, second, f = d // (2 * n_f), (d // n_f) % 2, d % n_f
        ang = pos[axis].T.astype(F32) * inv[f][None, :]
        cos = jnp.cos(ang)
        sin = jnp.sin(ang) * jnp.where(second == 1, 1.0, -1.0)[None, :]
        return cos, sin

    def finish(cos, sin, scale):
        cos = jnp.concatenate([cos, jnp.ones((NT_PAD - SEQ, LANES), F32)], axis=0)
        sin = jnp.concatenate([sin, jnp.zeros((NT_PAD - SEQ, LANES), F32)], axis=0)
        scale = scale * LOG2_E
        return [cos, sin, cos * scale, sin * scale]

    cos_a, sin_a = table(HEAD_DIM // 4, HEAD_DIM)
    tabs_a = finish(jnp.tile(cos_a, (1, 2)), jnp.tile(sin_a, (1, 2)), HEAD_DIM ** -0.5)
    cos_8, sin_8 = table(B_ROPE // 4, B_ROPE)
    tabs_8 = finish(jnp.tile(cos_8, (1, 4)), jnp.tile(sin_8, (1, 4)), C_DH ** -0.5)
    ones = jnp.ones((SEQ, B_NOPE), F32)
    pad1 = jnp.ones((SEQ, LANES - B_NOPE - B_ROPE), F32)
    cos_b = jnp.concatenate([ones, cos_8, pad1], axis=1)
    sin_b = jnp.concatenate([0 * ones, sin_8, 0 * pad1], axis=1)
    tabs_b = finish(cos_b, sin_b, (B_NOPE + B_ROPE) ** -0.5)
    return tabs_a + tabs_8 + tabs_b


def _window_bias():
    qi = np.arange(A_TQ)[:, None]
    kj = np.arange(4 * WIN)[None, :]
    band = (kj >= qi) & (kj <= qi + 2 * WIN)
    out = np.zeros((3, A_TQ, 4 * WIN + CTX_LEN), np.float32)
    for v, (has_prev, has_next) in enumerate([(False, True), (True, True), (True, False)]):
        ok = band & ((kj >= WIN) | has_prev) & ((kj < 3 * WIN) | has_next)
        out[v, :, :4 * WIN] = np.where(ok, 0.0, NEG_INF)
    return jnp.asarray(out)


def _prep_weights(w_in, w_uq, w_ukv, w_out, w_router):
    offs = np.cumsum((0, 512, 128, 128, 256, 128, 32, 256, 256, 256))
    seg = lambda i: w_in[:, :, offs[i]:offs[i + 1]]
    head_order = np.array([[j, A_GROUP + j] for j in range(A_GROUP)]).reshape(-1)
    a_cols = (head_order[:, None] * HEAD_DIM + np.arange(HEAD_DIM)[None, :]).reshape(-1)
    kr = jnp.pad(seg(5), ((0, 0), (0, 0), (B_NOPE, LANES - B_NOPE - B_ROPE)))
    w = jnp.concatenate([seg(0)[:, :, a_cols], seg(1), seg(2), seg(3), seg(4), kr,
                         seg(6), seg(7), seg(8)], axis=-1).astype(BF16)
    wuq = w_uq.reshape(DEPTH, B_Q_RANK, B_HEADS, B_NOPE + B_ROPE)
    wuq = jnp.pad(wuq, ((0, 0), (0, 0), (0, 0), (0, LANES - B_NOPE - B_ROPE)))
    wuq = wuq.reshape(DEPTH, B_Q_RANK, B_HEADS * LANES).astype(BF16)
    wukv = w_ukv.reshape(DEPTH, B_KV_RANK, B_HEADS, B_NOPE + B_V)
    wk = jnp.pad(wukv[..., :B_NOPE], ((0, 0), (0, 0), (0, 0), (0, LANES - B_NOPE)))
    wukv = jnp.concatenate([wk.reshape(DEPTH, B_KV_RANK, B_HEADS * LANES),
                            wukv[..., B_NOPE:].reshape(DEPTH, B_KV_RANK, B_HEADS * B_V)],
                           axis=-1).astype(BF16)
    wo = jnp.concatenate([w_out[:, :512][:, a_cols], w_out[:, 512:]], axis=1).astype(BF16)
    wr = jnp.pad(w_router, ((0, 0), (0, 0), (0, LANES - N_EXPERTS)))
    return w, wuq, wukv, wo, wr


def kernel(x, c, ctx, c_ctx, norm1, norm2, w_ada, b_ada, w_in, sink, mla_q_norm, w_uq, mla_kv_norm, w_ukv,
           lam_q1, lam_k1, lam_q2, lam_k2, diff_norm, w_out, w_router, w_gate, w_up, w_down, final_norm):
    xall = jnp.concatenate([x, ctx, jnp.zeros((BATCH, NT_PAD - NT, D_MODEL), F32)], axis=1)
    cc = jnp.concatenate([c, c_ctx[None, :], jnp.zeros((16 - BATCH - 1, D_MODEL), F32)], axis=0)
    mod = _modulation(cc, w_ada, b_ada).reshape(DEPTH, 16, 1, 6 * D_MODEL)
    tabs = _rope_tables()
    bias = _window_bias()
    tri = jnp.asarray(np.triu(np.ones((LANES, LANES), np.float32), k=1), BF16)
    w, wuq, wukv, wo, wr = _prep_weights(w_in, w_uq, w_ukv, w_out, w_router)
    sub_norm = jnp.tile(diff_norm, (1, C_HEADS)).reshape(DEPTH, 1, C_HEADS * C_V)
    row = lambda a, l: a[l].reshape(1, -1)

    for l in range(DEPTH):
        need_ctx = l < DEPTH - 1
        qa, ka, va, qb, kb, vb, qd, kd, vd = _inproj(
            xall, mod[l], row(norm1, l), w[l], row(mla_q_norm, l), wuq[l], row(mla_kv_norm, l), wukv[l], tabs)

        oa = _attn_a_latent(sink[l], qa, ka, va, bias)
        ob = _full_attention(_attn_b_kernel, "attn_mla", qb, kb, vb, [], lambda n: [], 256, 512, 512, 512)
        lam_init = 0.8 - 0.6 * math.exp(-0.3 * l)
        c_kernel = functools.partial(_attn_c_kernel, lam_init=lam_init)
        c_extra = [row(lam_q1, l), row(lam_k1, l), row(lam_q2, l), row(lam_k2, l), sub_norm[l]]

        def c_specs(n_grid, c_extra=c_extra):
            zmap = (lambda b, t: (0, 0)) if n_grid == 2 else (lambda b: (0, 0))
            return [pl.BlockSpec(a.shape, zmap) for a in c_extra]

        od = _full_attention(c_kernel, "attn_diff", qd, kd, vd, c_extra, c_specs, 256, 256, 256, 512)
        if need_ctx:
            oa = _attn_a_ctx(sink[l], qa, ka, va, oa)
            ob = _full_attention_ctx(_attn_b_kernel, "attn_mla", qb, kb, vb, [], lambda n: [],
                                     256, 512, 512, 512, ob)
            od = _full_attention_ctx(c_kernel, "attn_diff", qd, kd, vd, c_extra, c_specs,
                                     256, 256, 256, 512, od)

        n_tiles = N_TILES if need_ctx else N_LAT_TILES
        xall, h2, aff = _outproj(xall, oa, ob, od, wo[l], mod[l], row(norm2, l), wr[l], n_tiles)
        slot_rows, slot_cols = _route(aff, tri, need_ctx)
        gathered = _gather(slot_rows.reshape(BATCH * N_EXPERTS, 1, NT), h2, need_ctx)
        y = _ffn(gathered[0], gathered[1] if need_ctx else None, w_gate, w_up, w_down, l)
        xall = _combine(xall, slot_cols, aff, y, mod[l], need_ctx,
                        final_norm=None if need_ctx else final_norm.reshape(1, D_MODEL))
    return xall
```

```python
import functools
import math

import numpy as np
import jax
import jax.numpy as jnp
from jax import lax
from jax.experimental import pallas as pl
from jax.experimental.pallas import tpu as pltpu

D_MODEL = 1024
BATCH = 8
SEQ = 2048
DEPTH = 4
CTX_LEN = 256
NT = SEQ + CTX_LEN
GRID_W = 64
HEAD_DIM = 64
ROPE_BASE = 10000.0
EPS = 1e-6
NEG_INF = -1e30
LOG2_E = math.log2(math.e)

WIN = 128
A_HEADS = 8
A_KV_HEADS = 2
A_GROUP = A_HEADS // A_KV_HEADS
B_HEADS = 4
B_NOPE = 64
B_ROPE = 32
B_V = 64
B_Q_RANK = 256
B_KV_RANK = 128
C_HEADS = 4
C_DH = 32
C_V = 2 * C_DH
N_EXPERTS = 16
EC_FACTOR = 2
D_EXPERT = 512
CAP_LAT = EC_FACTOR * SEQ // N_EXPERTS
CAP_CTX = EC_FACTOR * CTX_LEN // N_EXPERTS

LANES = 128
TOK_TILE = 256
NT_PAD = SEQ + TOK_TILE
N_TILES = NT_PAD // TOK_TILE
N_LAT_TILES = SEQ // TOK_TILE
FFN_TILE = 512
FFN_LAT_TILES = SEQ // FFN_TILE
VMEM_LIMIT = 56 * 1024 * 1024

F32 = jnp.float32
BF16 = jnp.bfloat16

_NT_DIMS = (((1,), (1,)), ((), ()))


def _params(sem):
    return pltpu.CompilerParams(dimension_semantics=sem, vmem_limit_bytes=VMEM_LIMIT)


def _dot(a, b):
    return jnp.dot(a, b, preferred_element_type=F32)


def _dot_nt(a, b):
    return lax.dot_general(a, b, _NT_DIMS, preferred_element_type=F32)


def _split_bf16(v):
    hi = v.astype(BF16)
    lo = (v - hi.astype(F32)).astype(BF16)
    return hi, lo


def _dot3(a, b):
    a_hi, a_lo = _split_bf16(a)
    b_hi, b_lo = _split_bf16(b)
    return _dot(a_hi, b_hi) + _dot(a_lo, b_hi) + _dot(a_hi, b_lo)


def _indicator(cond):
    return jnp.where(cond, 1.0, 0.0).astype(BF16)


def _rms(v):
    return v * lax.rsqrt(jnp.mean(v * v, axis=-1, keepdims=True) + EPS)


def _softmax_pv(s, v_blk, head):
    m = jnp.max(s, axis=-1, keepdims=True)
    ol = _dot(jnp.exp2(s - m).astype(BF16), v_blk)
    ones_lane = 64 if head % 2 == 0 else 0
    return ol / ol[:, ones_lane:ones_lane + 1]


def _merge_head_pairs(per_head):
    lane = lax.broadcasted_iota(jnp.int32, (1, LANES), 1)
    blocks = [jnp.where(lane < 64, per_head[j], per_head[j + 1]) for j in range(0, len(per_head), 2)]
    return jnp.concatenate(blocks, axis=-1)


def _softmax_rows(s):
    m = jnp.max(s, axis=-1, keepdims=True)
    e = jnp.exp(s - m)
    return e / jnp.sum(e, axis=-1, keepdims=True)


def _mod_kernel(c_ref, w_ref, b_ref, o_ref):
    c = c_ref[...]
    s = c / (1.0 + jnp.exp(-c))
    o_ref[0] = _dot3(s, w_ref[0]) + b_ref[0]


def _modulation(cc, w_ada, b_ada):
    tn = 1536
    return pl.pallas_call(
        _mod_kernel,
        out_shape=jax.ShapeDtypeStruct((DEPTH, 16, 6 * D_MODEL), F32),
        grid=(DEPTH, 6 * D_MODEL // tn),
        in_specs=[
            pl.BlockSpec((16, D_MODEL), lambda l, j: (0, 0)),
            pl.BlockSpec((1, D_MODEL, tn), lambda l, j: (l, 0, j)),
            pl.BlockSpec((1, 1, tn), lambda l, j: (l, 0, j)),
        ],
        out_specs=pl.BlockSpec((1, 16, tn), lambda l, j: (l, 0, j)),
        compiler_params=_params(("arbitrary", "arbitrary")),
        name="adaln_mod",
    )(cc, w_ada, b_ada.reshape(DEPTH, 1, 6 * D_MODEL))


def _rope(p, cos, sin, half):
    lane = lax.broadcasted_iota(jnp.int32, (1, LANES), 1)
    first = (lane // half) % 2 == 0
    outs = []
    for j in range(p.shape[1] // LANES):
        blk = p[:, j * LANES:(j + 1) * LANES]
        partner = jnp.where(first, pltpu.roll(blk, LANES - half, 1), pltpu.roll(blk, half, 1))
        outs.append(blk * cos + partner * sin)
    return outs[0] if len(outs) == 1 else jnp.concatenate(outs, axis=-1)


def _per_head_with_ones(v):
    lane = lax.broadcasted_iota(jnp.int32, (1, LANES), 1)
    outs = []
    for j in range(v.shape[1] // LANES):
        blk = v[:, j * LANES:(j + 1) * LANES]
        outs.append(jnp.where(lane < 64, blk, jnp.where(lane == 64, 1.0, 0.0)))
        outs.append(jnp.where(lane >= 64, blk, jnp.where(lane == 0, 1.0, 0.0)))
    return jnp.concatenate(outs, axis=-1)


def _inproj_kernel(x_ref, mod_ref, g1_ref, w_ref, qn_ref, wuq_ref, kvn_ref, wukv_ref,
                   ca_ref, sa_ref, caq_ref, saq_ref, c8_ref, s8_ref, c8q_ref, s8q_ref,
                   cb_ref, sb_ref, cbq_ref, sbq_ref,
                   qa_ref, ka_ref, va_ref, qb_ref, kb_ref, vb_ref, qd_ref, kd_ref, vd_ref):
    x = x_ref[0]
    mod = mod_ref[0]
    sh1 = mod[:, 0:D_MODEL]
    sc1 = mod[:, D_MODEL:2 * D_MODEL]
    h = (_rms(x) * g1_ref[...]) * (1.0 + sc1) + sh1
    p = _dot(h.astype(BF16), w_ref[...])

    qa_ref[0] = _rope(p[:, 0:512], caq_ref[...], saq_ref[...], 16).astype(BF16)
    ka_ref[0] = _rope(p[:, 512:640], ca_ref[...], sa_ref[...], 16).astype(BF16)
    lane = lax.broadcasted_iota(jnp.int32, (x.shape[0], LANES), 1)
    va_ref[0] = jnp.concatenate([p[:, 640:768], jnp.where(lane == 0, 1.0, 0.0)], axis=-1).astype(BF16)

    cq = (_rms(p[:, 768:1024]) * qn_ref[...]).astype(BF16)
    qb = _dot(cq, wuq_ref[...])
    qb_ref[0] = _rope(qb, cbq_ref[...], sbq_ref[...], 8).astype(BF16)
    ckv = (_rms(p[:, 1024:1152]) * kvn_ref[...]).astype(BF16)
    kv = _dot(ckv, wukv_ref[...])
    kr = _rope(p[:, 1152:1280], cb_ref[...], sb_ref[...], 8)
    kb_ref[0] = (kv[:, 0:512] + jnp.concatenate([kr] * B_HEADS, axis=-1)).astype(BF16)
    vb_ref[0] = _per_head_with_ones(kv[:, 512:768]).astype(BF16)

    qd_ref[0] = _rope(p[:, 1280:1536], c8q_ref[...], s8q_ref[...], 8).astype(BF16)
    kd_ref[0] = _rope(p[:, 1536:1792], c8_ref[...], s8_ref[...], 8).astype(BF16)
    vd_ref[0] = _per_head_with_ones(p[:, 1792:2048]).astype(BF16)


def _mod_row(b, t):
    return jnp.where(t == N_LAT_TILES, BATCH, b)


def _inproj(xall, mod_l, g1, w, qn, wuq, kvn, wukv, tabs):
    full = lambda shape: pl.BlockSpec(shape, lambda b, t: (0,) * len(shape))
    tab_spec = pl.BlockSpec((TOK_TILE, LANES), lambda b, t: (t, 0))
    widths = (512, 128, 256, 512, 512, 512, 256, 256, 512)
    return pl.pallas_call(
        _inproj_kernel,
        out_shape=[jax.ShapeDtypeStruct((BATCH, NT_PAD, n), BF16) for n in widths],
        grid=(BATCH, N_TILES),
        in_specs=[
            pl.BlockSpec((1, TOK_TILE, D_MODEL), lambda b, t: (b, t, 0)),
            pl.BlockSpec((1, 1, 6 * D_MODEL), lambda b, t: (_mod_row(b, t), 0, 0)),
            full((1, D_MODEL)),
            full((D_MODEL, 2048)),
            full((1, B_Q_RANK)),
            full((B_Q_RANK, 512)),
            full((1, B_KV_RANK)),
            full((B_KV_RANK, 768)),
        ] + [tab_spec] * 12,
        out_specs=[pl.BlockSpec((1, TOK_TILE, n), lambda b, t: (b, t, 0)) for n in widths],
        compiler_params=_params(("arbitrary", "arbitrary")),
        name="in_projection",
    )(xall, mod_l, g1, w, qn, wuq, kvn, wukv, *tabs)


def _attn_a_kernel(*refs, n_kv_blocks, has_bias, tq):
    sink_ref, q_ref = refs[0], refs[1]
    k_refs = refs[2:2 + n_kv_blocks]
    v_refs = refs[2 + n_kv_blocks:2 + 2 * n_kv_blocks]
    pos = 2 + 2 * n_kv_blocks
    bias_ref = refs[pos] if has_bias else None
    o_ref = refs[-1]

    q = q_ref[0]
    kcat = jnp.concatenate([r[0] for r in k_refs], axis=0) if n_kv_blocks > 1 else k_refs[0][0]
    vcat = jnp.concatenate([r[0] for r in v_refs], axis=0) if n_kv_blocks > 1 else v_refs[0][0]
    lane = lax.broadcasted_iota(jnp.int32, (1, LANES), 1)
    acc = [jnp.zeros((tq, LANES), F32) for _ in range(A_GROUP)]
    for g in range(A_KV_HEADS):
        in_half = (lane < HEAD_DIM) if g == 0 else (lane >= HEAD_DIM)
        keep = _indicator(in_half)
        qs = jnp.concatenate([q[:, j * LANES:(j + 1) * LANES] * keep for j in range(A_GROUP)], axis=0)
        s = _dot_nt(qs, kcat)
        if has_bias:
            s = s + jnp.concatenate([bias_ref[0]] * A_GROUP, axis=0)
        sink = jnp.concatenate(
            [jnp.full((tq, 1), sink_ref[g * A_GROUP + j] * LOG2_E, F32) for j in range(A_GROUP)], axis=0)
        m = jnp.maximum(jnp.max(s, axis=-1, keepdims=True), sink)
        ol = _dot(jnp.exp2(s - m).astype(BF16), vcat)
        o = ol[:, 0:LANES] / (ol[:, LANES:LANES + 1] + jnp.exp2(sink - m))
        for j in range(A_GROUP):
            acc[j] = acc[j] + jnp.where(in_half, o[j * tq:(j + 1) * tq], 0.0)
    o_ref[0] = jnp.concatenate(acc, axis=-1).astype(BF16)


A_TQ = WIN
A_KEY_BLOCKS = A_TQ // WIN + 2


def _attn_a_latent(sink, qa, ka, va, bias):
    nq = SEQ // A_TQ
    last_blk = SEQ // WIN - 1
    ctx_blk = SEQ // CTX_LEN
    per_step = A_TQ // WIN

    def kv_map(i):
        return lambda b, n: (b, jnp.clip(n * per_step - 1 + i, 0, last_blk), 0)

    def kv_specs(width):
        return ([pl.BlockSpec((1, WIN, width), kv_map(i)) for i in range(A_KEY_BLOCKS)]
                + [pl.BlockSpec((1, CTX_LEN, width), lambda b, n: (b, ctx_blk, 0))])

    n_kv = A_KEY_BLOCKS + 1
    bias_spec = pl.BlockSpec(
        (1, A_TQ, A_KEY_BLOCKS * WIN + CTX_LEN),
        lambda b, n: (jnp.where(n == 0, 0, jnp.where(n == nq - 1, 2, 1)), 0, 0))
    return pl.pallas_call(
        functools.partial(_attn_a_kernel, n_kv_blocks=n_kv, has_bias=True, tq=A_TQ),
        out_shape=jax.ShapeDtypeStruct((BATCH, NT_PAD, 512), BF16),
        grid=(BATCH, nq),
        in_specs=[pl.BlockSpec(memory_space=pltpu.SMEM),
                  pl.BlockSpec((1, A_TQ, 512), lambda b, n: (b, n, 0))]
                 + kv_specs(LANES) + kv_specs(2 * LANES) + [bias_spec],
        out_specs=pl.BlockSpec((1, A_TQ, 512), lambda b, n: (b, n, 0)),
        compiler_params=_params(("arbitrary", "arbitrary")),
        name="attn_window_latent",
    )(sink, qa, *([ka] * n_kv), *([va] * n_kv), bias)


def _attn_a_ctx(sink, qa, ka, va, prev_out):
    ctx_blk = SEQ // CTX_LEN
    spec = lambda n: pl.BlockSpec((1, CTX_LEN, n), lambda b: (b, ctx_blk, 0))
    tile = lambda n: pl.BlockSpec((1, TOK_TILE, n), lambda b: (b, N_LAT_TILES, 0))

    def body(sink_ref, q_ref, k_ref, v_ref, prev_ref, o_ref):
        del prev_ref
        _attn_a_kernel(sink_ref, q_ref, k_ref, v_ref, o_ref, n_kv_blocks=1, has_bias=False, tq=TOK_TILE)

    return pl.pallas_call(
        body,
        out_shape=jax.ShapeDtypeStruct((BATCH, NT_PAD, 512), BF16),
        grid=(BATCH,),
        in_specs=[pl.BlockSpec(memory_space=pltpu.SMEM), tile(512), spec(LANES), spec(2 * LANES),
                  pl.BlockSpec(memory_space=pl.ANY)],
        out_specs=tile(512),
        input_output_aliases={4: 0},
        compiler_params=_params(("arbitrary",)),
        name="attn_window_ctx",
    )(sink, qa, ka, va, prev_out)


def _key_rows(ref):
    return ref[0, 0:NT, :] if ref.shape[1] == NT_PAD else ref[0]


def _attn_b_kernel(q_ref, k_ref, v_ref, o_ref):
    q = q_ref[0]
    k = _key_rows(k_ref)
    v = _key_rows(v_ref)
    outs = []
    for h in range(B_HEADS):
        s = _dot_nt(q[:, h * LANES:(h + 1) * LANES], k[:, h * LANES:(h + 1) * LANES])
        outs.append(_softmax_pv(s, v[:, h * LANES:(h + 1) * LANES], h))
    o_ref[0] = _merge_head_pairs(outs).astype(BF16)


ATT_TQ = 512


def _full_attention(kernel, name, q, k, v, extra, extra_specs, out_width, q_width, k_width, v_width):
    out = pl.pallas_call(
        kernel,
        out_shape=jax.ShapeDtypeStruct((BATCH, NT_PAD, out_width), BF16),
        grid=(BATCH, SEQ // ATT_TQ),
        in_specs=[pl.BlockSpec((1, ATT_TQ, q_width), lambda b, t: (b, t, 0)),
                  pl.BlockSpec((1, NT_PAD, k_width), lambda b, t: (b, 0, 0)),
                  pl.BlockSpec((1, NT_PAD, v_width), lambda b, t: (b, 0, 0))] + extra_specs(2),
        out_specs=pl.BlockSpec((1, ATT_TQ, out_width), lambda b, t: (b, t, 0)),
        compiler_params=_params(("arbitrary", "arbitrary")),
        name=name + "_latent",
    )(q, k, v, *extra)
    return out


def _full_attention_ctx(kernel, name, q, k, v, extra, extra_specs, out_width, q_width, k_width, v_width,
                        prev_out):
    ctx_blk = SEQ // CTX_LEN
    n_extra = len(extra)

    def body(*refs):
        ins = refs[:3 + n_extra]
        kernel(*ins, refs[-1])

    spec = lambda n: pl.BlockSpec((1, CTX_LEN, n), lambda b: (b, ctx_blk, 0))
    tile = lambda n: pl.BlockSpec((1, TOK_TILE, n), lambda b: (b, N_LAT_TILES, 0))
    return pl.pallas_call(
        body,
        out_shape=jax.ShapeDtypeStruct((BATCH, NT_PAD, out_width), BF16),
        grid=(BATCH,),
        in_specs=[tile(q_width), spec(k_width), spec(v_width)] + extra_specs(1)
                 + [pl.BlockSpec(memory_space=pl.ANY)],
        out_specs=tile(out_width),
        input_output_aliases={3 + n_extra: 0},
        compiler_params=_params(("arbitrary",)),
        name=name + "_ctx",
    )(q, k, v, *extra, prev_out)


def _attn_c_kernel(q_ref, k_ref, v_ref, lq1_ref, lk1_ref, lq2_ref, lk2_ref, sn_ref, o_ref, *, lam_init):
    q = q_ref[0]
    k = _key_rows(k_ref)
    v = _key_rows(v_ref)
    tq = q.shape[0]
    lam = (jnp.exp(jnp.sum(lq1_ref[...] * lk1_ref[...], axis=-1, keepdims=True))
           - jnp.exp(jnp.sum(lq2_ref[...] * lk2_ref[...], axis=-1, keepdims=True)) + lam_init)
    lane = lax.broadcasted_iota(jnp.int32, (1, C_HEADS * C_V), 1)
    outs = []
    for h in range(C_HEADS):
        lo = h * C_V
        q1 = q * _indicator((lane >= lo) & (lane < lo + C_DH))
        q2 = q * _indicator((lane >= lo + C_DH) & (lane < lo + C_V))
        s = _dot_nt(jnp.concatenate([q1, q2], axis=0), k)
        pv = _softmax_pv(s, v[:, h * LANES:(h + 1) * LANES], h)
        outs.append(pv[0:tq] - lam * pv[tq:2 * tq])
    acc = _merge_head_pairs(outs)
    sq = acc * acc
    inv = jnp.zeros((tq, C_HEADS * C_V), F32)
    for h in range(C_HEADS):
        in_head = (lane >= h * C_V) & (lane < (h + 1) * C_V)
        ms = jnp.sum(jnp.where(in_head, sq, 0.0), axis=-1, keepdims=True) * (1.0 / C_V)
        inv = inv + jnp.where(in_head, lax.rsqrt(ms + EPS), 0.0)
    o_ref[0] = ((acc * inv) * sn_ref[...] * (1.0 - lam_init)).astype(BF16)


def _outproj_kernel(x_ref, oa_ref, ob_ref, od_ref, w_ref, mod_ref, g2_ref, wr_ref,
                    xo_ref, h2_ref, aff_ref):
    mod = mod_ref[0]
    g1 = mod[:, 2 * D_MODEL:3 * D_MODEL]
    sh2 = mod[:, 3 * D_MODEL:4 * D_MODEL]
    sc2 = mod[:, 4 * D_MODEL:5 * D_MODEL]
    mix = jnp.concatenate([oa_ref[0], ob_ref[0], od_ref[0]], axis=-1)
    x = x_ref[0] + g1 * _dot(mix, w_ref[...])
    xo_ref[0] = x
    h2 = (_rms(x) * g2_ref[...]) * (1.0 + sc2) + sh2
    tile = h2.shape[0]
    h_hi, h_lo = _split_bf16(h2)
    h2_ref[0] = h_hi
    w_hi, w_lo = _split_bf16(wr_ref[...])
    prod = _dot(jnp.concatenate([h_hi, h_lo], axis=0), jnp.concatenate([w_hi, w_lo], axis=-1))
    logits = prod[0:tile, 0:LANES] + prod[0:tile, LANES:2 * LANES] + prod[tile:2 * tile, 0:LANES]
    lane = lax.broadcasted_iota(jnp.int32, (1, LANES), 1)
    logits = jnp.where(lane < N_EXPERTS, logits, NEG_INF)
    aff_ref[0] = _softmax_rows(logits)


def _outproj(xall, oa, ob, od, w, mod_l, g2, wr, n_tiles):
    full = lambda shape: pl.BlockSpec(shape, lambda b, t: (0,) * len(shape))
    tok = lambda n: pl.BlockSpec((1, TOK_TILE, n), lambda b, t: (b, t, 0))
    return pl.pallas_call(
        _outproj_kernel,
        out_shape=[jax.ShapeDtypeStruct((BATCH, NT_PAD, D_MODEL), F32),
                   jax.ShapeDtypeStruct((BATCH, NT_PAD, D_MODEL), BF16),
                   jax.ShapeDtypeStruct((BATCH, NT_PAD, LANES), F32)],
        grid=(BATCH, n_tiles),
        in_specs=[tok(D_MODEL), tok(512), tok(256), tok(256), full((D_MODEL, D_MODEL)),
                  pl.BlockSpec((1, 1, 6 * D_MODEL), lambda b, t: (_mod_row(b, t), 0, 0)),
                  full((1, D_MODEL)), full((D_MODEL, LANES))],
        out_specs=[tok(D_MODEL), tok(D_MODEL), tok(LANES)],
        compiler_params=_params(("arbitrary", "arbitrary")),
        name="out_projection",
    )(xall, oa, ob, od, w, mod_l, g2, wr)


def _prefix_count(flags_f32, tri_ref):
    rows, n = flags_f32.shape
    run = jnp.zeros((rows, 1), F32)
    outs = []
    for c in range(n // LANES):
        blk = flags_f32[:, c * LANES:(c + 1) * LANES]
        outs.append(_dot(blk.astype(BF16), tri_ref[...]) + run)
        run = run + jnp.sum(blk, axis=-1, keepdims=True)
    return jnp.concatenate(outs, axis=-1)


def _select_slots(a, cap, tri_ref):
    bits = pltpu.bitcast(a, jnp.int32)

    def step(i, t):
        cand = t | jnp.left_shift(jnp.int32(1), 30 - i)
        cnt = jnp.sum((bits >= cand).astype(jnp.int32), axis=-1, keepdims=True)
        return jnp.where(cnt >= cap, cand, t)

    thr = lax.fori_loop(0, 31, step, jnp.zeros((a.shape[0], 1), jnp.int32))
    above = bits > thr
    tied = bits == thr
    need = cap - jnp.sum(above.astype(F32), axis=-1, keepdims=True)
    tied_f = tied.astype(F32)
    sel = above | (tied & (_prefix_count(tied_f, tri_ref) < need))
    sel_f = sel.astype(F32)
    return jnp.where(sel, _prefix_count(sel_f, tri_ref), -1.0)


def _route_kernel(aff_ref, tri_ref, row_ref, col_ref, *, with_ctx):
    lat, ctx = [], []
    for b in range(BATCH):
        lat.append(aff_ref[b, 0:SEQ, :].T[0:N_EXPERTS])
        if with_ctx:
            ctx.append(aff_ref[b, SEQ:NT, :].T[0:N_EXPERTS])
    slot_lat = _select_slots(jnp.concatenate(lat, axis=0), CAP_LAT, tri_ref)
    if with_ctx:
        slot_ctx = _select_slots(jnp.concatenate(ctx, axis=0), CAP_CTX, tri_ref)
        slots = jnp.concatenate([slot_lat, slot_ctx], axis=-1)
    else:
        slots = slot_lat
    n = slots.shape[1]
    row_ref[:, 0:n] = slots
    pad = jnp.full((LANES - N_EXPERTS, n), -1.0, F32)
    for b in range(BATCH):
        blk = jnp.concatenate([slots[b * N_EXPERTS:(b + 1) * N_EXPERTS], pad], axis=0)
        col_ref[b, 0:n, :] = blk.T
        if with_ctx and NT_PAD > NT:
            col_ref[b, NT:NT_PAD, :] = jnp.full((NT_PAD - NT, LANES), -1.0, F32)


def _route(aff, tri, with_ctx):
    vm = pl.BlockSpec(memory_space=pltpu.VMEM)
    return pl.pallas_call(
        functools.partial(_route_kernel, with_ctx=with_ctx),
        out_shape=[jax.ShapeDtypeStruct((BATCH * N_EXPERTS, NT), F32),
                   jax.ShapeDtypeStruct((BATCH, NT_PAD, LANES), F32)],
        in_specs=[vm, vm],
        out_specs=[vm, vm],
        compiler_params=pltpu.CompilerParams(vmem_limit_bytes=VMEM_LIMIT),
        name="route_topc",
    )(aff, tri)


def _gather_kernel(row_ref, h_ref, *out_refs, with_ctx):
    slot = lax.broadcasted_iota(jnp.int32, (CAP_LAT, 1), 0).astype(F32)
    for e in range(N_EXPERTS):
        p = _indicator(row_ref[e, :, 0:SEQ] == slot)
        out_refs[0][e] = _dot(p, h_ref[0, 0:SEQ, :]).astype(BF16)
        if with_ctx:
            pc = _indicator(row_ref[e, :, SEQ:NT] == slot[0:CAP_CTX])
            out_refs[1][e] = _dot(pc, h_ref[0, SEQ:NT, :]).astype(BF16)


def _gather(slot_rows, h2, with_ctx):
    out_shape = [jax.ShapeDtypeStruct((N_EXPERTS, SEQ, D_MODEL), BF16)]
    out_specs = [pl.BlockSpec((N_EXPERTS, CAP_LAT, D_MODEL), lambda b: (0, b, 0))]
    if with_ctx:
        out_shape.append(jax.ShapeDtypeStruct((N_EXPERTS, CTX_LEN, D_MODEL), BF16))
        out_specs.append(pl.BlockSpec((N_EXPERTS, CAP_CTX, D_MODEL), lambda b: (0, b, 0)))
    return pl.pallas_call(
        functools.partial(_gather_kernel, with_ctx=with_ctx),
        out_shape=out_shape,
        grid=(BATCH,),
        in_specs=[pl.BlockSpec((N_EXPERTS, 1, NT), lambda b: (b, 0, 0)),
                  pl.BlockSpec((1, NT_PAD, D_MODEL), lambda b: (b, 0, 0))],
        out_specs=out_specs,
        compiler_params=_params(("arbitrary",)),
        name="expert_gather",
    )(slot_rows, h2)


def _ffn_kernel(*refs, with_ctx):
    if with_ctx:
        xl_ref, xc_ref, wg_ref, wu_ref, wd_ref, y_ref = refs
    else:
        xl_ref, wg_ref, wu_ref, wd_ref, y_ref = refs
    wg = wg_ref[0, 0].astype(BF16)
    wu = wu_ref[0, 0].astype(BF16)
    wd = wd_ref[0, 0].astype(BF16)

    def swiglu(x):
        gate = _dot(x, wg)
        hid = (gate / (1.0 + jnp.exp(-gate))) * _dot(x, wu)
        return _dot(hid.astype(BF16), wd).astype(BF16)

    for r in range(FFN_LAT_TILES):
        rows = slice(r * FFN_TILE, (r + 1) * FFN_TILE)
        y_ref[0, rows, :] = swiglu(xl_ref[0, rows, :])
    if with_ctx:
        y_ref[0, SEQ:NT, :] = swiglu(xc_ref[0])


def _ffn(xg_lat, xg_ctx, w_gate, w_up, w_down, layer):
    with_ctx = xg_ctx is not None
    ins = [xg_lat] + ([xg_ctx] if with_ctx else []) + [w_gate, w_up, w_down]
    in_specs = [pl.BlockSpec((1, SEQ, D_MODEL), lambda e: (e, 0, 0))]
    if with_ctx:
        in_specs.append(pl.BlockSpec((1, CTX_LEN, D_MODEL), lambda e: (e, 0, 0)))
    in_specs += [pl.BlockSpec((1, 1, D_MODEL, D_EXPERT), lambda e: (layer, e, 0, 0)),
                 pl.BlockSpec((1, 1, D_MODEL, D_EXPERT), lambda e: (layer, e, 0, 0)),
                 pl.BlockSpec((1, 1, D_EXPERT, D_MODEL), lambda e: (layer, e, 0, 0))]
    return pl.pallas_call(
        functools.partial(_ffn_kernel, with_ctx=with_ctx),
        out_shape=jax.ShapeDtypeStruct((N_EXPERTS, NT, D_MODEL), BF16),
        grid=(N_EXPERTS,),
        in_specs=in_specs,
        out_specs=pl.BlockSpec((1, NT, D_MODEL), lambda e: (e, 0, 0)),
        compiler_params=_params(("arbitrary",)),
        name="expert_ffn",
    )(*ins)


def _combine_kernel(*refs, with_ctx, final):
    if with_ctx:
        x_ref, col_ref, aff_ref, yl_ref, yc_ref, mod_ref, o_ref = refs
    elif final:
        x_ref, col_ref, aff_ref, yl_ref, mod_ref, fn_ref, o_ref = refs
    else:
        x_ref, col_ref, aff_ref, yl_ref, mod_ref, o_ref = refs
    g2 = mod_ref[0][:, 5 * D_MODEL:6 * D_MODEL]
    col = col_ref[0]
    aff = aff_ref[0]

    def finish(acc):
        x = x_ref[0] + g2 * acc
        if final:
            x = _rms(x) * fn_ref[...]
        o_ref[0] = x

    def latent():
        lane = lax.broadcasted_iota(jnp.int32, (1, CAP_LAT), 1).astype(F32)
        acc = jnp.zeros((TOK_TILE, D_MODEL), F32)
        for e in range(N_EXPERTS):
            pt = _indicator(col[:, e:e + 1] == lane)
            acc = acc + aff[:, e:e + 1] * _dot(pt, yl_ref[e])
        finish(acc)

    def context():
        lane = lax.broadcasted_iota(jnp.int32, (1, LANES), 1).astype(F32)
        pad = jnp.zeros((LANES - CAP_CTX, D_MODEL), BF16)
        acc = jnp.zeros((TOK_TILE, D_MODEL), F32)
        for e in range(N_EXPERTS):
            pt = _indicator(col[:, e:e + 1] == lane)
            acc = acc + aff[:, e:e + 1] * _dot(pt, jnp.concatenate([yc_ref[e], pad], axis=0))
        finish(acc)

    if with_ctx:
        t = pl.program_id(1)
        pl.when(t < N_LAT_TILES)(latent)
        pl.when(t == N_LAT_TILES)(context)
    else:
        latent()


def _combine(xall, slot_cols, aff, y, mod_l, with_ctx, final_norm=None):
    final = final_norm is not None
    n_tiles = N_TILES if with_ctx else N_LAT_TILES
    n_rows = SEQ if final else NT_PAD
    tok = lambda n: pl.BlockSpec((1, TOK_TILE, n), lambda b, t: (b, t, 0))
    ins = [xall, slot_cols, aff, y]
    in_specs = [tok(D_MODEL), tok(LANES), tok(LANES),
                pl.BlockSpec((N_EXPERTS, CAP_LAT, D_MODEL), lambda b, t: (0, b, 0))]
    if with_ctx:
        ins.append(y)
        in_specs.append(pl.BlockSpec((N_EXPERTS, CAP_CTX, D_MODEL), lambda b, t: (0, SEQ // CAP_CTX + b, 0)))
    ins.append(mod_l)
    in_specs.append(pl.BlockSpec((1, 1, 6 * D_MODEL), lambda b, t: (_mod_row(b, t), 0, 0)))
    if final:
        ins.append(final_norm)
        in_specs.append(pl.BlockSpec((1, D_MODEL), lambda b, t: (0, 0)))
    return pl.pallas_call(
        functools.partial(_combine_kernel, with_ctx=with_ctx, final=final),
        out_shape=jax.ShapeDtypeStruct((BATCH, n_rows, D_MODEL), F32),
        grid=(BATCH, n_tiles),
        in_specs=in_specs,
        out_specs=tok(D_MODEL),
        compiler_params=_params(("arbitrary", "arbitrary")),
        name="expert_combine",
    )(*ins)


def _rope_tables():
    t = jnp.arange(SEQ)
    pos = jnp.stack([t // GRID_W, t % GRID_W], axis=0)

    def table(n_f, lanes_per_unit):
        inv = ROPE_BASE ** (-jnp.arange(n_f, dtype=F32) / n_f)
        d = np.arange(lanes_per_unit)
        axis, second, f = d // (2 * n_f), (d // n_f) % 2, d % n_f
        ang = pos[axis].T.astype(F32) * inv[f][None, :]
        cos = jnp.cos(ang)
        sin = jnp.sin(ang) * jnp.where(second == 1, 1.0, -1.0)[None, :]
        return cos, sin

    def finish(cos, sin, scale):
        cos = jnp.concatenate([cos, jnp.ones((NT_PAD - SEQ, LANES), F32)], axis=0)
        sin = jnp.concatenate([sin, jnp.zeros((NT_PAD - SEQ, LANES), F32)], axis=0)
        scale = scale * LOG2_E
        return [cos, sin, cos * scale, sin * scale]

    cos_a, sin_a = table(HEAD_DIM // 4, HEAD_DIM)
    tabs_a = finish(jnp.tile(cos_a, (1, 2)), jnp.tile(sin_a, (1, 2)), HEAD_DIM ** -0.5)
    cos_8, sin_8 = table(B_ROPE // 4, B_ROPE)
    tabs_8 = finish(jnp.tile(cos_8, (1, 4)), jnp.tile(sin_8, (1, 4)), C_DH ** -0.5)
    ones = jnp.ones((SEQ, B_NOPE), F32)
    pad1 = jnp.ones((SEQ, LANES - B_NOPE - B_ROPE), F32)
    cos_b = jnp.concatenate([ones, cos_8, pad1], axis=1)
    sin_b = jnp.concatenate([0 * ones, sin_8, 0 * pad1], axis=1)
    tabs_b = finish(cos_b, sin_b, (B_NOPE + B_ROPE) ** -0.5)
    return tabs_a + tabs_8 + tabs_b


def _window_bias():
    n_local = A_KEY_BLOCKS * WIN
    qi = np.arange(A_TQ)[:, None]
    kj = np.arange(n_local)[None, :]
    band = (kj >= qi) & (kj <= qi + 2 * WIN)
    out = np.zeros((3, A_TQ, n_local + CTX_LEN), np.float32)
    for v, (has_prev, has_next) in enumerate([(False, True), (True, True), (True, False)]):
        ok = band & ((kj >= WIN) | has_prev) & ((kj < n_local - WIN) | has_next)
        out[v, :, :n_local] = np.where(ok, 0.0, NEG_INF)
    return jnp.asarray(out)


def _prep_weights(w_in, w_uq, w_ukv, w_out, w_router):
    offs = np.cumsum((0, 512, 128, 128, 256, 128, 32, 256, 256, 256))
    seg = lambda i: w_in[:, :, offs[i]:offs[i + 1]]
    head_order = np.array([[j, A_GROUP + j] for j in range(A_GROUP)]).reshape(-1)
    a_cols = (head_order[:, None] * HEAD_DIM + np.arange(HEAD_DIM)[None, :]).reshape(-1)
    kr = jnp.pad(seg(5), ((0, 0), (0, 0), (B_NOPE, LANES - B_NOPE - B_ROPE)))
    w = jnp.concatenate([seg(0)[:, :, a_cols], seg(1), seg(2), seg(3), seg(4), kr,
                         seg(6), seg(7), seg(8)], axis=-1).astype(BF16)
    wuq = w_uq.reshape(DEPTH, B_Q_RANK, B_HEADS, B_NOPE + B_ROPE)
    wuq = jnp.pad(wuq, ((0, 0), (0, 0), (0, 0), (0, LANES - B_NOPE - B_ROPE)))
    wuq = wuq.reshape(DEPTH, B_Q_RANK, B_HEADS * LANES).astype(BF16)
    wukv = w_ukv.reshape(DEPTH, B_KV_RANK, B_HEADS, B_NOPE + B_V)
    wk = jnp.pad(wukv[..., :B_NOPE], ((0, 0), (0, 0), (0, 0), (0, LANES - B_NOPE)))
    wukv = jnp.concatenate([wk.reshape(DEPTH, B_KV_RANK, B_HEADS * LANES),
                            wukv[..., B_NOPE:].reshape(DEPTH, B_KV_RANK, B_HEADS * B_V)],
                           axis=-1).astype(BF16)
    wo = jnp.concatenate([w_out[:, :512][:, a_cols], w_out[:, 512:]], axis=1).astype(BF16)
    wr = jnp.pad(w_router, ((0, 0), (0, 0), (0, LANES - N_EXPERTS)))
    return w, wuq, wukv, wo, wr


def kernel(x, c, ctx, c_ctx, norm1, norm2, w_ada, b_ada, w_in, sink, mla_q_norm, w_uq, mla_kv_norm, w_ukv,
           lam_q1, lam_k1, lam_q2, lam_k2, diff_norm, w_out, w_router, w_gate, w_up, w_down, final_norm):
    xall = jnp.concatenate([x, ctx, jnp.zeros((BATCH, NT_PAD - NT, D_MODEL), F32)], axis=1)
    cc = jnp.concatenate([c, c_ctx[None, :], jnp.zeros((16 - BATCH - 1, D_MODEL), F32)], axis=0)
    mod = _modulation(cc, w_ada, b_ada).reshape(DEPTH, 16, 1, 6 * D_MODEL)
    tabs = _rope_tables()
    bias = _window_bias()
    tri = jnp.asarray(np.triu(np.ones((LANES, LANES), np.float32), k=1), BF16)
    w, wuq, wukv, wo, wr = _prep_weights(w_in, w_uq, w_ukv, w_out, w_router)
    sub_norm = jnp.tile(diff_norm, (1, C_HEADS)).reshape(DEPTH, 1, C_HEADS * C_V)
    row = lambda a, l: a[l].reshape(1, -1)

    for l in range(DEPTH):
        need_ctx = l < DEPTH - 1
        qa, ka, va, qb, kb, vb, qd, kd, vd = _inproj(
            xall, mod[l], row(norm1, l), w[l], row(mla_q_norm, l), wuq[l], row(mla_kv_norm, l), wukv[l], tabs)

        oa = _attn_a_latent(sink[l], qa, ka, va, bias)
        ob = _full_attention(_attn_b_kernel, "attn_mla", qb, kb, vb, [], lambda n: [], 256, 512, 512, 512)
        lam_init = 0.8 - 0.6 * math.exp(-0.3 * l)
        c_kernel = functools.partial(_attn_c_kernel, lam_init=lam_init)
        c_extra = [row(lam_q1, l), row(lam_k1, l), row(lam_q2, l), row(lam_k2, l), sub_norm[l]]

        def c_specs(n_grid, c_extra=c_extra):
            zmap = (lambda b, t: (0, 0)) if n_grid == 2 else (lambda b: (0, 0))
            return [pl.BlockSpec(a.shape, zmap) for a in c_extra]

        od = _full_attention(c_kernel, "attn_diff", qd, kd, vd, c_extra, c_specs, 256, 256, 256, 512)
        if need_ctx:
            oa = _attn_a_ctx(sink[l], qa, ka, va, oa)
            ob = _full_attention_ctx(_attn_b_kernel, "attn_mla", qb, kb, vb, [], lambda n: [],
                                     256, 512, 512, 512, ob)
            od = _full_attention_ctx(c_kernel, "attn_diff", qd, kd, vd, c_extra, c_specs,
                                     256, 256, 256, 512, od)

        n_tiles = N_TILES if need_ctx else N_LAT_TILES
        xall, h2, aff = _outproj(xall, oa, ob, od, wo[l], mod[l], row(norm2, l), wr[l], n_tiles)
        slot_rows, slot_cols = _route(aff, tri, need_ctx)
        gathered = _gather(slot_rows.reshape(BATCH * N_EXPERTS, 1, NT), h2, need_ctx)
        y = _ffn(gathered[0], gathered[1] if need_ctx else None, w_gate, w_up, w_down, l)
        xall = _combine(xall, slot_cols, aff, y, mod[l], need_ctx,
                        final_norm=None if need_ctx else final_norm.reshape(1, D_MODEL))
    return xall
```

```python
import functools
import math

import numpy as np
import jax
import jax.numpy as jnp
from jax import lax
from jax.experimental import pallas as pl
from jax.experimental.pallas import tpu as pltpu

D_MODEL = 1024
BATCH = 8
SEQ = 2048
DEPTH = 4
CTX_LEN = 256
NT = SEQ + CTX_LEN
GRID_W = 64
HEAD_DIM = 64
ROPE_BASE = 10000.0
EPS = 1e-6
NEG_INF = -1e30
LOG2_E = math.log2(math.e)

WIN = 128
A_HEADS = 8
A_KV_HEADS = 2
A_GROUP = A_HEADS // A_KV_HEADS
B_HEADS = 4
B_NOPE = 64
B_ROPE = 32
B_V = 64
B_Q_RANK = 256
B_KV_RANK = 128
C_HEADS = 4
C_DH = 32
C_V = 2 * C_DH
N_EXPERTS = 16
EC_FACTOR = 2
D_EXPERT = 512
CAP_LAT = EC_FACTOR * SEQ // N_EXPERTS
CAP_CTX = EC_FACTOR * CTX_LEN // N_EXPERTS

LANES = 128
TOK_TILE = 256
NT_PAD = SEQ + TOK_TILE
N_TILES = NT_PAD // TOK_TILE
N_LAT_TILES = SEQ // TOK_TILE
FFN_TILE = 512
FFN_LAT_TILES = SEQ // FFN_TILE
VMEM_LIMIT = 56 * 1024 * 1024

F32 = jnp.float32
BF16 = jnp.bfloat16

_NT_DIMS = (((1,), (1,)), ((), ()))


def _params(sem):
    return pltpu.CompilerParams(dimension_semantics=sem, vmem_limit_bytes=VMEM_LIMIT)


def _dot(a, b):
    return jnp.dot(a, b, preferred_element_type=F32)


def _dot_nt(a, b):
    return lax.dot_general(a, b, _NT_DIMS, preferred_element_type=F32)


def _split_bf16(v):
    hi = v.astype(BF16)
    lo = (v - hi.astype(F32)).astype(BF16)
    return hi, lo


def _dot3(a, b):
    a_hi, a_lo = _split_bf16(a)
    b_hi, b_lo = _split_bf16(b)
    return _dot(a_hi, b_hi) + _dot(a_lo, b_hi) + _dot(a_hi, b_lo)


def _indicator(cond):
    return jnp.where(cond, 1.0, 0.0).astype(BF16)


def _rms(v):
    return v * lax.rsqrt(jnp.mean(v * v, axis=-1, keepdims=True) + EPS)


def _softmax_pv(s, v_blk, head):
    m = jnp.max(s, axis=-1, keepdims=True)
    ol = _dot(jnp.exp2(s - m).astype(BF16), v_blk)
    ones_lane = 64 if head % 2 == 0 else 0
    return ol / ol[:, ones_lane:ones_lane + 1]


def _merge_head_pairs(per_head):
    lane = lax.broadcasted_iota(jnp.int32, (1, LANES), 1)
    blocks = [jnp.where(lane < 64, per_head[j], per_head[j + 1]) for j in range(0, len(per_head), 2)]
    return jnp.concatenate(blocks, axis=-1)


def _softmax_rows(s):
    m = jnp.max(s, axis=-1, keepdims=True)
    e = jnp.exp(s - m)
    return e / jnp.sum(e, axis=-1, keepdims=True)


def _mod_kernel(c_ref, w_ref, b_ref, o_ref):
    c = c_ref[...]
    s = c / (1.0 + jnp.exp(-c))
    o_ref[0] = _dot3(s, w_ref[0]) + b_ref[0]


def _modulation(cc, w_ada, b_ada):
    tn = 1536
    return pl.pallas_call(
        _mod_kernel,
        out_shape=jax.ShapeDtypeStruct((DEPTH, 16, 6 * D_MODEL), F32),
        grid=(DEPTH, 6 * D_MODEL // tn),
        in_specs=[
            pl.BlockSpec((16, D_MODEL), lambda l, j: (0, 0)),
            pl.BlockSpec((1, D_MODEL, tn), lambda l, j: (l, 0, j)),
            pl.BlockSpec((1, 1, tn), lambda l, j: (l, 0, j)),
        ],
        out_specs=pl.BlockSpec((1, 16, tn), lambda l, j: (l, 0, j)),
        compiler_params=_params(("arbitrary", "arbitrary")),
        name="adaln_mod",
    )(cc, w_ada, b_ada.reshape(DEPTH, 1, 6 * D_MODEL))


def _rope(p, cos, sin, half):
    lane = lax.broadcasted_iota(jnp.int32, (1, LANES), 1)
    first = (lane // half) % 2 == 0
    outs = []
    for j in range(p.shape[1] // LANES):
        blk = p[:, j * LANES:(j + 1) * LANES]
        partner = jnp.where(first, pltpu.roll(blk, LANES - half, 1), pltpu.roll(blk, half, 1))
        outs.append(blk * cos + partner * sin)
    return outs[0] if len(outs) == 1 else jnp.concatenate(outs, axis=-1)


def _per_head_with_ones(v):
    lane = lax.broadcasted_iota(jnp.int32, (1, LANES), 1)
    outs = []
    for j in range(v.shape[1] // LANES):
        blk = v[:, j * LANES:(j + 1) * LANES]
        outs.append(jnp.where(lane < 64, blk, jnp.where(lane == 64, 1.0, 0.0)))
        outs.append(jnp.where(lane >= 64, blk, jnp.where(lane == 0, 1.0, 0.0)))
    return jnp.concatenate(outs, axis=-1)


def _inproj_kernel(x_ref, mod_ref, g1_ref, w_ref, qn_ref, wuq_ref, kvn_ref, wukv_ref,
                   ca_ref, sa_ref, caq_ref, saq_ref, c8_ref, s8_ref, c8q_ref, s8q_ref,
                   cb_ref, sb_ref, cbq_ref, sbq_ref,
                   qa_ref, ka_ref, va_ref, qb_ref, kb_ref, vb_ref, qd_ref, kd_ref, vd_ref):
    x = x_ref[0]
    mod = mod_ref[0]
    sh1 = mod[:, 0:D_MODEL]
    sc1 = mod[:, D_MODEL:2 * D_MODEL]
    h = (_rms(x) * g1_ref[...]) * (1.0 + sc1) + sh1
    p = _dot(h.astype(BF16), w_ref[...])

    qa_ref[0] = _rope(p[:, 0:512], caq_ref[...], saq_ref[...], 16).astype(BF16)
    ka_ref[0] = _rope(p[:, 512:640], ca_ref[...], sa_ref[...], 16).astype(BF16)
    lane = lax.broadcasted_iota(jnp.int32, (x.shape[0], LANES), 1)
    va_ref[0] = jnp.concatenate([p[:, 640:768], jnp.where(lane == 0, 1.0, 0.0)], axis=-1).astype(BF16)

    cq = (_rms(p[:, 768:1024]) * qn_ref[...]).astype(BF16)
    qb = _dot(cq, wuq_ref[...])
    qb_ref[0] = _rope(qb, cbq_ref[...], sbq_ref[...], 8).astype(BF16)
    ckv = (_rms(p[:, 1024:1152]) * kvn_ref[...]).astype(BF16)
    kv = _dot(ckv, wukv_ref[...])
    kr = _rope(p[:, 1152:1280], cb_ref[...], sb_ref[...], 8)
    kb_ref[0] = (kv[:, 0:512] + jnp.concatenate([kr] * B_HEADS, axis=-1)).astype(BF16)
    vb_ref[0] = _per_head_with_ones(kv[:, 512:768]).astype(BF16)

    qd_ref[0] = _rope(p[:, 1280:1536], c8q_ref[...], s8q_ref[...], 8).astype(BF16)
    kd_ref[0] = _rope(p[:, 1536:1792], c8_ref[...], s8_ref[...], 8).astype(BF16)
    vd_ref[0] = _per_head_with_ones(p[:, 1792:2048]).astype(BF16)


def _mod_row(b, t):
    return jnp.where(t == N_LAT_TILES, BATCH, b)


def _inproj(xall, mod_l, g1, w, qn, wuq, kvn, wukv, tabs):
    full = lambda shape: pl.BlockSpec(shape, lambda b, t: (0,) * len(shape))
    tab_spec = pl.BlockSpec((TOK_TILE, LANES), lambda b, t: (t, 0))
    widths = (512, 128, 256, 512, 512, 512, 256, 256, 512)
    return pl.pallas_call(
        _inproj_kernel,
        out_shape=[jax.ShapeDtypeStruct((BATCH, NT_PAD, n), BF16) for n in widths],
        grid=(BATCH, N_TILES),
        in_specs=[
            pl.BlockSpec((1, TOK_TILE, D_MODEL), lambda b, t: (b, t, 0)),
            pl.BlockSpec((1, 1, 6 * D_MODEL), lambda b, t: (_mod_row(b, t), 0, 0)),
            full((1, D_MODEL)),
            full((D_MODEL, 2048)),
            full((1, B_Q_RANK)),
            full((B_Q_RANK, 512)),
            full((1, B_KV_RANK)),
            full((B_KV_RANK, 768)),
        ] + [tab_spec] * 12,
        out_specs=[pl.BlockSpec((1, TOK_TILE, n), lambda b, t: (b, t, 0)) for n in widths],
        compiler_params=_params(("arbitrary", "arbitrary")),
        name="in_projection",
    )(xall, mod_l, g1, w, qn, wuq, kvn, wukv, *tabs)


def _window_heads(q, kcat, vcat, bias, sink_ref):
    tq = q.shape[0]
    lane = lax.broadcasted_iota(jnp.int32, (1, LANES), 1)
    acc = [jnp.zeros((tq, LANES), F32) for _ in range(A_GROUP)]
    for g in range(A_KV_HEADS):
        in_half = (lane < HEAD_DIM) if g == 0 else (lane >= HEAD_DIM)
        keep = _indicator(in_half)
        qs = jnp.concatenate([q[:, j * LANES:(j + 1) * LANES] * keep for j in range(A_GROUP)], axis=0)
        s = _dot_nt(qs, kcat)
        if bias is not None:
            s = s + jnp.concatenate([bias] * A_GROUP, axis=0)
        sink = jnp.concatenate(
            [jnp.full((tq, 1), sink_ref[g * A_GROUP + j] * LOG2_E, F32) for j in range(A_GROUP)], axis=0)
        m = jnp.maximum(jnp.max(s, axis=-1, keepdims=True), sink)
        ol = _dot(jnp.exp2(s - m).astype(BF16), vcat)
        o = ol[:, 0:LANES] / (ol[:, LANES:LANES + 1] + jnp.exp2(sink - m))
        for j in range(A_GROUP):
            acc[j] = acc[j] + jnp.where(in_half, o[j * tq:(j + 1) * tq], 0.0)
    return jnp.concatenate(acc, axis=-1).astype(BF16)


A_STEP_BLOCKS = 4
A_LOCAL = 3 * WIN


def _attn_a_latent_kernel(sink_ref, q_ref, k_ref, v_ref, bias_ref, o_ref):
    step = pl.program_id(1)
    last = SEQ // WIN - 1
    k_ctx = k_ref[0, SEQ:NT, :]
    v_ctx = v_ref[0, SEQ:NT, :]
    for j in range(A_STEP_BLOCKS):
        n = step * A_STEP_BLOCKS + j
        start = pl.multiple_of(jnp.clip((n - 1) * WIN, 0, SEQ - A_LOCAL), WIN)
        variant = jnp.where(n == 0, 0, jnp.where(n == last, 2, 1))
        kcat = jnp.concatenate([k_ref[0, pl.ds(start, A_LOCAL), :], k_ctx], axis=0)
        vcat = jnp.concatenate([v_ref[0, pl.ds(start, A_LOCAL), :], v_ctx], axis=0)
        rows = slice(j * WIN, (j + 1) * WIN)
        o_ref[0, rows, :] = _window_heads(q_ref[0, rows, :], kcat, vcat, bias_ref[variant], sink_ref)


def _attn_a_latent(sink, qa, ka, va, bias):
    tq = A_STEP_BLOCKS * WIN
    return pl.pallas_call(
        _attn_a_latent_kernel,
        out_shape=jax.ShapeDtypeStruct((BATCH, NT_PAD, 512), BF16),
        grid=(BATCH, SEQ // tq),
        in_specs=[pl.BlockSpec(memory_space=pltpu.SMEM),
                  pl.BlockSpec((1, tq, 512), lambda b, n: (b, n, 0)),
                  pl.BlockSpec((1, NT_PAD, LANES), lambda b, n: (b, 0, 0)),
                  pl.BlockSpec((1, NT_PAD, 2 * LANES), lambda b, n: (b, 0, 0)),
                  pl.BlockSpec((3, WIN, A_LOCAL + CTX_LEN), lambda b, n: (0, 0, 0))],
        out_specs=pl.BlockSpec((1, tq, 512), lambda b, n: (b, n, 0)),
        compiler_params=_params(("arbitrary", "arbitrary")),
        name="attn_window_latent",
    )(sink, qa, ka, va, bias)


def _attn_a_ctx(sink, qa, ka, va, prev_out):
    ctx_blk = SEQ // CTX_LEN
    spec = lambda n: pl.BlockSpec((1, CTX_LEN, n), lambda b: (b, ctx_blk, 0))
    tile = lambda n: pl.BlockSpec((1, TOK_TILE, n), lambda b: (b, N_LAT_TILES, 0))

    def body(sink_ref, q_ref, k_ref, v_ref, prev_ref, o_ref):
        del prev_ref
        o_ref[0] = _window_heads(q_ref[0], k_ref[0], v_ref[0], None, sink_ref)

    return pl.pallas_call(
        body,
        out_shape=jax.ShapeDtypeStruct((BATCH, NT_PAD, 512), BF16),
        grid=(BATCH,),
        in_specs=[pl.BlockSpec(memory_space=pltpu.SMEM), tile(512), spec(LANES), spec(2 * LANES),
                  pl.BlockSpec(memory_space=pl.ANY)],
        out_specs=tile(512),
        input_output_aliases={4: 0},
        compiler_params=_params(("arbitrary",)),
        name="attn_window_ctx",
    )(sink, qa, ka, va, prev_out)


def _key_rows(ref):
    return ref[0, 0:NT, :] if ref.shape[1] == NT_PAD else ref[0]


ATT_TQ = 1024
ATT_SUB = 256


def _attn_b_kernel(q_ref, k_ref, v_ref, o_ref):
    k = _key_rows(k_ref)
    v = _key_rows(v_ref)
    for sub in range(q_ref.shape[1] // ATT_SUB):
        rows = slice(sub * ATT_SUB, (sub + 1) * ATT_SUB)
        q = q_ref[0, rows, :]
        outs = []
        for h in range(B_HEADS):
            s = _dot_nt(q[:, h * LANES:(h + 1) * LANES], k[:, h * LANES:(h + 1) * LANES])
            outs.append(_softmax_pv(s, v[:, h * LANES:(h + 1) * LANES], h))
        o_ref[0, rows, :] = _merge_head_pairs(outs).astype(BF16)


def _full_attention(kernel, name, q, k, v, extra, extra_specs, out_width, q_width, k_width, v_width):
    out = pl.pallas_call(
        kernel,
        out_shape=jax.ShapeDtypeStruct((BATCH, NT_PAD, out_width), BF16),
        grid=(BATCH, SEQ // ATT_TQ),
        in_specs=[pl.BlockSpec((1, ATT_TQ, q_width), lambda b, t: (b, t, 0)),
                  pl.BlockSpec((1, NT_PAD, k_width), lambda b, t: (b, 0, 0)),
                  pl.BlockSpec((1, NT_PAD, v_width), lambda b, t: (b, 0, 0))] + extra_specs(2),
        out_specs=pl.BlockSpec((1, ATT_TQ, out_width), lambda b, t: (b, t, 0)),
        compiler_params=_params(("arbitrary", "arbitrary")),
        name=name + "_latent",
    )(q, k, v, *extra)
    return out


def _full_attention_ctx(kernel, name, q, k, v, extra, extra_specs, out_width, q_width, k_width, v_width,
                        prev_out):
    ctx_blk = SEQ // CTX_LEN
    n_extra = len(extra)

    def body(*refs):
        ins = refs[:3 + n_extra]
        kernel(*ins, refs[-1])

    spec = lambda n: pl.BlockSpec((1, CTX_LEN, n), lambda b: (b, ctx_blk, 0))
    tile = lambda n: pl.BlockSpec((1, TOK_TILE, n), lambda b: (b, N_LAT_TILES, 0))
    return pl.pallas_call(
        body,
        out_shape=jax.ShapeDtypeStruct((BATCH, NT_PAD, out_width), BF16),
        grid=(BATCH,),
        in_specs=[tile(q_width), spec(k_width), spec(v_width)] + extra_specs(1)
                 + [pl.BlockSpec(memory_space=pl.ANY)],
        out_specs=tile(out_width),
        input_output_aliases={3 + n_extra: 0},
        compiler_params=_params(("arbitrary",)),
        name=name + "_ctx",
    )(q, k, v, *extra, prev_out)


def _attn_c_kernel(q_ref, k_ref, v_ref, lq1_ref, lk1_ref, lq2_ref, lk2_ref, sn_ref, o_ref, *, lam_init):
    k = _key_rows(k_ref)
    v = _key_rows(v_ref)
    tq = ATT_SUB
    lam = (jnp.exp(jnp.sum(lq1_ref[...] * lk1_ref[...], axis=-1, keepdims=True))
           - jnp.exp(jnp.sum(lq2_ref[...] * lk2_ref[...], axis=-1, keepdims=True)) + lam_init)
    lane = lax.broadcasted_iota(jnp.int32, (1, C_HEADS * C_V), 1)
    for sub in range(q_ref.shape[1] // ATT_SUB):
        rows = slice(sub * ATT_SUB, (sub + 1) * ATT_SUB)
        q = q_ref[0, rows, :]
        outs = []
        for h in range(C_HEADS):
            lo = h * C_V
            q1 = q * _indicator((lane >= lo) & (lane < lo + C_DH))
            q2 = q * _indicator((lane >= lo + C_DH) & (lane < lo + C_V))
            s = _dot_nt(jnp.concatenate([q1, q2], axis=0), k)
            pv = _softmax_pv(s, v[:, h * LANES:(h + 1) * LANES], h)
            outs.append(pv[0:tq] - lam * pv[tq:2 * tq])
        acc = _merge_head_pairs(outs)
        sq = acc * acc
        inv = jnp.zeros((tq, C_HEADS * C_V), F32)
        for h in range(C_HEADS):
            in_head = (lane >= h * C_V) & (lane < (h + 1) * C_V)
            ms = jnp.sum(jnp.where(in_head, sq, 0.0), axis=-1, keepdims=True) * (1.0 / C_V)
            inv = inv + jnp.where(in_head, lax.rsqrt(ms + EPS), 0.0)
        o_ref[0, rows, :] = ((acc * inv) * sn_ref[...] * (1.0 - lam_init)).astype(BF16)


def _outproj_kernel(x_ref, oa_ref, ob_ref, od_ref, w_ref, mod_ref, g2_ref, wr_ref,
                    xo_ref, h2_ref, aff_ref):
    mod = mod_ref[0]
    g1 = mod[:, 2 * D_MODEL:3 * D_MODEL]
    sh2 = mod[:, 3 * D_MODEL:4 * D_MODEL]
    sc2 = mod[:, 4 * D_MODEL:5 * D_MODEL]
    mix = jnp.concatenate([oa_ref[0], ob_ref[0], od_ref[0]], axis=-1)
    x = x_ref[0] + g1 * _dot(mix, w_ref[...])
    xo_ref[0] = x
    h2 = (_rms(x) * g2_ref[...]) * (1.0 + sc2) + sh2
    tile = h2.shape[0]
    h_hi, h_lo = _split_bf16(h2)
    h2_ref[0] = h_hi
    w_hi, w_lo = _split_bf16(wr_ref[...])
    prod = _dot(jnp.concatenate([h_hi, h_lo], axis=0), jnp.concatenate([w_hi, w_lo], axis=-1))
    logits = prod[0:tile, 0:LANES] + prod[0:tile, LANES:2 * LANES] + prod[tile:2 * tile, 0:LANES]
    lane = lax.broadcasted_iota(jnp.int32, (1, LANES), 1)
    logits = jnp.where(lane < N_EXPERTS, logits, NEG_INF)
    aff_ref[0] = _softmax_rows(logits)


def _outproj(xall, oa, ob, od, w, mod_l, g2, wr, n_tiles):
    full = lambda shape: pl.BlockSpec(shape, lambda b, t: (0,) * len(shape))
    tok = lambda n: pl.BlockSpec((1, TOK_TILE, n), lambda b, t: (b, t, 0))
    return pl.pallas_call(
        _outproj_kernel,
        out_shape=[jax.ShapeDtypeStruct((BATCH, NT_PAD, D_MODEL), F32),
                   jax.ShapeDtypeStruct((BATCH, NT_PAD, D_MODEL), BF16),
                   jax.ShapeDtypeStruct((BATCH, NT_PAD, LANES), F32)],
        grid=(BATCH, n_tiles),
        in_specs=[tok(D_MODEL), tok(512), tok(256), tok(256), full((D_MODEL, D_MODEL)),
                  pl.BlockSpec((1, 1, 6 * D_MODEL), lambda b, t: (_mod_row(b, t), 0, 0)),
                  full((1, D_MODEL)), full((D_MODEL, LANES))],
        out_specs=[tok(D_MODEL), tok(D_MODEL), tok(LANES)],
        compiler_params=_params(("arbitrary", "arbitrary")),
        name="out_projection",
    )(xall, oa, ob, od, w, mod_l, g2, wr)


def _prefix_count(flags_f32, tri_ref):
    rows, n = flags_f32.shape
    run = jnp.zeros((rows, 1), F32)
    outs = []
    for c in range(n // LANES):
        blk = flags_f32[:, c * LANES:(c + 1) * LANES]
        outs.append(_dot(blk.astype(BF16), tri_ref[...]) + run)
        run = run + jnp.sum(blk, axis=-1, keepdims=True)
    return jnp.concatenate(outs, axis=-1)


def _select_slots(a, cap, tri_ref):
    bits = pltpu.bitcast(a, jnp.int32)

    def step(i, t):
        cand = t | jnp.left_shift(jnp.int32(1), 30 - i)
        cnt = jnp.sum((bits >= cand).astype(jnp.int32), axis=-1, keepdims=True)
        return jnp.where(cnt >= cap, cand, t)

    thr = lax.fori_loop(0, 31, step, jnp.zeros((a.shape[0], 1), jnp.int32))
    above = bits > thr
    tied = bits == thr
    need = cap - jnp.sum(above.astype(F32), axis=-1, keepdims=True)
    tied_f = tied.astype(F32)
    sel = above | (tied & (_prefix_count(tied_f, tri_ref) < need))
    sel_f = sel.astype(F32)
    return jnp.where(sel, _prefix_count(sel_f, tri_ref), -1.0)


def _route_kernel(aff_ref, tri_ref, row_ref, col_ref, *, with_ctx):
    lat, ctx = [], []
    for b in range(BATCH):
        lat.append(aff_ref[b, 0:SEQ, :].T[0:N_EXPERTS])
        if with_ctx:
            ctx.append(aff_ref[b, SEQ:NT, :].T[0:N_EXPERTS])
    slot_lat = _select_slots(jnp.concatenate(lat, axis=0), CAP_LAT, tri_ref)
    if with_ctx:
        slot_ctx = _select_slots(jnp.concatenate(ctx, axis=0), CAP_CTX, tri_ref)
        slots = jnp.concatenate([slot_lat, slot_ctx], axis=-1)
    else:
        slots = slot_lat
    n = slots.shape[1]
    row_ref[:, 0:n] = slots
    pad = jnp.full((LANES - N_EXPERTS, n), -1.0, F32)
    for b in range(BATCH):
        blk = jnp.concatenate([slots[b * N_EXPERTS:(b + 1) * N_EXPERTS], pad], axis=0)
        col_ref[b, 0:n, :] = blk.T
        if with_ctx and NT_PAD > NT:
            col_ref[b, NT:NT_PAD, :] = jnp.full((NT_PAD - NT, LANES), -1.0, F32)


def _route(aff, tri, with_ctx):
    vm = pl.BlockSpec(memory_space=pltpu.VMEM)
    return pl.pallas_call(
        functools.partial(_route_kernel, with_ctx=with_ctx),
        out_shape=[jax.ShapeDtypeStruct((BATCH * N_EXPERTS, NT), F32),
                   jax.ShapeDtypeStruct((BATCH, NT_PAD, LANES), F32)],
        in_specs=[vm, vm],
        out_specs=[vm, vm],
        compiler_params=pltpu.CompilerParams(vmem_limit_bytes=VMEM_LIMIT),
        name="route_topc",
    )(aff, tri)


def _gather_kernel(row_ref, h_ref, *out_refs, with_ctx):
    slot = lax.broadcasted_iota(jnp.int32, (CAP_LAT, 1), 0).astype(F32)
    for e in range(N_EXPERTS):
        p = _indicator(row_ref[e, :, 0:SEQ] == slot)
        out_refs[0][e] = _dot(p, h_ref[0, 0:SEQ, :]).astype(BF16)
        if with_ctx:
            pc = _indicator(row_ref[e, :, SEQ:NT] == slot[0:CAP_CTX])
            out_refs[1][e] = _dot(pc, h_ref[0, SEQ:NT, :]).astype(BF16)


def _gather(slot_rows, h2, with_ctx):
    out_shape = [jax.ShapeDtypeStruct((N_EXPERTS, SEQ, D_MODEL), BF16)]
    out_specs = [pl.BlockSpec((N_EXPERTS, CAP_LAT, D_MODEL), lambda b: (0, b, 0))]
    if with_ctx:
        out_shape.append(jax.ShapeDtypeStruct((N_EXPERTS, CTX_LEN, D_MODEL), BF16))
        out_specs.append(pl.BlockSpec((N_EXPERTS, CAP_CTX, D_MODEL), lambda b: (0, b, 0)))
    return pl.pallas_call(
        functools.partial(_gather_kernel, with_ctx=with_ctx),
        out_shape=out_shape,
        grid=(BATCH,),
        in_specs=[pl.BlockSpec((N_EXPERTS, 1, NT), lambda b: (b, 0, 0)),
                  pl.BlockSpec((1, NT_PAD, D_MODEL), lambda b: (b, 0, 0))],
        out_specs=out_specs,
        compiler_params=_params(("arbitrary",)),
        name="expert_gather",
    )(slot_rows, h2)


def _ffn_kernel(*refs, with_ctx):
    if with_ctx:
        xl_ref, xc_ref, wg_ref, wu_ref, wd_ref, y_ref = refs
    else:
        xl_ref, wg_ref, wu_ref, wd_ref, y_ref = refs
    wg = wg_ref[0, 0].astype(BF16)
    wu = wu_ref[0, 0].astype(BF16)
    wd = wd_ref[0, 0].astype(BF16)

    def swiglu(x):
        gate = _dot(x, wg)
        hid = (gate / (1.0 + jnp.exp(-gate))) * _dot(x, wu)
        return _dot(hid.astype(BF16), wd).astype(BF16)

    for r in range(FFN_LAT_TILES):
        rows = slice(r * FFN_TILE, (r + 1) * FFN_TILE)
        y_ref[0, rows, :] = swiglu(xl_ref[0, rows, :])
    if with_ctx:
        y_ref[0, SEQ:NT, :] = swiglu(xc_ref[0])


def _ffn(xg_lat, xg_ctx, w_gate, w_up, w_down, layer):
    with_ctx = xg_ctx is not None
    ins = [xg_lat] + ([xg_ctx] if with_ctx else []) + [w_gate, w_up, w_down]
    in_specs = [pl.BlockSpec((1, SEQ, D_MODEL), lambda e: (e, 0, 0))]
    if with_ctx:
        in_specs.append(pl.BlockSpec((1, CTX_LEN, D_MODEL), lambda e: (e, 0, 0)))
    in_specs += [pl.BlockSpec((1, 1, D_MODEL, D_EXPERT), lambda e: (layer, e, 0, 0)),
                 pl.BlockSpec((1, 1, D_MODEL, D_EXPERT), lambda e: (layer, e, 0, 0)),
                 pl.BlockSpec((1, 1, D_EXPERT, D_MODEL), lambda e: (layer, e, 0, 0))]
    return pl.pallas_call(
        functools.partial(_ffn_kernel, with_ctx=with_ctx),
        out_shape=jax.ShapeDtypeStruct((N_EXPERTS, NT, D_MODEL), BF16),
        grid=(N_EXPERTS,),
        in_specs=in_specs,
        out_specs=pl.BlockSpec((1, NT, D_MODEL), lambda e: (e, 0, 0)),
        compiler_params=_params(("arbitrary",)),
        name="expert_ffn",
    )(*ins)


def _combine_kernel(*refs, with_ctx, final):
    if with_ctx:
        x_ref, col_ref, aff_ref, yl_ref, yc_ref, mod_ref, o_ref = refs
    elif final:
        x_ref, col_ref, aff_ref, yl_ref, mod_ref, fn_ref, o_ref = refs
    else:
        x_ref, col_ref, aff_ref, yl_ref, mod_ref, o_ref = refs
    g2 = mod_ref[0][:, 5 * D_MODEL:6 * D_MODEL]
    col = col_ref[0]
    aff = aff_ref[0]

    def finish(acc):
        x = x_ref[0] + g2 * acc
        if final:
            x = _rms(x) * fn_ref[...]
        o_ref[0] = x

    def latent():
        lane = lax.broadcasted_iota(jnp.int32, (1, CAP_LAT), 1).astype(F32)
        acc = jnp.zeros((TOK_TILE, D_MODEL), F32)
        for e in range(N_EXPERTS):
            pt = _indicator(col[:, e:e + 1] == lane)
            acc = acc + aff[:, e:e + 1] * _dot(pt, yl_ref[e])
        finish(acc)

    def context():
        lane = lax.broadcasted_iota(jnp.int32, (1, LANES), 1).astype(F32)
        pad = jnp.zeros((LANES - CAP_CTX, D_MODEL), BF16)
        acc = jnp.zeros((TOK_TILE, D_MODEL), F32)
        for e in range(N_EXPERTS):
            pt = _indicator(col[:, e:e + 1] == lane)
            acc = acc + aff[:, e:e + 1] * _dot(pt, jnp.concatenate([yc_ref[e], pad], axis=0))
        finish(acc)

    if with_ctx:
        t = pl.program_id(1)
        pl.when(t < N_LAT_TILES)(latent)
        pl.when(t == N_LAT_TILES)(context)
    else:
        latent()


def _combine(xall, slot_cols, aff, y, mod_l, with_ctx, final_norm=None):
    final = final_norm is not None
    n_tiles = N_TILES if with_ctx else N_LAT_TILES
    n_rows = SEQ if final else NT_PAD
    tok = lambda n: pl.BlockSpec((1, TOK_TILE, n), lambda b, t: (b, t, 0))
    ins = [xall, slot_cols, aff, y]
    in_specs = [tok(D_MODEL), tok(LANES), tok(LANES),
                pl.BlockSpec((N_EXPERTS, CAP_LAT, D_MODEL), lambda b, t: (0, b, 0))]
    if with_ctx:
        ins.append(y)
        in_specs.append(pl.BlockSpec((N_EXPERTS, CAP_CTX, D_MODEL), lambda b, t: (0, SEQ // CAP_CTX + b, 0)))
    ins.append(mod_l)
    in_specs.append(pl.BlockSpec((1, 1, 6 * D_MODEL), lambda b, t: (_mod_row(b, t), 0, 0)))
    if final:
        ins.append(final_norm)
        in_specs.append(pl.BlockSpec((1, D_MODEL), lambda b, t: (0, 0)))
    return pl.pallas_call(
        functools.partial(_combine_kernel, with_ctx=with_ctx, final=final),
        out_shape=jax.ShapeDtypeStruct((BATCH, n_rows, D_MODEL), F32),
        grid=(BATCH, n_tiles),
        in_specs=in_specs,
        out_specs=tok(D_MODEL),
        compiler_params=_params(("arbitrary", "arbitrary")),
        name="expert_combine",
    )(*ins)


def _rope_tables():
    t = jnp.arange(SEQ)
    pos = jnp.stack([t // GRID_W, t % GRID_W], axis=0)

    def table(n_f, lanes_per_unit):
        inv = ROPE_BASE ** (-jnp.arange(n_f, dtype=F32) / n_f)
        d = np.arange(lanes_per_unit)
        axis, second, f = d // (2 * n_f), (d // n_f) % 2, d % n_f
        ang = pos[axis].T.astype(F32) * inv[f][None, :]
        cos = jnp.cos(ang)
        sin = jnp.sin(ang) * jnp.where(second == 1, 1.0, -1.0)[None, :]
        return cos, sin

    def finish(cos, sin, scale):
        cos = jnp.concatenate([cos, jnp.ones((NT_PAD - SEQ, LANES), F32)], axis=0)
        sin = jnp.concatenate([sin, jnp.zeros((NT_PAD - SEQ, LANES), F32)], axis=0)
        scale = scale * LOG2_E
        return [cos, sin, cos * scale, sin * scale]

    cos_a, sin_a = table(HEAD_DIM // 4, HEAD_DIM)
    tabs_a = finish(jnp.tile(cos_a, (1, 2)), jnp.tile(sin_a, (1, 2)), HEAD_DIM ** -0.5)
    cos_8, sin_8 = table(B_ROPE // 4, B_ROPE)
    tabs_8 = finish(jnp.tile(cos_8, (1, 4)), jnp.tile(sin_8, (1, 4)), C_DH ** -0.5)
    ones = jnp.ones((SEQ, B_NOPE), F32)
    pad1 = jnp.ones((SEQ, LANES - B_NOPE - B_ROPE), F32)
    cos_b = jnp.concatenate([ones, cos_8, pad1], axis=1)
    sin_b = jnp.concatenate([0 * ones, sin_8, 0 * pad1], axis=1)
    tabs_b = finish(cos_b, sin_b, (B_NOPE + B_ROPE) ** -0.5)
    return tabs_a + tabs_8 + tabs_b


def _window_bias():
    qi = np.arange(WIN)[:, None]
    kj = np.arange(A_LOCAL)[None, :]
    out = np.zeros((3, WIN, A_LOCAL + CTX_LEN), np.float32)
    for v in range(3):
        out[v, :, :A_LOCAL] = np.where(np.abs(kj - v * WIN - qi) <= WIN, 0.0, NEG_INF)
    return jnp.asarray(out)


def _prep_weights(w_in, w_uq, w_ukv, w_out, w_router):
    offs = np.cumsum((0, 512, 128, 128, 256, 128, 32, 256, 256, 256))
    seg = lambda i: w_in[:, :, offs[i]:offs[i + 1]]
    head_order = np.array([[j, A_GROUP + j] for j in range(A_GROUP)]).reshape(-1)
    a_cols = (head_order[:, None] * HEAD_DIM + np.arange(HEAD_DIM)[None, :]).reshape(-1)
    kr = jnp.pad(seg(5), ((0, 0), (0, 0), (B_NOPE, LANES - B_NOPE - B_ROPE)))
    w = jnp.concatenate([seg(0)[:, :, a_cols], seg(1), seg(2), seg(3), seg(4), kr,
                         seg(6), seg(7), seg(8)], axis=-1).astype(BF16)
    wuq = w_uq.reshape(DEPTH, B_Q_RANK, B_HEADS, B_NOPE + B_ROPE)
    wuq = jnp.pad(wuq, ((0, 0), (0, 0), (0, 0), (0, LANES - B_NOPE - B_ROPE)))
    wuq = wuq.reshape(DEPTH, B_Q_RANK, B_HEADS * LANES).astype(BF16)
    wukv = w_ukv.reshape(DEPTH, B_KV_RANK, B_HEADS, B_NOPE + B_V)
    wk = jnp.pad(wukv[..., :B_NOPE], ((0, 0), (0, 0), (0, 0), (0, LANES - B_NOPE)))
    wukv = jnp.concatenate([wk.reshape(DEPTH, B_KV_RANK, B_HEADS * LANES),
                            wukv[..., B_NOPE:].reshape(DEPTH, B_KV_RANK, B_HEADS * B_V)],
                           axis=-1).astype(BF16)
    wo = jnp.concatenate([w_out[:, :512][:, a_cols], w_out[:, 512:]], axis=1).astype(BF16)
    wr = jnp.pad(w_router, ((0, 0), (0, 0), (0, LANES - N_EXPERTS)))
    return w, wuq, wukv, wo, wr


def kernel(x, c, ctx, c_ctx, norm1, norm2, w_ada, b_ada, w_in, sink, mla_q_norm, w_uq, mla_kv_norm, w_ukv,
           lam_q1, lam_k1, lam_q2, lam_k2, diff_norm, w_out, w_router, w_gate, w_up, w_down, final_norm):
    xall = jnp.concatenate([x, ctx, jnp.zeros((BATCH, NT_PAD - NT, D_MODEL), F32)], axis=1)
    cc = jnp.concatenate([c, c_ctx[None, :], jnp.zeros((16 - BATCH - 1, D_MODEL), F32)], axis=0)
    mod = _modulation(cc, w_ada, b_ada).reshape(DEPTH, 16, 1, 6 * D_MODEL)
    tabs = _rope_tables()
    bias = _window_bias()
    tri = jnp.asarray(np.triu(np.ones((LANES, LANES), np.float32), k=1), BF16)
    w, wuq, wukv, wo, wr = _prep_weights(w_in, w_uq, w_ukv, w_out, w_router)
    sub_norm = jnp.tile(diff_norm, (1, C_HEADS)).reshape(DEPTH, 1, C_HEADS * C_V)
    row = lambda a, l: a[l].reshape(1, -1)

    for l in range(DEPTH):
        need_ctx = l < DEPTH - 1
        qa, ka, va, qb, kb, vb, qd, kd, vd = _inproj(
            xall, mod[l], row(norm1, l), w[l], row(mla_q_norm, l), wuq[l], row(mla_kv_norm, l), wukv[l], tabs)

        oa = _attn_a_latent(sink[l], qa, ka, va, bias)
        ob = _full_attention(_attn_b_kernel, "attn_mla", qb, kb, vb, [], lambda n: [], 256, 512, 512, 512)
        lam_init = 0.8 - 0.6 * math.exp(-0.3 * l)
        c_kernel = functools.partial(_attn_c_kernel, lam_init=lam_init)
        c_extra = [row(lam_q1, l), row(lam_k1, l), row(lam_q2, l), row(lam_k2, l), sub_norm[l]]

        def c_specs(n_grid, c_extra=c_extra):
            zmap = (lambda b, t: (0, 0)) if n_grid == 2 else (lambda b: (0, 0))
            return [pl.BlockSpec(a.shape, zmap) for a in c_extra]

        od = _full_attention(c_kernel, "attn_diff", qd, kd, vd, c_extra, c_specs, 256, 256, 256, 512)
        if need_ctx:
            oa = _attn_a_ctx(sink[l], qa, ka, va, oa)
            ob = _full_attention_ctx(_attn_b_kernel, "attn_mla", qb, kb, vb, [], lambda n: [],
                                     256, 512, 512, 512, ob)
            od = _full_attention_ctx(c_kernel, "attn_diff", qd, kd, vd, c_extra, c_specs,
                                     256, 256, 256, 512, od)

        n_tiles = N_TILES if need_ctx else N_LAT_TILES
        xall, h2, aff = _outproj(xall, oa, ob, od, wo[l], mod[l], row(norm2, l), wr[l], n_tiles)
        slot_rows, slot_cols = _route(aff, tri, need_ctx)
        gathered = _gather(slot_rows.reshape(BATCH * N_EXPERTS, 1, NT), h2, need_ctx)
        y = _ffn(gathered[0], gathered[1] if need_ctx else None, w_gate, w_up, w_down, l)
        xall = _combine(xall, slot_cols, aff, y, mod[l], need_ctx,
                        final_norm=None if need_ctx else final_norm.reshape(1, D_MODEL))
    return xall
```

```python
import functools
import math

import numpy as np
import jax
import jax.numpy as jnp
from jax import lax
from jax.experimental import pallas as pl
from jax.experimental.pallas import tpu as pltpu

D_MODEL = 1024
BATCH = 8
SEQ = 2048
DEPTH = 4
CTX_LEN = 256
NT = SEQ + CTX_LEN
GRID_W = 64
HEAD_DIM = 64
ROPE_BASE = 10000.0
EPS = 1e-6
NEG_INF = -1e30
LOG2_E = math.log2(math.e)

WIN = 128
A_HEADS = 8
A_KV_HEADS = 2
A_GROUP = A_HEADS // A_KV_HEADS
B_HEADS = 4
B_NOPE = 64
B_ROPE = 32
B_V = 64
B_Q_RANK = 256
B_KV_RANK = 128
C_HEADS = 4
C_DH = 32
C_V = 2 * C_DH
N_EXPERTS = 16
EC_FACTOR = 2
D_EXPERT = 512
CAP_LAT = EC_FACTOR * SEQ // N_EXPERTS
CAP_CTX = EC_FACTOR * CTX_LEN // N_EXPERTS

LANES = 128
TOK_TILE = 256
NT_PAD = SEQ + TOK_TILE
N_TILES = NT_PAD // TOK_TILE
N_LAT_TILES = SEQ // TOK_TILE
STEP_CHUNKS = 3
LAT_STEP_CHUNKS = 2
FFN_TILE = 512
FFN_LAT_TILES = SEQ // FFN_TILE
VMEM_LIMIT = 56 * 1024 * 1024

F32 = jnp.float32
BF16 = jnp.bfloat16

_NT_DIMS = (((1,), (1,)), ((), ()))


def _params(sem):
    return pltpu.CompilerParams(dimension_semantics=sem, vmem_limit_bytes=VMEM_LIMIT)


def _dot(a, b):
    return jnp.dot(a, b, preferred_element_type=F32)


def _dot_nt(a, b):
    return lax.dot_general(a, b, _NT_DIMS, preferred_element_type=F32)


def _split_bf16(v):
    hi = v.astype(BF16)
    lo = (v - hi.astype(F32)).astype(BF16)
    return hi, lo


def _dot3(a, b):
    a_hi, a_lo = _split_bf16(a)
    b_hi, b_lo = _split_bf16(b)
    return _dot(a_hi, b_hi) + _dot(a_lo, b_hi) + _dot(a_hi, b_lo)


def _indicator(cond):
    return jnp.where(cond, 1.0, 0.0).astype(BF16)


def _rms(v):
    return v * lax.rsqrt(jnp.mean(v * v, axis=-1, keepdims=True) + EPS)


def _softmax_pv(s, v_blk, head):
    m = jnp.max(s, axis=-1, keepdims=True)
    ol = _dot(jnp.exp2(s - m).astype(BF16), v_blk)
    ones_lane = 64 if head % 2 == 0 else 0
    return ol / ol[:, ones_lane:ones_lane + 1]


def _merge_head_pairs(per_head):
    lane = lax.broadcasted_iota(jnp.int32, (1, LANES), 1)
    blocks = [jnp.where(lane < 64, per_head[j], per_head[j + 1]) for j in range(0, len(per_head), 2)]
    return jnp.concatenate(blocks, axis=-1)


def _softmax_rows(s):
    m = jnp.max(s, axis=-1, keepdims=True)
    e = jnp.exp(s - m)
    return e / jnp.sum(e, axis=-1, keepdims=True)


def _mod_kernel(c_ref, w_ref, b_ref, o_ref):
    c = c_ref[...]
    s = c / (1.0 + jnp.exp(-c))
    o_ref[0] = _dot3(s, w_ref[0]) + b_ref[0]


def _modulation(cc, w_ada, b_ada):
    tn = 1536
    return pl.pallas_call(
        _mod_kernel,
        out_shape=jax.ShapeDtypeStruct((DEPTH, 16, 6 * D_MODEL), F32),
        grid=(DEPTH, 6 * D_MODEL // tn),
        in_specs=[
            pl.BlockSpec((16, D_MODEL), lambda l, j: (0, 0)),
            pl.BlockSpec((1, D_MODEL, tn), lambda l, j: (l, 0, j)),
            pl.BlockSpec((1, 1, tn), lambda l, j: (l, 0, j)),
        ],
        out_specs=pl.BlockSpec((1, 16, tn), lambda l, j: (l, 0, j)),
        compiler_params=_params(("arbitrary", "arbitrary")),
        name="adaln_mod",
    )(cc, w_ada, b_ada.reshape(DEPTH, 1, 6 * D_MODEL))


def _rope(p, cos, sin, half):
    lane = lax.broadcasted_iota(jnp.int32, (1, LANES), 1)
    first = (lane // half) % 2 == 0
    outs = []
    for j in range(p.shape[1] // LANES):
        blk = p[:, j * LANES:(j + 1) * LANES]
        partner = jnp.where(first, pltpu.roll(blk, LANES - half, 1), pltpu.roll(blk, half, 1))
        outs.append(blk * cos + partner * sin)
    return outs[0] if len(outs) == 1 else jnp.concatenate(outs, axis=-1)


def _per_head_with_ones(v):
    lane = lax.broadcasted_iota(jnp.int32, (1, LANES), 1)
    outs = []
    for j in range(v.shape[1] // LANES):
        blk = v[:, j * LANES:(j + 1) * LANES]
        outs.append(jnp.where(lane < 64, blk, jnp.where(lane == 64, 1.0, 0.0)))
        outs.append(jnp.where(lane >= 64, blk, jnp.where(lane == 0, 1.0, 0.0)))
    return jnp.concatenate(outs, axis=-1)


def _chunk_mod(modb_ref, modc_ref, j, n_chunks):
    mod = modb_ref[0]
    if n_chunks == STEP_CHUNKS and j == n_chunks - 1:
        mod = jnp.where(pl.program_id(1) == N_TILES // STEP_CHUNKS - 1, modc_ref[0], mod)
    return mod


def _inproj_kernel(x_ref, modb_ref, modc_ref, g1_ref, w_ref, qn_ref, wuq_ref, kvn_ref, wukv_ref,
                   ca_ref, sa_ref, caq_ref, saq_ref, c8_ref, s8_ref, c8q_ref, s8q_ref,
                   cb_ref, sb_ref, cbq_ref, sbq_ref,
                   qa_ref, ka_ref, va_ref, qb_ref, kb_ref, vb_ref, qd_ref, kd_ref, vd_ref):
    for j in range(STEP_CHUNKS):
        rows = slice(j * TOK_TILE, (j + 1) * TOK_TILE)
        x = x_ref[0, rows, :]
        mod = _chunk_mod(modb_ref, modc_ref, j, STEP_CHUNKS)
        sh1 = mod[:, 0:D_MODEL]
        sc1 = mod[:, D_MODEL:2 * D_MODEL]
        h = (_rms(x) * g1_ref[...]) * (1.0 + sc1) + sh1
        p = _dot(h.astype(BF16), w_ref[...])

        qa_ref[0, rows, :] = _rope(p[:, 0:512], caq_ref[rows, :], saq_ref[rows, :], 16).astype(BF16)
        ka_ref[0, rows, :] = _rope(p[:, 512:640], ca_ref[rows, :], sa_ref[rows, :], 16).astype(BF16)
        lane = lax.broadcasted_iota(jnp.int32, (TOK_TILE, LANES), 1)
        va_ref[0, rows, :] = jnp.concatenate(
            [p[:, 640:768], jnp.where(lane == 0, 1.0, 0.0)], axis=-1).astype(BF16)

        cq = (_rms(p[:, 768:1024]) * qn_ref[...]).astype(BF16)
        qb = _dot(cq, wuq_ref[...])
        qb_ref[0, rows, :] = _rope(qb, cbq_ref[rows, :], sbq_ref[rows, :], 8).astype(BF16)
        ckv = (_rms(p[:, 1024:1152]) * kvn_ref[...]).astype(BF16)
        kv = _dot(ckv, wukv_ref[...])
        kr = _rope(p[:, 1152:1280], cb_ref[rows, :], sb_ref[rows, :], 8)
        kb_ref[0, rows, :] = (kv[:, 0:512] + jnp.concatenate([kr] * B_HEADS, axis=-1)).astype(BF16)
        vb_ref[0, rows, :] = _per_head_with_ones(kv[:, 512:768]).astype(BF16)

        qd_ref[0, rows, :] = _rope(p[:, 1280:1536], c8q_ref[rows, :], s8q_ref[rows, :], 8).astype(BF16)
        kd_ref[0, rows, :] = _rope(p[:, 1536:1792], c8_ref[rows, :], s8_ref[rows, :], 8).astype(BF16)
        vd_ref[0, rows, :] = _per_head_with_ones(p[:, 1792:2048]).astype(BF16)


def _mod_specs():
    return [pl.BlockSpec((1, 1, 6 * D_MODEL), lambda b, t: (b, 0, 0)),
            pl.BlockSpec((1, 1, 6 * D_MODEL), lambda b, t: (BATCH, 0, 0))]


def _inproj(xall, mod_l, g1, w, qn, wuq, kvn, wukv, tabs):
    full = lambda shape: pl.BlockSpec(shape, lambda b, t: (0,) * len(shape))
    step_rows = STEP_CHUNKS * TOK_TILE
    tab_spec = pl.BlockSpec((step_rows, LANES), lambda b, t: (t, 0))
    widths = (512, 128, 256, 512, 512, 512, 256, 256, 512)
    return pl.pallas_call(
        _inproj_kernel,
        out_shape=[jax.ShapeDtypeStruct((BATCH, NT_PAD, n), BF16) for n in widths],
        grid=(BATCH, N_TILES // STEP_CHUNKS),
        in_specs=[pl.BlockSpec((1, step_rows, D_MODEL), lambda b, t: (b, t, 0))] + _mod_specs() + [
            full((1, D_MODEL)),
            full((D_MODEL, 2048)),
            full((1, B_Q_RANK)),
            full((B_Q_RANK, 512)),
            full((1, B_KV_RANK)),
            full((B_KV_RANK, 768)),
        ] + [tab_spec] * 12,
        out_specs=[pl.BlockSpec((1, step_rows, n), lambda b, t: (b, t, 0)) for n in widths],
        compiler_params=_params(("arbitrary", "arbitrary")),
        name="in_projection",
    )(xall, mod_l, mod_l, g1, w, qn, wuq, kvn, wukv, *tabs)


def _window_heads(q, kcat, vcat, bias, sink_ref):
    tq = q.shape[0]
    lane = lax.broadcasted_iota(jnp.int32, (1, LANES), 1)
    acc = [jnp.zeros((tq, LANES), F32) for _ in range(A_GROUP)]
    for g in range(A_KV_HEADS):
        in_half = (lane < HEAD_DIM) if g == 0 else (lane >= HEAD_DIM)
        keep = _indicator(in_half)
        qs = jnp.concatenate([q[:, j * LANES:(j + 1) * LANES] * keep for j in range(A_GROUP)], axis=0)
        s = _dot_nt(qs, kcat)
        if bias is not None:
            s = s + jnp.concatenate([bias] * A_GROUP, axis=0)
        sink = jnp.concatenate(
            [jnp.full((tq, 1), sink_ref[g * A_GROUP + j] * LOG2_E, F32) for j in range(A_GROUP)], axis=0)
        m = jnp.maximum(jnp.max(s, axis=-1, keepdims=True), sink)
        ol = _dot(jnp.exp2(s - m).astype(BF16), vcat)
        o = ol[:, 0:LANES] / (ol[:, LANES:LANES + 1] + jnp.exp2(sink - m))
        for j in range(A_GROUP):
            acc[j] = acc[j] + jnp.where(in_half, o[j * tq:(j + 1) * tq], 0.0)
    return jnp.concatenate(acc, axis=-1).astype(BF16)


A_STEP_BLOCKS = 4
A_LOCAL = 3 * WIN


def _attn_a_latent_kernel(sink_ref, q_ref, k_ref, v_ref, bias_ref, o_ref):
    step = pl.program_id(1)
    last = SEQ // WIN - 1
    k_ctx = k_ref[0, SEQ:NT, :]
    v_ctx = v_ref[0, SEQ:NT, :]
    for j in range(A_STEP_BLOCKS):
        n = step * A_STEP_BLOCKS + j
        start = pl.multiple_of(jnp.clip((n - 1) * WIN, 0, SEQ - A_LOCAL), WIN)
        variant = jnp.where(n == 0, 0, jnp.where(n == last, 2, 1))
        kcat = jnp.concatenate([k_ref[0, pl.ds(start, A_LOCAL), :], k_ctx], axis=0)
        vcat = jnp.concatenate([v_ref[0, pl.ds(start, A_LOCAL), :], v_ctx], axis=0)
        rows = slice(j * WIN, (j + 1) * WIN)
        o_ref[0, rows, :] = _window_heads(q_ref[0, rows, :], kcat, vcat, bias_ref[variant], sink_ref)


def _attn_a_latent(sink, qa, ka, va, bias):
    tq = A_STEP_BLOCKS * WIN
    return pl.pallas_call(
        _attn_a_latent_kernel,
        out_shape=jax.ShapeDtypeStruct((BATCH, NT_PAD, 512), BF16),
        grid=(BATCH, SEQ // tq),
        in_specs=[pl.BlockSpec(memory_space=pltpu.SMEM),
                  pl.BlockSpec((1, tq, 512), lambda b, n: (b, n, 0)),
                  pl.BlockSpec((1, NT_PAD, LANES), lambda b, n: (b, 0, 0)),
                  pl.BlockSpec((1, NT_PAD, 2 * LANES), lambda b, n: (b, 0, 0)),
                  pl.BlockSpec((3, WIN, A_LOCAL + CTX_LEN), lambda b, n: (0, 0, 0))],
        out_specs=pl.BlockSpec((1, tq, 512), lambda b, n: (b, n, 0)),
        compiler_params=_params(("arbitrary", "arbitrary")),
        name="attn_window_latent",
    )(sink, qa, ka, va, bias)


def _attn_a_ctx(sink, qa, ka, va, prev_out):
    ctx_blk = SEQ // CTX_LEN
    spec = lambda n: pl.BlockSpec((1, CTX_LEN, n), lambda b: (b, ctx_blk, 0))
    tile = lambda n: pl.BlockSpec((1, TOK_TILE, n), lambda b: (b, N_LAT_TILES, 0))

    def body(sink_ref, q_ref, k_ref, v_ref, prev_ref, o_ref):
        del prev_ref
        o_ref[0] = _window_heads(q_ref[0], k_ref[0], v_ref[0], None, sink_ref)

    return pl.pallas_call(
        body,
        out_shape=jax.ShapeDtypeStruct((BATCH, NT_PAD, 512), BF16),
        grid=(BATCH,),
        in_specs=[pl.BlockSpec(memory_space=pltpu.SMEM), tile(512), spec(LANES), spec(2 * LANES),
                  pl.BlockSpec(memory_space=pl.ANY)],
        out_specs=tile(512),
        input_output_aliases={4: 0},
        compiler_params=_params(("arbitrary",)),
        name="attn_window_ctx",
    )(sink, qa, ka, va, prev_out)


def _key_rows(ref):
    return ref[0, 0:NT, :] if ref.shape[1] == NT_PAD else ref[0]


ATT_TQ = 1024
ATT_SUB = 256


def _attn_b_kernel(q_ref, k_ref, v_ref, o_ref):
    k = _key_rows(k_ref)
    v = _key_rows(v_ref)
    for sub in range(q_ref.shape[1] // ATT_SUB):
        rows = slice(sub * ATT_SUB, (sub + 1) * ATT_SUB)
        q = q_ref[0, rows, :]
        outs = []
        for h in range(B_HEADS):
            s = _dot_nt(q[:, h * LANES:(h + 1) * LANES], k[:, h * LANES:(h + 1) * LANES])
            outs.append(_softmax_pv(s, v[:, h * LANES:(h + 1) * LANES], h))
        o_ref[0, rows, :] = _merge_head_pairs(outs).astype(BF16)


def _full_attention(kernel, name, q, k, v, extra, extra_specs, out_width, q_width, k_width, v_width):
    out = pl.pallas_call(
        kernel,
        out_shape=jax.ShapeDtypeStruct((BATCH, NT_PAD, out_width), BF16),
        grid=(BATCH, SEQ // ATT_TQ),
        in_specs=[pl.BlockSpec((1, ATT_TQ, q_width), lambda b, t: (b, t, 0)),
                  pl.BlockSpec((1, NT_PAD, k_width), lambda b, t: (b, 0, 0)),
                  pl.BlockSpec((1, NT_PAD, v_width), lambda b, t: (b, 0, 0))] + extra_specs(2),
        out_specs=pl.BlockSpec((1, ATT_TQ, out_width), lambda b, t: (b, t, 0)),
        compiler_params=_params(("arbitrary", "arbitrary")),
        name=name + "_latent",
    )(q, k, v, *extra)
    return out


def _full_attention_ctx(kernel, name, q, k, v, extra, extra_specs, out_width, q_width, k_width, v_width,
                        prev_out):
    ctx_blk = SEQ // CTX_LEN
    n_extra = len(extra)

    def body(*refs):
        ins = refs[:3 + n_extra]
        kernel(*ins, refs[-1])

    spec = lambda n: pl.BlockSpec((1, CTX_LEN, n), lambda b: (b, ctx_blk, 0))
    tile = lambda n: pl.BlockSpec((1, TOK_TILE, n), lambda b: (b, N_LAT_TILES, 0))
    return pl.pallas_call(
        body,
        out_shape=jax.ShapeDtypeStruct((BATCH, NT_PAD, out_width), BF16),
        grid=(BATCH,),
        in_specs=[tile(q_width), spec(k_width), spec(v_width)] + extra_specs(1)
                 + [pl.BlockSpec(memory_space=pl.ANY)],
        out_specs=tile(out_width),
        input_output_aliases={3 + n_extra: 0},
        compiler_params=_params(("arbitrary",)),
        name=name + "_ctx",
    )(q, k, v, *extra, prev_out)


def _attn_c_kernel(q_ref, k_ref, v_ref, lq1_ref, lk1_ref, lq2_ref, lk2_ref, sn_ref, o_ref, *, lam_init):
    k = _key_rows(k_ref)
    v = _key_rows(v_ref)
    tq = ATT_SUB
    lam = (jnp.exp(jnp.sum(lq1_ref[...] * lk1_ref[...], axis=-1, keepdims=True))
           - jnp.exp(jnp.sum(lq2_ref[...] * lk2_ref[...], axis=-1, keepdims=True)) + lam_init)
    lane = lax.broadcasted_iota(jnp.int32, (1, C_HEADS * C_V), 1)
    for sub in range(q_ref.shape[1] // ATT_SUB):
        rows = slice(sub * ATT_SUB, (sub + 1) * ATT_SUB)
        q = q_ref[0, rows, :]
        outs = []
        for h in range(C_HEADS):
            lo = h * C_V
            q1 = q * _indicator((lane >= lo) & (lane < lo + C_DH))
            q2 = q * _indicator((lane >= lo + C_DH) & (lane < lo + C_V))
            s = _dot_nt(jnp.concatenate([q1, q2], axis=0), k)
            pv = _softmax_pv(s, v[:, h * LANES:(h + 1) * LANES], h)
            outs.append(pv[0:tq] - lam * pv[tq:2 * tq])
        acc = _merge_head_pairs(outs)
        sq = acc * acc
        inv = jnp.zeros((tq, C_HEADS * C_V), F32)
        for h in range(C_HEADS):
            in_head = (lane >= h * C_V) & (lane < (h + 1) * C_V)
            ms = jnp.sum(jnp.where(in_head, sq, 0.0), axis=-1, keepdims=True) * (1.0 / C_V)
            inv = inv + jnp.where(in_head, lax.rsqrt(ms + EPS), 0.0)
        o_ref[0, rows, :] = ((acc * inv) * sn_ref[...] * (1.0 - lam_init)).astype(BF16)


def _outproj_kernel(x_ref, oa_ref, ob_ref, od_ref, w_ref, modb_ref, modc_ref, g2_ref, wr_ref,
                    xo_ref, h2_ref, aff_ref, *, n_chunks):
    w_hi, w_lo = _split_bf16(wr_ref[...])
    w_router = jnp.concatenate([w_hi, w_lo], axis=-1)
    lane = lax.broadcasted_iota(jnp.int32, (1, LANES), 1)
    for j in range(n_chunks):
        rows = slice(j * TOK_TILE, (j + 1) * TOK_TILE)
        mod = _chunk_mod(modb_ref, modc_ref, j, n_chunks)
        g1 = mod[:, 2 * D_MODEL:3 * D_MODEL]
        sh2 = mod[:, 3 * D_MODEL:4 * D_MODEL]
        sc2 = mod[:, 4 * D_MODEL:5 * D_MODEL]
        mix = jnp.concatenate([oa_ref[0, rows, :], ob_ref[0, rows, :], od_ref[0, rows, :]], axis=-1)
        x = x_ref[0, rows, :] + g1 * _dot(mix, w_ref[...])
        xo_ref[0, rows, :] = x
        h2 = (_rms(x) * g2_ref[...]) * (1.0 + sc2) + sh2
        h_hi, h_lo = _split_bf16(h2)
        h2_ref[0, rows, :] = h_hi
        prod = _dot(jnp.concatenate([h_hi, h_lo], axis=0), w_router)
        logits = (prod[0:TOK_TILE, 0:LANES] + prod[0:TOK_TILE, LANES:2 * LANES]
                  + prod[TOK_TILE:2 * TOK_TILE, 0:LANES])
        aff_ref[0, rows, :] = _softmax_rows(jnp.where(lane < N_EXPERTS, logits, NEG_INF))


def _outproj(xall, oa, ob, od, w, mod_l, g2, wr, with_ctx):
    n_chunks = STEP_CHUNKS if with_ctx else LAT_STEP_CHUNKS
    n_steps = (N_TILES if with_ctx else N_LAT_TILES) // n_chunks
    full = lambda shape: pl.BlockSpec(shape, lambda b, t: (0,) * len(shape))
    tok = lambda n: pl.BlockSpec((1, n_chunks * TOK_TILE, n), lambda b, t: (b, t, 0))
    return pl.pallas_call(
        functools.partial(_outproj_kernel, n_chunks=n_chunks),
        out_shape=[jax.ShapeDtypeStruct((BATCH, NT_PAD, D_MODEL), F32),
                   jax.ShapeDtypeStruct((BATCH, NT_PAD, D_MODEL), BF16),
                   jax.ShapeDtypeStruct((BATCH, NT_PAD, LANES), F32)],
        grid=(BATCH, n_steps),
        in_specs=[tok(D_MODEL), tok(512), tok(256), tok(256), full((D_MODEL, D_MODEL))] + _mod_specs()
                 + [full((1, D_MODEL)), full((D_MODEL, LANES))],
        out_specs=[tok(D_MODEL), tok(D_MODEL), tok(LANES)],
        compiler_params=_params(("arbitrary", "arbitrary")),
        name="out_projection",
    )(xall, oa, ob, od, w, mod_l, mod_l, g2, wr)


def _prefix_count(flags_f32, tri_ref):
    rows, n = flags_f32.shape
    run = jnp.zeros((rows, 1), F32)
    outs = []
    for c in range(n // LANES):
        blk = flags_f32[:, c * LANES:(c + 1) * LANES]
        outs.append(_dot(blk.astype(BF16), tri_ref[...]) + run)
        run = run + jnp.sum(blk, axis=-1, keepdims=True)
    return jnp.concatenate(outs, axis=-1)


def _select_slots(a, cap, tri_ref):
    bits = pltpu.bitcast(a, jnp.int32)

    def step(i, t):
        cand = t | jnp.left_shift(jnp.int32(1), 30 - i)
        cnt = jnp.sum((bits >= cand).astype(jnp.int32), axis=-1, keepdims=True)
        return jnp.where(cnt >= cap, cand, t)

    thr = lax.fori_loop(0, 31, step, jnp.zeros((a.shape[0], 1), jnp.int32))
    above = bits > thr
    tied = bits == thr
    need = cap - jnp.sum(above.astype(F32), axis=-1, keepdims=True)
    tied_f = tied.astype(F32)
    sel = above | (tied & (_prefix_count(tied_f, tri_ref) < need))
    sel_f = sel.astype(F32)
    return jnp.where(sel, _prefix_count(sel_f, tri_ref), -1.0)


def _route_kernel(aff_ref, tri_ref, row_ref, col_ref, *, with_ctx):
    lat, ctx = [], []
    for b in range(BATCH):
        lat.append(aff_ref[b, 0:SEQ, :].T[0:N_EXPERTS])
        if with_ctx:
            ctx.append(aff_ref[b, SEQ:NT, :].T[0:N_EXPERTS])
    slot_lat = _select_slots(jnp.concatenate(lat, axis=0), CAP_LAT, tri_ref)
    if with_ctx:
        slot_ctx = _select_slots(jnp.concatenate(ctx, axis=0), CAP_CTX, tri_ref)
        slots = jnp.concatenate([slot_lat, slot_ctx], axis=-1)
    else:
        slots = slot_lat
    n = slots.shape[1]
    row_ref[:, 0:n] = slots
    pad = jnp.full((LANES - N_EXPERTS, n), -1.0, F32)
    for b in range(BATCH):
        blk = jnp.concatenate([slots[b * N_EXPERTS:(b + 1) * N_EXPERTS], pad], axis=0)
        col_ref[b, 0:n, :] = blk.T
        if with_ctx and NT_PAD > NT:
            col_ref[b, NT:NT_PAD, :] = jnp.full((NT_PAD - NT, LANES), -1.0, F32)


def _route(aff, tri, with_ctx):
    vm = pl.BlockSpec(memory_space=pltpu.VMEM)
    return pl.pallas_call(
        functools.partial(_route_kernel, with_ctx=with_ctx),
        out_shape=[jax.ShapeDtypeStruct((BATCH * N_EXPERTS, NT), F32),
                   jax.ShapeDtypeStruct((BATCH, NT_PAD, LANES), F32)],
        in_specs=[vm, vm],
        out_specs=[vm, vm],
        compiler_params=pltpu.CompilerParams(vmem_limit_bytes=VMEM_LIMIT),
        name="route_topc",
    )(aff, tri)


def _gather_kernel(row_ref, h_ref, *out_refs, with_ctx):
    slot = lax.broadcasted_iota(jnp.int32, (CAP_LAT, 1), 0).astype(F32)
    for e in range(N_EXPERTS):
        p = _indicator(row_ref[e, :, 0:SEQ] == slot)
        out_refs[0][e] = _dot(p, h_ref[0, 0:SEQ, :]).astype(BF16)
        if with_ctx:
            pc = _indicator(row_ref[e, :, SEQ:NT] == slot[0:CAP_CTX])
            out_refs[1][e] = _dot(pc, h_ref[0, SEQ:NT, :]).astype(BF16)


def _gather(slot_rows, h2, with_ctx):
    out_shape = [jax.ShapeDtypeStruct((N_EXPERTS, SEQ, D_MODEL), BF16)]
    out_specs = [pl.BlockSpec((N_EXPERTS, CAP_LAT, D_MODEL), lambda b: (0, b, 0))]
    if with_ctx:
        out_shape.append(jax.ShapeDtypeStruct((N_EXPERTS, CTX_LEN, D_MODEL), BF16))
        out_specs.append(pl.BlockSpec((N_EXPERTS, CAP_CTX, D_MODEL), lambda b: (0, b, 0)))
    return pl.pallas_call(
        functools.partial(_gather_kernel, with_ctx=with_ctx),
        out_shape=out_shape,
        grid=(BATCH,),
        in_specs=[pl.BlockSpec((N_EXPERTS, 1, NT), lambda b: (b, 0, 0)),
                  pl.BlockSpec((1, NT_PAD, D_MODEL), lambda b: (b, 0, 0))],
        out_specs=out_specs,
        compiler_params=_params(("arbitrary",)),
        name="expert_gather",
    )(slot_rows, h2)


def _ffn_kernel(*refs, with_ctx):
    if with_ctx:
        xl_ref, xc_ref, wg_ref, wu_ref, wd_ref, y_ref = refs
    else:
        xl_ref, wg_ref, wu_ref, wd_ref, y_ref = refs
    wg = wg_ref[0, 0].astype(BF16)
    wu = wu_ref[0, 0].astype(BF16)
    wd = wd_ref[0, 0].astype(BF16)

    def swiglu(x):
        gate = _dot(x, wg)
        hid = (gate / (1.0 + jnp.exp(-gate))) * _dot(x, wu)
        return _dot(hid.astype(BF16), wd).astype(BF16)

    for r in range(FFN_LAT_TILES):
        rows = slice(r * FFN_TILE, (r + 1) * FFN_TILE)
        y_ref[0, rows, :] = swiglu(xl_ref[0, rows, :])
    if with_ctx:
        y_ref[0, SEQ:NT, :] = swiglu(xc_ref[0])


def _ffn(xg_lat, xg_ctx, w_gate, w_up, w_down, layer):
    with_ctx = xg_ctx is not None
    ins = [xg_lat] + ([xg_ctx] if with_ctx else []) + [w_gate, w_up, w_down]
    in_specs = [pl.BlockSpec((1, SEQ, D_MODEL), lambda e: (e, 0, 0))]
    if with_ctx:
        in_specs.append(pl.BlockSpec((1, CTX_LEN, D_MODEL), lambda e: (e, 0, 0)))
    in_specs += [pl.BlockSpec((1, 1, D_MODEL, D_EXPERT), lambda e: (layer, e, 0, 0)),
                 pl.BlockSpec((1, 1, D_MODEL, D_EXPERT), lambda e: (layer, e, 0, 0)),
                 pl.BlockSpec((1, 1, D_EXPERT, D_MODEL), lambda e: (layer, e, 0, 0))]
    return pl.pallas_call(
        functools.partial(_ffn_kernel, with_ctx=with_ctx),
        out_shape=jax.ShapeDtypeStruct((N_EXPERTS, NT, D_MODEL), BF16),
        grid=(N_EXPERTS,),
        in_specs=in_specs,
        out_specs=pl.BlockSpec((1, NT, D_MODEL), lambda e: (e, 0, 0)),
        compiler_params=_params(("arbitrary",)),
        name="expert_ffn",
    )(*ins)


def _combine_kernel(*refs, with_ctx, final):
    if with_ctx:
        x_ref, col_ref, aff_ref, yl_ref, yc_ref, modb_ref, modc_ref, o_ref = refs
    elif final:
        x_ref, col_ref, aff_ref, yl_ref, modb_ref, modc_ref, fn_ref, o_ref = refs
    else:
        x_ref, col_ref, aff_ref, yl_ref, modb_ref, modc_ref, o_ref = refs
    n_chunks = STEP_CHUNKS if with_ctx else LAT_STEP_CHUNKS

    def chunk(j, is_context):
        rows = slice(j * TOK_TILE, (j + 1) * TOK_TILE)
        g2 = _chunk_mod(modb_ref, modc_ref, j, n_chunks)[:, 5 * D_MODEL:6 * D_MODEL]
        col = col_ref[0, rows, :]
        aff = aff_ref[0, rows, :]
        n_slots = LANES if is_context else CAP_LAT
        lane = lax.broadcasted_iota(jnp.int32, (1, n_slots), 1).astype(F32)
        pad = jnp.zeros((LANES - CAP_CTX, D_MODEL), BF16)
        acc = jnp.zeros((TOK_TILE, D_MODEL), F32)
        for e in range(N_EXPERTS):
            pt = _indicator(col[:, e:e + 1] == lane)
            y = jnp.concatenate([yc_ref[e], pad], axis=0) if is_context else yl_ref[e]
            acc = acc + aff[:, e:e + 1] * _dot(pt, y)
        x = x_ref[0, rows, :] + g2 * acc
        if final:
            x = _rms(x) * fn_ref[...]
        o_ref[0, rows, :] = x

    for j in range(n_chunks):
        if with_ctx and j == n_chunks - 1:
            is_last = pl.program_id(1) == N_TILES // STEP_CHUNKS - 1
            pl.when(jnp.logical_not(is_last))(functools.partial(chunk, j, False))
            pl.when(is_last)(functools.partial(chunk, j, True))
        else:
            chunk(j, False)


def _combine(xall, slot_cols, aff, y, mod_l, with_ctx, final_norm=None):
    final = final_norm is not None
    n_chunks = STEP_CHUNKS if with_ctx else LAT_STEP_CHUNKS
    n_steps = (N_TILES if with_ctx else N_LAT_TILES) // n_chunks
    n_rows = SEQ if final else NT_PAD
    tok = lambda n: pl.BlockSpec((1, n_chunks * TOK_TILE, n), lambda b, t: (b, t, 0))
    ins = [xall, slot_cols, aff, y]
    in_specs = [tok(D_MODEL), tok(LANES), tok(LANES),
                pl.BlockSpec((N_EXPERTS, CAP_LAT, D_MODEL), lambda b, t: (0, b, 0))]
    if with_ctx:
        ins.append(y)
        in_specs.append(pl.BlockSpec((N_EXPERTS, CAP_CTX, D_MODEL), lambda b, t: (0, SEQ // CAP_CTX + b, 0)))
    ins += [mod_l, mod_l]
    in_specs += _mod_specs()
    if final:
        ins.append(final_norm)
        in_specs.append(pl.BlockSpec((1, D_MODEL), lambda b, t: (0, 0)))
    return pl.pallas_call(
        functools.partial(_combine_kernel, with_ctx=with_ctx, final=final),
        out_shape=jax.ShapeDtypeStruct((BATCH, n_rows, D_MODEL), F32),
        grid=(BATCH, n_steps),
        in_specs=in_specs,
        out_specs=tok(D_MODEL),
        compiler_params=_params(("arbitrary", "arbitrary")),
        name="expert_combine",
    )(*ins)


def _rope_tables():
    t = jnp.arange(SEQ)
    pos = jnp.stack([t // GRID_W, t % GRID_W], axis=0)

    def table(n_f, lanes_per_unit):
        inv = ROPE_BASE ** (-jnp.arange(n_f, dtype=F32) / n_f)
        d = np.arange(lanes_per_unit)
        axis, second, f = d // (2 * n_f), (d // n_f) % 2, d % n_f
        ang = pos[axis].T.astype(F32) * inv[f][None, :]
        cos = jnp.cos(ang)
        sin = jnp.sin(ang) * jnp.where(second == 1, 1.0, -1.0)[None, :]
        return cos, sin

    def finish(cos, sin, scale):
        cos = jnp.concatenate([cos, jnp.ones((NT_PAD - SEQ, LANES), F32)], axis=0)
        sin = jnp.concatenate([sin, jnp.zeros((NT_PAD - SEQ, LANES), F32)], axis=0)
        scale = scale * LOG2_E
        return [cos, sin, cos * scale, sin * scale]

    cos_a, sin_a = table(HEAD_DIM // 4, HEAD_DIM)
    tabs_a = finish(jnp.tile(cos_a, (1, 2)), jnp.tile(sin_a, (1, 2)), HEAD_DIM ** -0.5)
    cos_8, sin_8 = table(B_ROPE // 4, B_ROPE)
    tabs_8 = finish(jnp.tile(cos_8, (1, 4)), jnp.tile(sin_8, (1, 4)), C_DH ** -0.5)
    ones = jnp.ones((SEQ, B_NOPE), F32)
    pad1 = jnp.ones((SEQ, LANES - B_NOPE - B_ROPE), F32)
    cos_b = jnp.concatenate([ones, cos_8, pad1], axis=1)
    sin_b = jnp.concatenate([0 * ones, sin_8, 0 * pad1], axis=1)
    tabs_b = finish(cos_b, sin_b, (B_NOPE + B_ROPE) ** -0.5)
    return tabs_a + tabs_8 + tabs_b


def _window_bias():
    qi = np.arange(WIN)[:, None]
    kj = np.arange(A_LOCAL)[None, :]
    out = np.zeros((3, WIN, A_LOCAL + CTX_LEN), np.float32)
    for v in range(3):
        out[v, :, :A_LOCAL] = np.where(np.abs(kj - v * WIN - qi) <= WIN, 0.0, NEG_INF)
    return jnp.asarray(out)


def _prep_weights(w_in, w_uq, w_ukv, w_out, w_router):
    offs = np.cumsum((0, 512, 128, 128, 256, 128, 32, 256, 256, 256))
    seg = lambda i: w_in[:, :, offs[i]:offs[i + 1]]
    head_order = np.array([[j, A_GROUP + j] for j in range(A_GROUP)]).reshape(-1)
    a_cols = (head_order[:, None] * HEAD_DIM + np.arange(HEAD_DIM)[None, :]).reshape(-1)
    kr = jnp.pad(seg(5), ((0, 0), (0, 0), (B_NOPE, LANES - B_NOPE - B_ROPE)))
    w = jnp.concatenate([seg(0)[:, :, a_cols], seg(1), seg(2), seg(3), seg(4), kr,
                         seg(6), seg(7), seg(8)], axis=-1).astype(BF16)
    wuq = w_uq.reshape(DEPTH, B_Q_RANK, B_HEADS, B_NOPE + B_ROPE)
    wuq = jnp.pad(wuq, ((0, 0), (0, 0), (0, 0), (0, LANES - B_NOPE - B_ROPE)))
    wuq = wuq.reshape(DEPTH, B_Q_RANK, B_HEADS * LANES).astype(BF16)
    wukv = w_ukv.reshape(DEPTH, B_KV_RANK, B_HEADS, B_NOPE + B_V)
    wk = jnp.pad(wukv[..., :B_NOPE], ((0, 0), (0, 0), (0, 0), (0, LANES - B_NOPE)))
    wukv = jnp.concatenate([wk.reshape(DEPTH, B_KV_RANK, B_HEADS * LANES),
                            wukv[..., B_NOPE:].reshape(DEPTH, B_KV_RANK, B_HEADS * B_V)],
                           axis=-1).astype(BF16)
    wo = jnp.concatenate([w_out[:, :512][:, a_cols], w_out[:, 512:]], axis=1).astype(BF16)
    wr = jnp.pad(w_router, ((0, 0), (0, 0), (0, LANES - N_EXPERTS)))
    return w, wuq, wukv, wo, wr


def kernel(x, c, ctx, c_ctx, norm1, norm2, w_ada, b_ada, w_in, sink, mla_q_norm, w_uq, mla_kv_norm, w_ukv,
           lam_q1, lam_k1, lam_q2, lam_k2, diff_norm, w_out, w_router, w_gate, w_up, w_down, final_norm):
    xall = jnp.concatenate([x, ctx, jnp.zeros((BATCH, NT_PAD - NT, D_MODEL), F32)], axis=1)
    cc = jnp.concatenate([c, c_ctx[None, :], jnp.zeros((16 - BATCH - 1, D_MODEL), F32)], axis=0)
    mod = _modulation(cc, w_ada, b_ada).reshape(DEPTH, 16, 1, 6 * D_MODEL)
    tabs = _rope_tables()
    bias = _window_bias()
    tri = jnp.asarray(np.triu(np.ones((LANES, LANES), np.float32), k=1), BF16)
    w, wuq, wukv, wo, wr = _prep_weights(w_in, w_uq, w_ukv, w_out, w_router)
    sub_norm = jnp.tile(diff_norm, (1, C_HEADS)).reshape(DEPTH, 1, C_HEADS * C_V)
    row = lambda a, l: a[l].reshape(1, -1)

    for l in range(DEPTH):
        need_ctx = l < DEPTH - 1
        qa, ka, va, qb, kb, vb, qd, kd, vd = _inproj(
            xall, mod[l], row(norm1, l), w[l], row(mla_q_norm, l), wuq[l], row(mla_kv_norm, l), wukv[l], tabs)

        oa = _attn_a_latent(sink[l], qa, ka, va, bias)
        ob = _full_attention(_attn_b_kernel, "attn_mla", qb, kb, vb, [], lambda n: [], 256, 512, 512, 512)
        lam_init = 0.8 - 0.6 * math.exp(-0.3 * l)
        c_kernel = functools.partial(_attn_c_kernel, lam_init=lam_init)
        c_extra = [row(lam_q1, l), row(lam_k1, l), row(lam_q2, l), row(lam_k2, l), sub_norm[l]]

        def c_specs(n_grid, c_extra=c_extra):
            zmap = (lambda b, t: (0, 0)) if n_grid == 2 else (lambda b: (0, 0))
            return [pl.BlockSpec(a.shape, zmap) for a in c_extra]

        od = _full_attention(c_kernel, "attn_diff", qd, kd, vd, c_extra, c_specs, 256, 256, 256, 512)
        if need_ctx:
            oa = _attn_a_ctx(sink[l], qa, ka, va, oa)
            ob = _full_attention_ctx(_attn_b_kernel, "attn_mla", qb, kb, vb, [], lambda n: [],
                                     256, 512, 512, 512, ob)
            od = _full_attention_ctx(c_kernel, "attn_diff", qd, kd, vd, c_extra, c_specs,
                                     256, 256, 256, 512, od)

        xall, h2, aff = _outproj(xall, oa, ob, od, wo[l], mod[l], row(norm2, l), wr[l], need_ctx)
        slot_rows, slot_cols = _route(aff, tri, need_ctx)
        gathered = _gather(slot_rows.reshape(BATCH * N_EXPERTS, 1, NT), h2, need_ctx)
        y = _ffn(gathered[0], gathered[1] if need_ctx else None, w_gate, w_up, w_down, l)
        xall = _combine(xall, slot_cols, aff, y, mod[l], need_ctx,
                        final_norm=None if need_ctx else final_norm.reshape(1, D_MODEL))
    return xall
```

```python
import functools
import math

import numpy as np
import jax
import jax.numpy as jnp
from jax import lax
from jax.experimental import pallas as pl
from jax.experimental.pallas import tpu as pltpu

D_MODEL = 1024
BATCH = 8
SEQ = 2048
DEPTH = 4
CTX_LEN = 256
NT = SEQ + CTX_LEN
GRID_W = 64
HEAD_DIM = 64
ROPE_BASE = 10000.0
EPS = 1e-6
NEG_INF = -1e30
LOG2_E = math.log2(math.e)

WIN = 128
A_HEADS = 8
A_KV_HEADS = 2
A_GROUP = A_HEADS // A_KV_HEADS
B_HEADS = 4
B_NOPE = 64
B_ROPE = 32
B_V = 64
B_Q_RANK = 256
B_KV_RANK = 128
C_HEADS = 4
C_DH = 32
C_V = 2 * C_DH
N_EXPERTS = 16
EC_FACTOR = 2
D_EXPERT = 512
CAP_LAT = EC_FACTOR * SEQ // N_EXPERTS
CAP_CTX = EC_FACTOR * CTX_LEN // N_EXPERTS

LANES = 128
TOK_TILE = 256
NT_PAD = SEQ + TOK_TILE
N_TILES = NT_PAD // TOK_TILE
N_LAT_TILES = SEQ // TOK_TILE
STEP_CHUNKS = 3
LAT_STEP_CHUNKS = 2
FFN_TILE = 512
FFN_LAT_TILES = SEQ // FFN_TILE
VMEM_LIMIT = 56 * 1024 * 1024

F32 = jnp.float32
BF16 = jnp.bfloat16

_NT_DIMS = (((1,), (1,)), ((), ()))


def _params(sem):
    return pltpu.CompilerParams(dimension_semantics=sem, vmem_limit_bytes=VMEM_LIMIT)


def _dot(a, b):
    return jnp.dot(a, b, preferred_element_type=F32)


def _dot_nt(a, b):
    return lax.dot_general(a, b, _NT_DIMS, preferred_element_type=F32)


def _split_bf16(v):
    hi = v.astype(BF16)
    lo = (v - hi.astype(F32)).astype(BF16)
    return hi, lo


def _dot3(a, b):
    a_hi, a_lo = _split_bf16(a)
    b_hi, b_lo = _split_bf16(b)
    return _dot(a_hi, b_hi) + _dot(a_lo, b_hi) + _dot(a_hi, b_lo)


def _indicator(cond):
    return jnp.where(cond, 1.0, 0.0).astype(BF16)


def _rms(v):
    return v * lax.rsqrt(jnp.mean(v * v, axis=-1, keepdims=True) + EPS)


def _softmax_pv(s, v_blk, head):
    m = jnp.max(s, axis=-1, keepdims=True)
    ol = _dot(jnp.exp2(s - m).astype(BF16), v_blk)
    ones_lane = 64 if head % 2 == 0 else 0
    return ol / ol[:, ones_lane:ones_lane + 1]


def _merge_head_pairs(per_head):
    lane = lax.broadcasted_iota(jnp.int32, (1, LANES), 1)
    blocks = [jnp.where(lane < 64, per_head[j], per_head[j + 1]) for j in range(0, len(per_head), 2)]
    return jnp.concatenate(blocks, axis=-1)


def _softmax_rows(s):
    m = jnp.max(s, axis=-1, keepdims=True)
    e = jnp.exp(s - m)
    return e / jnp.sum(e, axis=-1, keepdims=True)


def _mod_kernel(c_ref, w_ref, b_ref, o_ref):
    c = c_ref[...]
    s = c / (1.0 + jnp.exp(-c))
    o_ref[0] = _dot3(s, w_ref[0]) + b_ref[0]


def _modulation(cc, w_ada, b_ada):
    tn = 1536
    return pl.pallas_call(
        _mod_kernel,
        out_shape=jax.ShapeDtypeStruct((DEPTH, 16, 6 * D_MODEL), F32),
        grid=(DEPTH, 6 * D_MODEL // tn),
        in_specs=[
            pl.BlockSpec((16, D_MODEL), lambda l, j: (0, 0)),
            pl.BlockSpec((1, D_MODEL, tn), lambda l, j: (l, 0, j)),
            pl.BlockSpec((1, 1, tn), lambda l, j: (l, 0, j)),
        ],
        out_specs=pl.BlockSpec((1, 16, tn), lambda l, j: (l, 0, j)),
        compiler_params=_params(("arbitrary", "arbitrary")),
        name="adaln_mod",
    )(cc, w_ada, b_ada.reshape(DEPTH, 1, 6 * D_MODEL))


def _rope(p, cos, sin, half):
    lane = lax.broadcasted_iota(jnp.int32, (1, LANES), 1)
    first = (lane // half) % 2 == 0
    outs = []
    for j in range(p.shape[1] // LANES):
        blk = p[:, j * LANES:(j + 1) * LANES]
        partner = jnp.where(first, pltpu.roll(blk, LANES - half, 1), pltpu.roll(blk, half, 1))
        outs.append(blk * cos + partner * sin)
    return outs[0] if len(outs) == 1 else jnp.concatenate(outs, axis=-1)


def _per_head_with_ones(v):
    lane = lax.broadcasted_iota(jnp.int32, (1, LANES), 1)
    outs = []
    for j in range(v.shape[1] // LANES):
        blk = v[:, j * LANES:(j + 1) * LANES]
        outs.append(jnp.where(lane < 64, blk, jnp.where(lane == 64, 1.0, 0.0)))
        outs.append(jnp.where(lane >= 64, blk, jnp.where(lane == 0, 1.0, 0.0)))
    return jnp.concatenate(outs, axis=-1)


def _chunk_mod(modb_ref, modc_ref, j, n_chunks):
    mod = modb_ref[0]
    if n_chunks == STEP_CHUNKS and j == n_chunks - 1:
        mod = jnp.where(pl.program_id(1) == N_TILES // STEP_CHUNKS - 1, modc_ref[0], mod)
    return mod


def _inproj_kernel(x_ref, modb_ref, modc_ref, g1_ref, w_ref, qn_ref, wuq_ref, kvn_ref, wukv_ref,
                   ca_ref, sa_ref, caq_ref, saq_ref, c8_ref, s8_ref, c8q_ref, s8q_ref,
                   cb_ref, sb_ref, cbq_ref, sbq_ref,
                   qa_ref, ka_ref, va_ref, qb_ref, kb_ref, vb_ref, qd_ref, kd_ref, vd_ref):
    for j in range(STEP_CHUNKS):
        rows = slice(j * TOK_TILE, (j + 1) * TOK_TILE)
        x = x_ref[0, rows, :]
        mod = _chunk_mod(modb_ref, modc_ref, j, STEP_CHUNKS)
        sh1 = mod[:, 0:D_MODEL]
        sc1 = mod[:, D_MODEL:2 * D_MODEL]
        h = (_rms(x) * g1_ref[...]) * (1.0 + sc1) + sh1
        p = _dot(h.astype(BF16), w_ref[...])

        qa_ref[0, rows, :] = _rope(p[:, 0:512], caq_ref[rows, :], saq_ref[rows, :], 16).astype(BF16)
        ka_ref[0, rows, :] = _rope(p[:, 512:640], ca_ref[rows, :], sa_ref[rows, :], 16).astype(BF16)
        lane = lax.broadcasted_iota(jnp.int32, (TOK_TILE, LANES), 1)
        va_ref[0, rows, :] = jnp.concatenate(
            [p[:, 640:768], jnp.where(lane == 0, 1.0, 0.0)], axis=-1).astype(BF16)

        cq = (_rms(p[:, 768:1024]) * qn_ref[...]).astype(BF16)
        qb = _dot(cq, wuq_ref[...])
        qb_ref[0, rows, :] = _rope(qb, cbq_ref[rows, :], sbq_ref[rows, :], 8).astype(BF16)
        ckv = (_rms(p[:, 1024:1152]) * kvn_ref[...]).astype(BF16)
        kv = _dot(ckv, wukv_ref[...])
        kr = _rope(p[:, 1152:1280], cb_ref[rows, :], sb_ref[rows, :], 8)
        kb_ref[0, rows, :] = (kv[:, 0:512] + jnp.concatenate([kr] * B_HEADS, axis=-1)).astype(BF16)
        vb_ref[0, rows, :] = _per_head_with_ones(kv[:, 512:768]).astype(BF16)

        qd_ref[0, rows, :] = _rope(p[:, 1280:1536], c8q_ref[rows, :], s8q_ref[rows, :], 8).astype(BF16)
        kd_ref[0, rows, :] = _rope(p[:, 1536:1792], c8_ref[rows, :], s8_ref[rows, :], 8).astype(BF16)
        vd_ref[0, rows, :] = _per_head_with_ones(p[:, 1792:2048]).astype(BF16)


def _mod_specs():
    return [pl.BlockSpec((1, 1, 6 * D_MODEL), lambda b, t: (b, 0, 0)),
            pl.BlockSpec((1, 1, 6 * D_MODEL), lambda b, t: (BATCH, 0, 0))]


def _inproj(xall, mod_l, g1, w, qn, wuq, kvn, wukv, tabs):
    full = lambda shape: pl.BlockSpec(shape, lambda b, t: (0,) * len(shape))
    step_rows = STEP_CHUNKS * TOK_TILE
    tab_spec = pl.BlockSpec((step_rows, LANES), lambda b, t: (t, 0))
    widths = (512, 128, 256, 512, 512, 512, 256, 256, 512)
    return pl.pallas_call(
        _inproj_kernel,
        out_shape=[jax.ShapeDtypeStruct((BATCH, NT_PAD, n), BF16) for n in widths],
        grid=(BATCH, N_TILES // STEP_CHUNKS),
        in_specs=[pl.BlockSpec((1, step_rows, D_MODEL), lambda b, t: (b, t, 0))] + _mod_specs() + [
            full((1, D_MODEL)),
            full((D_MODEL, 2048)),
            full((1, B_Q_RANK)),
            full((B_Q_RANK, 512)),
            full((1, B_KV_RANK)),
            full((B_KV_RANK, 768)),
        ] + [tab_spec] * 12,
        out_specs=[pl.BlockSpec((1, step_rows, n), lambda b, t: (b, t, 0)) for n in widths],
        compiler_params=_params(("arbitrary", "arbitrary")),
        name="in_projection",
    )(xall, mod_l, mod_l, g1, w, qn, wuq, kvn, wukv, *tabs)


def _window_heads(q, kcat, vcat, bias, sink_ref):
    tq = q.shape[0]
    lane = lax.broadcasted_iota(jnp.int32, (1, LANES), 1)
    acc = [jnp.zeros((tq, LANES), F32) for _ in range(A_GROUP)]
    for g in range(A_KV_HEADS):
        in_half = (lane < HEAD_DIM) if g == 0 else (lane >= HEAD_DIM)
        keep = _indicator(in_half)
        qs = jnp.concatenate([q[:, j * LANES:(j + 1) * LANES] * keep for j in range(A_GROUP)], axis=0)
        s = _dot_nt(qs, kcat)
        if bias is not None:
            s = s + jnp.concatenate([bias] * A_GROUP, axis=0)
        sink = jnp.concatenate(
            [jnp.full((tq, 1), sink_ref[g * A_GROUP + j] * LOG2_E, F32) for j in range(A_GROUP)], axis=0)
        m = jnp.maximum(jnp.max(s, axis=-1, keepdims=True), sink)
        ol = _dot(jnp.exp2(s - m).astype(BF16), vcat)
        o = ol[:, 0:LANES] / (ol[:, LANES:LANES + 1] + jnp.exp2(sink - m))
        for j in range(A_GROUP):
            acc[j] = acc[j] + jnp.where(in_half, o[j * tq:(j + 1) * tq], 0.0)
    return jnp.concatenate(acc, axis=-1).astype(BF16)


A_STEP_BLOCKS = 4
A_LOCAL = 3 * WIN


def _attn_a_latent_kernel(sink_ref, q_ref, k_ref, v_ref, bias_ref, o_ref):
    step = pl.program_id(1)
    last = SEQ // WIN - 1
    k_ctx = k_ref[0, SEQ:NT, :]
    v_ctx = v_ref[0, SEQ:NT, :]
    for j in range(A_STEP_BLOCKS):
        n = step * A_STEP_BLOCKS + j
        start = pl.multiple_of(jnp.clip((n - 1) * WIN, 0, SEQ - A_LOCAL), WIN)
        variant = jnp.where(n == 0, 0, jnp.where(n == last, 2, 1))
        kcat = jnp.concatenate([k_ref[0, pl.ds(start, A_LOCAL), :], k_ctx], axis=0)
        vcat = jnp.concatenate([v_ref[0, pl.ds(start, A_LOCAL), :], v_ctx], axis=0)
        rows = slice(j * WIN, (j + 1) * WIN)
        o_ref[0, rows, :] = _window_heads(q_ref[0, rows, :], kcat, vcat, bias_ref[variant], sink_ref)


def _attn_a_latent(sink, qa, ka, va, bias):
    tq = A_STEP_BLOCKS * WIN
    return pl.pallas_call(
        _attn_a_latent_kernel,
        out_shape=jax.ShapeDtypeStruct((BATCH, NT_PAD, 512), BF16),
        grid=(BATCH, SEQ // tq),
        in_specs=[pl.BlockSpec(memory_space=pltpu.SMEM),
                  pl.BlockSpec((1, tq, 512), lambda b, n: (b, n, 0)),
                  pl.BlockSpec((1, NT_PAD, LANES), lambda b, n: (b, 0, 0)),
                  pl.BlockSpec((1, NT_PAD, 2 * LANES), lambda b, n: (b, 0, 0)),
                  pl.BlockSpec((3, WIN, A_LOCAL + CTX_LEN), lambda b, n: (0, 0, 0))],
        out_specs=pl.BlockSpec((1, tq, 512), lambda b, n: (b, n, 0)),
        compiler_params=_params(("arbitrary", "arbitrary")),
        name="attn_window_latent",
    )(sink, qa, ka, va, bias)


def _attn_a_ctx(sink, qa, ka, va, prev_out):
    ctx_blk = SEQ // CTX_LEN
    spec = lambda n: pl.BlockSpec((BATCH, CTX_LEN, n), lambda i: (0, ctx_blk, 0))
    tile = lambda n: pl.BlockSpec((BATCH, TOK_TILE, n), lambda i: (0, N_LAT_TILES, 0))

    def body(sink_ref, q_ref, k_ref, v_ref, prev_ref, o_ref):
        del prev_ref
        for sample in range(BATCH):
            o_ref[sample] = _window_heads(q_ref[sample], k_ref[sample], v_ref[sample], None, sink_ref)

    return pl.pallas_call(
        body,
        out_shape=jax.ShapeDtypeStruct((BATCH, NT_PAD, 512), BF16),
        grid=(1,),
        in_specs=[pl.BlockSpec(memory_space=pltpu.SMEM), tile(512), spec(LANES), spec(2 * LANES),
                  pl.BlockSpec(memory_space=pl.ANY)],
        out_specs=tile(512),
        input_output_aliases={4: 0},
        compiler_params=_params(("arbitrary",)),
        name="attn_window_ctx",
    )(sink, qa, ka, va, prev_out)


def _key_rows(ref, sample):
    return ref[sample, 0:NT, :] if ref.shape[1] == NT_PAD else ref[sample]


MLA_TQ = 512
DIFF_TQ = 1024
ATT_SUB = 256


def _attn_b_kernel(q_ref, k_ref, v_ref, o_ref):
    for sample in range(q_ref.shape[0]):
        k = _key_rows(k_ref, sample)
        v = _key_rows(v_ref, sample)
        for sub in range(q_ref.shape[1] // ATT_SUB):
            rows = slice(sub * ATT_SUB, (sub + 1) * ATT_SUB)
            q = q_ref[sample, rows, :]
            outs = []
            for h in range(B_HEADS):
                s = _dot_nt(q[:, h * LANES:(h + 1) * LANES], k[:, h * LANES:(h + 1) * LANES])
                outs.append(_softmax_pv(s, v[:, h * LANES:(h + 1) * LANES], h))
            o_ref[sample, rows, :] = _merge_head_pairs(outs).astype(BF16)


def _full_attention(kernel, name, tq, q, k, v, extra, extra_specs, out_width, q_width, k_width, v_width):
    out = pl.pallas_call(
        kernel,
        out_shape=jax.ShapeDtypeStruct((BATCH, NT_PAD, out_width), BF16),
        grid=(BATCH, SEQ // tq),
        in_specs=[pl.BlockSpec((1, tq, q_width), lambda b, t: (b, t, 0)),
                  pl.BlockSpec((1, NT_PAD, k_width), lambda b, t: (b, 0, 0)),
                  pl.BlockSpec((1, NT_PAD, v_width), lambda b, t: (b, 0, 0))] + extra_specs(2),
        out_specs=pl.BlockSpec((1, tq, out_width), lambda b, t: (b, t, 0)),
        compiler_params=_params(("arbitrary", "arbitrary")),
        name=name + "_latent",
    )(q, k, v, *extra)
    return out


def _full_attention_ctx(kernel, name, q, k, v, extra, extra_specs, out_width, q_width, k_width, v_width,
                        prev_out):
    ctx_blk = SEQ // CTX_LEN
    n_extra = len(extra)

    def body(*refs):
        ins = refs[:3 + n_extra]
        kernel(*ins, refs[-1])

    spec = lambda n: pl.BlockSpec((BATCH, CTX_LEN, n), lambda i: (0, ctx_blk, 0))
    tile = lambda n: pl.BlockSpec((BATCH, TOK_TILE, n), lambda i: (0, N_LAT_TILES, 0))
    return pl.pallas_call(
        body,
        out_shape=jax.ShapeDtypeStruct((BATCH, NT_PAD, out_width), BF16),
        grid=(1,),
        in_specs=[tile(q_width), spec(k_width), spec(v_width)] + extra_specs(1)
                 + [pl.BlockSpec(memory_space=pl.ANY)],
        out_specs=tile(out_width),
        input_output_aliases={3 + n_extra: 0},
        compiler_params=_params(("arbitrary",)),
        name=name + "_ctx",
    )(q, k, v, *extra, prev_out)


def _attn_c_kernel(q_ref, k_ref, v_ref, lq1_ref, lk1_ref, lq2_ref, lk2_ref, sn_ref, o_ref, *, lam_init):
    tq = ATT_SUB
    lam = (jnp.exp(jnp.sum(lq1_ref[...] * lk1_ref[...], axis=-1, keepdims=True))
           - jnp.exp(jnp.sum(lq2_ref[...] * lk2_ref[...], axis=-1, keepdims=True)) + lam_init)
    lane = lax.broadcasted_iota(jnp.int32, (1, C_HEADS * C_V), 1)
    units = [(sample, sub) for sample in range(q_ref.shape[0]) for sub in range(q_ref.shape[1] // ATT_SUB)]
    for sample, sub in units:
        k = _key_rows(k_ref, sample)
        v = _key_rows(v_ref, sample)
        rows = slice(sub * ATT_SUB, (sub + 1) * ATT_SUB)
        q = q_ref[sample, rows, :]
        outs = []
        for h in range(C_HEADS):
            lo = h * C_V
            q1 = q * _indicator((lane >= lo) & (lane < lo + C_DH))
            q2 = q * _indicator((lane >= lo + C_DH) & (lane < lo + C_V))
            s = _dot_nt(jnp.concatenate([q1, q2], axis=0), k)
            pv = _softmax_pv(s, v[:, h * LANES:(h + 1) * LANES], h)
            outs.append(pv[0:tq] - lam * pv[tq:2 * tq])
        acc = _merge_head_pairs(outs)
        sq = acc * acc
        inv = jnp.zeros((tq, C_HEADS * C_V), F32)
        for h in range(C_HEADS):
            in_head = (lane >= h * C_V) & (lane < (h + 1) * C_V)
            ms = jnp.sum(jnp.where(in_head, sq, 0.0), axis=-1, keepdims=True) * (1.0 / C_V)
            inv = inv + jnp.where(in_head, lax.rsqrt(ms + EPS), 0.0)
        o_ref[sample, rows, :] = ((acc * inv) * sn_ref[...] * (1.0 - lam_init)).astype(BF16)


def _outproj_kernel(x_ref, oa_ref, ob_ref, od_ref, w_ref, modb_ref, modc_ref, g2_ref, wr_ref,
                    xo_ref, h2_ref, aff_ref, *, n_chunks):
    w_hi, w_lo = _split_bf16(wr_ref[...])
    w_router = jnp.concatenate([w_hi, w_lo], axis=-1)
    lane = lax.broadcasted_iota(jnp.int32, (1, LANES), 1)
    for j in range(n_chunks):
        rows = slice(j * TOK_TILE, (j + 1) * TOK_TILE)
        mod = _chunk_mod(modb_ref, modc_ref, j, n_chunks)
        g1 = mod[:, 2 * D_MODEL:3 * D_MODEL]
        sh2 = mod[:, 3 * D_MODEL:4 * D_MODEL]
        sc2 = mod[:, 4 * D_MODEL:5 * D_MODEL]
        mix = jnp.concatenate([oa_ref[0, rows, :], ob_ref[0, rows, :], od_ref[0, rows, :]], axis=-1)
        x = x_ref[0, rows, :] + g1 * _dot(mix, w_ref[...])
        xo_ref[0, rows, :] = x
        h2 = (_rms(x) * g2_ref[...]) * (1.0 + sc2) + sh2
        h_hi, h_lo = _split_bf16(h2)
        h2_ref[0, rows, :] = h_hi
        prod = _dot(jnp.concatenate([h_hi, h_lo], axis=0), w_router)
        logits = (prod[0:TOK_TILE, 0:LANES] + prod[0:TOK_TILE, LANES:2 * LANES]
                  + prod[TOK_TILE:2 * TOK_TILE, 0:LANES])
        aff_ref[0, rows, :] = _softmax_rows(jnp.where(lane < N_EXPERTS, logits, NEG_INF))


def _outproj(xall, oa, ob, od, w, mod_l, g2, wr, with_ctx):
    n_chunks = STEP_CHUNKS if with_ctx else LAT_STEP_CHUNKS
    n_steps = (N_TILES if with_ctx else N_LAT_TILES) // n_chunks
    full = lambda shape: pl.BlockSpec(shape, lambda b, t: (0,) * len(shape))
    tok = lambda n: pl.BlockSpec((1, n_chunks * TOK_TILE, n), lambda b, t: (b, t, 0))
    return pl.pallas_call(
        functools.partial(_outproj_kernel, n_chunks=n_chunks),
        out_shape=[jax.ShapeDtypeStruct((BATCH, NT_PAD, D_MODEL), F32),
                   jax.ShapeDtypeStruct((BATCH, NT_PAD, D_MODEL), BF16),
                   jax.ShapeDtypeStruct((BATCH, NT_PAD, LANES), F32)],
        grid=(BATCH, n_steps),
        in_specs=[tok(D_MODEL), tok(512), tok(256), tok(256), full((D_MODEL, D_MODEL))] + _mod_specs()
                 + [full((1, D_MODEL)), full((D_MODEL, LANES))],
        out_specs=[tok(D_MODEL), tok(D_MODEL), tok(LANES)],
        compiler_params=_params(("arbitrary", "arbitrary")),
        name="out_projection",
    )(xall, oa, ob, od, w, mod_l, mod_l, g2, wr)


def _prefix_count(flags_f32, tri_ref):
    rows, n = flags_f32.shape
    run = jnp.zeros((rows, 1), F32)
    outs = []
    for c in range(n // LANES):
        blk = flags_f32[:, c * LANES:(c + 1) * LANES]
        outs.append(_dot(blk.astype(BF16), tri_ref[...]) + run)
        run = run + jnp.sum(blk, axis=-1, keepdims=True)
    return jnp.concatenate(outs, axis=-1)


def _select_slots(a, cap, tri_ref):
    bits = pltpu.bitcast(a, jnp.int32)

    def step(i, t):
        cand = t | jnp.left_shift(jnp.int32(1), 30 - i)
        cnt = jnp.sum((bits >= cand).astype(jnp.int32), axis=-1, keepdims=True)
        return jnp.where(cnt >= cap, cand, t)

    thr = lax.fori_loop(0, 31, step, jnp.zeros((a.shape[0], 1), jnp.int32))
    above = bits > thr
    tied = bits == thr
    need = cap - jnp.sum(above.astype(F32), axis=-1, keepdims=True)
    tied_f = tied.astype(F32)
    sel = above | (tied & (_prefix_count(tied_f, tri_ref) < need))
    sel_f = sel.astype(F32)
    return jnp.where(sel, _prefix_count(sel_f, tri_ref), -1.0)


def _route_kernel(aff_ref, tri_ref, row_ref, col_ref, *, with_ctx):
    lat, ctx = [], []
    for b in range(BATCH):
        lat.append(aff_ref[b, 0:SEQ, :].T[0:N_EXPERTS])
        if with_ctx:
            ctx.append(aff_ref[b, SEQ:NT, :].T[0:N_EXPERTS])
    slot_lat = _select_slots(jnp.concatenate(lat, axis=0), CAP_LAT, tri_ref)
    if with_ctx:
        slot_ctx = _select_slots(jnp.concatenate(ctx, axis=0), CAP_CTX, tri_ref)
        slots = jnp.concatenate([slot_lat, slot_ctx], axis=-1)
    else:
        slots = slot_lat
    n = slots.shape[1]
    row_ref[:, 0:n] = slots
    pad = jnp.full((LANES - N_EXPERTS, n), -1.0, F32)
    for b in range(BATCH):
        blk = jnp.concatenate([slots[b * N_EXPERTS:(b + 1) * N_EXPERTS], pad], axis=0)
        col_ref[b, 0:n, :] = blk.T
        if with_ctx and NT_PAD > NT:
            col_ref[b, NT:NT_PAD, :] = jnp.full((NT_PAD - NT, LANES), -1.0, F32)


def _route(aff, tri, with_ctx):
    vm = pl.BlockSpec(memory_space=pltpu.VMEM)
    return pl.pallas_call(
        functools.partial(_route_kernel, with_ctx=with_ctx),
        out_shape=[jax.ShapeDtypeStruct((BATCH * N_EXPERTS, NT), F32),
                   jax.ShapeDtypeStruct((BATCH, NT_PAD, LANES), F32)],
        in_specs=[vm, vm],
        out_specs=[vm, vm],
        compiler_params=pltpu.CompilerParams(vmem_limit_bytes=VMEM_LIMIT),
        name="route_topc",
    )(aff, tri)


def _gather_kernel(row_ref, h_ref, *out_refs, with_ctx):
    slot = lax.broadcasted_iota(jnp.int32, (CAP_LAT, 1), 0).astype(F32)
    for e in range(N_EXPERTS):
        p = _indicator(row_ref[e, :, 0:SEQ] == slot)
        out_refs[0][e] = _dot(p, h_ref[0, 0:SEQ, :]).astype(BF16)
        if with_ctx:
            pc = _indicator(row_ref[e, :, SEQ:NT] == slot[0:CAP_CTX])
            out_refs[1][e] = _dot(pc, h_ref[0, SEQ:NT, :]).astype(BF16)


def _gather(slot_rows, h2, with_ctx):
    out_shape = [jax.ShapeDtypeStruct((N_EXPERTS, SEQ, D_MODEL), BF16)]
    out_specs = [pl.BlockSpec((N_EXPERTS, CAP_LAT, D_MODEL), lambda b: (0, b, 0))]
    if with_ctx:
        out_shape.append(jax.ShapeDtypeStruct((N_EXPERTS, CTX_LEN, D_MODEL), BF16))
        out_specs.append(pl.BlockSpec((N_EXPERTS, CAP_CTX, D_MODEL), lambda b: (0, b, 0)))
    return pl.pallas_call(
        functools.partial(_gather_kernel, with_ctx=with_ctx),
        out_shape=out_shape,
        grid=(BATCH,),
        in_specs=[pl.BlockSpec((N_EXPERTS, 1, NT), lambda b: (b, 0, 0)),
                  pl.BlockSpec((1, NT_PAD, D_MODEL), lambda b: (b, 0, 0))],
        out_specs=out_specs,
        compiler_params=_params(("arbitrary",)),
        name="expert_gather",
    )(slot_rows, h2)


def _ffn_kernel(*refs, with_ctx):
    if with_ctx:
        xl_ref, xc_ref, wg_ref, wu_ref, wd_ref, y_ref = refs
    else:
        xl_ref, wg_ref, wu_ref, wd_ref, y_ref = refs
    wg = wg_ref[0, 0].astype(BF16)
    wu = wu_ref[0, 0].astype(BF16)
    wd = wd_ref[0, 0].astype(BF16)

    def swiglu(x):
        gate = _dot(x, wg)
        hid = (gate / (1.0 + jnp.exp(-gate))) * _dot(x, wu)
        return _dot(hid.astype(BF16), wd).astype(BF16)

    for r in range(FFN_LAT_TILES):
        rows = slice(r * FFN_TILE, (r + 1) * FFN_TILE)
        y_ref[0, rows, :] = swiglu(xl_ref[0, rows, :])
    if with_ctx:
        y_ref[0, SEQ:NT, :] = swiglu(xc_ref[0])


def _ffn(xg_lat, xg_ctx, w_gate, w_up, w_down, layer):
    with_ctx = xg_ctx is not None
    ins = [xg_lat] + ([xg_ctx] if with_ctx else []) + [w_gate, w_up, w_down]
    in_specs = [pl.BlockSpec((1, SEQ, D_MODEL), lambda e: (e, 0, 0))]
    if with_ctx:
        in_specs.append(pl.BlockSpec((1, CTX_LEN, D_MODEL), lambda e: (e, 0, 0)))
    in_specs += [pl.BlockSpec((1, 1, D_MODEL, D_EXPERT), lambda e: (layer, e, 0, 0)),
                 pl.BlockSpec((1, 1, D_MODEL, D_EXPERT), lambda e: (layer, e, 0, 0)),
                 pl.BlockSpec((1, 1, D_EXPERT, D_MODEL), lambda e: (layer, e, 0, 0))]
    return pl.pallas_call(
        functools.partial(_ffn_kernel, with_ctx=with_ctx),
        out_shape=jax.ShapeDtypeStruct((N_EXPERTS, NT, D_MODEL), BF16),
        grid=(N_EXPERTS,),
        in_specs=in_specs,
        out_specs=pl.BlockSpec((1, NT, D_MODEL), lambda e: (e, 0, 0)),
        compiler_params=_params(("arbitrary",)),
        name="expert_ffn",
    )(*ins)


def _combine_kernel(*refs, with_ctx, final):
    if with_ctx:
        x_ref, col_ref, aff_ref, yl_ref, yc_ref, modb_ref, modc_ref, o_ref = refs
    elif final:
        x_ref, col_ref, aff_ref, yl_ref, modb_ref, modc_ref, fn_ref, o_ref = refs
    else:
        x_ref, col_ref, aff_ref, yl_ref, modb_ref, modc_ref, o_ref = refs
    n_chunks = STEP_CHUNKS if with_ctx else LAT_STEP_CHUNKS

    def chunk(j, is_context):
        rows = slice(j * TOK_TILE, (j + 1) * TOK_TILE)
        g2 = _chunk_mod(modb_ref, modc_ref, j, n_chunks)[:, 5 * D_MODEL:6 * D_MODEL]
        col = col_ref[0, rows, :]
        aff = aff_ref[0, rows, :]
        n_slots = LANES if is_context else CAP_LAT
        lane = lax.broadcasted_iota(jnp.int32, (1, n_slots), 1).astype(F32)
        pad = jnp.zeros((LANES - CAP_CTX, D_MODEL), BF16)
        acc = jnp.zeros((TOK_TILE, D_MODEL), F32)
        for e in range(N_EXPERTS):
            pt = _indicator(col[:, e:e + 1] == lane)
            y = jnp.concatenate([yc_ref[e], pad], axis=0) if is_context else yl_ref[e]
            acc = acc + aff[:, e:e + 1] * _dot(pt, y)
        x = x_ref[0, rows, :] + g2 * acc
        if final:
            x = _rms(x) * fn_ref[...]
        o_ref[0, rows, :] = x

    for j in range(n_chunks):
        if with_ctx and j == n_chunks - 1:
            is_last = pl.program_id(1) == N_TILES // STEP_CHUNKS - 1
            pl.when(jnp.logical_not(is_last))(functools.partial(chunk, j, False))
            pl.when(is_last)(functools.partial(chunk, j, True))
        else:
            chunk(j, False)


def _combine(xall, slot_cols, aff, y, mod_l, with_ctx, final_norm=None):
    final = final_norm is not None
    n_chunks = STEP_CHUNKS if with_ctx else LAT_STEP_CHUNKS
    n_steps = (N_TILES if with_ctx else N_LAT_TILES) // n_chunks
    n_rows = SEQ if final else NT_PAD
    tok = lambda n: pl.BlockSpec((1, n_chunks * TOK_TILE, n), lambda b, t: (b, t, 0))
    ins = [xall, slot_cols, aff, y]
    in_specs = [tok(D_MODEL), tok(LANES), tok(LANES),
                pl.BlockSpec((N_EXPERTS, CAP_LAT, D_MODEL), lambda b, t: (0, b, 0))]
    if with_ctx:
        ins.append(y)
        in_specs.append(pl.BlockSpec((N_EXPERTS, CAP_CTX, D_MODEL), lambda b, t: (0, SEQ // CAP_CTX + b, 0)))
    ins += [mod_l, mod_l]
    in_specs += _mod_specs()
    if final:
        ins.append(final_norm)
        in_specs.append(pl.BlockSpec((1, D_MODEL), lambda b, t: (0, 0)))
    return pl.pallas_call(
        functools.partial(_combine_kernel, with_ctx=with_ctx, final=final),
        out_shape=jax.ShapeDtypeStruct((BATCH, n_rows, D_MODEL), F32),
        grid=(BATCH, n_steps),
        in_specs=in_specs,
        out_specs=tok(D_MODEL),
        compiler_params=_params(("arbitrary", "arbitrary")),
        name="expert_combine",
    )(*ins)


def _rope_tables():
    t = jnp.arange(SEQ)
    pos = jnp.stack([t // GRID_W, t % GRID_W], axis=0)

    def table(n_f, lanes_per_unit):
        inv = ROPE_BASE ** (-jnp.arange(n_f, dtype=F32) / n_f)
        d = np.arange(lanes_per_unit)
        axis, second, f = d // (2 * n_f), (d // n_f) % 2, d % n_f
        ang = pos[axis].T.astype(F32) * inv[f][None, :]
        cos = jnp.cos(ang)
        sin = jnp.sin(ang) * jnp.where(second == 1, 1.0, -1.0)[None, :]
        return cos, sin

    def finish(cos, sin, scale):
        cos = jnp.concatenate([cos, jnp.ones((NT_PAD - SEQ, LANES), F32)], axis=0)
        sin = jnp.concatenate([sin, jnp.zeros((NT_PAD - SEQ, LANES), F32)], axis=0)
        scale = scale * LOG2_E
        return [cos, sin, cos * scale, sin * scale]

    cos_a, sin_a = table(HEAD_DIM // 4, HEAD_DIM)
    tabs_a = finish(jnp.tile(cos_a, (1, 2)), jnp.tile(sin_a, (1, 2)), HEAD_DIM ** -0.5)
    cos_8, sin_8 = table(B_ROPE // 4, B_ROPE)
    tabs_8 = finish(jnp.tile(cos_8, (1, 4)), jnp.tile(sin_8, (1, 4)), C_DH ** -0.5)
    ones = jnp.ones((SEQ, B_NOPE), F32)
    pad1 = jnp.ones((SEQ, LANES - B_NOPE - B_ROPE), F32)
    cos_b = jnp.concatenate([ones, cos_8, pad1], axis=1)
    sin_b = jnp.concatenate([0 * ones, sin_8, 0 * pad1], axis=1)
    tabs_b = finish(cos_b, sin_b, (B_NOPE + B_ROPE) ** -0.5)
    return tabs_a + tabs_8 + tabs_b


def _window_bias():
    qi = np.arange(WIN)[:, None]
    kj = np.arange(A_LOCAL)[None, :]
    out = np.zeros((3, WIN, A_LOCAL + CTX_LEN), np.float32)
    for v in range(3):
        out[v, :, :A_LOCAL] = np.where(np.abs(kj - v * WIN - qi) <= WIN, 0.0, NEG_INF)
    return jnp.asarray(out)


def _prep_weights(w_in, w_uq, w_ukv, w_out, w_router):
    offs = np.cumsum((0, 512, 128, 128, 256, 128, 32, 256, 256, 256))
    seg = lambda i: w_in[:, :, offs[i]:offs[i + 1]]
    wq = seg(0).reshape(DEPTH, D_MODEL, A_KV_HEADS, A_GROUP, HEAD_DIM)
    wq = jnp.swapaxes(wq, 2, 3).reshape(DEPTH, D_MODEL, A_HEADS * HEAD_DIM)
    kr = jnp.pad(seg(5), ((0, 0), (0, 0), (B_NOPE, LANES - B_NOPE - B_ROPE)))
    w = jnp.concatenate([wq, seg(1), seg(2), seg(3), seg(4), kr,
                         seg(6), seg(7), seg(8)], axis=-1).astype(BF16)
    wuq = w_uq.reshape(DEPTH, B_Q_RANK, B_HEADS, B_NOPE + B_ROPE)
    wuq = jnp.pad(wuq, ((0, 0), (0, 0), (0, 0), (0, LANES - B_NOPE - B_ROPE)))
    wuq = wuq.reshape(DEPTH, B_Q_RANK, B_HEADS * LANES).astype(BF16)
    wukv = w_ukv.reshape(DEPTH, B_KV_RANK, B_HEADS, B_NOPE + B_V)
    wk = jnp.pad(wukv[..., :B_NOPE], ((0, 0), (0, 0), (0, 0), (0, LANES - B_NOPE)))
    wukv = jnp.concatenate([wk.reshape(DEPTH, B_KV_RANK, B_HEADS * LANES),
                            wukv[..., B_NOPE:].reshape(DEPTH, B_KV_RANK, B_HEADS * B_V)],
                           axis=-1).astype(BF16)
    wo_a = w_out[:, :A_HEADS * HEAD_DIM].reshape(DEPTH, A_KV_HEADS, A_GROUP, HEAD_DIM, D_MODEL)
    wo_a = jnp.swapaxes(wo_a, 1, 2).reshape(DEPTH, A_HEADS * HEAD_DIM, D_MODEL)
    wo = jnp.concatenate([wo_a, w_out[:, A_HEADS * HEAD_DIM:]], axis=1).astype(BF16)
    wr = jnp.pad(w_router, ((0, 0), (0, 0), (0, LANES - N_EXPERTS)))
    return w, wuq, wukv, wo, wr


def kernel(x, c, ctx, c_ctx, norm1, norm2, w_ada, b_ada, w_in, sink, mla_q_norm, w_uq, mla_kv_norm, w_ukv,
           lam_q1, lam_k1, lam_q2, lam_k2, diff_norm, w_out, w_router, w_gate, w_up, w_down, final_norm):
    xall = jnp.concatenate([x, ctx, jnp.zeros((BATCH, NT_PAD - NT, D_MODEL), F32)], axis=1)
    cc = jnp.concatenate([c, c_ctx[None, :], jnp.zeros((16 - BATCH - 1, D_MODEL), F32)], axis=0)
    mod = _modulation(cc, w_ada, b_ada).reshape(DEPTH, 16, 1, 6 * D_MODEL)
    tabs = _rope_tables()
    bias = _window_bias()
    tri = jnp.asarray(np.triu(np.ones((LANES, LANES), np.float32), k=1), BF16)
    w, wuq, wukv, wo, wr = _prep_weights(w_in, w_uq, w_ukv, w_out, w_router)
    sub_norm = jnp.tile(diff_norm, (1, C_HEADS)).reshape(DEPTH, 1, C_HEADS * C_V)
    row = lambda a, l: a[l].reshape(1, -1)

    for l in range(DEPTH):
        need_ctx = l < DEPTH - 1
        qa, ka, va, qb, kb, vb, qd, kd, vd = _inproj(
            xall, mod[l], row(norm1, l), w[l], row(mla_q_norm, l), wuq[l], row(mla_kv_norm, l), wukv[l], tabs)

        oa = _attn_a_latent(sink[l], qa, ka, va, bias)
        ob = _full_attention(_attn_b_kernel, "attn_mla", MLA_TQ, qb, kb, vb, [], lambda n: [],
                             256, 512, 512, 512)
        lam_init = 0.8 - 0.6 * math.exp(-0.3 * l)
        c_kernel = functools.partial(_attn_c_kernel, lam_init=lam_init)
        c_extra = [row(lam_q1, l), row(lam_k1, l), row(lam_q2, l), row(lam_k2, l), sub_norm[l]]

        def c_specs(n_grid, c_extra=c_extra):
            zmap = (lambda b, t: (0, 0)) if n_grid == 2 else (lambda b: (0, 0))
            return [pl.BlockSpec(a.shape, zmap) for a in c_extra]

        od = _full_attention(c_kernel, "attn_diff", DIFF_TQ, qd, kd, vd, c_extra, c_specs, 256, 256, 256, 512)
        if need_ctx:
            oa = _attn_a_ctx(sink[l], qa, ka, va, oa)
            ob = _full_attention_ctx(_attn_b_kernel, "attn_mla", qb, kb, vb, [], lambda n: [],
                                     256, 512, 512, 512, ob)
            od = _full_attention_ctx(c_kernel, "attn_diff", qd, kd, vd, c_extra, c_specs,
                                     256, 256, 256, 512, od)

        xall, h2, aff = _outproj(xall, oa, ob, od, wo[l], mod[l], row(norm2, l), wr[l], need_ctx)
        slot_rows, slot_cols = _route(aff, tri, need_ctx)
        gathered = _gather(slot_rows.reshape(BATCH * N_EXPERTS, 1, NT), h2, need_ctx)
        y = _ffn(gathered[0], gathered[1] if need_ctx else None, w_gate, w_up, w_down, l)
        xall = _combine(xall, slot_cols, aff, y, mod[l], need_ctx,
                        final_norm=None if need_ctx else final_norm.reshape(1, D_MODEL))
    return xall
```

```python
import functools
import math

import numpy as np
import jax
import jax.numpy as jnp
from jax import lax
from jax.experimental import pallas as pl
from jax.experimental.pallas import tpu as pltpu

D_MODEL = 1024
BATCH = 8
SEQ = 2048
DEPTH = 4
CTX_LEN = 256
NT = SEQ + CTX_LEN
GRID_W = 64
HEAD_DIM = 64
ROPE_BASE = 10000.0
EPS = 1e-6
NEG_INF = -1e30
LOG2_E = math.log2(math.e)

WIN = 128
A_HEADS = 8
A_KV_HEADS = 2
A_GROUP = A_HEADS // A_KV_HEADS
B_HEADS = 4
B_NOPE = 64
B_ROPE = 32
B_V = 64
B_Q_RANK = 256
B_KV_RANK = 128
C_HEADS = 4
C_DH = 32
C_V = 2 * C_DH
N_EXPERTS = 16
EC_FACTOR = 2
D_EXPERT = 512
CAP_LAT = EC_FACTOR * SEQ // N_EXPERTS
CAP_CTX = EC_FACTOR * CTX_LEN // N_EXPERTS

LANES = 128
TOK_TILE = 256
NT_PAD = SEQ + TOK_TILE
N_TILES = NT_PAD // TOK_TILE
N_LAT_TILES = SEQ // TOK_TILE
STEP_CHUNKS = 3
LAT_STEP_CHUNKS = 2
FFN_TILE = 512
FFN_LAT_TILES = SEQ // FFN_TILE
VMEM_LIMIT = 56 * 1024 * 1024

F32 = jnp.float32
BF16 = jnp.bfloat16

_NT_DIMS = (((1,), (1,)), ((), ()))


def _params(sem):
    return pltpu.CompilerParams(dimension_semantics=sem, vmem_limit_bytes=VMEM_LIMIT)


def _dot(a, b):
    return jnp.dot(a, b, preferred_element_type=F32)


def _dot_nt(a, b):
    return lax.dot_general(a, b, _NT_DIMS, preferred_element_type=F32)


def _split_bf16(v):
    hi = v.astype(BF16)
    lo = (v - hi.astype(F32)).astype(BF16)
    return hi, lo


def _dot3(a, b):
    a_hi, a_lo = _split_bf16(a)
    b_hi, b_lo = _split_bf16(b)
    return _dot(a_hi, b_hi) + _dot(a_lo, b_hi) + _dot(a_hi, b_lo)


def _indicator(cond):
    return jnp.where(cond, 1.0, 0.0).astype(BF16)


def _rms(v):
    return v * lax.rsqrt(jnp.mean(v * v, axis=-1, keepdims=True) + EPS)


def _softmax_pv(s, v_blk, head):
    m = jnp.max(s, axis=-1, keepdims=True)
    ol = _dot(jnp.exp2(s - m).astype(BF16), v_blk)
    ones_lane = 64 if head % 2 == 0 else 0
    return ol / ol[:, ones_lane:ones_lane + 1]


def _merge_head_pairs(per_head):
    lane = lax.broadcasted_iota(jnp.int32, (1, LANES), 1)
    blocks = [jnp.where(lane < 64, per_head[j], per_head[j + 1]) for j in range(0, len(per_head), 2)]
    return jnp.concatenate(blocks, axis=-1)


def _softmax_rows(s):
    m = jnp.max(s, axis=-1, keepdims=True)
    e = jnp.exp(s - m)
    return e / jnp.sum(e, axis=-1, keepdims=True)


def _mod_kernel(c_ref, w_ref, b_ref, o_ref):
    c = c_ref[...]
    s = c / (1.0 + jnp.exp(-c))
    o_ref[0] = _dot3(s, w_ref[0]) + b_ref[0]


def _modulation(cc, w_ada, b_ada):
    tn = 1536
    return pl.pallas_call(
        _mod_kernel,
        out_shape=jax.ShapeDtypeStruct((DEPTH, 16, 6 * D_MODEL), F32),
        grid=(DEPTH, 6 * D_MODEL // tn),
        in_specs=[
            pl.BlockSpec((16, D_MODEL), lambda l, j: (0, 0)),
            pl.BlockSpec((1, D_MODEL, tn), lambda l, j: (l, 0, j)),
            pl.BlockSpec((1, 1, tn), lambda l, j: (l, 0, j)),
        ],
        out_specs=pl.BlockSpec((1, 16, tn), lambda l, j: (l, 0, j)),
        compiler_params=_params(("arbitrary", "arbitrary")),
        name="adaln_mod",
    )(cc, w_ada, b_ada.reshape(DEPTH, 1, 6 * D_MODEL))


def _rope(p, cos, sin, half):
    lane = lax.broadcasted_iota(jnp.int32, (1, LANES), 1)
    first = (lane // half) % 2 == 0
    outs = []
    for j in range(p.shape[1] // LANES):
        blk = p[:, j * LANES:(j + 1) * LANES]
        partner = jnp.where(first, pltpu.roll(blk, LANES - half, 1), pltpu.roll(blk, half, 1))
        outs.append(blk * cos + partner * sin)
    return outs[0] if len(outs) == 1 else jnp.concatenate(outs, axis=-1)


def _per_head_with_ones(v):
    lane = lax.broadcasted_iota(jnp.int32, (1, LANES), 1)
    outs = []
    for j in range(v.shape[1] // LANES):
        blk = v[:, j * LANES:(j + 1) * LANES]
        outs.append(jnp.where(lane < 64, blk, jnp.where(lane == 64, 1.0, 0.0)))
        outs.append(jnp.where(lane >= 64, blk, jnp.where(lane == 0, 1.0, 0.0)))
    return jnp.concatenate(outs, axis=-1)


def _chunk_mod(modb_ref, modc_ref, j, n_chunks):
    mod = modb_ref[0]
    if n_chunks == STEP_CHUNKS and j == n_chunks - 1:
        mod = jnp.where(pl.program_id(1) == N_TILES // STEP_CHUNKS - 1, modc_ref[0], mod)
    return mod


def _inproj_kernel(x_ref, modb_ref, modc_ref, g1_ref, w_ref, qn_ref, wuq_ref, kvn_ref, wukv_ref,
                   ca_ref, sa_ref, caq_ref, saq_ref, c8_ref, s8_ref, c8q_ref, s8q_ref,
                   cb_ref, sb_ref, cbq_ref, sbq_ref,
                   qa_ref, ka_ref, va_ref, qb_ref, kb_ref, vb_ref, qd_ref, kd_ref, vd_ref):
    for j in range(STEP_CHUNKS):
        rows = slice(j * TOK_TILE, (j + 1) * TOK_TILE)
        x = x_ref[0, rows, :]
        mod = _chunk_mod(modb_ref, modc_ref, j, STEP_CHUNKS)
        sh1 = mod[:, 0:D_MODEL]
        sc1 = mod[:, D_MODEL:2 * D_MODEL]
        h = (_rms(x) * g1_ref[...]) * (1.0 + sc1) + sh1
        p = _dot(h.astype(BF16), w_ref[...])

        qa_ref[0, rows, :] = _rope(p[:, 0:512], caq_ref[rows, :], saq_ref[rows, :], 16).astype(BF16)
        ka_ref[0, rows, :] = _rope(p[:, 512:640], ca_ref[rows, :], sa_ref[rows, :], 16).astype(BF16)
        lane = lax.broadcasted_iota(jnp.int32, (TOK_TILE, LANES), 1)
        va_ref[0, rows, :] = jnp.concatenate(
            [p[:, 640:768], jnp.where(lane == 0, 1.0, 0.0)], axis=-1).astype(BF16)

        cq = (_rms(p[:, 768:1024]) * qn_ref[...]).astype(BF16)
        qb = _dot(cq, wuq_ref[...])
        qb_ref[0, rows, :] = _rope(qb, cbq_ref[rows, :], sbq_ref[rows, :], 8).astype(BF16)
        ckv = (_rms(p[:, 1024:1152]) * kvn_ref[...]).astype(BF16)
        kv = _dot(ckv, wukv_ref[...])
        kr = _rope(p[:, 1152:1280], cb_ref[rows, :], sb_ref[rows, :], 8)
        kb_ref[0, rows, :] = (kv[:, 0:512] + jnp.concatenate([kr] * B_HEADS, axis=-1)).astype(BF16)
        vb_ref[0, rows, :] = _per_head_with_ones(kv[:, 512:768]).astype(BF16)

        qd_ref[0, rows, :] = _rope(p[:, 1280:1536], c8q_ref[rows, :], s8q_ref[rows, :], 8).astype(BF16)
        kd_ref[0, rows, :] = _rope(p[:, 1536:1792], c8_ref[rows, :], s8_ref[rows, :], 8).astype(BF16)
        vd_ref[0, rows, :] = _per_head_with_ones(p[:, 1792:2048]).astype(BF16)


def _mod_specs():
    return [pl.BlockSpec((1, 1, 6 * D_MODEL), lambda b, t: (b, 0, 0)),
            pl.BlockSpec((1, 1, 6 * D_MODEL), lambda b, t: (BATCH, 0, 0))]


def _inproj(xall, mod_l, g1, w, qn, wuq, kvn, wukv, tabs):
    full = lambda shape: pl.BlockSpec(shape, lambda b, t: (0,) * len(shape))
    step_rows = STEP_CHUNKS * TOK_TILE
    tab_spec = pl.BlockSpec((step_rows, LANES), lambda b, t: (t, 0))
    widths = (512, 128, 256, 512, 512, 512, 256, 256, 512)
    return pl.pallas_call(
        _inproj_kernel,
        out_shape=[jax.ShapeDtypeStruct((BATCH, NT_PAD, n), BF16) for n in widths],
        grid=(BATCH, N_TILES // STEP_CHUNKS),
        in_specs=[pl.BlockSpec((1, step_rows, D_MODEL), lambda b, t: (b, t, 0))] + _mod_specs() + [
            full((1, D_MODEL)),
            full((D_MODEL, 2048)),
            full((1, B_Q_RANK)),
            full((B_Q_RANK, 512)),
            full((1, B_KV_RANK)),
            full((B_KV_RANK, 768)),
        ] + [tab_spec] * 12,
        out_specs=[pl.BlockSpec((1, step_rows, n), lambda b, t: (b, t, 0)) for n in widths],
        compiler_params=_params(("arbitrary", "arbitrary")),
        name="in_projection",
    )(xall, mod_l, mod_l, g1, w, qn, wuq, kvn, wukv, *tabs)


def _window_heads(q, kcat, vcat, bias, sink_ref):
    tq = q.shape[0]
    lane = lax.broadcasted_iota(jnp.int32, (1, LANES), 1)
    acc = [jnp.zeros((tq, LANES), F32) for _ in range(A_GROUP)]
    for g in range(A_KV_HEADS):
        in_half = (lane < HEAD_DIM) if g == 0 else (lane >= HEAD_DIM)
        keep = _indicator(in_half)
        qs = jnp.concatenate([q[:, j * LANES:(j + 1) * LANES] * keep for j in range(A_GROUP)], axis=0)
        s = _dot_nt(qs, kcat)
        if bias is not None:
            s = s + jnp.concatenate([bias] * A_GROUP, axis=0)
        sink = jnp.concatenate(
            [jnp.full((tq, 1), sink_ref[g * A_GROUP + j] * LOG2_E, F32) for j in range(A_GROUP)], axis=0)
        m = jnp.maximum(jnp.max(s, axis=-1, keepdims=True), sink)
        ol = _dot(jnp.exp2(s - m).astype(BF16), vcat)
        o = ol[:, 0:LANES] / (ol[:, LANES:LANES + 1] + jnp.exp2(sink - m))
        for j in range(A_GROUP):
            acc[j] = acc[j] + jnp.where(in_half, o[j * tq:(j + 1) * tq], 0.0)
    return jnp.concatenate(acc, axis=-1).astype(BF16)


A_STEP_BLOCKS = 4
A_LOCAL = 3 * WIN


def _attn_a_latent_kernel(sink_ref, q_ref, k_ref, v_ref, bias_ref, o_ref):
    step = pl.program_id(1)
    last = SEQ // WIN - 1
    k_ctx = k_ref[0, SEQ:NT, :]
    v_ctx = v_ref[0, SEQ:NT, :]
    for j in range(A_STEP_BLOCKS):
        n = step * A_STEP_BLOCKS + j
        start = pl.multiple_of(jnp.clip((n - 1) * WIN, 0, SEQ - A_LOCAL), WIN)
        variant = jnp.where(n == 0, 0, jnp.where(n == last, 2, 1))
        kcat = jnp.concatenate([k_ref[0, pl.ds(start, A_LOCAL), :], k_ctx], axis=0)
        vcat = jnp.concatenate([v_ref[0, pl.ds(start, A_LOCAL), :], v_ctx], axis=0)
        rows = slice(j * WIN, (j + 1) * WIN)
        o_ref[0, rows, :] = _window_heads(q_ref[0, rows, :], kcat, vcat, bias_ref[variant], sink_ref)


def _attn_a_latent(sink, qa, ka, va, bias):
    tq = A_STEP_BLOCKS * WIN
    return pl.pallas_call(
        _attn_a_latent_kernel,
        out_shape=jax.ShapeDtypeStruct((BATCH, NT_PAD, 512), BF16),
        grid=(BATCH, SEQ // tq),
        in_specs=[pl.BlockSpec(memory_space=pltpu.SMEM),
                  pl.BlockSpec((1, tq, 512), lambda b, n: (b, n, 0)),
                  pl.BlockSpec((1, NT_PAD, LANES), lambda b, n: (b, 0, 0)),
                  pl.BlockSpec((1, NT_PAD, 2 * LANES), lambda b, n: (b, 0, 0)),
                  pl.BlockSpec((3, WIN, A_LOCAL + CTX_LEN), lambda b, n: (0, 0, 0))],
        out_specs=pl.BlockSpec((1, tq, 512), lambda b, n: (b, n, 0)),
        compiler_params=_params(("arbitrary", "arbitrary")),
        name="attn_window_latent",
    )(sink, qa, ka, va, bias)


def _attn_a_ctx(sink, qa, ka, va, prev_out):
    ctx_blk = SEQ // CTX_LEN
    spec = lambda n: pl.BlockSpec((BATCH, CTX_LEN, n), lambda i: (0, ctx_blk, 0))
    tile = lambda n: pl.BlockSpec((BATCH, TOK_TILE, n), lambda i: (0, N_LAT_TILES, 0))

    def body(sink_ref, q_ref, k_ref, v_ref, prev_ref, o_ref):
        del prev_ref
        for sample in range(BATCH):
            o_ref[sample] = _window_heads(q_ref[sample], k_ref[sample], v_ref[sample], None, sink_ref)

    return pl.pallas_call(
        body,
        out_shape=jax.ShapeDtypeStruct((BATCH, NT_PAD, 512), BF16),
        grid=(1,),
        in_specs=[pl.BlockSpec(memory_space=pltpu.SMEM), tile(512), spec(LANES), spec(2 * LANES),
                  pl.BlockSpec(memory_space=pl.ANY)],
        out_specs=tile(512),
        input_output_aliases={4: 0},
        compiler_params=_params(("arbitrary",)),
        name="attn_window_ctx",
    )(sink, qa, ka, va, prev_out)


def _key_rows(ref, sample):
    return ref[sample, 0:NT, :] if ref.shape[1] == NT_PAD else ref[sample]


MLA_TQ = 1024
MLA_SUB = 512
DIFF_TQ = 1024
ATT_SUB = 256


def _attn_b_kernel(q_ref, k_ref, v_ref, o_ref):
    for sample in range(q_ref.shape[0]):
        k = _key_rows(k_ref, sample)
        v = _key_rows(v_ref, sample)
        sub_rows = min(MLA_SUB, q_ref.shape[1])
        for sub in range(q_ref.shape[1] // sub_rows):
            rows = slice(sub * sub_rows, (sub + 1) * sub_rows)
            q = q_ref[sample, rows, :]
            outs = []
            for h in range(B_HEADS):
                s = _dot_nt(q[:, h * LANES:(h + 1) * LANES], k[:, h * LANES:(h + 1) * LANES])
                outs.append(_softmax_pv(s, v[:, h * LANES:(h + 1) * LANES], h))
            o_ref[sample, rows, :] = _merge_head_pairs(outs).astype(BF16)


def _full_attention(kernel, name, tq, q, k, v, extra, extra_specs, out_width, q_width, k_width, v_width):
    out = pl.pallas_call(
        kernel,
        out_shape=jax.ShapeDtypeStruct((BATCH, NT_PAD, out_width), BF16),
        grid=(BATCH, SEQ // tq),
        in_specs=[pl.BlockSpec((1, tq, q_width), lambda b, t: (b, t, 0)),
                  pl.BlockSpec((1, NT_PAD, k_width), lambda b, t: (b, 0, 0)),
                  pl.BlockSpec((1, NT_PAD, v_width), lambda b, t: (b, 0, 0))] + extra_specs(2),
        out_specs=pl.BlockSpec((1, tq, out_width), lambda b, t: (b, t, 0)),
        compiler_params=_params(("arbitrary", "arbitrary")),
        name=name + "_latent",
    )(q, k, v, *extra)
    return out


def _full_attention_ctx(kernel, name, q, k, v, extra, extra_specs, out_width, q_width, k_width, v_width,
                        prev_out):
    ctx_blk = SEQ // CTX_LEN
    n_extra = len(extra)

    def body(*refs):
        ins = refs[:3 + n_extra]
        kernel(*ins, refs[-1])

    spec = lambda n: pl.BlockSpec((BATCH, CTX_LEN, n), lambda i: (0, ctx_blk, 0))
    tile = lambda n: pl.BlockSpec((BATCH, TOK_TILE, n), lambda i: (0, N_LAT_TILES, 0))
    return pl.pallas_call(
        body,
        out_shape=jax.ShapeDtypeStruct((BATCH, NT_PAD, out_width), BF16),
        grid=(1,),
        in_specs=[tile(q_width), spec(k_width), spec(v_width)] + extra_specs(1)
                 + [pl.BlockSpec(memory_space=pl.ANY)],
        out_specs=tile(out_width),
        input_output_aliases={3 + n_extra: 0},
        compiler_params=_params(("arbitrary",)),
        name=name + "_ctx",
    )(q, k, v, *extra, prev_out)


def _attn_c_kernel(q_ref, k_ref, v_ref, lq1_ref, lk1_ref, lq2_ref, lk2_ref, sn_ref, o_ref, *, lam_init):
    tq = ATT_SUB
    lam = (jnp.exp(jnp.sum(lq1_ref[...] * lk1_ref[...], axis=-1, keepdims=True))
           - jnp.exp(jnp.sum(lq2_ref[...] * lk2_ref[...], axis=-1, keepdims=True)) + lam_init)
    lane = lax.broadcasted_iota(jnp.int32, (1, C_HEADS * C_V), 1)
    units = [(sample, sub) for sample in range(q_ref.shape[0]) for sub in range(q_ref.shape[1] // ATT_SUB)]
    for sample, sub in units:
        k = _key_rows(k_ref, sample)
        v = _key_rows(v_ref, sample)
        rows = slice(sub * ATT_SUB, (sub + 1) * ATT_SUB)
        q = q_ref[sample, rows, :]
        outs = []
        for h in range(C_HEADS):
            lo = h * C_V
            q1 = q * _indicator((lane >= lo) & (lane < lo + C_DH))
            q2 = q * _indicator((lane >= lo + C_DH) & (lane < lo + C_V))
            s = _dot_nt(jnp.concatenate([q1, q2], axis=0), k)
            pv = _softmax_pv(s, v[:, h * LANES:(h + 1) * LANES], h)
            outs.append(pv[0:tq] - lam * pv[tq:2 * tq])
        acc = _merge_head_pairs(outs)
        sq = acc * acc
        inv = jnp.zeros((tq, C_HEADS * C_V), F32)
        for h in range(C_HEADS):
            in_head = (lane >= h * C_V) & (lane < (h + 1) * C_V)
            ms = jnp.sum(jnp.where(in_head, sq, 0.0), axis=-1, keepdims=True) * (1.0 / C_V)
            inv = inv + jnp.where(in_head, lax.rsqrt(ms + EPS), 0.0)
        o_ref[sample, rows, :] = ((acc * inv) * sn_ref[...] * (1.0 - lam_init)).astype(BF16)


def _outproj_kernel(x_ref, oa_ref, ob_ref, od_ref, w_ref, modb_ref, modc_ref, g2_ref, wr_ref,
                    xo_ref, h2_ref, aff_ref, *, n_chunks):
    w_hi, w_lo = _split_bf16(wr_ref[...])
    w_router = jnp.concatenate([w_hi, w_lo], axis=-1)
    lane = lax.broadcasted_iota(jnp.int32, (1, LANES), 1)
    for j in range(n_chunks):
        rows = slice(j * TOK_TILE, (j + 1) * TOK_TILE)
        mod = _chunk_mod(modb_ref, modc_ref, j, n_chunks)
        g1 = mod[:, 2 * D_MODEL:3 * D_MODEL]
        sh2 = mod[:, 3 * D_MODEL:4 * D_MODEL]
        sc2 = mod[:, 4 * D_MODEL:5 * D_MODEL]
        mix = jnp.concatenate([oa_ref[0, rows, :], ob_ref[0, rows, :], od_ref[0, rows, :]], axis=-1)
        x = x_ref[0, rows, :] + g1 * _dot(mix, w_ref[...])
        xo_ref[0, rows, :] = x
        h2 = (_rms(x) * g2_ref[...]) * (1.0 + sc2) + sh2
        h_hi, h_lo = _split_bf16(h2)
        h2_ref[0, rows, :] = h_hi
        prod = _dot(jnp.concatenate([h_hi, h_lo], axis=0), w_router)
        logits = (prod[0:TOK_TILE, 0:LANES] + prod[0:TOK_TILE, LANES:2 * LANES]
                  + prod[TOK_TILE:2 * TOK_TILE, 0:LANES])
        aff_ref[0, rows, :] = _softmax_rows(jnp.where(lane < N_EXPERTS, logits, NEG_INF))


def _outproj(xall, oa, ob, od, w, mod_l, g2, wr, with_ctx):
    n_chunks = STEP_CHUNKS if with_ctx else LAT_STEP_CHUNKS
    n_steps = (N_TILES if with_ctx else N_LAT_TILES) // n_chunks
    full = lambda shape: pl.BlockSpec(shape, lambda b, t: (0,) * len(shape))
    tok = lambda n: pl.BlockSpec((1, n_chunks * TOK_TILE, n), lambda b, t: (b, t, 0))
    return pl.pallas_call(
        functools.partial(_outproj_kernel, n_chunks=n_chunks),
        out_shape=[jax.ShapeDtypeStruct((BATCH, NT_PAD, D_MODEL), F32),
                   jax.ShapeDtypeStruct((BATCH, NT_PAD, D_MODEL), BF16),
                   jax.ShapeDtypeStruct((BATCH, NT_PAD, LANES), F32)],
        grid=(BATCH, n_steps),
        in_specs=[tok(D_MODEL), tok(512), tok(256), tok(256), full((D_MODEL, D_MODEL))] + _mod_specs()
                 + [full((1, D_MODEL)), full((D_MODEL, LANES))],
        out_specs=[tok(D_MODEL), tok(D_MODEL), tok(LANES)],
        compiler_params=_params(("arbitrary", "arbitrary")),
        name="out_projection",
    )(xall, oa, ob, od, w, mod_l, mod_l, g2, wr)


def _prefix_count(flags_f32, tri_ref):
    rows, n = flags_f32.shape
    run = jnp.zeros((rows, 1), F32)
    outs = []
    for c in range(n // LANES):
        blk = flags_f32[:, c * LANES:(c + 1) * LANES]
        outs.append(_dot(blk.astype(BF16), tri_ref[...]) + run)
        run = run + jnp.sum(blk, axis=-1, keepdims=True)
    return jnp.concatenate(outs, axis=-1)


def _select_slots(a, cap, tri_ref):
    bits = pltpu.bitcast(a, jnp.int32)

    def step(i, t):
        cand = t | jnp.left_shift(jnp.int32(1), 30 - i)
        cnt = jnp.sum((bits >= cand).astype(jnp.int32), axis=-1, keepdims=True)
        return jnp.where(cnt >= cap, cand, t)

    thr = lax.fori_loop(0, 31, step, jnp.zeros((a.shape[0], 1), jnp.int32))
    above = bits > thr
    tied = bits == thr
    need = cap - jnp.sum(above.astype(F32), axis=-1, keepdims=True)
    tied_f = tied.astype(F32)
    sel = above | (tied & (_prefix_count(tied_f, tri_ref) < need))
    sel_f = sel.astype(F32)
    return jnp.where(sel, _prefix_count(sel_f, tri_ref), -1.0)


def _route_kernel(aff_ref, tri_ref, row_ref, arow_ref, col_ref, *, with_ctx):
    lat, ctx = [], []
    for b in range(BATCH):
        lat.append(aff_ref[b, 0:SEQ, :].T[0:N_EXPERTS])
        if with_ctx:
            ctx.append(aff_ref[b, SEQ:NT, :].T[0:N_EXPERTS])
    aff_lat = jnp.concatenate(lat, axis=0)
    arow_ref[:, 0:SEQ] = aff_lat
    slot_lat = _select_slots(aff_lat, CAP_LAT, tri_ref)
    if with_ctx:
        aff_ctx = jnp.concatenate(ctx, axis=0)
        arow_ref[:, SEQ:NT] = aff_ctx
        slot_ctx = _select_slots(aff_ctx, CAP_CTX, tri_ref)
        slots = jnp.concatenate([slot_lat, slot_ctx], axis=-1)
    else:
        slots = slot_lat
    n = slots.shape[1]
    row_ref[:, 0:n] = slots
    pad = jnp.full((LANES - N_EXPERTS, n), -1.0, F32)
    for b in range(BATCH):
        blk = jnp.concatenate([slots[b * N_EXPERTS:(b + 1) * N_EXPERTS], pad], axis=0)
        col_ref[b, 0:n, :] = blk.T
        if with_ctx and NT_PAD > NT:
            col_ref[b, NT:NT_PAD, :] = jnp.full((NT_PAD - NT, LANES), -1.0, F32)


def _route(aff, tri, with_ctx):
    vm = pl.BlockSpec(memory_space=pltpu.VMEM)
    return pl.pallas_call(
        functools.partial(_route_kernel, with_ctx=with_ctx),
        out_shape=[jax.ShapeDtypeStruct((BATCH * N_EXPERTS, NT), F32),
                   jax.ShapeDtypeStruct((BATCH * N_EXPERTS, NT), F32),
                   jax.ShapeDtypeStruct((BATCH, NT_PAD, LANES), F32)],
        in_specs=[vm, vm],
        out_specs=[vm, vm, vm],
        compiler_params=pltpu.CompilerParams(vmem_limit_bytes=VMEM_LIMIT),
        name="route_topc",
    )(aff, tri)


def _gather_kernel(row_ref, arow_ref, h_ref, *out_refs, with_ctx):
    slot = lax.broadcasted_iota(jnp.int32, (CAP_LAT, 1), 0).astype(F32)

    def pick(e, cols, n_slots, h_rows, x_ref, g_ref):
        hot = jnp.where(row_ref[e, :, cols] == slot[0:n_slots], 1.0, 0.0)
        x_ref[e] = _dot(hot.astype(BF16), h_rows).astype(BF16)
        gate = jnp.sum(hot * arow_ref[e, :, cols], axis=-1, keepdims=True)
        g_ref[e] = jnp.broadcast_to(gate, (n_slots, LANES))

    for e in range(N_EXPERTS):
        pick(e, slice(0, SEQ), CAP_LAT, h_ref[0, 0:SEQ, :], out_refs[0], out_refs[1])
        if with_ctx:
            pick(e, slice(SEQ, NT), CAP_CTX, h_ref[0, SEQ:NT, :], out_refs[2], out_refs[3])


def _gather(slot_rows, aff_rows, h2, with_ctx):
    out_shape = [jax.ShapeDtypeStruct((N_EXPERTS, SEQ, D_MODEL), BF16),
                 jax.ShapeDtypeStruct((N_EXPERTS, SEQ, LANES), F32)]
    out_specs = [pl.BlockSpec((N_EXPERTS, CAP_LAT, D_MODEL), lambda b: (0, b, 0)),
                 pl.BlockSpec((N_EXPERTS, CAP_LAT, LANES), lambda b: (0, b, 0))]
    if with_ctx:
        out_shape += [jax.ShapeDtypeStruct((N_EXPERTS, CTX_LEN, D_MODEL), BF16),
                      jax.ShapeDtypeStruct((N_EXPERTS, CTX_LEN, LANES), F32)]
        out_specs += [pl.BlockSpec((N_EXPERTS, CAP_CTX, D_MODEL), lambda b: (0, b, 0)),
                      pl.BlockSpec((N_EXPERTS, CAP_CTX, LANES), lambda b: (0, b, 0))]
    row_spec = pl.BlockSpec((N_EXPERTS, 1, NT), lambda b: (b, 0, 0))
    return pl.pallas_call(
        functools.partial(_gather_kernel, with_ctx=with_ctx),
        out_shape=out_shape,
        grid=(BATCH,),
        in_specs=[row_spec, row_spec, pl.BlockSpec((1, NT_PAD, D_MODEL), lambda b: (b, 0, 0))],
        out_specs=out_specs,
        compiler_params=_params(("arbitrary",)),
        name="expert_gather",
    )(slot_rows, aff_rows, h2)


def _ffn_kernel(*refs, with_ctx):
    if with_ctx:
        xl_ref, gl_ref, xc_ref, gc_ref, wg_ref, wu_ref, wd_ref, y_ref = refs
    else:
        xl_ref, gl_ref, wg_ref, wu_ref, wd_ref, y_ref = refs
    wg = wg_ref[0, 0].astype(BF16)
    wu = wu_ref[0, 0].astype(BF16)
    wd = wd_ref[0, 0].astype(BF16)

    def swiglu(x, slot_gate):
        gate = _dot(x, wg)
        hid = (gate / (1.0 + jnp.exp(-gate))) * _dot(x, wu)
        scale = jnp.concatenate([slot_gate] * (D_MODEL // LANES), axis=-1)
        return (_dot(hid.astype(BF16), wd) * scale).astype(BF16)

    for r in range(FFN_LAT_TILES):
        rows = slice(r * FFN_TILE, (r + 1) * FFN_TILE)
        y_ref[0, rows, :] = swiglu(xl_ref[0, rows, :], gl_ref[0, rows, :])
    if with_ctx:
        y_ref[0, SEQ:NT, :] = swiglu(xc_ref[0], gc_ref[0])


def _ffn(gathered, w_gate, w_up, w_down, layer):
    with_ctx = len(gathered) == 4
    ins = list(gathered) + [w_gate, w_up, w_down]
    in_specs = [pl.BlockSpec((1, SEQ, D_MODEL), lambda e: (e, 0, 0)),
                pl.BlockSpec((1, SEQ, LANES), lambda e: (e, 0, 0))]
    if with_ctx:
        in_specs += [pl.BlockSpec((1, CTX_LEN, D_MODEL), lambda e: (e, 0, 0)),
                     pl.BlockSpec((1, CTX_LEN, LANES), lambda e: (e, 0, 0))]
    in_specs += [pl.BlockSpec((1, 1, D_MODEL, D_EXPERT), lambda e: (layer, e, 0, 0)),
                 pl.BlockSpec((1, 1, D_MODEL, D_EXPERT), lambda e: (layer, e, 0, 0)),
                 pl.BlockSpec((1, 1, D_EXPERT, D_MODEL), lambda e: (layer, e, 0, 0))]
    return pl.pallas_call(
        functools.partial(_ffn_kernel, with_ctx=with_ctx),
        out_shape=jax.ShapeDtypeStruct((N_EXPERTS, NT, D_MODEL), BF16),
        grid=(N_EXPERTS,),
        in_specs=in_specs,
        out_specs=pl.BlockSpec((1, NT, D_MODEL), lambda e: (e, 0, 0)),
        compiler_params=_params(("arbitrary",)),
        name="expert_ffn",
    )(*ins)


def _combine_kernel(*refs, with_ctx, final):
    if with_ctx:
        x_ref, col_ref, yl_ref, yc_ref, modb_ref, modc_ref, o_ref = refs
    elif final:
        x_ref, col_ref, yl_ref, modb_ref, modc_ref, fn_ref, o_ref = refs
    else:
        x_ref, col_ref, yl_ref, modb_ref, modc_ref, o_ref = refs
    n_chunks = STEP_CHUNKS if with_ctx else LAT_STEP_CHUNKS

    def chunk(j, is_context):
        rows = slice(j * TOK_TILE, (j + 1) * TOK_TILE)
        g2 = _chunk_mod(modb_ref, modc_ref, j, n_chunks)[:, 5 * D_MODEL:6 * D_MODEL]
        col = col_ref[0, rows, :]
        n_slots = LANES if is_context else CAP_LAT
        lane = lax.broadcasted_iota(jnp.int32, (1, n_slots), 1).astype(F32)
        pad = jnp.zeros((LANES - CAP_CTX, D_MODEL), BF16)
        hot, ys = [], []
        for e in range(N_EXPERTS):
            hot.append(_indicator(col[:, e:e + 1] == lane))
            ys.append(jnp.concatenate([yc_ref[e], pad], axis=0) if is_context else yl_ref[e])
        acc = _dot(jnp.concatenate(hot, axis=-1), jnp.concatenate(ys, axis=0))
        x = x_ref[0, rows, :] + g2 * acc
        if final:
            x = _rms(x) * fn_ref[...]
        o_ref[0, rows, :] = x

    for j in range(n_chunks):
        if with_ctx and j == n_chunks - 1:
            is_last = pl.program_id(1) == N_TILES // STEP_CHUNKS - 1
            pl.when(jnp.logical_not(is_last))(functools.partial(chunk, j, False))
            pl.when(is_last)(functools.partial(chunk, j, True))
        else:
            chunk(j, False)


def _combine(xall, slot_cols, y, mod_l, with_ctx, final_norm=None):
    final = final_norm is not None
    n_chunks = STEP_CHUNKS if with_ctx else LAT_STEP_CHUNKS
    n_steps = (N_TILES if with_ctx else N_LAT_TILES) // n_chunks
    n_rows = SEQ if final else NT_PAD
    tok = lambda n: pl.BlockSpec((1, n_chunks * TOK_TILE, n), lambda b, t: (b, t, 0))
    ins = [xall, slot_cols, y]
    in_specs = [tok(D_MODEL), tok(LANES),
                pl.BlockSpec((N_EXPERTS, CAP_LAT, D_MODEL), lambda b, t: (0, b, 0))]
    if with_ctx:
        ins.append(y)
        in_specs.append(pl.BlockSpec((N_EXPERTS, CAP_CTX, D_MODEL), lambda b, t: (0, SEQ // CAP_CTX + b, 0)))
    ins += [mod_l, mod_l]
    in_specs += _mod_specs()
    if final:
        ins.append(final_norm)
        in_specs.append(pl.BlockSpec((1, D_MODEL), lambda b, t: (0, 0)))
    return pl.pallas_call(
        functools.partial(_combine_kernel, with_ctx=with_ctx, final=final),
        out_shape=jax.ShapeDtypeStruct((BATCH, n_rows, D_MODEL), F32),
        grid=(BATCH, n_steps),
        in_specs=in_specs,
        out_specs=tok(D_MODEL),
        compiler_params=_params(("arbitrary", "arbitrary")),
        name="expert_combine",
    )(*ins)


def _rope_tables():
    t = jnp.arange(SEQ)
    pos = jnp.stack([t // GRID_W, t % GRID_W], axis=0)

    def table(n_f, lanes_per_unit):
        inv = ROPE_BASE ** (-jnp.arange(n_f, dtype=F32) / n_f)
        d = np.arange(lanes_per_unit)
        axis, second, f = d // (2 * n_f), (d // n_f) % 2, d % n_f
        ang = pos[axis].T.astype(F32) * inv[f][None, :]
        cos = jnp.cos(ang)
        sin = jnp.sin(ang) * jnp.where(second == 1, 1.0, -1.0)[None, :]
        return cos, sin

    def finish(cos, sin, scale):
        cos = jnp.concatenate([cos, jnp.ones((NT_PAD - SEQ, LANES), F32)], axis=0)
        sin = jnp.concatenate([sin, jnp.zeros((NT_PAD - SEQ, LANES), F32)], axis=0)
        scale = scale * LOG2_E
        return [cos, sin, cos * scale, sin * scale]

    cos_a, sin_a = table(HEAD_DIM // 4, HEAD_DIM)
    tabs_a = finish(jnp.tile(cos_a, (1, 2)), jnp.tile(sin_a, (1, 2)), HEAD_DIM ** -0.5)
    cos_8, sin_8 = table(B_ROPE // 4, B_ROPE)
    tabs_8 = finish(jnp.tile(cos_8, (1, 4)), jnp.tile(sin_8, (1, 4)), C_DH ** -0.5)
    ones = jnp.ones((SEQ, B_NOPE), F32)
    pad1 = jnp.ones((SEQ, LANES - B_NOPE - B_ROPE), F32)
    cos_b = jnp.concatenate([ones, cos_8, pad1], axis=1)
    sin_b = jnp.concatenate([0 * ones, sin_8, 0 * pad1], axis=1)
    tabs_b = finish(cos_b, sin_b, (B_NOPE + B_ROPE) ** -0.5)
    return tabs_a + tabs_8 + tabs_b


def _window_bias():
    qi = np.arange(WIN)[:, None]
    kj = np.arange(A_LOCAL)[None, :]
    out = np.zeros((3, WIN, A_LOCAL + CTX_LEN), np.float32)
    for v in range(3):
        out[v, :, :A_LOCAL] = np.where(np.abs(kj - v * WIN - qi) <= WIN, 0.0, NEG_INF)
    return jnp.asarray(out)


def _prep_weights(w_in, w_uq, w_ukv, w_out, w_router):
    offs = np.cumsum((0, 512, 128, 128, 256, 128, 32, 256, 256, 256))
    seg = lambda i: w_in[:, :, offs[i]:offs[i + 1]]
    wq = seg(0).reshape(DEPTH, D_MODEL, A_KV_HEADS, A_GROUP, HEAD_DIM)
    wq = jnp.swapaxes(wq, 2, 3).reshape(DEPTH, D_MODEL, A_HEADS * HEAD_DIM)
    kr = jnp.pad(seg(5), ((0, 0), (0, 0), (B_NOPE, LANES - B_NOPE - B_ROPE)))
    w = jnp.concatenate([wq, seg(1), seg(2), seg(3), seg(4), kr,
                         seg(6), seg(7), seg(8)], axis=-1).astype(BF16)
    wuq = w_uq.reshape(DEPTH, B_Q_RANK, B_HEADS, B_NOPE + B_ROPE)
    wuq = jnp.pad(wuq, ((0, 0), (0, 0), (0, 0), (0, LANES - B_NOPE - B_ROPE)))
    wuq = wuq.reshape(DEPTH, B_Q_RANK, B_HEADS * LANES).astype(BF16)
    wukv = w_ukv.reshape(DEPTH, B_KV_RANK, B_HEADS, B_NOPE + B_V)
    wk = jnp.pad(wukv[..., :B_NOPE], ((0, 0), (0, 0), (0, 0), (0, LANES - B_NOPE)))
    wukv = jnp.concatenate([wk.reshape(DEPTH, B_KV_RANK, B_HEADS * LANES),
                            wukv[..., B_NOPE:].reshape(DEPTH, B_KV_RANK, B_HEADS * B_V)],
                           axis=-1).astype(BF16)
    wo_a = w_out[:, :A_HEADS * HEAD_DIM].reshape(DEPTH, A_KV_HEADS, A_GROUP, HEAD_DIM, D_MODEL)
    wo_a = jnp.swapaxes(wo_a, 1, 2).reshape(DEPTH, A_HEADS * HEAD_DIM, D_MODEL)
    wo = jnp.concatenate([wo_a, w_out[:, A_HEADS * HEAD_DIM:]], axis=1).astype(BF16)
    wr = jnp.pad(w_router, ((0, 0), (0, 0), (0, LANES - N_EXPERTS)))
    return w, wuq, wukv, wo, wr


def kernel(x, c, ctx, c_ctx, norm1, norm2, w_ada, b_ada, w_in, sink, mla_q_norm, w_uq, mla_kv_norm, w_ukv,
           lam_q1, lam_k1, lam_q2, lam_k2, diff_norm, w_out, w_router, w_gate, w_up, w_down, final_norm):
    xall = jnp.concatenate([x, ctx, jnp.zeros((BATCH, NT_PAD - NT, D_MODEL), F32)], axis=1)
    cc = jnp.concatenate([c, c_ctx[None, :], jnp.zeros((16 - BATCH - 1, D_MODEL), F32)], axis=0)
    mod = _modulation(cc, w_ada, b_ada).reshape(DEPTH, 16, 1, 6 * D_MODEL)
    tabs = _rope_tables()
    bias = _window_bias()
    tri = jnp.asarray(np.triu(np.ones((LANES, LANES), np.float32), k=1), BF16)
    w, wuq, wukv, wo, wr = _prep_weights(w_in, w_uq, w_ukv, w_out, w_router)
    sub_norm = jnp.tile(diff_norm, (1, C_HEADS)).reshape(DEPTH, 1, C_HEADS * C_V)
    row = lambda a, l: a[l].reshape(1, -1)

    for l in range(DEPTH):
        need_ctx = l < DEPTH - 1
        qa, ka, va, qb, kb, vb, qd, kd, vd = _inproj(
            xall, mod[l], row(norm1, l), w[l], row(mla_q_norm, l), wuq[l], row(mla_kv_norm, l), wukv[l], tabs)

        oa = _attn_a_latent(sink[l], qa, ka, va, bias)
        ob = _full_attention(_attn_b_kernel, "attn_mla", MLA_TQ, qb, kb, vb, [], lambda n: [],
                             256, 512, 512, 512)
        lam_init = 0.8 - 0.6 * math.exp(-0.3 * l)
        c_kernel = functools.partial(_attn_c_kernel, lam_init=lam_init)
        c_extra = [row(lam_q1, l), row(lam_k1, l), row(lam_q2, l), row(lam_k2, l), sub_norm[l]]

        def c_specs(n_grid, c_extra=c_extra):
            zmap = (lambda b, t: (0, 0)) if n_grid == 2 else (lambda b: (0, 0))
            return [pl.BlockSpec(a.shape, zmap) for a in c_extra]

        od = _full_attention(c_kernel, "attn_diff", DIFF_TQ, qd, kd, vd, c_extra, c_specs, 256, 256, 256, 512)
        if need_ctx:
            oa = _attn_a_ctx(sink[l], qa, ka, va, oa)
            ob = _full_attention_ctx(_attn_b_kernel, "attn_mla", qb, kb, vb, [], lambda n: [],
                                     256, 512, 512, 512, ob)
            od = _full_attention_ctx(c_kernel, "attn_diff", qd, kd, vd, c_extra, c_specs,
                                     256, 256, 256, 512, od)

        xall, h2, aff = _outproj(xall, oa, ob, od, wo[l], mod[l], row(norm2, l), wr[l], need_ctx)
        slot_rows, aff_rows, slot_cols = _route(aff, tri, need_ctx)
        per_row = lambda a: a.reshape(BATCH * N_EXPERTS, 1, NT)
        gathered = _gather(per_row(slot_rows), per_row(aff_rows), h2, need_ctx)
        y = _ffn(gathered, w_gate, w_up, w_down, l)
        xall = _combine(xall, slot_cols, y, mod[l], need_ctx,
                        final_norm=None if need_ctx else final_norm.reshape(1, D_MODEL))
    return xall
```

```python
import functools
import math

import numpy as np
import jax
import jax.numpy as jnp
from jax import lax
from jax.experimental import pallas as pl
from jax.experimental.pallas import tpu as pltpu

D_MODEL = 1024
BATCH = 8
SEQ = 2048
DEPTH = 4
CTX_LEN = 256
NT = SEQ + CTX_LEN
GRID_W = 64
HEAD_DIM = 64
ROPE_BASE = 10000.0
EPS = 1e-6
NEG_INF = -1e30
LOG2_E = math.log2(math.e)

WIN = 128
A_HEADS = 8
A_KV_HEADS = 2
A_GROUP = A_HEADS // A_KV_HEADS
B_HEADS = 4
B_NOPE = 64
B_ROPE = 32
B_V = 64
B_Q_RANK = 256
B_KV_RANK = 128
C_HEADS = 4
C_DH = 32
C_V = 2 * C_DH
N_EXPERTS = 16
EC_FACTOR = 2
D_EXPERT = 512
CAP_LAT = EC_FACTOR * SEQ // N_EXPERTS
CAP_CTX = EC_FACTOR * CTX_LEN // N_EXPERTS

LANES = 128
TOK_TILE = 256
NT_PAD = SEQ + TOK_TILE
N_TILES = NT_PAD // TOK_TILE
N_LAT_TILES = SEQ // TOK_TILE
STEP_CHUNKS = 3
LAT_STEP_CHUNKS = 2
FFN_TILE = 512
FFN_LAT_TILES = SEQ // FFN_TILE
VMEM_LIMIT = 56 * 1024 * 1024

F32 = jnp.float32
BF16 = jnp.bfloat16

_NT_DIMS = (((1,), (1,)), ((), ()))


def _params(sem):
    return pltpu.CompilerParams(dimension_semantics=sem, vmem_limit_bytes=VMEM_LIMIT)


def _dot(a, b):
    return jnp.dot(a, b, preferred_element_type=F32)


def _dot_nt(a, b):
    return lax.dot_general(a, b, _NT_DIMS, preferred_element_type=F32)


def _split_bf16(v):
    hi = v.astype(BF16)
    lo = (v - hi.astype(F32)).astype(BF16)
    return hi, lo


def _dot3(a, b):
    a_hi, a_lo = _split_bf16(a)
    b_hi, b_lo = _split_bf16(b)
    return _dot(a_hi, b_hi) + _dot(a_lo, b_hi) + _dot(a_hi, b_lo)


def _indicator(cond):
    return jnp.where(cond, 1.0, 0.0).astype(BF16)


def _rms(v):
    return v * lax.rsqrt(jnp.mean(v * v, axis=-1, keepdims=True) + EPS)


def _softmax_pv(s, v_blk, head):
    m = jnp.max(s, axis=-1, keepdims=True)
    ol = _dot(jnp.exp2(s - m).astype(BF16), v_blk)
    ones_lane = 64 if head % 2 == 0 else 0
    return ol / ol[:, ones_lane:ones_lane + 1]


def _merge_head_pairs(per_head):
    lane = lax.broadcasted_iota(jnp.int32, (1, LANES), 1)
    blocks = [jnp.where(lane < 64, per_head[j], per_head[j + 1]) for j in range(0, len(per_head), 2)]
    return jnp.concatenate(blocks, axis=-1)


def _softmax_rows(s):
    m = jnp.max(s, axis=-1, keepdims=True)
    e = jnp.exp(s - m)
    return e / jnp.sum(e, axis=-1, keepdims=True)


def _mod_kernel(c_ref, w_ref, b_ref, o_ref):
    c = c_ref[...]
    s = c / (1.0 + jnp.exp(-c))
    o_ref[0] = _dot3(s, w_ref[0]) + b_ref[0]


def _modulation(cc, w_ada, b_ada):
    tn = 1536
    return pl.pallas_call(
        _mod_kernel,
        out_shape=jax.ShapeDtypeStruct((DEPTH, 16, 6 * D_MODEL), F32),
        grid=(DEPTH, 6 * D_MODEL // tn),
        in_specs=[
            pl.BlockSpec((16, D_MODEL), lambda l, j: (0, 0)),
            pl.BlockSpec((1, D_MODEL, tn), lambda l, j: (l, 0, j)),
            pl.BlockSpec((1, 1, tn), lambda l, j: (l, 0, j)),
        ],
        out_specs=pl.BlockSpec((1, 16, tn), lambda l, j: (l, 0, j)),
        compiler_params=_params(("arbitrary", "arbitrary")),
        name="adaln_mod",
    )(cc, w_ada, b_ada.reshape(DEPTH, 1, 6 * D_MODEL))


def _rope(p, cos, sin, half):
    lane = lax.broadcasted_iota(jnp.int32, (1, LANES), 1)
    first = (lane // half) % 2 == 0
    outs = []
    for j in range(p.shape[1] // LANES):
        blk = p[:, j * LANES:(j + 1) * LANES]
        partner = jnp.where(first, pltpu.roll(blk, LANES - half, 1), pltpu.roll(blk, half, 1))
        outs.append(blk * cos + partner * sin)
    return outs[0] if len(outs) == 1 else jnp.concatenate(outs, axis=-1)


def _per_head_with_ones(v):
    lane = lax.broadcasted_iota(jnp.int32, (1, LANES), 1)
    outs = []
    for j in range(v.shape[1] // LANES):
        blk = v[:, j * LANES:(j + 1) * LANES]
        outs.append(jnp.where(lane < 64, blk, jnp.where(lane == 64, 1.0, 0.0)))
        outs.append(jnp.where(lane >= 64, blk, jnp.where(lane == 0, 1.0, 0.0)))
    return jnp.concatenate(outs, axis=-1)


def _chunk_mod(modb_ref, modc_ref, j, n_chunks):
    mod = modb_ref[0]
    if n_chunks == STEP_CHUNKS and j == n_chunks - 1:
        mod = jnp.where(pl.program_id(1) == N_TILES // STEP_CHUNKS - 1, modc_ref[0], mod)
    return mod


def _inproj_kernel(x_ref, modb_ref, modc_ref, g1_ref, w_ref, qn_ref, wuq_ref, kvn_ref, wukv_ref,
                   ca_ref, sa_ref, caq_ref, saq_ref, c8_ref, s8_ref, c8q_ref, s8q_ref,
                   cb_ref, sb_ref, cbq_ref, sbq_ref,
                   qa_ref, ka_ref, va_ref, qb_ref, kb_ref, vb_ref, qd_ref, kd_ref, vd_ref):
    hs = []
    for j in range(STEP_CHUNKS):
        x = x_ref[0, j * TOK_TILE:(j + 1) * TOK_TILE, :]
        mod = _chunk_mod(modb_ref, modc_ref, j, STEP_CHUNKS)
        sh1 = mod[:, 0:D_MODEL]
        sc1 = mod[:, D_MODEL:2 * D_MODEL]
        hs.append(((_rms(x) * g1_ref[...]) * (1.0 + sc1) + sh1).astype(BF16))
    p_all = _dot(jnp.concatenate(hs, axis=0), w_ref[...])

    for j in range(STEP_CHUNKS):
        rows = slice(j * TOK_TILE, (j + 1) * TOK_TILE)
        p = p_all[rows]

        qa_ref[0, rows, :] = _rope(p[:, 0:512], caq_ref[rows, :], saq_ref[rows, :], 16).astype(BF16)
        ka_ref[0, rows, :] = _rope(p[:, 512:640], ca_ref[rows, :], sa_ref[rows, :], 16).astype(BF16)
        lane = lax.broadcasted_iota(jnp.int32, (TOK_TILE, LANES), 1)
        va_ref[0, rows, :] = jnp.concatenate(
            [p[:, 640:768], jnp.where(lane == 0, 1.0, 0.0)], axis=-1).astype(BF16)

        cq = (_rms(p[:, 768:1024]) * qn_ref[...]).astype(BF16)
        qb = _dot(cq, wuq_ref[...])
        qb_ref[0, rows, :] = _rope(qb, cbq_ref[rows, :], sbq_ref[rows, :], 8).astype(BF16)
        ckv = (_rms(p[:, 1024:1152]) * kvn_ref[...]).astype(BF16)
        kv = _dot(ckv, wukv_ref[...])
        kr = _rope(p[:, 1152:1280], cb_ref[rows, :], sb_ref[rows, :], 8)
        kb_ref[0, rows, :] = (kv[:, 0:512] + jnp.concatenate([kr] * B_HEADS, axis=-1)).astype(BF16)
        vb_ref[0, rows, :] = _per_head_with_ones(kv[:, 512:768]).astype(BF16)

        qd_ref[0, rows, :] = _rope(p[:, 1280:1536], c8q_ref[rows, :], s8q_ref[rows, :], 8).astype(BF16)
        kd_ref[0, rows, :] = _rope(p[:, 1536:1792], c8_ref[rows, :], s8_ref[rows, :], 8).astype(BF16)
        vd_ref[0, rows, :] = _per_head_with_ones(p[:, 1792:2048]).astype(BF16)


def _mod_specs():
    return [pl.BlockSpec((1, 1, 6 * D_MODEL), lambda b, t: (b, 0, 0)),
            pl.BlockSpec((1, 1, 6 * D_MODEL), lambda b, t: (BATCH, 0, 0))]


def _inproj(xall, mod_l, g1, w, qn, wuq, kvn, wukv, tabs):
    full = lambda shape: pl.BlockSpec(shape, lambda b, t: (0,) * len(shape))
    step_rows = STEP_CHUNKS * TOK_TILE
    tab_spec = pl.BlockSpec((step_rows, LANES), lambda b, t: (t, 0))
    widths = (512, 128, 256, 512, 512, 512, 256, 256, 512)
    return pl.pallas_call(
        _inproj_kernel,
        out_shape=[jax.ShapeDtypeStruct((BATCH, NT_PAD, n), BF16) for n in widths],
        grid=(BATCH, N_TILES // STEP_CHUNKS),
        in_specs=[pl.BlockSpec((1, step_rows, D_MODEL), lambda b, t: (b, t, 0))] + _mod_specs() + [
            full((1, D_MODEL)),
            full((D_MODEL, 2048)),
            full((1, B_Q_RANK)),
            full((B_Q_RANK, 512)),
            full((1, B_KV_RANK)),
            full((B_KV_RANK, 768)),
        ] + [tab_spec] * 12,
        out_specs=[pl.BlockSpec((1, step_rows, n), lambda b, t: (b, t, 0)) for n in widths],
        compiler_params=_params(("arbitrary", "arbitrary")),
        name="in_projection",
    )(xall, mod_l, mod_l, g1, w, qn, wuq, kvn, wukv, *tabs)


def _window_heads(q, kcat, vcat, bias, sink_ref):
    tq = q.shape[0]
    lane = lax.broadcasted_iota(jnp.int32, (1, LANES), 1)
    acc = [jnp.zeros((tq, LANES), F32) for _ in range(A_GROUP)]
    for g in range(A_KV_HEADS):
        in_half = (lane < HEAD_DIM) if g == 0 else (lane >= HEAD_DIM)
        keep = _indicator(in_half)
        qs = jnp.concatenate([q[:, j * LANES:(j + 1) * LANES] * keep for j in range(A_GROUP)], axis=0)
        s = _dot_nt(qs, kcat)
        if bias is not None:
            s = s + jnp.concatenate([bias] * A_GROUP, axis=0)
        sink = jnp.concatenate(
            [jnp.full((tq, 1), sink_ref[g * A_GROUP + j] * LOG2_E, F32) for j in range(A_GROUP)], axis=0)
        m = jnp.maximum(jnp.max(s, axis=-1, keepdims=True), sink)
        ol = _dot(jnp.exp2(s - m).astype(BF16), vcat)
        o = ol[:, 0:LANES] / (ol[:, LANES:LANES + 1] + jnp.exp2(sink - m))
        for j in range(A_GROUP):
            acc[j] = acc[j] + jnp.where(in_half, o[j * tq:(j + 1) * tq], 0.0)
    return jnp.concatenate(acc, axis=-1).astype(BF16)


A_STEP_BLOCKS = 4
A_LOCAL = 3 * WIN


def _attn_a_latent_kernel(sink_ref, q_ref, k_ref, v_ref, bias_ref, o_ref):
    step = pl.program_id(1)
    last = SEQ // WIN - 1
    k_ctx = k_ref[0, SEQ:NT, :]
    v_ctx = v_ref[0, SEQ:NT, :]
    for j in range(A_STEP_BLOCKS):
        n = step * A_STEP_BLOCKS + j
        start = pl.multiple_of(jnp.clip((n - 1) * WIN, 0, SEQ - A_LOCAL), WIN)
        variant = jnp.where(n == 0, 0, jnp.where(n == last, 2, 1))
        kcat = jnp.concatenate([k_ref[0, pl.ds(start, A_LOCAL), :], k_ctx], axis=0)
        vcat = jnp.concatenate([v_ref[0, pl.ds(start, A_LOCAL), :], v_ctx], axis=0)
        rows = slice(j * WIN, (j + 1) * WIN)
        o_ref[0, rows, :] = _window_heads(q_ref[0, rows, :], kcat, vcat, bias_ref[variant], sink_ref)


def _attn_a_latent(sink, qa, ka, va, bias):
    tq = A_STEP_BLOCKS * WIN
    return pl.pallas_call(
        _attn_a_latent_kernel,
        out_shape=jax.ShapeDtypeStruct((BATCH, NT_PAD, 512), BF16),
        grid=(BATCH, SEQ // tq),
        in_specs=[pl.BlockSpec(memory_space=pltpu.SMEM),
                  pl.BlockSpec((1, tq, 512), lambda b, n: (b, n, 0)),
                  pl.BlockSpec((1, NT_PAD, LANES), lambda b, n: (b, 0, 0)),
                  pl.BlockSpec((1, NT_PAD, 2 * LANES), lambda b, n: (b, 0, 0)),
                  pl.BlockSpec((3, WIN, A_LOCAL + CTX_LEN), lambda b, n: (0, 0, 0))],
        out_specs=pl.BlockSpec((1, tq, 512), lambda b, n: (b, n, 0)),
        compiler_params=_params(("arbitrary", "arbitrary")),
        name="attn_window_latent",
    )(sink, qa, ka, va, bias)


def _attn_a_ctx(sink, qa, ka, va, prev_out):
    ctx_blk = SEQ // CTX_LEN
    spec = lambda n: pl.BlockSpec((BATCH, CTX_LEN, n), lambda i: (0, ctx_blk, 0))
    tile = lambda n: pl.BlockSpec((BATCH, TOK_TILE, n), lambda i: (0, N_LAT_TILES, 0))

    def body(sink_ref, q_ref, k_ref, v_ref, prev_ref, o_ref):
        del prev_ref
        for sample in range(BATCH):
            o_ref[sample] = _window_heads(q_ref[sample], k_ref[sample], v_ref[sample], None, sink_ref)

    return pl.pallas_call(
        body,
        out_shape=jax.ShapeDtypeStruct((BATCH, NT_PAD, 512), BF16),
        grid=(1,),
        in_specs=[pl.BlockSpec(memory_space=pltpu.SMEM), tile(512), spec(LANES), spec(2 * LANES),
                  pl.BlockSpec(memory_space=pl.ANY)],
        out_specs=tile(512),
        input_output_aliases={4: 0},
        compiler_params=_params(("arbitrary",)),
        name="attn_window_ctx",
    )(sink, qa, ka, va, prev_out)


def _key_rows(ref, sample):
    return ref[sample, 0:NT, :] if ref.shape[1] == NT_PAD else ref[sample]


MLA_TQ = 1024
MLA_SUB = 512
DIFF_TQ = 1024
DIFF_SUB = 512


def _attn_b_kernel(q_ref, k_ref, v_ref, o_ref):
    for sample in range(q_ref.shape[0]):
        k = _key_rows(k_ref, sample)
        v = _key_rows(v_ref, sample)
        sub_rows = min(MLA_SUB, q_ref.shape[1])
        for sub in range(q_ref.shape[1] // sub_rows):
            rows = slice(sub * sub_rows, (sub + 1) * sub_rows)
            q = q_ref[sample, rows, :]
            outs = []
            for h in range(B_HEADS):
                s = _dot_nt(q[:, h * LANES:(h + 1) * LANES], k[:, h * LANES:(h + 1) * LANES])
                outs.append(_softmax_pv(s, v[:, h * LANES:(h + 1) * LANES], h))
            o_ref[sample, rows, :] = _merge_head_pairs(outs).astype(BF16)


def _full_attention(kernel, name, tq, q, k, v, extra, extra_specs, out_width, q_width, k_width, v_width):
    out = pl.pallas_call(
        kernel,
        out_shape=jax.ShapeDtypeStruct((BATCH, NT_PAD, out_width), BF16),
        grid=(BATCH, SEQ // tq),
        in_specs=[pl.BlockSpec((1, tq, q_width), lambda b, t: (b, t, 0)),
                  pl.BlockSpec((1, NT_PAD, k_width), lambda b, t: (b, 0, 0)),
                  pl.BlockSpec((1, NT_PAD, v_width), lambda b, t: (b, 0, 0))] + extra_specs(2),
        out_specs=pl.BlockSpec((1, tq, out_width), lambda b, t: (b, t, 0)),
        compiler_params=_params(("arbitrary", "arbitrary")),
        name=name + "_latent",
    )(q, k, v, *extra)
    return out


def _full_attention_ctx(kernel, name, q, k, v, extra, extra_specs, out_width, q_width, k_width, v_width,
                        prev_out):
    ctx_blk = SEQ // CTX_LEN
    n_extra = len(extra)

    def body(*refs):
        ins = refs[:3 + n_extra]
        kernel(*ins, refs[-1])

    spec = lambda n: pl.BlockSpec((BATCH, CTX_LEN, n), lambda i: (0, ctx_blk, 0))
    tile = lambda n: pl.BlockSpec((BATCH, TOK_TILE, n), lambda i: (0, N_LAT_TILES, 0))
    return pl.pallas_call(
        body,
        out_shape=jax.ShapeDtypeStruct((BATCH, NT_PAD, out_width), BF16),
        grid=(1,),
        in_specs=[tile(q_width), spec(k_width), spec(v_width)] + extra_specs(1)
                 + [pl.BlockSpec(memory_space=pl.ANY)],
        out_specs=tile(out_width),
        input_output_aliases={3 + n_extra: 0},
        compiler_params=_params(("arbitrary",)),
        name=name + "_ctx",
    )(q, k, v, *extra, prev_out)


def _attn_c_kernel(q_ref, k_ref, v_ref, lq1_ref, lk1_ref, lq2_ref, lk2_ref, sn_ref, o_ref, *, lam_init):
    tq = min(DIFF_SUB, q_ref.shape[1])
    lam = (jnp.exp(jnp.sum(lq1_ref[...] * lk1_ref[...], axis=-1, keepdims=True))
           - jnp.exp(jnp.sum(lq2_ref[...] * lk2_ref[...], axis=-1, keepdims=True)) + lam_init)
    lane = lax.broadcasted_iota(jnp.int32, (1, C_HEADS * C_V), 1)
    units = [(sample, sub) for sample in range(q_ref.shape[0]) for sub in range(q_ref.shape[1] // tq)]
    for sample, sub in units:
        k = _key_rows(k_ref, sample)
        v = _key_rows(v_ref, sample)
        rows = slice(sub * tq, (sub + 1) * tq)
        q = q_ref[sample, rows, :]
        outs = []
        for h in range(C_HEADS):
            lo = h * C_V
            q1 = q * _indicator((lane >= lo) & (lane < lo + C_DH))
            q2 = q * _indicator((lane >= lo + C_DH) & (lane < lo + C_V))
            s = _dot_nt(jnp.concatenate([q1, q2], axis=0), k)
            pv = _softmax_pv(s, v[:, h * LANES:(h + 1) * LANES], h)
            outs.append(pv[0:tq] - lam * pv[tq:2 * tq])
        acc = _merge_head_pairs(outs)
        sq = acc * acc
        inv = jnp.zeros((tq, C_HEADS * C_V), F32)
        for h in range(C_HEADS):
            in_head = (lane >= h * C_V) & (lane < (h + 1) * C_V)
            ms = jnp.sum(jnp.where(in_head, sq, 0.0), axis=-1, keepdims=True) * (1.0 / C_V)
            inv = inv + jnp.where(in_head, lax.rsqrt(ms + EPS), 0.0)
        o_ref[sample, rows, :] = ((acc * inv) * sn_ref[...] * (1.0 - lam_init)).astype(BF16)


def _outproj_kernel(x_ref, oa_ref, ob_ref, od_ref, w_ref, modb_ref, modc_ref, g2_ref, wr_ref,
                    xo_ref, h2_ref, aff_ref, *, n_chunks):
    w_hi, w_lo = _split_bf16(wr_ref[...])
    w_router = jnp.concatenate([w_hi, w_lo], axis=-1)
    lane = lax.broadcasted_iota(jnp.int32, (1, LANES), 1)
    mixed = _dot(jnp.concatenate([oa_ref[0], ob_ref[0], od_ref[0]], axis=-1), w_ref[...])
    for j in range(n_chunks):
        rows = slice(j * TOK_TILE, (j + 1) * TOK_TILE)
        mod = _chunk_mod(modb_ref, modc_ref, j, n_chunks)
        g1 = mod[:, 2 * D_MODEL:3 * D_MODEL]
        sh2 = mod[:, 3 * D_MODEL:4 * D_MODEL]
        sc2 = mod[:, 4 * D_MODEL:5 * D_MODEL]
        x = x_ref[0, rows, :] + g1 * mixed[rows]
        xo_ref[0, rows, :] = x
        h2 = (_rms(x) * g2_ref[...]) * (1.0 + sc2) + sh2
        h_hi, h_lo = _split_bf16(h2)
        h2_ref[0, rows, :] = h_hi
        prod = _dot(jnp.concatenate([h_hi, h_lo], axis=0), w_router)
        logits = (prod[0:TOK_TILE, 0:LANES] + prod[0:TOK_TILE, LANES:2 * LANES]
                  + prod[TOK_TILE:2 * TOK_TILE, 0:LANES])
        aff_ref[0, rows, :] = _softmax_rows(jnp.where(lane < N_EXPERTS, logits, NEG_INF))


def _outproj(xall, oa, ob, od, w, mod_l, g2, wr, with_ctx):
    n_chunks = STEP_CHUNKS if with_ctx else LAT_STEP_CHUNKS
    n_steps = (N_TILES if with_ctx else N_LAT_TILES) // n_chunks
    full = lambda shape: pl.BlockSpec(shape, lambda b, t: (0,) * len(shape))
    tok = lambda n: pl.BlockSpec((1, n_chunks * TOK_TILE, n), lambda b, t: (b, t, 0))
    return pl.pallas_call(
        functools.partial(_outproj_kernel, n_chunks=n_chunks),
        out_shape=[jax.ShapeDtypeStruct((BATCH, NT_PAD, D_MODEL), F32),
                   jax.ShapeDtypeStruct((BATCH, NT_PAD, D_MODEL), BF16),
                   jax.ShapeDtypeStruct((BATCH, NT_PAD, LANES), F32)],
        grid=(BATCH, n_steps),
        in_specs=[tok(D_MODEL), tok(512), tok(256), tok(256), full((D_MODEL, D_MODEL))] + _mod_specs()
                 + [full((1, D_MODEL)), full((D_MODEL, LANES))],
        out_specs=[tok(D_MODEL), tok(D_MODEL), tok(LANES)],
        compiler_params=_params(("arbitrary", "arbitrary")),
        name="out_projection",
    )(xall, oa, ob, od, w, mod_l, mod_l, g2, wr)


def _prefix_count(flags_f32, tri_ref):
    rows, n = flags_f32.shape
    run = jnp.zeros((rows, 1), F32)
    outs = []
    for c in range(n // LANES):
        blk = flags_f32[:, c * LANES:(c + 1) * LANES]
        outs.append(_dot(blk.astype(BF16), tri_ref[...]) + run)
        run = run + jnp.sum(blk, axis=-1, keepdims=True)
    return jnp.concatenate(outs, axis=-1)


def _select_slots(a, cap, tri_ref):
    bits = pltpu.bitcast(a, jnp.int32)

    def step(i, t):
        cand = t | jnp.left_shift(jnp.int32(1), 30 - i)
        cnt = jnp.sum((bits >= cand).astype(jnp.int32), axis=-1, keepdims=True)
        return jnp.where(cnt >= cap, cand, t)

    thr = lax.fori_loop(0, 31, step, jnp.zeros((a.shape[0], 1), jnp.int32))
    above = bits > thr
    tied = bits == thr
    need = cap - jnp.sum(above.astype(F32), axis=-1, keepdims=True)
    tied_f = tied.astype(F32)
    sel = above | (tied & (_prefix_count(tied_f, tri_ref) < need))
    sel_f = sel.astype(F32)
    return jnp.where(sel, _prefix_count(sel_f, tri_ref), -1.0)


def _route_kernel(aff_ref, tri_ref, row_ref, arow_ref, col_ref, *, with_ctx):
    lat, ctx = [], []
    for b in range(BATCH):
        lat.append(aff_ref[b, 0:SEQ, :].T[0:N_EXPERTS])
        if with_ctx:
            ctx.append(aff_ref[b, SEQ:NT, :].T[0:N_EXPERTS])
    aff_lat = jnp.concatenate(lat, axis=0)
    arow_ref[:, 0:SEQ] = aff_lat
    slot_lat = _select_slots(aff_lat, CAP_LAT, tri_ref)
    if with_ctx:
        aff_ctx = jnp.concatenate(ctx, axis=0)
        arow_ref[:, SEQ:NT] = aff_ctx
        slot_ctx = _select_slots(aff_ctx, CAP_CTX, tri_ref)
        slots = jnp.concatenate([slot_lat, slot_ctx], axis=-1)
    else:
        slots = slot_lat
    n = slots.shape[1]
    row_ref[:, 0:n] = slots
    pad = jnp.full((LANES - N_EXPERTS, n), -1.0, F32)
    for b in range(BATCH):
        blk = jnp.concatenate([slots[b * N_EXPERTS:(b + 1) * N_EXPERTS], pad], axis=0)
        col_ref[b, 0:n, :] = blk.T
        if with_ctx and NT_PAD > NT:
            col_ref[b, NT:NT_PAD, :] = jnp.full((NT_PAD - NT, LANES), -1.0, F32)


def _route(aff, tri, with_ctx):
    vm = pl.BlockSpec(memory_space=pltpu.VMEM)
    return pl.pallas_call(
        functools.partial(_route_kernel, with_ctx=with_ctx),
        out_shape=[jax.ShapeDtypeStruct((BATCH * N_EXPERTS, NT), F32),
                   jax.ShapeDtypeStruct((BATCH * N_EXPERTS, NT), F32),
                   jax.ShapeDtypeStruct((BATCH, NT_PAD, LANES), F32)],
        in_specs=[vm, vm],
        out_specs=[vm, vm, vm],
        compiler_params=pltpu.CompilerParams(vmem_limit_bytes=VMEM_LIMIT),
        name="route_topc",
    )(aff, tri)


GATHER_GROUP = 4


def _gather_kernel(row_ref, arow_ref, h_ref, *out_refs, with_ctx):
    slot = lax.broadcasted_iota(jnp.int32, (CAP_LAT, 1), 0).astype(F32)

    def pick(e0, cols, n_slots, h_rows, x_ref, g_ref):
        hots = []
        for e in range(e0, e0 + GATHER_GROUP):
            hot = jnp.where(row_ref[e, :, cols] == slot[0:n_slots], 1.0, 0.0)
            gate = jnp.sum(hot * arow_ref[e, :, cols], axis=-1, keepdims=True)
            g_ref[e] = jnp.broadcast_to(gate, (n_slots, LANES))
            hots.append(hot.astype(BF16))
        x = _dot(jnp.concatenate(hots, axis=0), h_rows).astype(BF16)
        for i in range(GATHER_GROUP):
            x_ref[e0 + i] = x[i * n_slots:(i + 1) * n_slots]

    for e0 in range(0, N_EXPERTS, GATHER_GROUP):
        pick(e0, slice(0, SEQ), CAP_LAT, h_ref[0, 0:SEQ, :], out_refs[0], out_refs[1])
        if with_ctx:
            pick(e0, slice(SEQ, NT), CAP_CTX, h_ref[0, SEQ:NT, :], out_refs[2], out_refs[3])


def _gather(slot_rows, aff_rows, h2, with_ctx):
    out_shape = [jax.ShapeDtypeStruct((N_EXPERTS, SEQ, D_MODEL), BF16),
                 jax.ShapeDtypeStruct((N_EXPERTS, SEQ, LANES), F32)]
    out_specs = [pl.BlockSpec((N_EXPERTS, CAP_LAT, D_MODEL), lambda b: (0, b, 0)),
                 pl.BlockSpec((N_EXPERTS, CAP_LAT, LANES), lambda b: (0, b, 0))]
    if with_ctx:
        out_shape += [jax.ShapeDtypeStruct((N_EXPERTS, CTX_LEN, D_MODEL), BF16),
                      jax.ShapeDtypeStruct((N_EXPERTS, CTX_LEN, LANES), F32)]
        out_specs += [pl.BlockSpec((N_EXPERTS, CAP_CTX, D_MODEL), lambda b: (0, b, 0)),
                      pl.BlockSpec((N_EXPERTS, CAP_CTX, LANES), lambda b: (0, b, 0))]
    row_spec = pl.BlockSpec((N_EXPERTS, 1, NT), lambda b: (b, 0, 0))
    return pl.pallas_call(
        functools.partial(_gather_kernel, with_ctx=with_ctx),
        out_shape=out_shape,
        grid=(BATCH,),
        in_specs=[row_spec, row_spec, pl.BlockSpec((1, NT_PAD, D_MODEL), lambda b: (b, 0, 0))],
        out_specs=out_specs,
        compiler_params=_params(("arbitrary",)),
        name="expert_gather",
    )(slot_rows, aff_rows, h2)


def _ffn_kernel(*refs, with_ctx):
    if with_ctx:
        xl_ref, gl_ref, xc_ref, gc_ref, wg_ref, wu_ref, wd_ref, y_ref = refs
    else:
        xl_ref, gl_ref, wg_ref, wu_ref, wd_ref, y_ref = refs
    wg = wg_ref[0, 0].astype(BF16)
    wu = wu_ref[0, 0].astype(BF16)
    wd = wd_ref[0, 0].astype(BF16)

    def swiglu(x, slot_gate):
        gate = _dot(x, wg)
        hid = (gate / (1.0 + jnp.exp(-gate))) * _dot(x, wu)
        scale = jnp.concatenate([slot_gate] * (D_MODEL // LANES), axis=-1)
        return (_dot(hid.astype(BF16), wd) * scale).astype(BF16)

    for r in range(FFN_LAT_TILES):
        rows = slice(r * FFN_TILE, (r + 1) * FFN_TILE)
        y_ref[0, rows, :] = swiglu(xl_ref[0, rows, :], gl_ref[0, rows, :])
    if with_ctx:
        y_ref[0, SEQ:NT, :] = swiglu(xc_ref[0], gc_ref[0])


def _ffn(gathered, w_gate, w_up, w_down, layer):
    with_ctx = len(gathered) == 4
    ins = list(gathered) + [w_gate, w_up, w_down]
    in_specs = [pl.BlockSpec((1, SEQ, D_MODEL), lambda e: (e, 0, 0)),
                pl.BlockSpec((1, SEQ, LANES), lambda e: (e, 0, 0))]
    if with_ctx:
        in_specs += [pl.BlockSpec((1, CTX_LEN, D_MODEL), lambda e: (e, 0, 0)),
                     pl.BlockSpec((1, CTX_LEN, LANES), lambda e: (e, 0, 0))]
    in_specs += [pl.BlockSpec((1, 1, D_MODEL, D_EXPERT), lambda e: (layer, e, 0, 0)),
                 pl.BlockSpec((1, 1, D_MODEL, D_EXPERT), lambda e: (layer, e, 0, 0)),
                 pl.BlockSpec((1, 1, D_EXPERT, D_MODEL), lambda e: (layer, e, 0, 0))]
    return pl.pallas_call(
        functools.partial(_ffn_kernel, with_ctx=with_ctx),
        out_shape=jax.ShapeDtypeStruct((N_EXPERTS, NT, D_MODEL), BF16),
        grid=(N_EXPERTS,),
        in_specs=in_specs,
        out_specs=pl.BlockSpec((1, NT, D_MODEL), lambda e: (e, 0, 0)),
        compiler_params=_params(("arbitrary",)),
        name="expert_ffn",
    )(*ins)


def _combine_kernel(*refs, with_ctx, final):
    if with_ctx:
        x_ref, col_ref, yl_ref, yc_ref, modb_ref, modc_ref, o_ref = refs
    elif final:
        x_ref, col_ref, yl_ref, modb_ref, modc_ref, fn_ref, o_ref = refs
    else:
        x_ref, col_ref, yl_ref, modb_ref, modc_ref, o_ref = refs
    n_chunks = STEP_CHUNKS if with_ctx else LAT_STEP_CHUNKS

    def chunk(j, is_context):
        rows = slice(j * TOK_TILE, (j + 1) * TOK_TILE)
        g2 = _chunk_mod(modb_ref, modc_ref, j, n_chunks)[:, 5 * D_MODEL:6 * D_MODEL]
        col = col_ref[0, rows, :]
        n_slots = LANES if is_context else CAP_LAT
        lane = lax.broadcasted_iota(jnp.int32, (1, n_slots), 1).astype(F32)
        pad = jnp.zeros((LANES - CAP_CTX, D_MODEL), BF16)
        hot, ys = [], []
        for e in range(N_EXPERTS):
            hot.append(_indicator(col[:, e:e + 1] == lane))
            ys.append(jnp.concatenate([yc_ref[e], pad], axis=0) if is_context else yl_ref[e])
        acc = _dot(jnp.concatenate(hot, axis=-1), jnp.concatenate(ys, axis=0))
        x = x_ref[0, rows, :] + g2 * acc
        if final:
            x = _rms(x) * fn_ref[...]
        o_ref[0, rows, :] = x

    for j in range(n_chunks):
        if with_ctx and j == n_chunks - 1:
            is_last = pl.program_id(1) == N_TILES // STEP_CHUNKS - 1
            pl.when(jnp.logical_not(is_last))(functools.partial(chunk, j, False))
            pl.when(is_last)(functools.partial(chunk, j, True))
        else:
            chunk(j, False)


def _combine(xall, slot_cols, y, mod_l, with_ctx, final_norm=None):
    final = final_norm is not None
    n_chunks = STEP_CHUNKS if with_ctx else LAT_STEP_CHUNKS
    n_steps = (N_TILES if with_ctx else N_LAT_TILES) // n_chunks
    n_rows = SEQ if final else NT_PAD
    tok = lambda n: pl.BlockSpec((1, n_chunks * TOK_TILE, n), lambda b, t: (b, t, 0))
    ins = [xall, slot_cols, y]
    in_specs = [tok(D_MODEL), tok(LANES),
                pl.BlockSpec((N_EXPERTS, CAP_LAT, D_MODEL), lambda b, t: (0, b, 0))]
    if with_ctx:
        ins.append(y)
        in_specs.append(pl.BlockSpec((N_EXPERTS, CAP_CTX, D_MODEL), lambda b, t: (0, SEQ // CAP_CTX + b, 0)))
    ins += [mod_l, mod_l]
    in_specs += _mod_specs()
    if final:
        ins.append(final_norm)
        in_specs.append(pl.BlockSpec((1, D_MODEL), lambda b, t: (0, 0)))
    return pl.pallas_call(
        functools.partial(_combine_kernel, with_ctx=with_ctx, final=final),
        out_shape=jax.ShapeDtypeStruct((BATCH, n_rows, D_MODEL), F32),
        grid=(BATCH, n_steps),
        in_specs=in_specs,
        out_specs=tok(D_MODEL),
        compiler_params=_params(("arbitrary", "arbitrary")),
        name="expert_combine",
    )(*ins)


def _rope_tables():
    t = jnp.arange(SEQ)
    pos = jnp.stack([t // GRID_W, t % GRID_W], axis=0)

    def table(n_f, lanes_per_unit):
        inv = ROPE_BASE ** (-jnp.arange(n_f, dtype=F32) / n_f)
        d = np.arange(lanes_per_unit)
        axis, second, f = d // (2 * n_f), (d // n_f) % 2, d % n_f
        ang = pos[axis].T.astype(F32) * inv[f][None, :]
        cos = jnp.cos(ang)
        sin = jnp.sin(ang) * jnp.where(second == 1, 1.0, -1.0)[None, :]
        return cos, sin

    def finish(cos, sin, scale):
        cos = jnp.concatenate([cos, jnp.ones((NT_PAD - SEQ, LANES), F32)], axis=0)
        sin = jnp.concatenate([sin, jnp.zeros((NT_PAD - SEQ, LANES), F32)], axis=0)
        scale = scale * LOG2_E
        return [cos, sin, cos * scale, sin * scale]

    cos_a, sin_a = table(HEAD_DIM // 4, HEAD_DIM)
    tabs_a = finish(jnp.tile(cos_a, (1, 2)), jnp.tile(sin_a, (1, 2)), HEAD_DIM ** -0.5)
    cos_8, sin_8 = table(B_ROPE // 4, B_ROPE)
    tabs_8 = finish(jnp.tile(cos_8, (1, 4)), jnp.tile(sin_8, (1, 4)), C_DH ** -0.5)
    ones = jnp.ones((SEQ, B_NOPE), F32)
    pad1 = jnp.ones((SEQ, LANES - B_NOPE - B_ROPE), F32)
    cos_b = jnp.concatenate([ones, cos_8, pad1], axis=1)
    sin_b = jnp.concatenate([0 * ones, sin_8, 0 * pad1], axis=1)
    tabs_b = finish(cos_b, sin_b, (B_NOPE + B_ROPE) ** -0.5)
    return tabs_a + tabs_8 + tabs_b


def _window_bias():
    qi = np.arange(WIN)[:, None]
    kj = np.arange(A_LOCAL)[None, :]
    out = np.zeros((3, WIN, A_LOCAL + CTX_LEN), np.float32)
    for v in range(3):
        out[v, :, :A_LOCAL] = np.where(np.abs(kj - v * WIN - qi) <= WIN, 0.0, NEG_INF)
    return jnp.asarray(out)


def _prep_weights(w_in, w_uq, w_ukv, w_out, w_router):
    offs = np.cumsum((0, 512, 128, 128, 256, 128, 32, 256, 256, 256))
    seg = lambda i: w_in[:, :, offs[i]:offs[i + 1]]
    wq = seg(0).reshape(DEPTH, D_MODEL, A_KV_HEADS, A_GROUP, HEAD_DIM)
    wq = jnp.swapaxes(wq, 2, 3).reshape(DEPTH, D_MODEL, A_HEADS * HEAD_DIM)
    kr = jnp.pad(seg(5), ((0, 0), (0, 0), (B_NOPE, LANES - B_NOPE - B_ROPE)))
    w = jnp.concatenate([wq, seg(1), seg(2), seg(3), seg(4), kr,
                         seg(6), seg(7), seg(8)], axis=-1).astype(BF16)
    wuq = w_uq.reshape(DEPTH, B_Q_RANK, B_HEADS, B_NOPE + B_ROPE)
    wuq = jnp.pad(wuq, ((0, 0), (0, 0), (0, 0), (0, LANES - B_NOPE - B_ROPE)))
    wuq = wuq.reshape(DEPTH, B_Q_RANK, B_HEADS * LANES).astype(BF16)
    wukv = w_ukv.reshape(DEPTH, B_KV_RANK, B_HEADS, B_NOPE + B_V)
    wk = jnp.pad(wukv[..., :B_NOPE], ((0, 0), (0, 0), (0, 0), (0, LANES - B_NOPE)))
    wukv = jnp.concatenate([wk.reshape(DEPTH, B_KV_RANK, B_HEADS * LANES),
                            wukv[..., B_NOPE:].reshape(DEPTH, B_KV_RANK, B_HEADS * B_V)],
                           axis=-1).astype(BF16)
    wo_a = w_out[:, :A_HEADS * HEAD_DIM].reshape(DEPTH, A_KV_HEADS, A_GROUP, HEAD_DIM, D_MODEL)
    wo_a = jnp.swapaxes(wo_a, 1, 2).reshape(DEPTH, A_HEADS * HEAD_DIM, D_MODEL)
    wo = jnp.concatenate([wo_a, w_out[:, A_HEADS * HEAD_DIM:]], axis=1).astype(BF16)
    wr = jnp.pad(w_router, ((0, 0), (0, 0), (0, LANES - N_EXPERTS)))
    return w, wuq, wukv, wo, wr


def kernel(x, c, ctx, c_ctx, norm1, norm2, w_ada, b_ada, w_in, sink, mla_q_norm, w_uq, mla_kv_norm, w_ukv,
           lam_q1, lam_k1, lam_q2, lam_k2, diff_norm, w_out, w_router, w_gate, w_up, w_down, final_norm):
    xall = jnp.concatenate([x, ctx, jnp.zeros((BATCH, NT_PAD - NT, D_MODEL), F32)], axis=1)
    cc = jnp.concatenate([c, c_ctx[None, :], jnp.zeros((16 - BATCH - 1, D_MODEL), F32)], axis=0)
    mod = _modulation(cc, w_ada, b_ada).reshape(DEPTH, 16, 1, 6 * D_MODEL)
    tabs = _rope_tables()
    bias = _window_bias()
    tri = jnp.asarray(np.triu(np.ones((LANES, LANES), np.float32), k=1), BF16)
    w, wuq, wukv, wo, wr = _prep_weights(w_in, w_uq, w_ukv, w_out, w_router)
    sub_norm = jnp.tile(diff_norm, (1, C_HEADS)).reshape(DEPTH, 1, C_HEADS * C_V)
    row = lambda a, l: a[l].reshape(1, -1)

    for l in range(DEPTH):
        need_ctx = l < DEPTH - 1
        qa, ka, va, qb, kb, vb, qd, kd, vd = _inproj(
            xall, mod[l], row(norm1, l), w[l], row(mla_q_norm, l), wuq[l], row(mla_kv_norm, l), wukv[l], tabs)

        oa = _attn_a_latent(sink[l], qa, ka, va, bias)
        ob = _full_attention(_attn_b_kernel, "attn_mla", MLA_TQ, qb, kb, vb, [], lambda n: [],
                             256, 512, 512, 512)
        lam_init = 0.8 - 0.6 * math.exp(-0.3 * l)
        c_kernel = functools.partial(_attn_c_kernel, lam_init=lam_init)
        c_extra = [row(lam_q1, l), row(lam_k1, l), row(lam_q2, l), row(lam_k2, l), sub_norm[l]]

        def c_specs(n_grid, c_extra=c_extra):
            zmap = (lambda b, t: (0, 0)) if n_grid == 2 else (lambda b: (0, 0))
            return [pl.BlockSpec(a.shape, zmap) for a in c_extra]

        od = _full_attention(c_kernel, "attn_diff", DIFF_TQ, qd, kd, vd, c_extra, c_specs, 256, 256, 256, 512)
        if need_ctx:
            oa = _attn_a_ctx(sink[l], qa, ka, va, oa)
            ob = _full_attention_ctx(_attn_b_kernel, "attn_mla", qb, kb, vb, [], lambda n: [],
                                     256, 512, 512, 512, ob)
            od = _full_attention_ctx(c_kernel, "attn_diff", qd, kd, vd, c_extra, c_specs,
                                     256, 256, 256, 512, od)

        xall, h2, aff = _outproj(xall, oa, ob, od, wo[l], mod[l], row(norm2, l), wr[l], need_ctx)
        slot_rows, aff_rows, slot_cols = _route(aff, tri, need_ctx)
        per_row = lambda a: a.reshape(BATCH * N_EXPERTS, 1, NT)
        gathered = _gather(per_row(slot_rows), per_row(aff_rows), h2, need_ctx)
        y = _ffn(gathered, w_gate, w_up, w_down, l)
        xall = _combine(xall, slot_cols, y, mod[l], need_ctx,
                        final_norm=None if need_ctx else final_norm.reshape(1, D_MODEL))
    return xall
```

```python
import functools
import math

import numpy as np
import jax
import jax.numpy as jnp
from jax import lax
from jax.experimental import pallas as pl
from jax.experimental.pallas import tpu as pltpu

D_MODEL = 1024
BATCH = 8
SEQ = 2048
DEPTH = 4
CTX_LEN = 256
NT = SEQ + CTX_LEN
GRID_W = 64
HEAD_DIM = 64
ROPE_BASE = 10000.0
EPS = 1e-6
NEG_INF = -1e30
LOG2_E = math.log2(math.e)

WIN = 128
A_HEADS = 8
A_KV_HEADS = 2
A_GROUP = A_HEADS // A_KV_HEADS
B_HEADS = 4
B_NOPE = 64
B_ROPE = 32
B_V = 64
B_Q_RANK = 256
B_KV_RANK = 128
C_HEADS = 4
C_DH = 32
C_V = 2 * C_DH
N_EXPERTS = 16
EC_FACTOR = 2
D_EXPERT = 512
CAP_LAT = EC_FACTOR * SEQ // N_EXPERTS
CAP_CTX = EC_FACTOR * CTX_LEN // N_EXPERTS

LANES = 128
TOK_TILE = 256
NT_PAD = SEQ + TOK_TILE
N_TILES = NT_PAD // TOK_TILE
N_LAT_TILES = SEQ // TOK_TILE
STEP_CHUNKS = 3
LAT_STEP_CHUNKS = 2
FFN_TILE = 512
FFN_LAT_TILES = SEQ // FFN_TILE
VMEM_LIMIT = 56 * 1024 * 1024

F32 = jnp.float32
BF16 = jnp.bfloat16

_NT_DIMS = (((1,), (1,)), ((), ()))


def _params(sem):
    return pltpu.CompilerParams(dimension_semantics=sem, vmem_limit_bytes=VMEM_LIMIT)


def _dot(a, b):
    return jnp.dot(a, b, preferred_element_type=F32)


def _dot_nt(a, b):
    return lax.dot_general(a, b, _NT_DIMS, preferred_element_type=F32)


def _split_bf16(v):
    hi = v.astype(BF16)
    lo = (v - hi.astype(F32)).astype(BF16)
    return hi, lo


def _dot3(a, b):
    a_hi, a_lo = _split_bf16(a)
    b_hi, b_lo = _split_bf16(b)
    return _dot(a_hi, b_hi) + _dot(a_lo, b_hi) + _dot(a_hi, b_lo)


def _indicator(cond):
    return jnp.where(cond, 1.0, 0.0).astype(BF16)


def _rms(v):
    return v * lax.rsqrt(jnp.mean(v * v, axis=-1, keepdims=True) + EPS)


def _softmax_pv(s, v_blk, head):
    m = jnp.max(s, axis=-1, keepdims=True)
    ol = _dot(jnp.exp2(s - m).astype(BF16), v_blk)
    ones_lane = 64 if head % 2 == 0 else 0
    return ol / ol[:, ones_lane:ones_lane + 1]


def _merge_head_pairs(per_head):
    lane = lax.broadcasted_iota(jnp.int32, (1, LANES), 1)
    blocks = [jnp.where(lane < 64, per_head[j], per_head[j + 1]) for j in range(0, len(per_head), 2)]
    return jnp.concatenate(blocks, axis=-1)


def _softmax_rows(s):
    m = jnp.max(s, axis=-1, keepdims=True)
    e = jnp.exp(s - m)
    return e / jnp.sum(e, axis=-1, keepdims=True)


def _mod_kernel(c_ref, w_ref, b_ref, o_ref):
    c = c_ref[...]
    s = c / (1.0 + jnp.exp(-c))
    o_ref[0] = _dot3(s, w_ref[0]) + b_ref[0]


def _modulation(cc, w_ada, b_ada):
    tn = 1536
    return pl.pallas_call(
        _mod_kernel,
        out_shape=jax.ShapeDtypeStruct((DEPTH, 16, 6 * D_MODEL), F32),
        grid=(DEPTH, 6 * D_MODEL // tn),
        in_specs=[
            pl.BlockSpec((16, D_MODEL), lambda l, j: (0, 0)),
            pl.BlockSpec((1, D_MODEL, tn), lambda l, j: (l, 0, j)),
            pl.BlockSpec((1, 1, tn), lambda l, j: (l, 0, j)),
        ],
        out_specs=pl.BlockSpec((1, 16, tn), lambda l, j: (l, 0, j)),
        compiler_params=_params(("arbitrary", "arbitrary")),
        name="adaln_mod",
    )(cc, w_ada, b_ada.reshape(DEPTH, 1, 6 * D_MODEL))


def _rope(p, cos, sin, half):
    lane = lax.broadcasted_iota(jnp.int32, (1, LANES), 1)
    first = (lane // half) % 2 == 0
    outs = []
    for j in range(p.shape[1] // LANES):
        blk = p[:, j * LANES:(j + 1) * LANES]
        partner = jnp.where(first, pltpu.roll(blk, LANES - half, 1), pltpu.roll(blk, half, 1))
        outs.append(blk * cos + partner * sin)
    return outs[0] if len(outs) == 1 else jnp.concatenate(outs, axis=-1)


def _per_head_with_ones(v):
    lane = lax.broadcasted_iota(jnp.int32, (1, LANES), 1)
    outs = []
    for j in range(v.shape[1] // LANES):
        blk = v[:, j * LANES:(j + 1) * LANES]
        outs.append(jnp.where(lane < 64, blk, jnp.where(lane == 64, 1.0, 0.0)))
        outs.append(jnp.where(lane >= 64, blk, jnp.where(lane == 0, 1.0, 0.0)))
    return jnp.concatenate(outs, axis=-1)


def _chunk_mod(modb_ref, modc_ref, j, n_chunks):
    mod = modb_ref[0]
    if n_chunks == STEP_CHUNKS and j == n_chunks - 1:
        mod = jnp.where(pl.program_id(1) == N_TILES // STEP_CHUNKS - 1, modc_ref[0], mod)
    return mod


def _inproj_kernel(x_ref, modb_ref, modc_ref, g1_ref, w_ref, qn_ref, wuq_ref, kvn_ref, wukv_ref,
                   ca_ref, sa_ref, caq_ref, saq_ref, c8_ref, s8_ref, c8q_ref, s8q_ref,
                   cb_ref, sb_ref, cbq_ref, sbq_ref,
                   qa_ref, ka_ref, va_ref, qb_ref, kb_ref, vb_ref, qd_ref, kd_ref, vd_ref):
    for j in range(STEP_CHUNKS):
        rows = slice(j * TOK_TILE, (j + 1) * TOK_TILE)
        x = x_ref[0, rows, :]
        mod = _chunk_mod(modb_ref, modc_ref, j, STEP_CHUNKS)
        sh1 = mod[:, 0:D_MODEL]
        sc1 = mod[:, D_MODEL:2 * D_MODEL]
        h = (_rms(x) * g1_ref[...]) * (1.0 + sc1) + sh1
        p = _dot(h.astype(BF16), w_ref[...])

        qa_ref[0, rows, :] = _rope(p[:, 0:512], caq_ref[rows, :], saq_ref[rows, :], 16).astype(BF16)
        ka_ref[0, rows, :] = _rope(p[:, 512:640], ca_ref[rows, :], sa_ref[rows, :], 16).astype(BF16)
        lane = lax.broadcasted_iota(jnp.int32, (TOK_TILE, LANES), 1)
        va_ref[0, rows, :] = jnp.concatenate(
            [p[:, 640:768], jnp.where(lane == 0, 1.0, 0.0)], axis=-1).astype(BF16)

        cq = (_rms(p[:, 768:1024]) * qn_ref[...]).astype(BF16)
        qb = _dot(cq, wuq_ref[...])
        qb_ref[0, rows, :] = _rope(qb, cbq_ref[rows, :], sbq_ref[rows, :], 8).astype(BF16)
        ckv = (_rms(p[:, 1024:1152]) * kvn_ref[...]).astype(BF16)
        kv = _dot(ckv, wukv_ref[...])
        kr = _rope(p[:, 1152:1280], cb_ref[rows, :], sb_ref[rows, :], 8)
        kb_ref[0, rows, :] = (kv[:, 0:512] + jnp.concatenate([kr] * B_HEADS, axis=-1)).astype(BF16)
        vb_ref[0, rows, :] = _per_head_with_ones(kv[:, 512:768]).astype(BF16)

        qd_ref[0, rows, :] = _rope(p[:, 1280:1536], c8q_ref[rows, :], s8q_ref[rows, :], 8).astype(BF16)
        kd_ref[0, rows, :] = _rope(p[:, 1536:1792], c8_ref[rows, :], s8_ref[rows, :], 8).astype(BF16)
        vd_ref[0, rows, :] = _per_head_with_ones(p[:, 1792:2048]).astype(BF16)


def _mod_specs():
    return [pl.BlockSpec((1, 1, 6 * D_MODEL), lambda b, t: (b, 0, 0)),
            pl.BlockSpec((1, 1, 6 * D_MODEL), lambda b, t: (BATCH, 0, 0))]


def _inproj(xall, mod_l, g1, w, qn, wuq, kvn, wukv, tabs):
    full = lambda shape: pl.BlockSpec(shape, lambda b, t: (0,) * len(shape))
    step_rows = STEP_CHUNKS * TOK_TILE
    tab_spec = pl.BlockSpec((step_rows, LANES), lambda b, t: (t, 0))
    widths = (512, 128, 256, 512, 512, 512, 256, 256, 512)
    return pl.pallas_call(
        _inproj_kernel,
        out_shape=[jax.ShapeDtypeStruct((BATCH, NT_PAD, n), BF16) for n in widths],
        grid=(BATCH, N_TILES // STEP_CHUNKS),
        in_specs=[pl.BlockSpec((1, step_rows, D_MODEL), lambda b, t: (b, t, 0))] + _mod_specs() + [
            full((1, D_MODEL)),
            full((D_MODEL, 2048)),
            full((1, B_Q_RANK)),
            full((B_Q_RANK, 512)),
            full((1, B_KV_RANK)),
            full((B_KV_RANK, 768)),
        ] + [tab_spec] * 12,
        out_specs=[pl.BlockSpec((1, step_rows, n), lambda b, t: (b, t, 0)) for n in widths],
        compiler_params=_params(("arbitrary", "arbitrary")),
        name="in_projection",
    )(xall, mod_l, mod_l, g1, w, qn, wuq, kvn, wukv, *tabs)


def _window_heads(q, kcat, vcat, bias, sink_ref):
    tq = q.shape[0]
    lane = lax.broadcasted_iota(jnp.int32, (1, LANES), 1)
    acc = [jnp.zeros((tq, LANES), F32) for _ in range(A_GROUP)]
    for g in range(A_KV_HEADS):
        in_half = (lane < HEAD_DIM) if g == 0 else (lane >= HEAD_DIM)
        keep = _indicator(in_half)
        qs = jnp.concatenate([q[:, j * LANES:(j + 1) * LANES] * keep for j in range(A_GROUP)], axis=0)
        s = _dot_nt(qs, kcat)
        if bias is not None:
            s = s + jnp.concatenate([bias] * A_GROUP, axis=0)
        sink = jnp.concatenate(
            [jnp.full((tq, 1), sink_ref[g * A_GROUP + j] * LOG2_E, F32) for j in range(A_GROUP)], axis=0)
        m = jnp.maximum(jnp.max(s, axis=-1, keepdims=True), sink)
        ol = _dot(jnp.exp2(s - m).astype(BF16), vcat)
        o = ol[:, 0:LANES] / (ol[:, LANES:LANES + 1] + jnp.exp2(sink - m))
        for j in range(A_GROUP):
            acc[j] = acc[j] + jnp.where(in_half, o[j * tq:(j + 1) * tq], 0.0)
    return jnp.concatenate(acc, axis=-1).astype(BF16)


A_STEP_BLOCKS = 4
A_LOCAL = 3 * WIN


def _attn_a_latent_kernel(sink_ref, q_ref, k_ref, v_ref, bias_ref, o_ref):
    step = pl.program_id(1)
    last = SEQ // WIN - 1
    k_ctx = k_ref[0, SEQ:NT, :]
    v_ctx = v_ref[0, SEQ:NT, :]
    for j in range(A_STEP_BLOCKS):
        n = step * A_STEP_BLOCKS + j
        start = pl.multiple_of(jnp.clip((n - 1) * WIN, 0, SEQ - A_LOCAL), WIN)
        variant = jnp.where(n == 0, 0, jnp.where(n == last, 2, 1))
        kcat = jnp.concatenate([k_ref[0, pl.ds(start, A_LOCAL), :], k_ctx], axis=0)
        vcat = jnp.concatenate([v_ref[0, pl.ds(start, A_LOCAL), :], v_ctx], axis=0)
        rows = slice(j * WIN, (j + 1) * WIN)
        o_ref[0, rows, :] = _window_heads(q_ref[0, rows, :], kcat, vcat, bias_ref[variant], sink_ref)


def _attn_a_latent(sink, qa, ka, va, bias):
    tq = A_STEP_BLOCKS * WIN
    return pl.pallas_call(
        _attn_a_latent_kernel,
        out_shape=jax.ShapeDtypeStruct((BATCH, NT_PAD, 512), BF16),
        grid=(BATCH, SEQ // tq),
        in_specs=[pl.BlockSpec(memory_space=pltpu.SMEM),
                  pl.BlockSpec((1, tq, 512), lambda b, n: (b, n, 0)),
                  pl.BlockSpec((1, NT_PAD, LANES), lambda b, n: (b, 0, 0)),
                  pl.BlockSpec((1, NT_PAD, 2 * LANES), lambda b, n: (b, 0, 0)),
                  pl.BlockSpec((3, WIN, A_LOCAL + CTX_LEN), lambda b, n: (0, 0, 0))],
        out_specs=pl.BlockSpec((1, tq, 512), lambda b, n: (b, n, 0)),
        compiler_params=_params(("arbitrary", "arbitrary")),
        name="attn_window_latent",
    )(sink, qa, ka, va, bias)


def _attn_a_ctx(sink, qa, ka, va, prev_out):
    ctx_blk = SEQ // CTX_LEN
    spec = lambda n: pl.BlockSpec((BATCH, CTX_LEN, n), lambda i: (0, ctx_blk, 0))
    tile = lambda n: pl.BlockSpec((BATCH, TOK_TILE, n), lambda i: (0, N_LAT_TILES, 0))

    def body(sink_ref, q_ref, k_ref, v_ref, prev_ref, o_ref):
        del prev_ref
        for sample in range(BATCH):
            o_ref[sample] = _window_heads(q_ref[sample], k_ref[sample], v_ref[sample], None, sink_ref)

    return pl.pallas_call(
        body,
        out_shape=jax.ShapeDtypeStruct((BATCH, NT_PAD, 512), BF16),
        grid=(1,),
        in_specs=[pl.BlockSpec(memory_space=pltpu.SMEM), tile(512), spec(LANES), spec(2 * LANES),
                  pl.BlockSpec(memory_space=pl.ANY)],
        out_specs=tile(512),
        input_output_aliases={4: 0},
        compiler_params=_params(("arbitrary",)),
        name="attn_window_ctx",
    )(sink, qa, ka, va, prev_out)


def _key_rows(ref, sample):
    return ref[sample, 0:NT, :] if ref.shape[1] == NT_PAD else ref[sample]


MLA_TQ = 1024
MLA_SUB = 512
DIFF_TQ = 1024
DIFF_SUB = 256


def _attn_b_kernel(q_ref, k_ref, v_ref, o_ref):
    for sample in range(q_ref.shape[0]):
        k = _key_rows(k_ref, sample)
        v = _key_rows(v_ref, sample)
        sub_rows = min(MLA_SUB, q_ref.shape[1])
        for sub in range(q_ref.shape[1] // sub_rows):
            rows = slice(sub * sub_rows, (sub + 1) * sub_rows)
            q = q_ref[sample, rows, :]
            outs = []
            for h in range(B_HEADS):
                s = _dot_nt(q[:, h * LANES:(h + 1) * LANES], k[:, h * LANES:(h + 1) * LANES])
                outs.append(_softmax_pv(s, v[:, h * LANES:(h + 1) * LANES], h))
            o_ref[sample, rows, :] = _merge_head_pairs(outs).astype(BF16)


def _full_attention(kernel, name, tq, q, k, v, extra, extra_specs, out_width, q_width, k_width, v_width):
    out = pl.pallas_call(
        kernel,
        out_shape=jax.ShapeDtypeStruct((BATCH, NT_PAD, out_width), BF16),
        grid=(BATCH, SEQ // tq),
        in_specs=[pl.BlockSpec((1, tq, q_width), lambda b, t: (b, t, 0)),
                  pl.BlockSpec((1, NT_PAD, k_width), lambda b, t: (b, 0, 0)),
                  pl.BlockSpec((1, NT_PAD, v_width), lambda b, t: (b, 0, 0))] + extra_specs(2),
        out_specs=pl.BlockSpec((1, tq, out_width), lambda b, t: (b, t, 0)),
        compiler_params=_params(("arbitrary", "arbitrary")),
        name=name + "_latent",
    )(q, k, v, *extra)
    return out


def _full_attention_ctx(kernel, name, q, k, v, extra, extra_specs, out_width, q_width, k_width, v_width,
                        prev_out):
    ctx_blk = SEQ // CTX_LEN
    n_extra = len(extra)

    def body(*refs):
        ins = refs[:3 + n_extra]
        kernel(*ins, refs[-1])

    spec = lambda n: pl.BlockSpec((BATCH, CTX_LEN, n), lambda i: (0, ctx_blk, 0))
    tile = lambda n: pl.BlockSpec((BATCH, TOK_TILE, n), lambda i: (0, N_LAT_TILES, 0))
    return pl.pallas_call(
        body,
        out_shape=jax.ShapeDtypeStruct((BATCH, NT_PAD, out_width), BF16),
        grid=(1,),
        in_specs=[tile(q_width), spec(k_width), spec(v_width)] + extra_specs(1)
                 + [pl.BlockSpec(memory_space=pl.ANY)],
        out_specs=tile(out_width),
        input_output_aliases={3 + n_extra: 0},
        compiler_params=_params(("arbitrary",)),
        name=name + "_ctx",
    )(q, k, v, *extra, prev_out)


def _attn_c_kernel(q_ref, k_ref, v_ref, lq1_ref, lk1_ref, lq2_ref, lk2_ref, sn_ref, o_ref, *, lam_init):
    tq = min(DIFF_SUB, q_ref.shape[1])
    lam = (jnp.exp(jnp.sum(lq1_ref[...] * lk1_ref[...], axis=-1, keepdims=True))
           - jnp.exp(jnp.sum(lq2_ref[...] * lk2_ref[...], axis=-1, keepdims=True)) + lam_init)
    lane = lax.broadcasted_iota(jnp.int32, (1, C_HEADS * C_V), 1)
    units = [(sample, sub) for sample in range(q_ref.shape[0]) for sub in range(q_ref.shape[1] // tq)]
    for sample, sub in units:
        k = _key_rows(k_ref, sample)
        v = _key_rows(v_ref, sample)
        rows = slice(sub * tq, (sub + 1) * tq)
        q = q_ref[sample, rows, :]
        outs = []
        for h in range(C_HEADS):
            lo = h * C_V
            q1 = q * _indicator((lane >= lo) & (lane < lo + C_DH))
            q2 = q * _indicator((lane >= lo + C_DH) & (lane < lo + C_V))
            s = _dot_nt(jnp.concatenate([q1, q2], axis=0), k)
            pv = _softmax_pv(s, v[:, h * LANES:(h + 1) * LANES], h)
            outs.append(pv[0:tq] - lam * pv[tq:2 * tq])
        acc = _merge_head_pairs(outs)
        sq = acc * acc
        inv = jnp.zeros((tq, C_HEADS * C_V), F32)
        for h in range(C_HEADS):
            in_head = (lane >= h * C_V) & (lane < (h + 1) * C_V)
            ms = jnp.sum(jnp.where(in_head, sq, 0.0), axis=-1, keepdims=True) * (1.0 / C_V)
            inv = inv + jnp.where(in_head, lax.rsqrt(ms + EPS), 0.0)
        o_ref[sample, rows, :] = ((acc * inv) * sn_ref[...] * (1.0 - lam_init)).astype(BF16)


def _outproj_kernel(x_ref, oa_ref, ob_ref, od_ref, w_ref, modb_ref, modc_ref, g2_ref, wr_ref,
                    xo_ref, h2_ref, aff_ref, *, n_chunks):
    w_hi, w_lo = _split_bf16(wr_ref[...])
    w_router = jnp.concatenate([w_hi, w_lo], axis=-1)
    lane = lax.broadcasted_iota(jnp.int32, (1, LANES), 1)
    mixed = _dot(jnp.concatenate([oa_ref[0], ob_ref[0], od_ref[0]], axis=-1), w_ref[...])
    for j in range(n_chunks):
        rows = slice(j * TOK_TILE, (j + 1) * TOK_TILE)
        mod = _chunk_mod(modb_ref, modc_ref, j, n_chunks)
        g1 = mod[:, 2 * D_MODEL:3 * D_MODEL]
        sh2 = mod[:, 3 * D_MODEL:4 * D_MODEL]
        sc2 = mod[:, 4 * D_MODEL:5 * D_MODEL]
        x = x_ref[0, rows, :] + g1 * mixed[rows]
        xo_ref[0, rows, :] = x
        h2 = (_rms(x) * g2_ref[...]) * (1.0 + sc2) + sh2
        h_hi, h_lo = _split_bf16(h2)
        h2_ref[0, rows, :] = h_hi
        prod = _dot(jnp.concatenate([h_hi, h_lo], axis=0), w_router)
        logits = (prod[0:TOK_TILE, 0:LANES] + prod[0:TOK_TILE, LANES:2 * LANES]
                  + prod[TOK_TILE:2 * TOK_TILE, 0:LANES])
        aff_ref[0, rows, :] = _softmax_rows(jnp.where(lane < N_EXPERTS, logits, NEG_INF))


def _outproj(xall, oa, ob, od, w, mod_l, g2, wr, with_ctx):
    n_chunks = STEP_CHUNKS if with_ctx else LAT_STEP_CHUNKS
    n_steps = (N_TILES if with_ctx else N_LAT_TILES) // n_chunks
    full = lambda shape: pl.BlockSpec(shape, lambda b, t: (0,) * len(shape))
    tok = lambda n: pl.BlockSpec((1, n_chunks * TOK_TILE, n), lambda b, t: (b, t, 0))
    return pl.pallas_call(
        functools.partial(_outproj_kernel, n_chunks=n_chunks),
        out_shape=[jax.ShapeDtypeStruct((BATCH, NT_PAD, D_MODEL), F32),
                   jax.ShapeDtypeStruct((BATCH, NT_PAD, D_MODEL), BF16),
                   jax.ShapeDtypeStruct((BATCH, NT_PAD, LANES), F32)],
        grid=(BATCH, n_steps),
        in_specs=[tok(D_MODEL), tok(512), tok(256), tok(256), full((D_MODEL, D_MODEL))] + _mod_specs()
                 + [full((1, D_MODEL)), full((D_MODEL, LANES))],
        out_specs=[tok(D_MODEL), tok(D_MODEL), tok(LANES)],
        compiler_params=_params(("arbitrary", "arbitrary")),
        name="out_projection",
    )(xall, oa, ob, od, w, mod_l, mod_l, g2, wr)


def _prefix_count(flags_f32, tri_ref):
    rows, n = flags_f32.shape
    run = jnp.zeros((rows, 1), F32)
    outs = []
    for c in range(n // LANES):
        blk = flags_f32[:, c * LANES:(c + 1) * LANES]
        outs.append(_dot(blk.astype(BF16), tri_ref[...]) + run)
        run = run + jnp.sum(blk, axis=-1, keepdims=True)
    return jnp.concatenate(outs, axis=-1)


def _select_slots(a, cap, tri_ref):
    bits = pltpu.bitcast(a, jnp.int32)

    def step(i, t):
        cand = t | jnp.left_shift(jnp.int32(1), 30 - i)
        cnt = jnp.sum((bits >= cand).astype(jnp.int32), axis=-1, keepdims=True)
        return jnp.where(cnt >= cap, cand, t)

    thr = lax.fori_loop(0, 31, step, jnp.zeros((a.shape[0], 1), jnp.int32))
    above = bits > thr
    tied = bits == thr
    need = cap - jnp.sum(above.astype(F32), axis=-1, keepdims=True)
    tied_f = tied.astype(F32)
    sel = above | (tied & (_prefix_count(tied_f, tri_ref) < need))
    sel_f = sel.astype(F32)
    return jnp.where(sel, _prefix_count(sel_f, tri_ref), -1.0)


def _route_kernel(aff_ref, tri_ref, row_ref, arow_ref, col_ref, *, with_ctx):
    lat, ctx = [], []
    for b in range(BATCH):
        lat.append(aff_ref[b, 0:SEQ, :].T[0:N_EXPERTS])
        if with_ctx:
            ctx.append(aff_ref[b, SEQ:NT, :].T[0:N_EXPERTS])
    aff_lat = jnp.concatenate(lat, axis=0)
    arow_ref[:, 0:SEQ] = aff_lat
    slot_lat = _select_slots(aff_lat, CAP_LAT, tri_ref)
    if with_ctx:
        aff_ctx = jnp.concatenate(ctx, axis=0)
        arow_ref[:, SEQ:NT] = aff_ctx
        slot_ctx = _select_slots(aff_ctx, CAP_CTX, tri_ref)
        slots = jnp.concatenate([slot_lat, slot_ctx], axis=-1)
    else:
        slots = slot_lat
    n = slots.shape[1]
    row_ref[:, 0:n] = slots
    pad = jnp.full((LANES - N_EXPERTS, n), -1.0, F32)
    for b in range(BATCH):
        blk = jnp.concatenate([slots[b * N_EXPERTS:(b + 1) * N_EXPERTS], pad], axis=0)
        col_ref[b, 0:n, :] = blk.T
        if with_ctx and NT_PAD > NT:
            col_ref[b, NT:NT_PAD, :] = jnp.full((NT_PAD - NT, LANES), -1.0, F32)


def _route(aff, tri, with_ctx):
    vm = pl.BlockSpec(memory_space=pltpu.VMEM)
    return pl.pallas_call(
        functools.partial(_route_kernel, with_ctx=with_ctx),
        out_shape=[jax.ShapeDtypeStruct((BATCH * N_EXPERTS, NT), F32),
                   jax.ShapeDtypeStruct((BATCH * N_EXPERTS, NT), F32),
                   jax.ShapeDtypeStruct((BATCH, NT_PAD, LANES), F32)],
        in_specs=[vm, vm],
        out_specs=[vm, vm, vm],
        compiler_params=pltpu.CompilerParams(vmem_limit_bytes=VMEM_LIMIT),
        name="route_topc",
    )(aff, tri)


GATHER_GROUP = 4


def _gather_kernel(row_ref, arow_ref, h_ref, *out_refs, with_ctx):
    slot = lax.broadcasted_iota(jnp.int32, (CAP_LAT, 1), 0).astype(F32)

    def pick(e0, cols, n_slots, h_rows, x_ref, g_ref):
        hots = []
        for e in range(e0, e0 + GATHER_GROUP):
            hot = jnp.where(row_ref[e, :, cols] == slot[0:n_slots], 1.0, 0.0)
            gate = jnp.sum(hot * arow_ref[e, :, cols], axis=-1, keepdims=True)
            g_ref[e] = jnp.broadcast_to(gate, (n_slots, LANES))
            hots.append(hot.astype(BF16))
        x = _dot(jnp.concatenate(hots, axis=0), h_rows).astype(BF16)
        for i in range(GATHER_GROUP):
            x_ref[e0 + i] = x[i * n_slots:(i + 1) * n_slots]

    for e0 in range(0, N_EXPERTS, GATHER_GROUP):
        pick(e0, slice(0, SEQ), CAP_LAT, h_ref[0, 0:SEQ, :], out_refs[0], out_refs[1])
        if with_ctx:
            pick(e0, slice(SEQ, NT), CAP_CTX, h_ref[0, SEQ:NT, :], out_refs[2], out_refs[3])


def _gather(slot_rows, aff_rows, h2, with_ctx):
    out_shape = [jax.ShapeDtypeStruct((N_EXPERTS, SEQ, D_MODEL), BF16),
                 jax.ShapeDtypeStruct((N_EXPERTS, SEQ, LANES), F32)]
    out_specs = [pl.BlockSpec((N_EXPERTS, CAP_LAT, D_MODEL), lambda b: (0, b, 0)),
                 pl.BlockSpec((N_EXPERTS, CAP_LAT, LANES), lambda b: (0, b, 0))]
    if with_ctx:
        out_shape += [jax.ShapeDtypeStruct((N_EXPERTS, CTX_LEN, D_MODEL), BF16),
                      jax.ShapeDtypeStruct((N_EXPERTS, CTX_LEN, LANES), F32)]
        out_specs += [pl.BlockSpec((N_EXPERTS, CAP_CTX, D_MODEL), lambda b: (0, b, 0)),
                      pl.BlockSpec((N_EXPERTS, CAP_CTX, LANES), lambda b: (0, b, 0))]
    row_spec = pl.BlockSpec((N_EXPERTS, 1, NT), lambda b: (b, 0, 0))
    return pl.pallas_call(
        functools.partial(_gather_kernel, with_ctx=with_ctx),
        out_shape=out_shape,
        grid=(BATCH,),
        in_specs=[row_spec, row_spec, pl.BlockSpec((1, NT_PAD, D_MODEL), lambda b: (b, 0, 0))],
        out_specs=out_specs,
        compiler_params=_params(("arbitrary",)),
        name="expert_gather",
    )(slot_rows, aff_rows, h2)


def _ffn_kernel(*refs, with_ctx):
    if with_ctx:
        xl_ref, gl_ref, xc_ref, gc_ref, wg_ref, wu_ref, wd_ref, y_ref = refs
    else:
        xl_ref, gl_ref, wg_ref, wu_ref, wd_ref, y_ref = refs
    wg = wg_ref[0, 0].astype(BF16)
    wu = wu_ref[0, 0].astype(BF16)
    wd = wd_ref[0, 0].astype(BF16)

    def swiglu(x, slot_gate):
        gate = _dot(x, wg)
        hid = (gate / (1.0 + jnp.exp(-gate))) * _dot(x, wu)
        scale = jnp.concatenate([slot_gate] * (D_MODEL // LANES), axis=-1)
        return (_dot(hid.astype(BF16), wd) * scale).astype(BF16)

    for r in range(FFN_LAT_TILES):
        rows = slice(r * FFN_TILE, (r + 1) * FFN_TILE)
        y_ref[0, rows, :] = swiglu(xl_ref[0, rows, :], gl_ref[0, rows, :])
    if with_ctx:
        y_ref[0, SEQ:NT, :] = swiglu(xc_ref[0], gc_ref[0])


def _ffn(gathered, w_gate, w_up, w_down, layer):
    with_ctx = len(gathered) == 4
    ins = list(gathered) + [w_gate, w_up, w_down]
    in_specs = [pl.BlockSpec((1, SEQ, D_MODEL), lambda e: (e, 0, 0)),
                pl.BlockSpec((1, SEQ, LANES), lambda e: (e, 0, 0))]
    if with_ctx:
        in_specs += [pl.BlockSpec((1, CTX_LEN, D_MODEL), lambda e: (e, 0, 0)),
                     pl.BlockSpec((1, CTX_LEN, LANES), lambda e: (e, 0, 0))]
    in_specs += [pl.BlockSpec((1, 1, D_MODEL, D_EXPERT), lambda e: (layer, e, 0, 0)),
                 pl.BlockSpec((1, 1, D_MODEL, D_EXPERT), lambda e: (layer, e, 0, 0)),
                 pl.BlockSpec((1, 1, D_EXPERT, D_MODEL), lambda e: (layer, e, 0, 0))]
    return pl.pallas_call(
        functools.partial(_ffn_kernel, with_ctx=with_ctx),
        out_shape=jax.ShapeDtypeStruct((N_EXPERTS, NT, D_MODEL), BF16),
        grid=(N_EXPERTS,),
        in_specs=in_specs,
        out_specs=pl.BlockSpec((1, NT, D_MODEL), lambda e: (e, 0, 0)),
        compiler_params=_params(("arbitrary",)),
        name="expert_ffn",
    )(*ins)


COMBINE_ROWS = 512


def _scatter_add(col, ys, n_slots):
    lane = lax.broadcasted_iota(jnp.int32, (1, n_slots), 1).astype(F32)
    hot = [_indicator(col[:, e:e + 1] == lane) for e in range(N_EXPERTS)]
    return _dot(jnp.concatenate(hot, axis=-1), jnp.concatenate(ys, axis=0))


def _combine_latent_kernel(*refs, final):
    if final:
        x_ref, col_ref, y_ref, mod_ref, fn_ref, o_ref = refs
    else:
        x_ref, col_ref, y_ref, mod_ref, o_ref = refs
    g2 = mod_ref[0][:, 5 * D_MODEL:6 * D_MODEL]
    acc = _scatter_add(col_ref[0], [y_ref[e] for e in range(N_EXPERTS)], CAP_LAT)
    x = x_ref[0] + g2 * acc
    if final:
        x = _rms(x) * fn_ref[...]
    o_ref[0] = x


def _combine_ctx_kernel(x_ref, col_ref, y_ref, mod_ref, prev_ref, o_ref):
    del prev_ref
    g2 = mod_ref[0][:, 5 * D_MODEL:6 * D_MODEL]
    pad = jnp.zeros((LANES - CAP_CTX, D_MODEL), BF16)
    for b in range(BATCH):
        ys = [jnp.concatenate([y_ref[e, b * CAP_CTX:(b + 1) * CAP_CTX, :], pad], axis=0)
              for e in range(N_EXPERTS)]
        o_ref[b] = x_ref[b] + g2 * _scatter_add(col_ref[b], ys, LANES)


def _combine(xall, slot_cols, y, mod_l, with_ctx, final_norm=None):
    final = final_norm is not None
    n_rows = SEQ if final else NT_PAD
    tok = lambda n: pl.BlockSpec((1, COMBINE_ROWS, n), lambda b, t: (b, t, 0))
    ins = [xall, slot_cols, y, mod_l]
    in_specs = [tok(D_MODEL), tok(LANES),
                pl.BlockSpec((N_EXPERTS, CAP_LAT, D_MODEL), lambda b, t: (0, b, 0)),
                pl.BlockSpec((1, 1, 6 * D_MODEL), lambda b, t: (b, 0, 0))]
    if final:
        ins.append(final_norm)
        in_specs.append(pl.BlockSpec((1, D_MODEL), lambda b, t: (0, 0)))
    out = pl.pallas_call(
        functools.partial(_combine_latent_kernel, final=final),
        out_shape=jax.ShapeDtypeStruct((BATCH, n_rows, D_MODEL), F32),
        grid=(BATCH, SEQ // COMBINE_ROWS),
        in_specs=in_specs,
        out_specs=tok(D_MODEL),
        compiler_params=_params(("arbitrary", "arbitrary")),
        name="expert_combine",
    )(*ins)
    if not with_ctx:
        return out
    ctx_blk = SEQ // CTX_LEN
    ctx_rows = lambda n: pl.BlockSpec((BATCH, CTX_LEN, n), lambda i: (0, ctx_blk, 0))
    return pl.pallas_call(
        _combine_ctx_kernel,
        out_shape=jax.ShapeDtypeStruct((BATCH, NT_PAD, D_MODEL), F32),
        grid=(1,),
        in_specs=[ctx_rows(D_MODEL), ctx_rows(LANES),
                  pl.BlockSpec((N_EXPERTS, CTX_LEN, D_MODEL), lambda i: (0, ctx_blk, 0)),
                  pl.BlockSpec((1, 1, 6 * D_MODEL), lambda i: (BATCH, 0, 0)),
                  pl.BlockSpec(memory_space=pl.ANY)],
        out_specs=ctx_rows(D_MODEL),
        input_output_aliases={4: 0},
        compiler_params=_params(("arbitrary",)),
        name="expert_combine_ctx",
    )(xall, slot_cols, y, mod_l, out)


def _rope_tables():
    t = jnp.arange(SEQ)
    pos = jnp.stack([t // GRID_W, t % GRID_W], axis=0)

    def table(n_f, lanes_per_unit):
        inv = ROPE_BASE ** (-jnp.arange(n_f, dtype=F32) / n_f)
        d = np.arange(lanes_per_unit)
        axis, second, f = d // (2 * n_f), (d // n_f) % 2, d % n_f
        ang = pos[axis].T.astype(F32) * inv[f][None, :]
        cos = jnp.cos(ang)
        sin = jnp.sin(ang) * jnp.where(second == 1, 1.0, -1.0)[None, :]
        return cos, sin

    def finish(cos, sin, scale):
        cos = jnp.concatenate([cos, jnp.ones((NT_PAD - SEQ, LANES), F32)], axis=0)
        sin = jnp.concatenate([sin, jnp.zeros((NT_PAD - SEQ, LANES), F32)], axis=0)
        scale = scale * LOG2_E
        return [cos, sin, cos * scale, sin * scale]

    cos_a, sin_a = table(HEAD_DIM // 4, HEAD_DIM)
    tabs_a = finish(jnp.tile(cos_a, (1, 2)), jnp.tile(sin_a, (1, 2)), HEAD_DIM ** -0.5)
    cos_8, sin_8 = table(B_ROPE // 4, B_ROPE)
    tabs_8 = finish(jnp.tile(cos_8, (1, 4)), jnp.tile(sin_8, (1, 4)), C_DH ** -0.5)
    ones = jnp.ones((SEQ, B_NOPE), F32)
    pad1 = jnp.ones((SEQ, LANES - B_NOPE - B_ROPE), F32)
    cos_b = jnp.concatenate([ones, cos_8, pad1], axis=1)
    sin_b = jnp.concatenate([0 * ones, sin_8, 0 * pad1], axis=1)
    tabs_b = finish(cos_b, sin_b, (B_NOPE + B_ROPE) ** -0.5)
    return tabs_a + tabs_8 + tabs_b


def _window_bias():
    qi = np.arange(WIN)[:, None]
    kj = np.arange(A_LOCAL)[None, :]
    out = np.zeros((3, WIN, A_LOCAL + CTX_LEN), np.float32)
    for v in range(3):
        out[v, :, :A_LOCAL] = np.where(np.abs(kj - v * WIN - qi) <= WIN, 0.0, NEG_INF)
    return jnp.asarray(out)


def _prep_weights(w_in, w_uq, w_ukv, w_out, w_router):
    offs = np.cumsum((0, 512, 128, 128, 256, 128, 32, 256, 256, 256))
    seg = lambda i: w_in[:, :, offs[i]:offs[i + 1]]
    wq = seg(0).reshape(DEPTH, D_MODEL, A_KV_HEADS, A_GROUP, HEAD_DIM)
    wq = jnp.swapaxes(wq, 2, 3).reshape(DEPTH, D_MODEL, A_HEADS * HEAD_DIM)
    kr = jnp.pad(seg(5), ((0, 0), (0, 0), (B_NOPE, LANES - B_NOPE - B_ROPE)))
    w = jnp.concatenate([wq, seg(1), seg(2), seg(3), seg(4), kr,
                         seg(6), seg(7), seg(8)], axis=-1).astype(BF16)
    wuq = w_uq.reshape(DEPTH, B_Q_RANK, B_HEADS, B_NOPE + B_ROPE)
    wuq = jnp.pad(wuq, ((0, 0), (0, 0), (0, 0), (0, LANES - B_NOPE - B_ROPE)))
    wuq = wuq.reshape(DEPTH, B_Q_RANK, B_HEADS * LANES).astype(BF16)
    wukv = w_ukv.reshape(DEPTH, B_KV_RANK, B_HEADS, B_NOPE + B_V)
    wk = jnp.pad(wukv[..., :B_NOPE], ((0, 0), (0, 0), (0, 0), (0, LANES - B_NOPE)))
    wukv = jnp.concatenate([wk.reshape(DEPTH, B_KV_RANK, B_HEADS * LANES),
                            wukv[..., B_NOPE:].reshape(DEPTH, B_KV_RANK, B_HEADS * B_V)],
                           axis=-1).astype(BF16)
    wo_a = w_out[:, :A_HEADS * HEAD_DIM].reshape(DEPTH, A_KV_HEADS, A_GROUP, HEAD_DIM, D_MODEL)
    wo_a = jnp.swapaxes(wo_a, 1, 2).reshape(DEPTH, A_HEADS * HEAD_DIM, D_MODEL)
    wo = jnp.concatenate([wo_a, w_out[:, A_HEADS * HEAD_DIM:]], axis=1).astype(BF16)
    wr = jnp.pad(w_router, ((0, 0), (0, 0), (0, LANES - N_EXPERTS)))
    return w, wuq, wukv, wo, wr


def kernel(x, c, ctx, c_ctx, norm1, norm2, w_ada, b_ada, w_in, sink, mla_q_norm, w_uq, mla_kv_norm, w_ukv,
           lam_q1, lam_k1, lam_q2, lam_k2, diff_norm, w_out, w_router, w_gate, w_up, w_down, final_norm):
    xall = jnp.concatenate([x, ctx, jnp.zeros((BATCH, NT_PAD - NT, D_MODEL), F32)], axis=1)
    cc = jnp.concatenate([c, c_ctx[None, :], jnp.zeros((16 - BATCH - 1, D_MODEL), F32)], axis=0)
    mod = _modulation(cc, w_ada, b_ada).reshape(DEPTH, 16, 1, 6 * D_MODEL)
    tabs = _rope_tables()
    bias = _window_bias()
    tri = jnp.asarray(np.triu(np.ones((LANES, LANES), np.float32), k=1), BF16)
    w, wuq, wukv, wo, wr = _prep_weights(w_in, w_uq, w_ukv, w_out, w_router)
    sub_norm = jnp.tile(diff_norm, (1, C_HEADS)).reshape(DEPTH, 1, C_HEADS * C_V)
    row = lambda a, l: a[l].reshape(1, -1)

    for l in range(DEPTH):
        need_ctx = l < DEPTH - 1
        qa, ka, va, qb, kb, vb, qd, kd, vd = _inproj(
            xall, mod[l], row(norm1, l), w[l], row(mla_q_norm, l), wuq[l], row(mla_kv_norm, l), wukv[l], tabs)

        oa = _attn_a_latent(sink[l], qa, ka, va, bias)
        ob = _full_attention(_attn_b_kernel, "attn_mla", MLA_TQ, qb, kb, vb, [], lambda n: [],
                             256, 512, 512, 512)
        lam_init = 0.8 - 0.6 * math.exp(-0.3 * l)
        c_kernel = functools.partial(_attn_c_kernel, lam_init=lam_init)
        c_extra = [row(lam_q1, l), row(lam_k1, l), row(lam_q2, l), row(lam_k2, l), sub_norm[l]]

        def c_specs(n_grid, c_extra=c_extra):
            zmap = (lambda b, t: (0, 0)) if n_grid == 2 else (lambda b: (0, 0))
            return [pl.BlockSpec(a.shape, zmap) for a in c_extra]

        od = _full_attention(c_kernel, "attn_diff", DIFF_TQ, qd, kd, vd, c_extra, c_specs, 256, 256, 256, 512)
        if need_ctx:
            oa = _attn_a_ctx(sink[l], qa, ka, va, oa)
            ob = _full_attention_ctx(_attn_b_kernel, "attn_mla", qb, kb, vb, [], lambda n: [],
                                     256, 512, 512, 512, ob)
            od = _full_attention_ctx(c_kernel, "attn_diff", qd, kd, vd, c_extra, c_specs,
                                     256, 256, 256, 512, od)

        xall, h2, aff = _outproj(xall, oa, ob, od, wo[l], mod[l], row(norm2, l), wr[l], need_ctx)
        slot_rows, aff_rows, slot_cols = _route(aff, tri, need_ctx)
        per_row = lambda a: a.reshape(BATCH * N_EXPERTS, 1, NT)
        gathered = _gather(per_row(slot_rows), per_row(aff_rows), h2, need_ctx)
        y = _ffn(gathered, w_gate, w_up, w_down, l)
        xall = _combine(xall, slot_cols, y, mod[l], need_ctx,
                        final_norm=None if need_ctx else final_norm.reshape(1, D_MODEL))
    return xall
```

```python
import functools
import math

import numpy as np
import jax
import jax.numpy as jnp
from jax import lax
from jax.experimental import pallas as pl
from jax.experimental.pallas import tpu as pltpu

D_MODEL = 1024
BATCH = 8
SEQ = 2048
DEPTH = 4
CTX_LEN = 256
NT = SEQ + CTX_LEN
GRID_W = 64
HEAD_DIM = 64
ROPE_BASE = 10000.0
EPS = 1e-6
NEG_INF = -1e30
LOG2_E = math.log2(math.e)

WIN = 128
A_HEADS = 8
A_KV_HEADS = 2
A_GROUP = A_HEADS // A_KV_HEADS
B_HEADS = 4
B_NOPE = 64
B_ROPE = 32
B_V = 64
B_Q_RANK = 256
B_KV_RANK = 128
C_HEADS = 4
C_DH = 32
C_V = 2 * C_DH
N_EXPERTS = 16
EC_FACTOR = 2
D_EXPERT = 512
CAP_LAT = EC_FACTOR * SEQ // N_EXPERTS
CAP_CTX = EC_FACTOR * CTX_LEN // N_EXPERTS

LANES = 128
TOK_TILE = 256
NT_PAD = SEQ + TOK_TILE
N_TILES = NT_PAD // TOK_TILE
N_LAT_TILES = SEQ // TOK_TILE
STEP_CHUNKS = 3
LAT_STEP_CHUNKS = 2
FFN_TILE = 512
FFN_LAT_TILES = SEQ // FFN_TILE
VMEM_LIMIT = 56 * 1024 * 1024

F32 = jnp.float32
BF16 = jnp.bfloat16

_NT_DIMS = (((1,), (1,)), ((), ()))


def _params(sem):
    return pltpu.CompilerParams(dimension_semantics=sem, vmem_limit_bytes=VMEM_LIMIT)


def _dot(a, b):
    return jnp.dot(a, b, preferred_element_type=F32)


def _dot_nt(a, b):
    return lax.dot_general(a, b, _NT_DIMS, preferred_element_type=F32)


def _split_bf16(v):
    hi = v.astype(BF16)
    lo = (v - hi.astype(F32)).astype(BF16)
    return hi, lo


def _dot3(a, b):
    a_hi, a_lo = _split_bf16(a)
    b_hi, b_lo = _split_bf16(b)
    return _dot(a_hi, b_hi) + _dot(a_lo, b_hi) + _dot(a_hi, b_lo)


def _indicator(cond):
    return jnp.where(cond, 1.0, 0.0).astype(BF16)


def _rms(v):
    return v * lax.rsqrt(jnp.mean(v * v, axis=-1, keepdims=True) + EPS)


def _softmax_pv(s, v_blk, head, bf16_scores=False):
    if bf16_scores:
        s = s.astype(BF16)
    m = jnp.max(s, axis=-1, keepdims=True)
    ol = _dot(jnp.exp2(s - m).astype(BF16), v_blk)
    ones_lane = 64 if head % 2 == 0 else 0
    return ol / ol[:, ones_lane:ones_lane + 1]


def _merge_head_pairs(per_head):
    lane = lax.broadcasted_iota(jnp.int32, (1, LANES), 1)
    blocks = [jnp.where(lane < 64, per_head[j], per_head[j + 1]) for j in range(0, len(per_head), 2)]
    return jnp.concatenate(blocks, axis=-1)


def _softmax_rows(s):
    m = jnp.max(s, axis=-1, keepdims=True)
    e = jnp.exp(s - m)
    return e / jnp.sum(e, axis=-1, keepdims=True)


def _mod_kernel(c_ref, w_ref, b_ref, o_ref):
    c = c_ref[...]
    s = c / (1.0 + jnp.exp(-c))
    o_ref[0] = _dot3(s, w_ref[0]) + b_ref[0]


def _modulation(cc, w_ada, b_ada):
    tn = 1536
    return pl.pallas_call(
        _mod_kernel,
        out_shape=jax.ShapeDtypeStruct((DEPTH, 16, 6 * D_MODEL), F32),
        grid=(DEPTH, 6 * D_MODEL // tn),
        in_specs=[
            pl.BlockSpec((16, D_MODEL), lambda l, j: (0, 0)),
            pl.BlockSpec((1, D_MODEL, tn), lambda l, j: (l, 0, j)),
            pl.BlockSpec((1, 1, tn), lambda l, j: (l, 0, j)),
        ],
        out_specs=pl.BlockSpec((1, 16, tn), lambda l, j: (l, 0, j)),
        compiler_params=_params(("arbitrary", "arbitrary")),
        name="adaln_mod",
    )(cc, w_ada, b_ada.reshape(DEPTH, 1, 6 * D_MODEL))


def _rope(p, cos, sin, half):
    lane = lax.broadcasted_iota(jnp.int32, (1, LANES), 1)
    first = (lane // half) % 2 == 0
    outs = []
    for j in range(p.shape[1] // LANES):
        blk = p[:, j * LANES:(j + 1) * LANES]
        partner = jnp.where(first, pltpu.roll(blk, LANES - half, 1), pltpu.roll(blk, half, 1))
        outs.append(blk * cos + partner * sin)
    return outs[0] if len(outs) == 1 else jnp.concatenate(outs, axis=-1)


def _per_head_with_ones(v):
    lane = lax.broadcasted_iota(jnp.int32, (1, LANES), 1)
    outs = []
    for j in range(v.shape[1] // LANES):
        blk = v[:, j * LANES:(j + 1) * LANES]
        outs.append(jnp.where(lane < 64, blk, jnp.where(lane == 64, 1.0, 0.0)))
        outs.append(jnp.where(lane >= 64, blk, jnp.where(lane == 0, 1.0, 0.0)))
    return jnp.concatenate(outs, axis=-1)


def _chunk_mod(modb_ref, modc_ref, j, n_chunks):
    mod = modb_ref[0]
    if n_chunks == STEP_CHUNKS and j == n_chunks - 1:
        mod = jnp.where(pl.program_id(1) == N_TILES // STEP_CHUNKS - 1, modc_ref[0], mod)
    return mod


def _inproj_kernel(x_ref, modb_ref, modc_ref, g1_ref, w_ref, qn_ref, wuq_ref, kvn_ref, wukv_ref,
                   ca_ref, sa_ref, caq_ref, saq_ref, c8_ref, s8_ref, c8q_ref, s8q_ref,
                   cb_ref, sb_ref, cbq_ref, sbq_ref,
                   qa_ref, ka_ref, va_ref, qb_ref, kb_ref, vb_ref, qd_ref, kd_ref, vd_ref):
    for j in range(STEP_CHUNKS):
        rows = slice(j * TOK_TILE, (j + 1) * TOK_TILE)
        x = x_ref[0, rows, :]
        mod = _chunk_mod(modb_ref, modc_ref, j, STEP_CHUNKS)
        sh1 = mod[:, 0:D_MODEL]
        sc1 = mod[:, D_MODEL:2 * D_MODEL]
        h = (_rms(x) * g1_ref[...]) * (1.0 + sc1) + sh1
        p = _dot(h.astype(BF16), w_ref[0])

        qa_ref[0, rows, :] = _rope(p[:, 0:512], caq_ref[rows, :], saq_ref[rows, :], 16).astype(BF16)
        ka_ref[0, rows, :] = _rope(p[:, 512:640], ca_ref[rows, :], sa_ref[rows, :], 16).astype(BF16)
        lane = lax.broadcasted_iota(jnp.int32, (TOK_TILE, LANES), 1)
        va_ref[0, rows, :] = jnp.concatenate(
            [p[:, 640:768], jnp.where(lane == 0, 1.0, 0.0)], axis=-1).astype(BF16)

        cq = (_rms(p[:, 768:1024]) * qn_ref[...]).astype(BF16)
        qb = _dot(cq, wuq_ref[...])
        qb_ref[0, rows, :] = _rope(qb, cbq_ref[rows, :], sbq_ref[rows, :], 8).astype(BF16)
        ckv = (_rms(p[:, 1024:1152]) * kvn_ref[...]).astype(BF16)
        kv = _dot(ckv, wukv_ref[...])
        kr = _rope(p[:, 1152:1280], cb_ref[rows, :], sb_ref[rows, :], 8)
        kb_ref[0, rows, :] = (kv[:, 0:512] + jnp.concatenate([kr] * B_HEADS, axis=-1)).astype(BF16)
        vb_ref[0, rows, :] = _per_head_with_ones(kv[:, 512:768]).astype(BF16)

        qd_ref[0, rows, :] = _rope(p[:, 1280:1536], c8q_ref[rows, :], s8q_ref[rows, :], 8).astype(BF16)
        kd_ref[0, rows, :] = _rope(p[:, 1536:1792], c8_ref[rows, :], s8_ref[rows, :], 8).astype(BF16)
        vd_ref[0, rows, :] = _per_head_with_ones(p[:, 1792:2048]).astype(BF16)


def _mod_specs():
    return [pl.BlockSpec((1, 1, 6 * D_MODEL), lambda b, t: (b, 0, 0)),
            pl.BlockSpec((1, 1, 6 * D_MODEL), lambda b, t: (BATCH, 0, 0))]


def _inproj(xall, mod_l, g1, w, qn, wuq, kvn, wukv, tabs, layer):
    full = lambda shape: pl.BlockSpec(shape, lambda b, t: (0,) * len(shape))
    step_rows = STEP_CHUNKS * TOK_TILE
    tab_spec = pl.BlockSpec((step_rows, LANES), lambda b, t: (t, 0))
    widths = (512, 128, 256, 512, 512, 512, 256, 256, 512)
    return pl.pallas_call(
        _inproj_kernel,
        out_shape=[jax.ShapeDtypeStruct((BATCH, NT_PAD, n), BF16) for n in widths],
        grid=(BATCH, N_TILES // STEP_CHUNKS),
        in_specs=[pl.BlockSpec((1, step_rows, D_MODEL), lambda b, t: (b, t, 0))] + _mod_specs() + [
            full((1, D_MODEL)),
            pl.BlockSpec((1, D_MODEL, 2048), lambda b, t: (layer, 0, 0)),
            full((1, B_Q_RANK)),
            full((B_Q_RANK, 512)),
            full((1, B_KV_RANK)),
            full((B_KV_RANK, 768)),
        ] + [tab_spec] * 12,
        out_specs=[pl.BlockSpec((1, step_rows, n), lambda b, t: (b, t, 0)) for n in widths],
        compiler_params=_params(("arbitrary", "arbitrary")),
        name="in_projection",
    )(xall, mod_l, mod_l, g1, w, qn, wuq, kvn, wukv, *tabs)


def _window_heads(q, kcat, vcat, bias, sink_ref):
    tq = q.shape[0]
    lane = lax.broadcasted_iota(jnp.int32, (1, LANES), 1)
    acc = [jnp.zeros((tq, LANES), F32) for _ in range(A_GROUP)]
    for g in range(A_KV_HEADS):
        in_half = (lane < HEAD_DIM) if g == 0 else (lane >= HEAD_DIM)
        keep = _indicator(in_half)
        qs = jnp.concatenate([q[:, j * LANES:(j + 1) * LANES] * keep for j in range(A_GROUP)], axis=0)
        s = _dot_nt(qs, kcat)
        if bias is not None:
            s = s + jnp.concatenate([bias] * A_GROUP, axis=0)
        sink = jnp.concatenate(
            [jnp.full((tq, 1), sink_ref[g * A_GROUP + j] * LOG2_E, F32) for j in range(A_GROUP)], axis=0)
        m = jnp.maximum(jnp.max(s, axis=-1, keepdims=True), sink)
        ol = _dot(jnp.exp2(s - m).astype(BF16), vcat)
        o = ol[:, 0:LANES] / (ol[:, LANES:LANES + 1] + jnp.exp2(sink - m))
        for j in range(A_GROUP):
            acc[j] = acc[j] + jnp.where(in_half, o[j * tq:(j + 1) * tq], 0.0)
    return jnp.concatenate(acc, axis=-1).astype(BF16)


A_STEP_BLOCKS = 4
A_LOCAL = 3 * WIN


def _attn_a_latent_kernel(sink_ref, q_ref, k_ref, v_ref, bias_ref, o_ref):
    step = pl.program_id(1)
    last = SEQ // WIN - 1
    k_ctx = k_ref[0, SEQ:NT, :]
    v_ctx = v_ref[0, SEQ:NT, :]
    for j in range(A_STEP_BLOCKS):
        n = step * A_STEP_BLOCKS + j
        start = pl.multiple_of(jnp.clip((n - 1) * WIN, 0, SEQ - A_LOCAL), WIN)
        variant = jnp.where(n == 0, 0, jnp.where(n == last, 2, 1))
        kcat = jnp.concatenate([k_ref[0, pl.ds(start, A_LOCAL), :], k_ctx], axis=0)
        vcat = jnp.concatenate([v_ref[0, pl.ds(start, A_LOCAL), :], v_ctx], axis=0)
        rows = slice(j * WIN, (j + 1) * WIN)
        o_ref[0, rows, :] = _window_heads(q_ref[0, rows, :], kcat, vcat, bias_ref[variant], sink_ref)


def _attn_a_latent(sink, qa, ka, va, bias):
    tq = A_STEP_BLOCKS * WIN
    return pl.pallas_call(
        _attn_a_latent_kernel,
        out_shape=jax.ShapeDtypeStruct((BATCH, NT_PAD, 512), BF16),
        grid=(BATCH, SEQ // tq),
        in_specs=[pl.BlockSpec(memory_space=pltpu.SMEM),
                  pl.BlockSpec((1, tq, 512), lambda b, n: (b, n, 0)),
                  pl.BlockSpec((1, NT_PAD, LANES), lambda b, n: (b, 0, 0)),
                  pl.BlockSpec((1, NT_PAD, 2 * LANES), lambda b, n: (b, 0, 0)),
                  pl.BlockSpec((3, WIN, A_LOCAL + CTX_LEN), lambda b, n: (0, 0, 0))],
        out_specs=pl.BlockSpec((1, tq, 512), lambda b, n: (b, n, 0)),
        compiler_params=_params(("arbitrary", "arbitrary")),
        name="attn_window_latent",
    )(sink, qa, ka, va, bias)


def _attn_a_ctx(sink, qa, ka, va, prev_out):
    ctx_blk = SEQ // CTX_LEN
    spec = lambda n: pl.BlockSpec((BATCH, CTX_LEN, n), lambda i: (0, ctx_blk, 0))
    tile = lambda n: pl.BlockSpec((BATCH, TOK_TILE, n), lambda i: (0, N_LAT_TILES, 0))

    def body(sink_ref, q_ref, k_ref, v_ref, prev_ref, o_ref):
        del prev_ref
        for sample in range(BATCH):
            o_ref[sample] = _window_heads(q_ref[sample], k_ref[sample], v_ref[sample], None, sink_ref)

    return pl.pallas_call(
        body,
        out_shape=jax.ShapeDtypeStruct((BATCH, NT_PAD, 512), BF16),
        grid=(1,),
        in_specs=[pl.BlockSpec(memory_space=pltpu.SMEM), tile(512), spec(LANES), spec(2 * LANES),
                  pl.BlockSpec(memory_space=pl.ANY)],
        out_specs=tile(512),
        input_output_aliases={4: 0},
        compiler_params=_params(("arbitrary",)),
        name="attn_window_ctx",
    )(sink, qa, ka, va, prev_out)


def _key_rows(ref, sample):
    return ref[sample, 0:NT, :] if ref.shape[1] == NT_PAD else ref[sample]


MLA_TQ = 1024
MLA_SUB = 512
DIFF_TQ = 1024
DIFF_SUB = 256


def _attn_b_kernel(q_ref, k_ref, v_ref, o_ref):
    for sample in range(q_ref.shape[0]):
        k = _key_rows(k_ref, sample)
        v = _key_rows(v_ref, sample)
        sub_rows = min(MLA_SUB, q_ref.shape[1])
        for sub in range(q_ref.shape[1] // sub_rows):
            rows = slice(sub * sub_rows, (sub + 1) * sub_rows)
            q = q_ref[sample, rows, :]
            outs = []
            for h in range(B_HEADS):
                s = _dot_nt(q[:, h * LANES:(h + 1) * LANES], k[:, h * LANES:(h + 1) * LANES])
                outs.append(_softmax_pv(s, v[:, h * LANES:(h + 1) * LANES], h))
            o_ref[sample, rows, :] = _merge_head_pairs(outs).astype(BF16)


def _full_attention(kernel, name, tq, q, k, v, extra, extra_specs, out_width, q_width, k_width, v_width):
    out = pl.pallas_call(
        kernel,
        out_shape=jax.ShapeDtypeStruct((BATCH, NT_PAD, out_width), BF16),
        grid=(BATCH, SEQ // tq),
        in_specs=[pl.BlockSpec((1, tq, q_width), lambda b, t: (b, t, 0)),
                  pl.BlockSpec((1, NT_PAD, k_width), lambda b, t: (b, 0, 0)),
                  pl.BlockSpec((1, NT_PAD, v_width), lambda b, t: (b, 0, 0))] + extra_specs(2),
        out_specs=pl.BlockSpec((1, tq, out_width), lambda b, t: (b, t, 0)),
        compiler_params=_params(("arbitrary", "arbitrary")),
        name=name + "_latent",
    )(q, k, v, *extra)
    return out


def _full_attention_ctx(kernel, name, q, k, v, extra, extra_specs, out_width, q_width, k_width, v_width,
                        prev_out):
    ctx_blk = SEQ // CTX_LEN
    n_extra = len(extra)

    def body(*refs):
        ins = refs[:3 + n_extra]
        kernel(*ins, refs[-1])

    spec = lambda n: pl.BlockSpec((BATCH, CTX_LEN, n), lambda i: (0, ctx_blk, 0))
    tile = lambda n: pl.BlockSpec((BATCH, TOK_TILE, n), lambda i: (0, N_LAT_TILES, 0))
    return pl.pallas_call(
        body,
        out_shape=jax.ShapeDtypeStruct((BATCH, NT_PAD, out_width), BF16),
        grid=(1,),
        in_specs=[tile(q_width), spec(k_width), spec(v_width)] + extra_specs(1)
                 + [pl.BlockSpec(memory_space=pl.ANY)],
        out_specs=tile(out_width),
        input_output_aliases={3 + n_extra: 0},
        compiler_params=_params(("arbitrary",)),
        name=name + "_ctx",
    )(q, k, v, *extra, prev_out)


def _attn_c_kernel(q_ref, k_ref, v_ref, lq1_ref, lk1_ref, lq2_ref, lk2_ref, sn_ref, o_ref, *, lam_init):
    tq = min(DIFF_SUB, q_ref.shape[1])
    lam = (jnp.exp(jnp.sum(lq1_ref[...] * lk1_ref[...], axis=-1, keepdims=True))
           - jnp.exp(jnp.sum(lq2_ref[...] * lk2_ref[...], axis=-1, keepdims=True)) + lam_init)
    lane = lax.broadcasted_iota(jnp.int32, (1, C_HEADS * C_V), 1)
    units = [(sample, sub) for sample in range(q_ref.shape[0]) for sub in range(q_ref.shape[1] // tq)]
    for sample, sub in units:
        k = _key_rows(k_ref, sample)
        v = _key_rows(v_ref, sample)
        rows = slice(sub * tq, (sub + 1) * tq)
        q = q_ref[sample, rows, :]
        outs = []
        for h in range(C_HEADS):
            lo = h * C_V
            q1 = q * _indicator((lane >= lo) & (lane < lo + C_DH))
            q2 = q * _indicator((lane >= lo + C_DH) & (lane < lo + C_V))
            s = _dot_nt(jnp.concatenate([q1, q2], axis=0), k)
            pv = _softmax_pv(s, v[:, h * LANES:(h + 1) * LANES], h, bf16_scores=True)
            outs.append(pv[0:tq] - lam * pv[tq:2 * tq])
        acc = _merge_head_pairs(outs)
        sq = acc * acc
        inv = jnp.zeros((tq, C_HEADS * C_V), F32)
        for h in range(C_HEADS):
            in_head = (lane >= h * C_V) & (lane < (h + 1) * C_V)
            ms = jnp.sum(jnp.where(in_head, sq, 0.0), axis=-1, keepdims=True) * (1.0 / C_V)
            inv = inv + jnp.where(in_head, lax.rsqrt(ms + EPS), 0.0)
        o_ref[sample, rows, :] = ((acc * inv) * sn_ref[...] * (1.0 - lam_init)).astype(BF16)


def _outproj_kernel(x_ref, oa_ref, ob_ref, od_ref, w_ref, modb_ref, modc_ref, g2_ref, wr_ref,
                    xo_ref, h2_ref, aff_ref, *, n_chunks):
    w_hi, w_lo = _split_bf16(wr_ref[...])
    w_router = jnp.concatenate([w_hi, w_lo], axis=-1)
    lane = lax.broadcasted_iota(jnp.int32, (1, LANES), 1)
    mixed = _dot(jnp.concatenate([oa_ref[0], ob_ref[0], od_ref[0]], axis=-1), w_ref[0])
    for j in range(n_chunks):
        rows = slice(j * TOK_TILE, (j + 1) * TOK_TILE)
        mod = _chunk_mod(modb_ref, modc_ref, j, n_chunks)
        g1 = mod[:, 2 * D_MODEL:3 * D_MODEL]
        sh2 = mod[:, 3 * D_MODEL:4 * D_MODEL]
        sc2 = mod[:, 4 * D_MODEL:5 * D_MODEL]
        x = x_ref[0, rows, :] + g1 * mixed[rows]
        xo_ref[0, rows, :] = x
        h2 = (_rms(x) * g2_ref[...]) * (1.0 + sc2) + sh2
        h_hi, h_lo = _split_bf16(h2)
        h2_ref[0, rows, :] = h_hi
        prod = _dot(jnp.concatenate([h_hi, h_lo], axis=0), w_router)
        logits = (prod[0:TOK_TILE, 0:LANES] + prod[0:TOK_TILE, LANES:2 * LANES]
                  + prod[TOK_TILE:2 * TOK_TILE, 0:LANES])
        aff_ref[0, rows, :] = _softmax_rows(jnp.where(lane < N_EXPERTS, logits, NEG_INF))


def _outproj(xall, oa, ob, od, w, mod_l, g2, wr, with_ctx, layer):
    n_chunks = STEP_CHUNKS if with_ctx else LAT_STEP_CHUNKS
    n_steps = (N_TILES if with_ctx else N_LAT_TILES) // n_chunks
    full = lambda shape: pl.BlockSpec(shape, lambda b, t: (0,) * len(shape))
    tok = lambda n: pl.BlockSpec((1, n_chunks * TOK_TILE, n), lambda b, t: (b, t, 0))
    return pl.pallas_call(
        functools.partial(_outproj_kernel, n_chunks=n_chunks),
        out_shape=[jax.ShapeDtypeStruct((BATCH, NT_PAD, D_MODEL), F32),
                   jax.ShapeDtypeStruct((BATCH, NT_PAD, D_MODEL), BF16),
                   jax.ShapeDtypeStruct((BATCH, NT_PAD, LANES), F32)],
        grid=(BATCH, n_steps),
        in_specs=[tok(D_MODEL), tok(512), tok(256), tok(256),
                  pl.BlockSpec((1, D_MODEL, D_MODEL), lambda b, t: (layer, 0, 0))] + _mod_specs()
                 + [full((1, D_MODEL)), full((D_MODEL, LANES))],
        out_specs=[tok(D_MODEL), tok(D_MODEL), tok(LANES)],
        compiler_params=_params(("arbitrary", "arbitrary")),
        name="out_projection",
    )(xall, oa, ob, od, w, mod_l, mod_l, g2, wr)


def _prefix_count(flags_f32, tri_ref):
    rows, n = flags_f32.shape
    run = jnp.zeros((rows, 1), F32)
    outs = []
    for c in range(n // LANES):
        blk = flags_f32[:, c * LANES:(c + 1) * LANES]
        outs.append(_dot(blk.astype(BF16), tri_ref[...]) + run)
        run = run + jnp.sum(blk, axis=-1, keepdims=True)
    return jnp.concatenate(outs, axis=-1)


def _select_slots(a, cap, tri_ref):
    bits = pltpu.bitcast(a, jnp.int32)

    def step(i, t):
        cand = t | jnp.left_shift(jnp.int32(1), 30 - i)
        cnt = jnp.sum((bits >= cand).astype(jnp.int32), axis=-1, keepdims=True)
        return jnp.where(cnt >= cap, cand, t)

    thr = lax.fori_loop(0, 31, step, jnp.zeros((a.shape[0], 1), jnp.int32))
    above = bits > thr
    tied = bits == thr
    need = cap - jnp.sum(above.astype(F32), axis=-1, keepdims=True)
    tied_f = tied.astype(F32)
    sel = above | (tied & (_prefix_count(tied_f, tri_ref) < need))
    sel_f = sel.astype(F32)
    return jnp.where(sel, _prefix_count(sel_f, tri_ref), -1.0)


def _route_kernel(aff_ref, tri_ref, row_ref, arow_ref, col_ref, *, with_ctx):
    lat, ctx = [], []
    for b in range(BATCH):
        lat.append(aff_ref[b, 0:SEQ, :].T[0:N_EXPERTS])
        if with_ctx:
            ctx.append(aff_ref[b, SEQ:NT, :].T[0:N_EXPERTS])
    aff_lat = jnp.concatenate(lat, axis=0)
    arow_ref[:, 0:SEQ] = aff_lat
    slot_lat = _select_slots(aff_lat, CAP_LAT, tri_ref)
    if with_ctx:
        aff_ctx = jnp.concatenate(ctx, axis=0)
        arow_ref[:, SEQ:NT] = aff_ctx
        slot_ctx = _select_slots(aff_ctx, CAP_CTX, tri_ref)
        slots = jnp.concatenate([slot_lat, slot_ctx], axis=-1)
    else:
        slots = slot_lat
    n = slots.shape[1]
    row_ref[:, 0:n] = slots
    pad = jnp.full((LANES - N_EXPERTS, n), -1.0, F32)
    for b in range(BATCH):
        blk = jnp.concatenate([slots[b * N_EXPERTS:(b + 1) * N_EXPERTS], pad], axis=0)
        col_ref[b, 0:n, :] = blk.T
        if with_ctx and NT_PAD > NT:
            col_ref[b, NT:NT_PAD, :] = jnp.full((NT_PAD - NT, LANES), -1.0, F32)


def _route(aff, tri, with_ctx):
    vm = pl.BlockSpec(memory_space=pltpu.VMEM)
    return pl.pallas_call(
        functools.partial(_route_kernel, with_ctx=with_ctx),
        out_shape=[jax.ShapeDtypeStruct((BATCH * N_EXPERTS, NT), F32),
                   jax.ShapeDtypeStruct((BATCH * N_EXPERTS, NT), F32),
                   jax.ShapeDtypeStruct((BATCH, NT_PAD, LANES), F32)],
        in_specs=[vm, vm],
        out_specs=[vm, vm, vm],
        compiler_params=pltpu.CompilerParams(vmem_limit_bytes=VMEM_LIMIT),
        name="route_topc",
    )(aff, tri)


GATHER_GROUP = 4


def _gather_kernel(row_ref, arow_ref, h_ref, *out_refs, with_ctx):
    slot = lax.broadcasted_iota(jnp.int32, (CAP_LAT, 1), 0).astype(F32)

    def pick(e0, cols, n_slots, h_rows, x_ref, g_ref):
        hots = []
        for e in range(e0, e0 + GATHER_GROUP):
            hot = jnp.where(row_ref[e:e + 1, cols] == slot[0:n_slots], 1.0, 0.0)
            gate = jnp.sum(hot * arow_ref[e:e + 1, cols], axis=-1, keepdims=True)
            g_ref[e] = jnp.broadcast_to(gate, (n_slots, LANES))
            hots.append(hot.astype(BF16))
        x = _dot(jnp.concatenate(hots, axis=0), h_rows).astype(BF16)
        for i in range(GATHER_GROUP):
            x_ref[e0 + i] = x[i * n_slots:(i + 1) * n_slots]

    for e0 in range(0, N_EXPERTS, GATHER_GROUP):
        pick(e0, slice(0, SEQ), CAP_LAT, h_ref[0, 0:SEQ, :], out_refs[0], out_refs[1])
        if with_ctx:
            pick(e0, slice(SEQ, NT), CAP_CTX, h_ref[0, SEQ:NT, :], out_refs[2], out_refs[3])


def _gather(slot_rows, aff_rows, h2, with_ctx):
    out_shape = [jax.ShapeDtypeStruct((N_EXPERTS, SEQ, D_MODEL), BF16),
                 jax.ShapeDtypeStruct((N_EXPERTS, SEQ, LANES), F32)]
    out_specs = [pl.BlockSpec((N_EXPERTS, CAP_LAT, D_MODEL), lambda b: (0, b, 0)),
                 pl.BlockSpec((N_EXPERTS, CAP_LAT, LANES), lambda b: (0, b, 0))]
    if with_ctx:
        out_shape += [jax.ShapeDtypeStruct((N_EXPERTS, CTX_LEN, D_MODEL), BF16),
                      jax.ShapeDtypeStruct((N_EXPERTS, CTX_LEN, LANES), F32)]
        out_specs += [pl.BlockSpec((N_EXPERTS, CAP_CTX, D_MODEL), lambda b: (0, b, 0)),
                      pl.BlockSpec((N_EXPERTS, CAP_CTX, LANES), lambda b: (0, b, 0))]
    row_spec = pl.BlockSpec((N_EXPERTS, NT), lambda b: (b, 0))
    return pl.pallas_call(
        functools.partial(_gather_kernel, with_ctx=with_ctx),
        out_shape=out_shape,
        grid=(BATCH,),
        in_specs=[row_spec, row_spec, pl.BlockSpec((1, NT_PAD, D_MODEL), lambda b: (b, 0, 0))],
        out_specs=out_specs,
        compiler_params=_params(("arbitrary",)),
        name="expert_gather",
    )(slot_rows, aff_rows, h2)


def _ffn_kernel(*refs, with_ctx):
    if with_ctx:
        xl_ref, gl_ref, xc_ref, gc_ref, wg_ref, wu_ref, wd_ref, y_ref = refs
    else:
        xl_ref, gl_ref, wg_ref, wu_ref, wd_ref, y_ref = refs
    wg = wg_ref[0, 0].astype(BF16)
    wu = wu_ref[0, 0].astype(BF16)
    wd = wd_ref[0, 0].astype(BF16)

    def swiglu(x, slot_gate):
        gate = _dot(x, wg)
        hid = (gate / (1.0 + jnp.exp(-gate))) * _dot(x, wu)
        scale = jnp.concatenate([slot_gate] * (D_MODEL // LANES), axis=-1)
        return (_dot(hid.astype(BF16), wd) * scale).astype(BF16)

    for r in range(FFN_LAT_TILES):
        rows = slice(r * FFN_TILE, (r + 1) * FFN_TILE)
        y_ref[0, rows, :] = swiglu(xl_ref[0, rows, :], gl_ref[0, rows, :])
    if with_ctx:
        y_ref[0, SEQ:NT, :] = swiglu(xc_ref[0], gc_ref[0])


def _ffn(gathered, w_gate, w_up, w_down, layer):
    with_ctx = len(gathered) == 4
    ins = list(gathered) + [w_gate, w_up, w_down]
    in_specs = [pl.BlockSpec((1, SEQ, D_MODEL), lambda e: (e, 0, 0)),
                pl.BlockSpec((1, SEQ, LANES), lambda e: (e, 0, 0))]
    if with_ctx:
        in_specs += [pl.BlockSpec((1, CTX_LEN, D_MODEL), lambda e: (e, 0, 0)),
                     pl.BlockSpec((1, CTX_LEN, LANES), lambda e: (e, 0, 0))]
    in_specs += [pl.BlockSpec((1, 1, D_MODEL, D_EXPERT), lambda e: (layer, e, 0, 0)),
                 pl.BlockSpec((1, 1, D_MODEL, D_EXPERT), lambda e: (layer, e, 0, 0)),
                 pl.BlockSpec((1, 1, D_EXPERT, D_MODEL), lambda e: (layer, e, 0, 0))]
    return pl.pallas_call(
        functools.partial(_ffn_kernel, with_ctx=with_ctx),
        out_shape=jax.ShapeDtypeStruct((N_EXPERTS, NT, D_MODEL), BF16),
        grid=(N_EXPERTS,),
        in_specs=in_specs,
        out_specs=pl.BlockSpec((1, NT, D_MODEL), lambda e: (e, 0, 0)),
        compiler_params=_params(("arbitrary",)),
        name="expert_ffn",
    )(*ins)


COMBINE_ROWS = 512


def _scatter_add(col, ys, n_slots):
    lane = lax.broadcasted_iota(jnp.int32, (1, n_slots), 1).astype(F32)
    hot = [_indicator(col[:, e:e + 1] == lane) for e in range(N_EXPERTS)]
    return _dot(jnp.concatenate(hot, axis=-1), jnp.concatenate(ys, axis=0))


def _combine_latent_kernel(*refs, final):
    if final:
        x_ref, col_ref, y_ref, mod_ref, fn_ref, o_ref = refs
    else:
        x_ref, col_ref, y_ref, mod_ref, o_ref = refs
    g2 = mod_ref[0][:, 5 * D_MODEL:6 * D_MODEL]
    acc = _scatter_add(col_ref[0], [y_ref[e] for e in range(N_EXPERTS)], CAP_LAT)
    x = x_ref[0] + g2 * acc
    if final:
        x = _rms(x) * fn_ref[...]
    o_ref[0] = x


def _combine_ctx_kernel(x_ref, col_ref, y_ref, mod_ref, prev_ref, o_ref):
    del prev_ref
    g2 = mod_ref[0][:, 5 * D_MODEL:6 * D_MODEL]
    pad = jnp.zeros((LANES - CAP_CTX, D_MODEL), BF16)
    for b in range(BATCH):
        ys = [jnp.concatenate([y_ref[e, b * CAP_CTX:(b + 1) * CAP_CTX, :], pad], axis=0)
              for e in range(N_EXPERTS)]
        o_ref[b] = x_ref[b] + g2 * _scatter_add(col_ref[b], ys, LANES)


def _combine(xall, slot_cols, y, mod_l, with_ctx, final_norm=None):
    final = final_norm is not None
    n_rows = SEQ if final else NT_PAD
    tok = lambda n: pl.BlockSpec((1, COMBINE_ROWS, n), lambda b, t: (b, t, 0))
    ins = [xall, slot_cols, y, mod_l]
    in_specs = [tok(D_MODEL), tok(LANES),
                pl.BlockSpec((N_EXPERTS, CAP_LAT, D_MODEL), lambda b, t: (0, b, 0)),
                pl.BlockSpec((1, 1, 6 * D_MODEL), lambda b, t: (b, 0, 0))]
    if final:
        ins.append(final_norm)
        in_specs.append(pl.BlockSpec((1, D_MODEL), lambda b, t: (0, 0)))
    out = pl.pallas_call(
        functools.partial(_combine_latent_kernel, final=final),
        out_shape=jax.ShapeDtypeStruct((BATCH, n_rows, D_MODEL), F32),
        grid=(BATCH, SEQ // COMBINE_ROWS),
        in_specs=in_specs,
        out_specs=tok(D_MODEL),
        compiler_params=_params(("arbitrary", "arbitrary")),
        name="expert_combine",
    )(*ins)
    if not with_ctx:
        return out
    ctx_blk = SEQ // CTX_LEN
    ctx_rows = lambda n: pl.BlockSpec((BATCH, CTX_LEN, n), lambda i: (0, ctx_blk, 0))
    return pl.pallas_call(
        _combine_ctx_kernel,
        out_shape=jax.ShapeDtypeStruct((BATCH, NT_PAD, D_MODEL), F32),
        grid=(1,),
        in_specs=[ctx_rows(D_MODEL), ctx_rows(LANES),
                  pl.BlockSpec((N_EXPERTS, CTX_LEN, D_MODEL), lambda i: (0, ctx_blk, 0)),
                  pl.BlockSpec((1, 1, 6 * D_MODEL), lambda i: (BATCH, 0, 0)),
                  pl.BlockSpec(memory_space=pl.ANY)],
        out_specs=ctx_rows(D_MODEL),
        input_output_aliases={4: 0},
        compiler_params=_params(("arbitrary",)),
        name="expert_combine_ctx",
    )(xall, slot_cols, y, mod_l, out)


def _rope_tables():
    t = jnp.arange(SEQ)
    pos = jnp.stack([t // GRID_W, t % GRID_W], axis=0)

    def table(n_f, lanes_per_unit):
        inv = ROPE_BASE ** (-jnp.arange(n_f, dtype=F32) / n_f)
        d = np.arange(lanes_per_unit)
        axis, second, f = d // (2 * n_f), (d // n_f) % 2, d % n_f
        ang = pos[axis].T.astype(F32) * inv[f][None, :]
        cos = jnp.cos(ang)
        sin = jnp.sin(ang) * jnp.where(second == 1, 1.0, -1.0)[None, :]
        return cos, sin

    def finish(cos, sin, scale):
        cos = jnp.concatenate([cos, jnp.ones((NT_PAD - SEQ, LANES), F32)], axis=0)
        sin = jnp.concatenate([sin, jnp.zeros((NT_PAD - SEQ, LANES), F32)], axis=0)
        scale = scale * LOG2_E
        return [cos, sin, cos * scale, sin * scale]

    cos_a, sin_a = table(HEAD_DIM // 4, HEAD_DIM)
    tabs_a = finish(jnp.tile(cos_a, (1, 2)), jnp.tile(sin_a, (1, 2)), HEAD_DIM ** -0.5)
    cos_8, sin_8 = table(B_ROPE // 4, B_ROPE)
    tabs_8 = finish(jnp.tile(cos_8, (1, 4)), jnp.tile(sin_8, (1, 4)), C_DH ** -0.5)
    ones = jnp.ones((SEQ, B_NOPE), F32)
    pad1 = jnp.ones((SEQ, LANES - B_NOPE - B_ROPE), F32)
    cos_b = jnp.concatenate([ones, cos_8, pad1], axis=1)
    sin_b = jnp.concatenate([0 * ones, sin_8, 0 * pad1], axis=1)
    tabs_b = finish(cos_b, sin_b, (B_NOPE + B_ROPE) ** -0.5)
    return tabs_a + tabs_8 + tabs_b


def _window_bias():
    qi = np.arange(WIN)[:, None]
    kj = np.arange(A_LOCAL)[None, :]
    out = np.zeros((3, WIN, A_LOCAL + CTX_LEN), np.float32)
    for v in range(3):
        out[v, :, :A_LOCAL] = np.where(np.abs(kj - v * WIN - qi) <= WIN, 0.0, NEG_INF)
    return jnp.asarray(out)


def _prep_weights(w_in, w_uq, w_ukv, w_out, w_router):
    offs = np.cumsum((0, 512, 128, 128, 256, 128, 32, 256, 256, 256))
    seg = lambda i: w_in[:, :, offs[i]:offs[i + 1]]
    wq = seg(0).reshape(DEPTH, D_MODEL, A_KV_HEADS, A_GROUP, HEAD_DIM)
    wq = jnp.swapaxes(wq, 2, 3).reshape(DEPTH, D_MODEL, A_HEADS * HEAD_DIM)
    kr = jnp.pad(seg(5), ((0, 0), (0, 0), (B_NOPE, LANES - B_NOPE - B_ROPE)))
    w = jnp.concatenate([wq, seg(1), seg(2), seg(3), seg(4), kr,
                         seg(6), seg(7), seg(8)], axis=-1).astype(BF16)
    wuq = w_uq.reshape(DEPTH, B_Q_RANK, B_HEADS, B_NOPE + B_ROPE)
    wuq = jnp.pad(wuq, ((0, 0), (0, 0), (0, 0), (0, LANES - B_NOPE - B_ROPE)))
    wuq = wuq.reshape(DEPTH, B_Q_RANK, B_HEADS * LANES).astype(BF16)
    wukv = w_ukv.reshape(DEPTH, B_KV_RANK, B_HEADS, B_NOPE + B_V)
    wk = jnp.pad(wukv[..., :B_NOPE], ((0, 0), (0, 0), (0, 0), (0, LANES - B_NOPE)))
    wukv = jnp.concatenate([wk.reshape(DEPTH, B_KV_RANK, B_HEADS * LANES),
                            wukv[..., B_NOPE:].reshape(DEPTH, B_KV_RANK, B_HEADS * B_V)],
                           axis=-1).astype(BF16)
    wo_a = w_out[:, :A_HEADS * HEAD_DIM].reshape(DEPTH, A_KV_HEADS, A_GROUP, HEAD_DIM, D_MODEL)
    wo_a = jnp.swapaxes(wo_a, 1, 2).reshape(DEPTH, A_HEADS * HEAD_DIM, D_MODEL)
    wo = jnp.concatenate([wo_a, w_out[:, A_HEADS * HEAD_DIM:]], axis=1).astype(BF16)
    wr = jnp.pad(w_router, ((0, 0), (0, 0), (0, LANES - N_EXPERTS)))
    return w, wuq, wukv, wo, wr


def kernel(x, c, ctx, c_ctx, norm1, norm2, w_ada, b_ada, w_in, sink, mla_q_norm, w_uq, mla_kv_norm, w_ukv,
           lam_q1, lam_k1, lam_q2, lam_k2, diff_norm, w_out, w_router, w_gate, w_up, w_down, final_norm):
    xall = jnp.concatenate([x, ctx, jnp.zeros((BATCH, NT_PAD - NT, D_MODEL), F32)], axis=1)
    cc = jnp.concatenate([c, c_ctx[None, :], jnp.zeros((16 - BATCH - 1, D_MODEL), F32)], axis=0)
    mod = _modulation(cc, w_ada, b_ada).reshape(DEPTH, 16, 1, 6 * D_MODEL)
    tabs = _rope_tables()
    bias = _window_bias()
    tri = jnp.asarray(np.triu(np.ones((LANES, LANES), np.float32), k=1), BF16)
    w, wuq, wukv, wo, wr = _prep_weights(w_in, w_uq, w_ukv, w_out, w_router)
    sub_norm = jnp.tile(diff_norm, (1, C_HEADS)).reshape(DEPTH, 1, C_HEADS * C_V)
    row = lambda a, l: a[l].reshape(1, -1)

    for l in range(DEPTH):
        need_ctx = l < DEPTH - 1
        qa, ka, va, qb, kb, vb, qd, kd, vd = _inproj(
            xall, mod[l], row(norm1, l), w, row(mla_q_norm, l), wuq[l], row(mla_kv_norm, l), wukv[l], tabs, l)

        oa = _attn_a_latent(sink[l], qa, ka, va, bias)
        ob = _full_attention(_attn_b_kernel, "attn_mla", MLA_TQ, qb, kb, vb, [], lambda n: [],
                             256, 512, 512, 512)
        lam_init = 0.8 - 0.6 * math.exp(-0.3 * l)
        c_kernel = functools.partial(_attn_c_kernel, lam_init=lam_init)
        c_extra = [row(lam_q1, l), row(lam_k1, l), row(lam_q2, l), row(lam_k2, l), sub_norm[l]]

        def c_specs(n_grid, c_extra=c_extra):
            zmap = (lambda b, t: (0, 0)) if n_grid == 2 else (lambda b: (0, 0))
            return [pl.BlockSpec(a.shape, zmap) for a in c_extra]

        od = _full_attention(c_kernel, "attn_diff", DIFF_TQ, qd, kd, vd, c_extra, c_specs, 256, 256, 256, 512)
        if need_ctx:
            oa = _attn_a_ctx(sink[l], qa, ka, va, oa)
            ob = _full_attention_ctx(_attn_b_kernel, "attn_mla", qb, kb, vb, [], lambda n: [],
                                     256, 512, 512, 512, ob)
            od = _full_attention_ctx(c_kernel, "attn_diff", qd, kd, vd, c_extra, c_specs,
                                     256, 256, 256, 512, od)

        xall, h2, aff = _outproj(xall, oa, ob, od, wo, mod[l], row(norm2, l), wr[l], need_ctx, l)
        slot_rows, aff_rows, slot_cols = _route(aff, tri, need_ctx)
        gathered = _gather(slot_rows, aff_rows, h2, need_ctx)
        y = _ffn(gathered, w_gate, w_up, w_down, l)
        xall = _combine(xall, slot_cols, y, mod[l], need_ctx,
                        final_norm=None if need_ctx else final_norm.reshape(1, D_MODEL))
    return xall
```

```python
import functools
import math

import numpy as np
import jax
import jax.numpy as jnp
from jax import lax
from jax.experimental import pallas as pl
from jax.experimental.pallas import tpu as pltpu

D_MODEL = 1024
BATCH = 8
SEQ = 2048
DEPTH = 4
CTX_LEN = 256
NT = SEQ + CTX_LEN
GRID_W = 64
HEAD_DIM = 64
ROPE_BASE = 10000.0
EPS = 1e-6
NEG_INF = -1e30
LOG2_E = math.log2(math.e)

WIN = 128
A_HEADS = 8
A_KV_HEADS = 2
A_GROUP = A_HEADS // A_KV_HEADS
B_HEADS = 4
B_NOPE = 64
B_ROPE = 32
B_V = 64
B_Q_RANK = 256
B_KV_RANK = 128
C_HEADS = 4
C_DH = 32
C_V = 2 * C_DH
N_EXPERTS = 16
EC_FACTOR = 2
D_EXPERT = 512
CAP_LAT = EC_FACTOR * SEQ // N_EXPERTS
CAP_CTX = EC_FACTOR * CTX_LEN // N_EXPERTS

LANES = 128
TOK_TILE = 256
NT_PAD = SEQ + TOK_TILE
N_TILES = NT_PAD // TOK_TILE
N_LAT_TILES = SEQ // TOK_TILE
STEP_CHUNKS = 3
LAT_STEP_CHUNKS = 2
FFN_TILE = 1024
FFN_LAT_TILES = SEQ // FFN_TILE
VMEM_LIMIT = 56 * 1024 * 1024

F32 = jnp.float32
BF16 = jnp.bfloat16

_NT_DIMS = (((1,), (1,)), ((), ()))


def _params(sem):
    return pltpu.CompilerParams(dimension_semantics=sem, vmem_limit_bytes=VMEM_LIMIT)


def _dot(a, b):
    return jnp.dot(a, b, preferred_element_type=F32)


def _dot_nt(a, b):
    return lax.dot_general(a, b, _NT_DIMS, preferred_element_type=F32)


def _split_bf16(v):
    hi = v.astype(BF16)
    lo = (v - hi.astype(F32)).astype(BF16)
    return hi, lo


def _dot3(a, b):
    a_hi, a_lo = _split_bf16(a)
    b_hi, b_lo = _split_bf16(b)
    return _dot(a_hi, b_hi) + _dot(a_lo, b_hi) + _dot(a_hi, b_lo)


def _indicator(cond):
    return jnp.where(cond, 1.0, 0.0).astype(BF16)


def _rms(v):
    return v * lax.rsqrt(jnp.mean(v * v, axis=-1, keepdims=True) + EPS)


def _softmax_pv(s, v_blk, head):
    m = jnp.max(s, axis=-1, keepdims=True)
    ol = _dot(jnp.exp2(s - m).astype(BF16), v_blk)
    ones_lane = 64 if head % 2 == 0 else 0
    return ol / ol[:, ones_lane:ones_lane + 1]


def _merge_head_pairs(per_head):
    lane = lax.broadcasted_iota(jnp.int32, (1, LANES), 1)
    blocks = [jnp.where(lane < 64, per_head[j], per_head[j + 1]) for j in range(0, len(per_head), 2)]
    return jnp.concatenate(blocks, axis=-1)


def _softmax_rows(s):
    m = jnp.max(s, axis=-1, keepdims=True)
    e = jnp.exp(s - m)
    return e / jnp.sum(e, axis=-1, keepdims=True)


def _mod_kernel(c_ref, w_ref, b_ref, o_ref):
    c = c_ref[...]
    s = c / (1.0 + jnp.exp(-c))
    o_ref[0] = _dot3(s, w_ref[0]) + b_ref[0]


def _modulation(cc, w_ada, b_ada):
    tn = 1536
    return pl.pallas_call(
        _mod_kernel,
        out_shape=jax.ShapeDtypeStruct((DEPTH, 16, 6 * D_MODEL), F32),
        grid=(DEPTH, 6 * D_MODEL // tn),
        in_specs=[
            pl.BlockSpec((16, D_MODEL), lambda l, j: (0, 0)),
            pl.BlockSpec((1, D_MODEL, tn), lambda l, j: (l, 0, j)),
            pl.BlockSpec((1, 1, tn), lambda l, j: (l, 0, j)),
        ],
        out_specs=pl.BlockSpec((1, 16, tn), lambda l, j: (l, 0, j)),
        compiler_params=_params(("arbitrary", "arbitrary")),
        name="adaln_mod",
    )(cc, w_ada, b_ada.reshape(DEPTH, 1, 6 * D_MODEL))


def _rope(p, cos, sin, half):
    lane = lax.broadcasted_iota(jnp.int32, (1, LANES), 1)
    first = (lane // half) % 2 == 0
    outs = []
    for j in range(p.shape[1] // LANES):
        blk = p[:, j * LANES:(j + 1) * LANES]
        partner = jnp.where(first, pltpu.roll(blk, LANES - half, 1), pltpu.roll(blk, half, 1))
        outs.append(blk * cos + partner * sin)
    return outs[0] if len(outs) == 1 else jnp.concatenate(outs, axis=-1)


def _per_head_with_ones(v):
    lane = lax.broadcasted_iota(jnp.int32, (1, LANES), 1)
    outs = []
    for j in range(v.shape[1] // LANES):
        blk = v[:, j * LANES:(j + 1) * LANES]
        outs.append(jnp.where(lane < 64, blk, jnp.where(lane == 64, 1.0, 0.0)))
        outs.append(jnp.where(lane >= 64, blk, jnp.where(lane == 0, 1.0, 0.0)))
    return jnp.concatenate(outs, axis=-1)


def _chunk_mod(modb_ref, modc_ref, j, n_chunks):
    mod = modb_ref[0]
    if n_chunks == STEP_CHUNKS and j == n_chunks - 1:
        mod = jnp.where(pl.program_id(1) == N_TILES // STEP_CHUNKS - 1, modc_ref[0], mod)
    return mod


def _inproj_kernel(x_ref, modb_ref, modc_ref, g1_ref, w_ref, qn_ref, wuq_ref, kvn_ref, wukv_ref,
                   ca_ref, sa_ref, caq_ref, saq_ref, c8_ref, s8_ref, c8q_ref, s8q_ref,
                   cb_ref, sb_ref, cbq_ref, sbq_ref,
                   qa_ref, ka_ref, va_ref, qb_ref, kb_ref, vb_ref, qd_ref, kd_ref, vd_ref):
    for j in range(STEP_CHUNKS):
        rows = slice(j * TOK_TILE, (j + 1) * TOK_TILE)
        x = x_ref[0, rows, :]
        mod = _chunk_mod(modb_ref, modc_ref, j, STEP_CHUNKS)
        sh1 = mod[:, 0:D_MODEL]
        sc1 = mod[:, D_MODEL:2 * D_MODEL]
        h = (_rms(x) * g1_ref[...]) * (1.0 + sc1) + sh1
        p = _dot(h.astype(BF16), w_ref[0])

        qa_ref[0, rows, :] = _rope(p[:, 0:512], caq_ref[rows, :], saq_ref[rows, :], 16).astype(BF16)
        ka_ref[0, rows, :] = _rope(p[:, 512:640], ca_ref[rows, :], sa_ref[rows, :], 16).astype(BF16)
        lane = lax.broadcasted_iota(jnp.int32, (TOK_TILE, LANES), 1)
        va_ref[0, rows, :] = jnp.concatenate(
            [p[:, 640:768], jnp.where(lane == 0, 1.0, 0.0)], axis=-1).astype(BF16)

        cq = (_rms(p[:, 768:1024]) * qn_ref[...]).astype(BF16)
        qb = _dot(cq, wuq_ref[...])
        qb_ref[0, rows, :] = _rope(qb, cbq_ref[rows, :], sbq_ref[rows, :], 8).astype(BF16)
        ckv = (_rms(p[:, 1024:1152]) * kvn_ref[...]).astype(BF16)
        kv = _dot(ckv, wukv_ref[...])
        kr = _rope(p[:, 1152:1280], cb_ref[rows, :], sb_ref[rows, :], 8)
        kb_ref[0, rows, :] = (kv[:, 0:512] + jnp.concatenate([kr] * B_HEADS, axis=-1)).astype(BF16)
        vb_ref[0, rows, :] = _per_head_with_ones(kv[:, 512:768]).astype(BF16)

        qd_ref[0, rows, :] = _rope(p[:, 1280:1536], c8q_ref[rows, :], s8q_ref[rows, :], 8).astype(BF16)
        kd_ref[0, rows, :] = _rope(p[:, 1536:1792], c8_ref[rows, :], s8_ref[rows, :], 8).astype(BF16)
        vd_ref[0, rows, :] = _per_head_with_ones(p[:, 1792:2048]).astype(BF16)


def _mod_specs():
    return [pl.BlockSpec((1, 1, 6 * D_MODEL), lambda b, t: (b, 0, 0)),
            pl.BlockSpec((1, 1, 6 * D_MODEL), lambda b, t: (BATCH, 0, 0))]


def _inproj(xall, mod_l, g1, w, qn, wuq, kvn, wukv, tabs, layer):
    full = lambda shape: pl.BlockSpec(shape, lambda b, t: (0,) * len(shape))
    step_rows = STEP_CHUNKS * TOK_TILE
    tab_spec = pl.BlockSpec((step_rows, LANES), lambda b, t: (t, 0))
    widths = (512, 128, 256, 512, 512, 512, 256, 256, 512)
    return pl.pallas_call(
        _inproj_kernel,
        out_shape=[jax.ShapeDtypeStruct((BATCH, NT_PAD, n), BF16) for n in widths],
        grid=(BATCH, N_TILES // STEP_CHUNKS),
        in_specs=[pl.BlockSpec((1, step_rows, D_MODEL), lambda b, t: (b, t, 0))] + _mod_specs() + [
            full((1, D_MODEL)),
            pl.BlockSpec((1, D_MODEL, 2048), lambda b, t: (layer, 0, 0)),
            full((1, B_Q_RANK)),
            full((B_Q_RANK, 512)),
            full((1, B_KV_RANK)),
            full((B_KV_RANK, 768)),
        ] + [tab_spec] * 12,
        out_specs=[pl.BlockSpec((1, step_rows, n), lambda b, t: (b, t, 0)) for n in widths],
        compiler_params=_params(("arbitrary", "arbitrary")),
        name="in_projection",
    )(xall, mod_l, mod_l, g1, w, qn, wuq, kvn, wukv, *tabs)


def _window_heads(q, kcat, vcat, bias, sink_ref):
    tq = q.shape[0]
    lane = lax.broadcasted_iota(jnp.int32, (1, LANES), 1)
    acc = [jnp.zeros((tq, LANES), F32) for _ in range(A_GROUP)]
    for g in range(A_KV_HEADS):
        in_half = (lane < HEAD_DIM) if g == 0 else (lane >= HEAD_DIM)
        keep = _indicator(in_half)
        qs = jnp.concatenate([q[:, j * LANES:(j + 1) * LANES] * keep for j in range(A_GROUP)], axis=0)
        s = _dot_nt(qs, kcat)
        if bias is not None:
            s = s + jnp.concatenate([bias] * A_GROUP, axis=0)
        sink = jnp.concatenate(
            [jnp.full((tq, 1), sink_ref[g * A_GROUP + j] * LOG2_E, F32) for j in range(A_GROUP)], axis=0)
        m = jnp.maximum(jnp.max(s, axis=-1, keepdims=True), sink)
        ol = _dot(jnp.exp2(s - m).astype(BF16), vcat)
        o = ol[:, 0:LANES] / (ol[:, LANES:LANES + 1] + jnp.exp2(sink - m))
        for j in range(A_GROUP):
            acc[j] = acc[j] + jnp.where(in_half, o[j * tq:(j + 1) * tq], 0.0)
    return jnp.concatenate(acc, axis=-1).astype(BF16)


A_STEP_BLOCKS = 8
A_LOCAL = 3 * WIN


def _attn_a_latent_kernel(sink_ref, q_ref, k_ref, v_ref, bias_ref, o_ref):
    step = pl.program_id(1)
    last = SEQ // WIN - 1
    k_ctx = k_ref[0, SEQ:NT, :]
    v_ctx = v_ref[0, SEQ:NT, :]
    for j in range(A_STEP_BLOCKS):
        n = step * A_STEP_BLOCKS + j
        start = pl.multiple_of(jnp.clip((n - 1) * WIN, 0, SEQ - A_LOCAL), WIN)
        variant = jnp.where(n == 0, 0, jnp.where(n == last, 2, 1))
        kcat = jnp.concatenate([k_ref[0, pl.ds(start, A_LOCAL), :], k_ctx], axis=0)
        vcat = jnp.concatenate([v_ref[0, pl.ds(start, A_LOCAL), :], v_ctx], axis=0)
        rows = slice(j * WIN, (j + 1) * WIN)
        o_ref[0, rows, :] = _window_heads(q_ref[0, rows, :], kcat, vcat, bias_ref[variant], sink_ref)


def _attn_a_latent(sink, qa, ka, va, bias):
    tq = A_STEP_BLOCKS * WIN
    return pl.pallas_call(
        _attn_a_latent_kernel,
        out_shape=jax.ShapeDtypeStruct((BATCH, NT_PAD, 512), BF16),
        grid=(BATCH, SEQ // tq),
        in_specs=[pl.BlockSpec(memory_space=pltpu.SMEM),
                  pl.BlockSpec((1, tq, 512), lambda b, n: (b, n, 0)),
                  pl.BlockSpec((1, NT_PAD, LANES), lambda b, n: (b, 0, 0)),
                  pl.BlockSpec((1, NT_PAD, 2 * LANES), lambda b, n: (b, 0, 0)),
                  pl.BlockSpec((3, WIN, A_LOCAL + CTX_LEN), lambda b, n: (0, 0, 0))],
        out_specs=pl.BlockSpec((1, tq, 512), lambda b, n: (b, n, 0)),
        compiler_params=_params(("arbitrary", "arbitrary")),
        name="attn_window_latent",
    )(sink, qa, ka, va, bias)


def _attn_a_ctx(sink, qa, ka, va, prev_out):
    ctx_blk = SEQ // CTX_LEN
    spec = lambda n: pl.BlockSpec((BATCH, CTX_LEN, n), lambda i: (0, ctx_blk, 0))
    tile = lambda n: pl.BlockSpec((BATCH, TOK_TILE, n), lambda i: (0, N_LAT_TILES, 0))

    def body(sink_ref, q_ref, k_ref, v_ref, prev_ref, o_ref):
        del prev_ref
        for sample in range(BATCH):
            o_ref[sample] = _window_heads(q_ref[sample], k_ref[sample], v_ref[sample], None, sink_ref)

    return pl.pallas_call(
        body,
        out_shape=jax.ShapeDtypeStruct((BATCH, NT_PAD, 512), BF16),
        grid=(1,),
        in_specs=[pl.BlockSpec(memory_space=pltpu.SMEM), tile(512), spec(LANES), spec(2 * LANES),
                  pl.BlockSpec(memory_space=pl.ANY)],
        out_specs=tile(512),
        input_output_aliases={4: 0},
        compiler_params=_params(("arbitrary",)),
        name="attn_window_ctx",
    )(sink, qa, ka, va, prev_out)


def _key_rows(ref, sample):
    return ref[sample, 0:NT, :] if ref.shape[1] == NT_PAD else ref[sample]


MLA_TQ = 1024
MLA_SUB = 512
DIFF_TQ = 1024
DIFF_SUB = 256


def _attn_b_kernel(q_ref, k_ref, v_ref, o_ref):
    for sample in range(q_ref.shape[0]):
        k = _key_rows(k_ref, sample)
        v = _key_rows(v_ref, sample)
        sub_rows = min(MLA_SUB, q_ref.shape[1])
        for sub in range(q_ref.shape[1] // sub_rows):
            rows = slice(sub * sub_rows, (sub + 1) * sub_rows)
            q = q_ref[sample, rows, :]
            outs = []
            for h in range(B_HEADS):
                s = _dot_nt(q[:, h * LANES:(h + 1) * LANES], k[:, h * LANES:(h + 1) * LANES])
                outs.append(_softmax_pv(s, v[:, h * LANES:(h + 1) * LANES], h))
            o_ref[sample, rows, :] = _merge_head_pairs(outs).astype(BF16)


def _full_attention(kernel, name, tq, q, k, v, extra, extra_specs, out_width, q_width, k_width, v_width):
    out = pl.pallas_call(
        kernel,
        out_shape=jax.ShapeDtypeStruct((BATCH, NT_PAD, out_width), BF16),
        grid=(BATCH, SEQ // tq),
        in_specs=[pl.BlockSpec((1, tq, q_width), lambda b, t: (b, t, 0)),
                  pl.BlockSpec((1, NT_PAD, k_width), lambda b, t: (b, 0, 0)),
                  pl.BlockSpec((1, NT_PAD, v_width), lambda b, t: (b, 0, 0))] + extra_specs(2),
        out_specs=pl.BlockSpec((1, tq, out_width), lambda b, t: (b, t, 0)),
        compiler_params=_params(("arbitrary", "arbitrary")),
        name=name + "_latent",
    )(q, k, v, *extra)
    return out


def _full_attention_ctx(kernel, name, q, k, v, extra, extra_specs, out_width, q_width, k_width, v_width,
                        prev_out):
    ctx_blk = SEQ // CTX_LEN
    n_extra = len(extra)

    def body(*refs):
        ins = refs[:3 + n_extra]
        kernel(*ins, refs[-1])

    spec = lambda n: pl.BlockSpec((BATCH, CTX_LEN, n), lambda i: (0, ctx_blk, 0))
    tile = lambda n: pl.BlockSpec((BATCH, TOK_TILE, n), lambda i: (0, N_LAT_TILES, 0))
    return pl.pallas_call(
        body,
        out_shape=jax.ShapeDtypeStruct((BATCH, NT_PAD, out_width), BF16),
        grid=(1,),
        in_specs=[tile(q_width), spec(k_width), spec(v_width)] + extra_specs(1)
                 + [pl.BlockSpec(memory_space=pl.ANY)],
        out_specs=tile(out_width),
        input_output_aliases={3 + n_extra: 0},
        compiler_params=_params(("arbitrary",)),
        name=name + "_ctx",
    )(q, k, v, *extra, prev_out)


def _attn_c_kernel(q_ref, k_ref, v_ref, lq1_ref, lk1_ref, lq2_ref, lk2_ref, sn_ref, o_ref, *, lam_init):
    tq = min(DIFF_SUB, q_ref.shape[1])
    lam = (jnp.exp(jnp.sum(lq1_ref[...] * lk1_ref[...], axis=-1, keepdims=True))
           - jnp.exp(jnp.sum(lq2_ref[...] * lk2_ref[...], axis=-1, keepdims=True)) + lam_init)
    lane = lax.broadcasted_iota(jnp.int32, (1, C_HEADS * C_V), 1)
    units = [(sample, sub) for sample in range(q_ref.shape[0]) for sub in range(q_ref.shape[1] // tq)]
    for sample, sub in units:
        k = _key_rows(k_ref, sample)
        v = _key_rows(v_ref, sample)
        rows = slice(sub * tq, (sub + 1) * tq)
        q = q_ref[sample, rows, :]
        outs = []
        for h in range(C_HEADS):
            lo = h * C_V
            q1 = q * _indicator((lane >= lo) & (lane < lo + C_DH))
            q2 = q * _indicator((lane >= lo + C_DH) & (lane < lo + C_V))
            s = _dot_nt(jnp.concatenate([q1, q2], axis=0), k)
            pv = _softmax_pv(s, v[:, h * LANES:(h + 1) * LANES], h)
            outs.append(pv[0:tq] - lam * pv[tq:2 * tq])
        acc = _merge_head_pairs(outs)
        sq = acc * acc
        inv = jnp.zeros((tq, C_HEADS * C_V), F32)
        for h in range(C_HEADS):
            in_head = (lane >= h * C_V) & (lane < (h + 1) * C_V)
            ms = jnp.sum(jnp.where(in_head, sq, 0.0), axis=-1, keepdims=True) * (1.0 / C_V)
            inv = inv + jnp.where(in_head, lax.rsqrt(ms + EPS), 0.0)
        o_ref[sample, rows, :] = ((acc * inv) * sn_ref[...] * (1.0 - lam_init)).astype(BF16)


def _outproj_kernel(x_ref, oa_ref, ob_ref, od_ref, w_ref, modb_ref, modc_ref, g2_ref, wr_ref,
                    xo_ref, h2_ref, aff_ref, *, n_chunks):
    w_hi, w_lo = _split_bf16(wr_ref[...])
    w_router = jnp.concatenate([w_hi, w_lo], axis=-1)
    lane = lax.broadcasted_iota(jnp.int32, (1, LANES), 1)
    mixed = _dot(jnp.concatenate([oa_ref[0], ob_ref[0], od_ref[0]], axis=-1), w_ref[0])
    for j in range(n_chunks):
        rows = slice(j * TOK_TILE, (j + 1) * TOK_TILE)
        mod = _chunk_mod(modb_ref, modc_ref, j, n_chunks)
        g1 = mod[:, 2 * D_MODEL:3 * D_MODEL]
        sh2 = mod[:, 3 * D_MODEL:4 * D_MODEL]
        sc2 = mod[:, 4 * D_MODEL:5 * D_MODEL]
        x = x_ref[0, rows, :] + g1 * mixed[rows]
        xo_ref[0, rows, :] = x
        h2 = (_rms(x) * g2_ref[...]) * (1.0 + sc2) + sh2
        h_hi, h_lo = _split_bf16(h2)
        h2_ref[0, rows, :] = h_hi
        prod = _dot(jnp.concatenate([h_hi, h_lo], axis=0), w_router)
        logits = (prod[0:TOK_TILE, 0:LANES] + prod[0:TOK_TILE, LANES:2 * LANES]
                  + prod[TOK_TILE:2 * TOK_TILE, 0:LANES])
        aff_ref[0, rows, :] = _softmax_rows(jnp.where(lane < N_EXPERTS, logits, NEG_INF))


def _outproj(xall, oa, ob, od, w, mod_l, g2, wr, with_ctx, layer):
    n_chunks = STEP_CHUNKS if with_ctx else LAT_STEP_CHUNKS
    n_steps = (N_TILES if with_ctx else N_LAT_TILES) // n_chunks
    full = lambda shape: pl.BlockSpec(shape, lambda b, t: (0,) * len(shape))
    tok = lambda n: pl.BlockSpec((1, n_chunks * TOK_TILE, n), lambda b, t: (b, t, 0))
    return pl.pallas_call(
        functools.partial(_outproj_kernel, n_chunks=n_chunks),
        out_shape=[jax.ShapeDtypeStruct((BATCH, NT_PAD, D_MODEL), F32),
                   jax.ShapeDtypeStruct((BATCH, NT_PAD, D_MODEL), BF16),
                   jax.ShapeDtypeStruct((BATCH, NT_PAD, LANES), F32)],
        grid=(BATCH, n_steps),
        in_specs=[tok(D_MODEL), tok(512), tok(256), tok(256),
                  pl.BlockSpec((1, D_MODEL, D_MODEL), lambda b, t: (layer, 0, 0))] + _mod_specs()
                 + [full((1, D_MODEL)), full((D_MODEL, LANES))],
        out_specs=[tok(D_MODEL), tok(D_MODEL), tok(LANES)],
        compiler_params=_params(("arbitrary", "arbitrary")),
        name="out_projection",
    )(xall, oa, ob, od, w, mod_l, mod_l, g2, wr)


def _prefix_count(flags_f32, tri_ref):
    rows, n = flags_f32.shape
    run = jnp.zeros((rows, 1), F32)
    outs = []
    for c in range(n // LANES):
        blk = flags_f32[:, c * LANES:(c + 1) * LANES]
        outs.append(_dot(blk.astype(BF16), tri_ref[...]) + run)
        run = run + jnp.sum(blk, axis=-1, keepdims=True)
    return jnp.concatenate(outs, axis=-1)


def _select_slots(a, cap, tri_ref):
    bits = pltpu.bitcast(a, jnp.int32)

    def step(i, t):
        cand = t | jnp.left_shift(jnp.int32(1), 30 - i)
        cnt = jnp.sum((bits >= cand).astype(jnp.int32), axis=-1, keepdims=True)
        return jnp.where(cnt >= cap, cand, t)

    thr = lax.fori_loop(0, 31, step, jnp.zeros((a.shape[0], 1), jnp.int32))
    above = bits > thr
    tied = bits == thr
    need = cap - jnp.sum(above.astype(F32), axis=-1, keepdims=True)
    tied_f = tied.astype(F32)
    sel = above | (tied & (_prefix_count(tied_f, tri_ref) < need))
    sel_f = sel.astype(F32)
    return jnp.where(sel, _prefix_count(sel_f, tri_ref), -1.0)


def _route_kernel(aff_ref, tri_ref, row_ref, arow_ref, col_ref, *, with_ctx):
    lat, ctx = [], []
    for b in range(BATCH):
        lat.append(aff_ref[b, 0:SEQ, :].T[0:N_EXPERTS])
        if with_ctx:
            ctx.append(aff_ref[b, SEQ:NT, :].T[0:N_EXPERTS])
    aff_lat = jnp.concatenate(lat, axis=0)
    arow_ref[:, 0:SEQ] = aff_lat
    slot_lat = _select_slots(aff_lat, CAP_LAT, tri_ref)
    if with_ctx:
        aff_ctx = jnp.concatenate(ctx, axis=0)
        arow_ref[:, SEQ:NT] = aff_ctx
        slot_ctx = _select_slots(aff_ctx, CAP_CTX, tri_ref)
        slots = jnp.concatenate([slot_lat, slot_ctx], axis=-1)
    else:
        slots = slot_lat
    n = slots.shape[1]
    row_ref[:, 0:n] = slots
    pad = jnp.full((LANES - N_EXPERTS, n), -1.0, F32)
    for b in range(BATCH):
        blk = jnp.concatenate([slots[b * N_EXPERTS:(b + 1) * N_EXPERTS], pad], axis=0)
        col_ref[b, 0:n, :] = blk.T
        if with_ctx and NT_PAD > NT:
            col_ref[b, NT:NT_PAD, :] = jnp.full((NT_PAD - NT, LANES), -1.0, F32)


def _route(aff, tri, with_ctx):
    vm = pl.BlockSpec(memory_space=pltpu.VMEM)
    return pl.pallas_call(
        functools.partial(_route_kernel, with_ctx=with_ctx),
        out_shape=[jax.ShapeDtypeStruct((BATCH * N_EXPERTS, NT), F32),
                   jax.ShapeDtypeStruct((BATCH * N_EXPERTS, NT), F32),
                   jax.ShapeDtypeStruct((BATCH, NT_PAD, LANES), F32)],
        in_specs=[vm, vm],
        out_specs=[vm, vm, vm],
        compiler_params=pltpu.CompilerParams(vmem_limit_bytes=VMEM_LIMIT),
        name="route_topc",
    )(aff, tri)


GATHER_GROUP = 4


def _gather_kernel(row_ref, arow_ref, h_ref, *out_refs, with_ctx):
    slot = lax.broadcasted_iota(jnp.int32, (CAP_LAT, 1), 0).astype(F32)

    def pick(e0, cols, n_slots, h_rows, x_ref, g_ref):
        hots = []
        for e in range(e0, e0 + GATHER_GROUP):
            hot = jnp.where(row_ref[e:e + 1, cols] == slot[0:n_slots], 1.0, 0.0)
            gate = jnp.sum(hot * arow_ref[e:e + 1, cols], axis=-1, keepdims=True)
            g_ref[e] = jnp.broadcast_to(gate, (n_slots, LANES))
            hots.append(hot.astype(BF16))
        x = _dot(jnp.concatenate(hots, axis=0), h_rows).astype(BF16)
        for i in range(GATHER_GROUP):
            x_ref[e0 + i] = x[i * n_slots:(i + 1) * n_slots]

    for e0 in range(0, N_EXPERTS, GATHER_GROUP):
        pick(e0, slice(0, SEQ), CAP_LAT, h_ref[0, 0:SEQ, :], out_refs[0], out_refs[1])
        if with_ctx:
            pick(e0, slice(SEQ, NT), CAP_CTX, h_ref[0, SEQ:NT, :], out_refs[2], out_refs[3])


def _gather(slot_rows, aff_rows, h2, with_ctx):
    out_shape = [jax.ShapeDtypeStruct((N_EXPERTS, SEQ, D_MODEL), BF16),
                 jax.ShapeDtypeStruct((N_EXPERTS, SEQ, LANES), F32)]
    out_specs = [pl.BlockSpec((N_EXPERTS, CAP_LAT, D_MODEL), lambda b: (0, b, 0)),
                 pl.BlockSpec((N_EXPERTS, CAP_LAT, LANES), lambda b: (0, b, 0))]
    if with_ctx:
        out_shape += [jax.ShapeDtypeStruct((N_EXPERTS, CTX_LEN, D_MODEL), BF16),
                      jax.ShapeDtypeStruct((N_EXPERTS, CTX_LEN, LANES), F32)]
        out_specs += [pl.BlockSpec((N_EXPERTS, CAP_CTX, D_MODEL), lambda b: (0, b, 0)),
                      pl.BlockSpec((N_EXPERTS, CAP_CTX, LANES), lambda b: (0, b, 0))]
    row_spec = pl.BlockSpec((N_EXPERTS, NT), lambda b: (b, 0))
    return pl.pallas_call(
        functools.partial(_gather_kernel, with_ctx=with_ctx),
        out_shape=out_shape,
        grid=(BATCH,),
        in_specs=[row_spec, row_spec, pl.BlockSpec((1, NT_PAD, D_MODEL), lambda b: (b, 0, 0))],
        out_specs=out_specs,
        compiler_params=_params(("arbitrary",)),
        name="expert_gather",
    )(slot_rows, aff_rows, h2)


def _ffn_kernel(*refs, with_ctx):
    if with_ctx:
        xl_ref, gl_ref, xc_ref, gc_ref, wg_ref, wu_ref, wd_ref, y_ref = refs
    else:
        xl_ref, gl_ref, wg_ref, wu_ref, wd_ref, y_ref = refs
    wg = wg_ref[0, 0].astype(BF16)
    wu = wu_ref[0, 0].astype(BF16)
    wd = wd_ref[0, 0].astype(BF16)

    def swiglu(x, slot_gate):
        gate = _dot(x, wg)
        hid = (gate / (1.0 + jnp.exp(-gate))) * _dot(x, wu)
        scale = jnp.concatenate([slot_gate] * (D_MODEL // LANES), axis=-1)
        return (_dot(hid.astype(BF16), wd) * scale).astype(BF16)

    for r in range(FFN_LAT_TILES):
        rows = slice(r * FFN_TILE, (r + 1) * FFN_TILE)
        y_ref[0, rows, :] = swiglu(xl_ref[0, rows, :], gl_ref[0, rows, :])
    if with_ctx:
        y_ref[0, SEQ:NT, :] = swiglu(xc_ref[0], gc_ref[0])


def _ffn(gathered, w_gate, w_up, w_down, layer):
    with_ctx = len(gathered) == 4
    ins = list(gathered) + [w_gate, w_up, w_down]
    in_specs = [pl.BlockSpec((1, SEQ, D_MODEL), lambda e: (e, 0, 0)),
                pl.BlockSpec((1, SEQ, LANES), lambda e: (e, 0, 0))]
    if with_ctx:
        in_specs += [pl.BlockSpec((1, CTX_LEN, D_MODEL), lambda e: (e, 0, 0)),
                     pl.BlockSpec((1, CTX_LEN, LANES), lambda e: (e, 0, 0))]
    in_specs += [pl.BlockSpec((1, 1, D_MODEL, D_EXPERT), lambda e: (layer, e, 0, 0)),
                 pl.BlockSpec((1, 1, D_MODEL, D_EXPERT), lambda e: (layer, e, 0, 0)),
                 pl.BlockSpec((1, 1, D_EXPERT, D_MODEL), lambda e: (layer, e, 0, 0))]
    return pl.pallas_call(
        functools.partial(_ffn_kernel, with_ctx=with_ctx),
        out_shape=jax.ShapeDtypeStruct((N_EXPERTS, NT, D_MODEL), BF16),
        grid=(N_EXPERTS,),
        in_specs=in_specs,
        out_specs=pl.BlockSpec((1, NT, D_MODEL), lambda e: (e, 0, 0)),
        compiler_params=_params(("arbitrary",)),
        name="expert_ffn",
    )(*ins)


COMBINE_ROWS = 1024


def _scatter_add(col, ys, n_slots):
    lane = lax.broadcasted_iota(jnp.int32, (1, n_slots), 1).astype(F32)
    hot = [_indicator(col[:, e:e + 1] == lane) for e in range(N_EXPERTS)]
    return _dot(jnp.concatenate(hot, axis=-1), jnp.concatenate(ys, axis=0))


def _combine_latent_kernel(*refs, final):
    if final:
        x_ref, col_ref, y_ref, mod_ref, fn_ref, o_ref = refs
    else:
        x_ref, col_ref, y_ref, mod_ref, o_ref = refs
    g2 = mod_ref[0][:, 5 * D_MODEL:6 * D_MODEL]
    acc = _scatter_add(col_ref[0], [y_ref[e] for e in range(N_EXPERTS)], CAP_LAT)
    x = x_ref[0] + g2 * acc
    if final:
        x = _rms(x) * fn_ref[...]
    o_ref[0] = x


def _combine_ctx_kernel(x_ref, col_ref, y_ref, mod_ref, prev_ref, o_ref):
    del prev_ref
    g2 = mod_ref[0][:, 5 * D_MODEL:6 * D_MODEL]
    pad = jnp.zeros((LANES - CAP_CTX, D_MODEL), BF16)
    for b in range(BATCH):
        ys = [jnp.concatenate([y_ref[e, b * CAP_CTX:(b + 1) * CAP_CTX, :], pad], axis=0)
              for e in range(N_EXPERTS)]
        o_ref[b] = x_ref[b] + g2 * _scatter_add(col_ref[b], ys, LANES)


def _combine(xall, slot_cols, y, mod_l, with_ctx, final_norm=None):
    final = final_norm is not None
    n_rows = SEQ if final else NT_PAD
    tok = lambda n: pl.BlockSpec((1, COMBINE_ROWS, n), lambda b, t: (b, t, 0))
    ins = [xall, slot_cols, y, mod_l]
    in_specs = [tok(D_MODEL), tok(LANES),
                pl.BlockSpec((N_EXPERTS, CAP_LAT, D_MODEL), lambda b, t: (0, b, 0)),
                pl.BlockSpec((1, 1, 6 * D_MODEL), lambda b, t: (b, 0, 0))]
    if final:
        ins.append(final_norm)
        in_specs.append(pl.BlockSpec((1, D_MODEL), lambda b, t: (0, 0)))
    out = pl.pallas_call(
        functools.partial(_combine_latent_kernel, final=final),
        out_shape=jax.ShapeDtypeStruct((BATCH, n_rows, D_MODEL), F32),
        grid=(BATCH, SEQ // COMBINE_ROWS),
        in_specs=in_specs,
        out_specs=tok(D_MODEL),
        compiler_params=_params(("arbitrary", "arbitrary")),
        name="expert_combine",
    )(*ins)
    if not with_ctx:
        return out
    ctx_blk = SEQ // CTX_LEN
    ctx_rows = lambda n: pl.BlockSpec((BATCH, CTX_LEN, n), lambda i: (0, ctx_blk, 0))
    return pl.pallas_call(
        _combine_ctx_kernel,
        out_shape=jax.ShapeDtypeStruct((BATCH, NT_PAD, D_MODEL), F32),
        grid=(1,),
        in_specs=[ctx_rows(D_MODEL), ctx_rows(LANES),
                  pl.BlockSpec((N_EXPERTS, CTX_LEN, D_MODEL), lambda i: (0, ctx_blk, 0)),
                  pl.BlockSpec((1, 1, 6 * D_MODEL), lambda i: (BATCH, 0, 0)),
                  pl.BlockSpec(memory_space=pl.ANY)],
        out_specs=ctx_rows(D_MODEL),
        input_output_aliases={4: 0},
        compiler_params=_params(("arbitrary",)),
        name="expert_combine_ctx",
    )(xall, slot_cols, y, mod_l, out)


def _rope_tables():
    t = jnp.arange(SEQ)
    pos = jnp.stack([t // GRID_W, t % GRID_W], axis=0)

    def table(n_f, lanes_per_unit):
        inv = ROPE_BASE ** (-jnp.arange(n_f, dtype=F32) / n_f)
        d = np.arange(lanes_per_unit)
        axis, second, f = d // (2 * n_f), (d // n_f) % 2, d % n_f
        ang = pos[axis].T.astype(F32) * inv[f][None, :]
        cos = jnp.cos(ang)
        sin = jnp.sin(ang) * jnp.where(second == 1, 1.0, -1.0)[None, :]
        return cos, sin

    def finish(cos, sin, scale):
        cos = jnp.concatenate([cos, jnp.ones((NT_PAD - SEQ, LANES), F32)], axis=0)
        sin = jnp.concatenate([sin, jnp.zeros((NT_PAD - SEQ, LANES), F32)], axis=0)
        scale = scale * LOG2_E
        return [cos, sin, cos * scale, sin * scale]

    cos_a, sin_a = table(HEAD_DIM // 4, HEAD_DIM)
    tabs_a = finish(jnp.tile(cos_a, (1, 2)), jnp.tile(sin_a, (1, 2)), HEAD_DIM ** -0.5)
    cos_8, sin_8 = table(B_ROPE // 4, B_ROPE)
    tabs_8 = finish(jnp.tile(cos_8, (1, 4)), jnp.tile(sin_8, (1, 4)), C_DH ** -0.5)
    ones = jnp.ones((SEQ, B_NOPE), F32)
    pad1 = jnp.ones((SEQ, LANES - B_NOPE - B_ROPE), F32)
    cos_b = jnp.concatenate([ones, cos_8, pad1], axis=1)
    sin_b = jnp.concatenate([0 * ones, sin_8, 0 * pad1], axis=1)
    tabs_b = finish(cos_b, sin_b, (B_NOPE + B_ROPE) ** -0.5)
    return tabs_a + tabs_8 + tabs_b


def _window_bias():
    qi = np.arange(WIN)[:, None]
    kj = np.arange(A_LOCAL)[None, :]
    out = np.zeros((3, WIN, A_LOCAL + CTX_LEN), np.float32)
    for v in range(3):
        out[v, :, :A_LOCAL] = np.where(np.abs(kj - v * WIN - qi) <= WIN, 0.0, NEG_INF)
    return jnp.asarray(out)


def _prep_weights(w_in, w_uq, w_ukv, w_out, w_router):
    offs = np.cumsum((0, 512, 128, 128, 256, 128, 32, 256, 256, 256))
    seg = lambda i: w_in[:, :, offs[i]:offs[i + 1]]
    wq = seg(0).reshape(DEPTH, D_MODEL, A_KV_HEADS, A_GROUP, HEAD_DIM)
    wq = jnp.swapaxes(wq, 2, 3).reshape(DEPTH, D_MODEL, A_HEADS * HEAD_DIM)
    kr = jnp.pad(seg(5), ((0, 0), (0, 0), (B_NOPE, LANES - B_NOPE - B_ROPE)))
    w = jnp.concatenate([wq, seg(1), seg(2), seg(3), seg(4), kr,
                         seg(6), seg(7), seg(8)], axis=-1).astype(BF16)
    wuq = w_uq.reshape(DEPTH, B_Q_RANK, B_HEADS, B_NOPE + B_ROPE)
    wuq = jnp.pad(wuq, ((0, 0), (0, 0), (0, 0), (0, LANES - B_NOPE - B_ROPE)))
    wuq = wuq.reshape(DEPTH, B_Q_RANK, B_HEADS * LANES).astype(BF16)
    wukv = w_ukv.reshape(DEPTH, B_KV_RANK, B_HEADS, B_NOPE + B_V)
    wk = jnp.pad(wukv[..., :B_NOPE], ((0, 0), (0, 0), (0, 0), (0, LANES - B_NOPE)))
    wukv = jnp.concatenate([wk.reshape(DEPTH, B_KV_RANK, B_HEADS * LANES),
                            wukv[..., B_NOPE:].reshape(DEPTH, B_KV_RANK, B_HEADS * B_V)],
                           axis=-1).astype(BF16)
    wo_a = w_out[:, :A_HEADS * HEAD_DIM].reshape(DEPTH, A_KV_HEADS, A_GROUP, HEAD_DIM, D_MODEL)
    wo_a = jnp.swapaxes(wo_a, 1, 2).reshape(DEPTH, A_HEADS * HEAD_DIM, D_MODEL)
    wo = jnp.concatenate([wo_a, w_out[:, A_HEADS * HEAD_DIM:]], axis=1).astype(BF16)
    wr = jnp.pad(w_router, ((0, 0), (0, 0), (0, LANES - N_EXPERTS)))
    return w, wuq, wukv, wo, wr


def kernel(x, c, ctx, c_ctx, norm1, norm2, w_ada, b_ada, w_in, sink, mla_q_norm, w_uq, mla_kv_norm, w_ukv,
           lam_q1, lam_k1, lam_q2, lam_k2, diff_norm, w_out, w_router, w_gate, w_up, w_down, final_norm):
    xall = jnp.concatenate([x, ctx, jnp.zeros((BATCH, NT_PAD - NT, D_MODEL), F32)], axis=1)
    cc = jnp.concatenate([c, c_ctx[None, :], jnp.zeros((16 - BATCH - 1, D_MODEL), F32)], axis=0)
    mod = _modulation(cc, w_ada, b_ada).reshape(DEPTH, 16, 1, 6 * D_MODEL)
    tabs = _rope_tables()
    bias = _window_bias()
    tri = jnp.asarray(np.triu(np.ones((LANES, LANES), np.float32), k=1), BF16)
    w, wuq, wukv, wo, wr = _prep_weights(w_in, w_uq, w_ukv, w_out, w_router)
    sub_norm = jnp.tile(diff_norm, (1, C_HEADS)).reshape(DEPTH, 1, C_HEADS * C_V)
    row = lambda a, l: a[l].reshape(1, -1)

    for l in range(DEPTH):
        need_ctx = l < DEPTH - 1
        qa, ka, va, qb, kb, vb, qd, kd, vd = _inproj(
            xall, mod[l], row(norm1, l), w, row(mla_q_norm, l), wuq[l], row(mla_kv_norm, l), wukv[l], tabs, l)

        oa = _attn_a_latent(sink[l], qa, ka, va, bias)
        ob = _full_attention(_attn_b_kernel, "attn_mla", MLA_TQ, qb, kb, vb, [], lambda n: [],
                             256, 512, 512, 512)
        lam_init = 0.8 - 0.6 * math.exp(-0.3 * l)
        c_kernel = functools.partial(_attn_c_kernel, lam_init=lam_init)
        c_extra = [row(lam_q1, l), row(lam_k1, l), row(lam_q2, l), row(lam_k2, l), sub_norm[l]]

        def c_specs(n_grid, c_extra=c_extra):
            zmap = (lambda b, t: (0, 0)) if n_grid == 2 else (lambda b: (0, 0))
            return [pl.BlockSpec(a.shape, zmap) for a in c_extra]

        od = _full_attention(c_kernel, "attn_diff", DIFF_TQ, qd, kd, vd, c_extra, c_specs, 256, 256, 256, 512)
        if need_ctx:
            oa = _attn_a_ctx(sink[l], qa, ka, va, oa)
            ob = _full_attention_ctx(_attn_b_kernel, "attn_mla", qb, kb, vb, [], lambda n: [],
                                     256, 512, 512, 512, ob)
            od = _full_attention_ctx(c_kernel, "attn_diff", qd, kd, vd, c_extra, c_specs,
                                     256, 256, 256, 512, od)

        xall, h2, aff = _outproj(xall, oa, ob, od, wo, mod[l], row(norm2, l), wr[l], need_ctx, l)
        slot_rows, aff_rows, slot_cols = _route(aff, tri, need_ctx)
        gathered = _gather(slot_rows, aff_rows, h2, need_ctx)
        y = _ffn(gathered, w_gate, w_up, w_down, l)
        xall = _combine(xall, slot_cols, y, mod[l], need_ctx,
                        final_norm=None if need_ctx else final_norm.reshape(1, D_MODEL))
    return xall
```

```python
import functools
import math

import numpy as np
import jax
import jax.numpy as jnp
from jax import lax
from jax.experimental import pallas as pl
from jax.experimental.pallas import tpu as pltpu

D_MODEL = 1024
BATCH = 8
SEQ = 2048
DEPTH = 4
CTX_LEN = 256
NT = SEQ + CTX_LEN
GRID_W = 64
HEAD_DIM = 64
ROPE_BASE = 10000.0
EPS = 1e-6
NEG_INF = -1e30
LOG2_E = math.log2(math.e)

WIN = 128
A_HEADS = 8
A_KV_HEADS = 2
A_GROUP = A_HEADS // A_KV_HEADS
B_HEADS = 4
B_NOPE = 64
B_ROPE = 32
B_V = 64
B_Q_RANK = 256
B_KV_RANK = 128
C_HEADS = 4
C_DH = 32
C_V = 2 * C_DH
N_EXPERTS = 16
EC_FACTOR = 2
D_EXPERT = 512
CAP_LAT = EC_FACTOR * SEQ // N_EXPERTS
CAP_CTX = EC_FACTOR * CTX_LEN // N_EXPERTS

LANES = 128
TOK_TILE = 256
NT_PAD = SEQ + TOK_TILE
N_TILES = NT_PAD // TOK_TILE
N_LAT_TILES = SEQ // TOK_TILE
STEP_CHUNKS = 3
LAT_STEP_CHUNKS = 2
FFN_TILE = 1024
FFN_LAT_TILES = SEQ // FFN_TILE
VMEM_LIMIT = 56 * 1024 * 1024

F32 = jnp.float32
BF16 = jnp.bfloat16

_NT_DIMS = (((1,), (1,)), ((), ()))


def _params(sem):
    return pltpu.CompilerParams(dimension_semantics=sem, vmem_limit_bytes=VMEM_LIMIT)


def _dot(a, b):
    return jnp.dot(a, b, preferred_element_type=F32)


def _dot_nt(a, b):
    return lax.dot_general(a, b, _NT_DIMS, preferred_element_type=F32)


def _split_bf16(v):
    hi = v.astype(BF16)
    lo = (v - hi.astype(F32)).astype(BF16)
    return hi, lo


def _dot3(a, b):
    a_hi, a_lo = _split_bf16(a)
    b_hi, b_lo = _split_bf16(b)
    return _dot(a_hi, b_hi) + _dot(a_lo, b_hi) + _dot(a_hi, b_lo)


def _indicator(cond):
    return jnp.where(cond, 1.0, 0.0).astype(BF16)


def _rms(v):
    return v * lax.rsqrt(jnp.mean(v * v, axis=-1, keepdims=True) + EPS)


def _softmax_pv(s, v_blk, head):
    m = jnp.max(s, axis=-1, keepdims=True)
    ol = _dot(jnp.exp2(s - m).astype(BF16), v_blk)
    ones_lane = 64 if head % 2 == 0 else 0
    return ol / ol[:, ones_lane:ones_lane + 1]


def _merge_head_pairs(per_head):
    lane = lax.broadcasted_iota(jnp.int32, (1, LANES), 1)
    blocks = [jnp.where(lane < 64, per_head[j], per_head[j + 1]) for j in range(0, len(per_head), 2)]
    return jnp.concatenate(blocks, axis=-1)


def _softmax_rows(s):
    m = jnp.max(s, axis=-1, keepdims=True)
    e = jnp.exp(s - m)
    return e / jnp.sum(e, axis=-1, keepdims=True)


def _mod_kernel(c_ref, w_ref, b_ref, o_ref):
    c = c_ref[...]
    s = c / (1.0 + jnp.exp(-c))
    o_ref[0] = _dot3(s, w_ref[0]) + b_ref[0]


def _modulation(cc, w_ada, b_ada):
    tn = 1536
    return pl.pallas_call(
        _mod_kernel,
        out_shape=jax.ShapeDtypeStruct((DEPTH, 16, 6 * D_MODEL), F32),
        grid=(DEPTH, 6 * D_MODEL // tn),
        in_specs=[
            pl.BlockSpec((16, D_MODEL), lambda l, j: (0, 0)),
            pl.BlockSpec((1, D_MODEL, tn), lambda l, j: (l, 0, j)),
            pl.BlockSpec((1, 1, tn), lambda l, j: (l, 0, j)),
        ],
        out_specs=pl.BlockSpec((1, 16, tn), lambda l, j: (l, 0, j)),
        compiler_params=_params(("arbitrary", "arbitrary")),
        name="adaln_mod",
    )(cc, w_ada, b_ada.reshape(DEPTH, 1, 6 * D_MODEL))


def _rope(p, cos, sin, half):
    lane = lax.broadcasted_iota(jnp.int32, (1, LANES), 1)
    first = (lane // half) % 2 == 0
    outs = []
    for j in range(p.shape[1] // LANES):
        blk = p[:, j * LANES:(j + 1) * LANES]
        partner = jnp.where(first, pltpu.roll(blk, LANES - half, 1), pltpu.roll(blk, half, 1))
        outs.append(blk * cos + partner * sin)
    return outs[0] if len(outs) == 1 else jnp.concatenate(outs, axis=-1)


def _per_head_with_ones(v):
    lane = lax.broadcasted_iota(jnp.int32, (1, LANES), 1)
    outs = []
    for j in range(v.shape[1] // LANES):
        blk = v[:, j * LANES:(j + 1) * LANES]
        outs.append(jnp.where(lane < 64, blk, jnp.where(lane == 64, 1.0, 0.0)))
        outs.append(jnp.where(lane >= 64, blk, jnp.where(lane == 0, 1.0, 0.0)))
    return jnp.concatenate(outs, axis=-1)


def _chunk_mod(modb_ref, modc_ref, j, n_chunks):
    mod = modb_ref[0]
    if n_chunks == STEP_CHUNKS and j == n_chunks - 1:
        mod = jnp.where(pl.program_id(1) == N_TILES // STEP_CHUNKS - 1, modc_ref[0], mod)
    return mod


def _inproj_kernel(x_ref, modb_ref, modc_ref, g1_ref, w_ref, qn_ref, wuq_ref, kvn_ref, wukv_ref,
                   ca_ref, sa_ref, caq_ref, saq_ref, c8_ref, s8_ref, c8q_ref, s8q_ref,
                   cb_ref, sb_ref, cbq_ref, sbq_ref,
                   qa_ref, ka_ref, va_ref, qb_ref, kb_ref, vb_ref, qd_ref, kd_ref, vd_ref):
    for j in range(STEP_CHUNKS):
        rows = slice(j * TOK_TILE, (j + 1) * TOK_TILE)
        x = x_ref[0, rows, :]
        mod = _chunk_mod(modb_ref, modc_ref, j, STEP_CHUNKS)
        sh1 = mod[:, 0:D_MODEL]
        sc1 = mod[:, D_MODEL:2 * D_MODEL]
        h = (_rms(x) * g1_ref[...]) * (1.0 + sc1) + sh1
        p = _dot(h.astype(BF16), w_ref[0])

        qa_ref[0, rows, :] = _rope(p[:, 0:512], caq_ref[rows, :], saq_ref[rows, :], 16).astype(BF16)
        ka_ref[0, rows, :] = _rope(p[:, 512:640], ca_ref[rows, :], sa_ref[rows, :], 16).astype(BF16)
        lane = lax.broadcasted_iota(jnp.int32, (TOK_TILE, LANES), 1)
        va_ref[0, rows, :] = jnp.concatenate(
            [p[:, 640:768], jnp.where(lane == 0, 1.0, 0.0)], axis=-1).astype(BF16)

        cq = (_rms(p[:, 768:1024]) * qn_ref[...]).astype(BF16)
        qb = _dot(cq, wuq_ref[...])
        qb_ref[0, rows, :] = _rope(qb, cbq_ref[rows, :], sbq_ref[rows, :], 8).astype(BF16)
        ckv = (_rms(p[:, 1024:1152]) * kvn_ref[...]).astype(BF16)
        kv = _dot(ckv, wukv_ref[...])
        kr = _rope(p[:, 1152:1280], cb_ref[rows, :], sb_ref[rows, :], 8)
        kb_ref[0, rows, :] = (kv[:, 0:512] + jnp.concatenate([kr] * B_HEADS, axis=-1)).astype(BF16)
        vb_ref[0, rows, :] = _per_head_with_ones(kv[:, 512:768]).astype(BF16)

        qd_ref[0, rows, :] = _rope(p[:, 1280:1536], c8q_ref[rows, :], s8q_ref[rows, :], 8).astype(BF16)
        kd_ref[0, rows, :] = _rope(p[:, 1536:1792], c8_ref[rows, :], s8_ref[rows, :], 8).astype(BF16)
        vd_ref[0, rows, :] = _per_head_with_ones(p[:, 1792:2048]).astype(BF16)


def _mod_specs():
    return [pl.BlockSpec((1, 1, 6 * D_MODEL), lambda b, t: (b, 0, 0)),
            pl.BlockSpec((1, 1, 6 * D_MODEL), lambda b, t: (BATCH, 0, 0))]


def _inproj(xall, mod_l, g1, w, qn, wuq, kvn, wukv, tabs, layer):
    full = lambda shape: pl.BlockSpec(shape, lambda b, t: (0,) * len(shape))
    step_rows = STEP_CHUNKS * TOK_TILE
    tab_spec = pl.BlockSpec((step_rows, LANES), lambda b, t: (t, 0))
    widths = (512, 128, 256, 512, 512, 512, 256, 256, 512)
    return pl.pallas_call(
        _inproj_kernel,
        out_shape=[jax.ShapeDtypeStruct((BATCH, NT_PAD, n), BF16) for n in widths],
        grid=(BATCH, N_TILES // STEP_CHUNKS),
        in_specs=[pl.BlockSpec((1, step_rows, D_MODEL), lambda b, t: (b, t, 0))] + _mod_specs() + [
            full((1, D_MODEL)),
            pl.BlockSpec((1, D_MODEL, 2048), lambda b, t: (layer, 0, 0)),
            full((1, B_Q_RANK)),
            full((B_Q_RANK, 512)),
            full((1, B_KV_RANK)),
            full((B_KV_RANK, 768)),
        ] + [tab_spec] * 12,
        out_specs=[pl.BlockSpec((1, step_rows, n), lambda b, t: (b, t, 0)) for n in widths],
        compiler_params=_params(("arbitrary", "arbitrary")),
        name="in_projection",
    )(xall, mod_l, mod_l, g1, w, qn, wuq, kvn, wukv, *tabs)


def _window_heads(q, kcat, vcat, bias, sink_ref):
    tq = q.shape[0]
    lane = lax.broadcasted_iota(jnp.int32, (1, LANES), 1)
    acc = [jnp.zeros((tq, LANES), F32) for _ in range(A_GROUP)]
    for g in range(A_KV_HEADS):
        in_half = (lane < HEAD_DIM) if g == 0 else (lane >= HEAD_DIM)
        keep = _indicator(in_half)
        qs = jnp.concatenate([q[:, j * LANES:(j + 1) * LANES] * keep for j in range(A_GROUP)], axis=0)
        s = _dot_nt(qs, kcat)
        if bias is not None:
            s = s + jnp.concatenate([bias] * A_GROUP, axis=0)
        sink = jnp.concatenate(
            [jnp.full((tq, 1), sink_ref[g * A_GROUP + j] * LOG2_E, F32) for j in range(A_GROUP)], axis=0)
        m = jnp.maximum(jnp.max(s, axis=-1, keepdims=True), sink)
        ol = _dot(jnp.exp2(s - m).astype(BF16), vcat)
        o = ol[:, 0:LANES] / (ol[:, LANES:LANES + 1] + jnp.exp2(sink - m))
        for j in range(A_GROUP):
            acc[j] = acc[j] + jnp.where(in_half, o[j * tq:(j + 1) * tq], 0.0)
    return jnp.concatenate(acc, axis=-1).astype(BF16)


A_STEP_BLOCKS = 8
A_LOCAL = 3 * WIN


def _attn_a_latent_kernel(sink_ref, q_ref, k_ref, v_ref, bias_ref, o_ref):
    step = pl.program_id(1)
    last = SEQ // WIN - 1
    k_ctx = k_ref[0, SEQ:NT, :]
    v_ctx = v_ref[0, SEQ:NT, :]
    for j in range(A_STEP_BLOCKS):
        n = step * A_STEP_BLOCKS + j
        start = pl.multiple_of(jnp.clip((n - 1) * WIN, 0, SEQ - A_LOCAL), WIN)
        variant = jnp.where(n == 0, 0, jnp.where(n == last, 2, 1))
        kcat = jnp.concatenate([k_ref[0, pl.ds(start, A_LOCAL), :], k_ctx], axis=0)
        vcat = jnp.concatenate([v_ref[0, pl.ds(start, A_LOCAL), :], v_ctx], axis=0)
        rows = slice(j * WIN, (j + 1) * WIN)
        o_ref[0, rows, :] = _window_heads(q_ref[0, rows, :], kcat, vcat, bias_ref[variant], sink_ref)


def _attn_a_latent(sink, qa, ka, va, bias):
    tq = A_STEP_BLOCKS * WIN
    return pl.pallas_call(
        _attn_a_latent_kernel,
        out_shape=jax.ShapeDtypeStruct((BATCH, NT_PAD, 512), BF16),
        grid=(BATCH, SEQ // tq),
        in_specs=[pl.BlockSpec(memory_space=pltpu.SMEM),
                  pl.BlockSpec((1, tq, 512), lambda b, n: (b, n, 0)),
                  pl.BlockSpec((1, NT_PAD, LANES), lambda b, n: (b, 0, 0)),
                  pl.BlockSpec((1, NT_PAD, 2 * LANES), lambda b, n: (b, 0, 0)),
                  pl.BlockSpec((3, WIN, A_LOCAL + CTX_LEN), lambda b, n: (0, 0, 0))],
        out_specs=pl.BlockSpec((1, tq, 512), lambda b, n: (b, n, 0)),
        compiler_params=_params(("arbitrary", "arbitrary")),
        name="attn_window_latent",
    )(sink, qa, ka, va, bias)


def _attn_a_ctx(sink, qa, ka, va, prev_out):
    ctx_blk = SEQ // CTX_LEN
    spec = lambda n: pl.BlockSpec((BATCH, CTX_LEN, n), lambda i: (0, ctx_blk, 0))
    tile = lambda n: pl.BlockSpec((BATCH, TOK_TILE, n), lambda i: (0, N_LAT_TILES, 0))

    def body(sink_ref, q_ref, k_ref, v_ref, prev_ref, o_ref):
        del prev_ref
        for sample in range(BATCH):
            o_ref[sample] = _window_heads(q_ref[sample], k_ref[sample], v_ref[sample], None, sink_ref)

    return pl.pallas_call(
        body,
        out_shape=jax.ShapeDtypeStruct((BATCH, NT_PAD, 512), BF16),
        grid=(1,),
        in_specs=[pl.BlockSpec(memory_space=pltpu.SMEM), tile(512), spec(LANES), spec(2 * LANES),
                  pl.BlockSpec(memory_space=pl.ANY)],
        out_specs=tile(512),
        input_output_aliases={4: 0},
        compiler_params=_params(("arbitrary",)),
        name="attn_window_ctx",
    )(sink, qa, ka, va, prev_out)


def _key_rows(ref, sample):
    return ref[sample, 0:NT, :] if ref.shape[1] == NT_PAD else ref[sample]


MLA_TQ = 1024
MLA_SUB = 512
DIFF_TQ = 1024
DIFF_SUB = 256


def _attn_b_kernel(q_ref, k_ref, v_ref, o_ref):
    for sample in range(q_ref.shape[0]):
        k = _key_rows(k_ref, sample)
        v = _key_rows(v_ref, sample)
        sub_rows = min(MLA_SUB, q_ref.shape[1])
        for sub in range(q_ref.shape[1] // sub_rows):
            rows = slice(sub * sub_rows, (sub + 1) * sub_rows)
            q = q_ref[sample, rows, :]
            outs = []
            for h in range(B_HEADS):
                s = _dot_nt(q[:, h * LANES:(h + 1) * LANES], k[:, h * LANES:(h + 1) * LANES])
                outs.append(_softmax_pv(s, v[:, h * LANES:(h + 1) * LANES], h))
            o_ref[sample, rows, :] = _merge_head_pairs(outs).astype(BF16)


def _full_attention(kernel, name, tq, q, k, v, extra, extra_specs, out_width, q_width, k_width, v_width):
    out = pl.pallas_call(
        kernel,
        out_shape=jax.ShapeDtypeStruct((BATCH, NT_PAD, out_width), BF16),
        grid=(BATCH, SEQ // tq),
        in_specs=[pl.BlockSpec((1, tq, q_width), lambda b, t: (b, t, 0)),
                  pl.BlockSpec((1, NT_PAD, k_width), lambda b, t: (b, 0, 0)),
                  pl.BlockSpec((1, NT_PAD, v_width), lambda b, t: (b, 0, 0))] + extra_specs(2),
        out_specs=pl.BlockSpec((1, tq, out_width), lambda b, t: (b, t, 0)),
        compiler_params=_params(("arbitrary", "arbitrary")),
        name=name + "_latent",
    )(q, k, v, *extra)
    return out


def _full_attention_ctx(kernel, name, q, k, v, extra, extra_specs, out_width, q_width, k_width, v_width,
                        prev_out):
    ctx_blk = SEQ // CTX_LEN
    n_extra = len(extra)

    def body(*refs):
        ins = refs[:3 + n_extra]
        kernel(*ins, refs[-1])

    spec = lambda n: pl.BlockSpec((BATCH, CTX_LEN, n), lambda i: (0, ctx_blk, 0))
    tile = lambda n: pl.BlockSpec((BATCH, TOK_TILE, n), lambda i: (0, N_LAT_TILES, 0))
    return pl.pallas_call(
        body,
        out_shape=jax.ShapeDtypeStruct((BATCH, NT_PAD, out_width), BF16),
        grid=(1,),
        in_specs=[tile(q_width), spec(k_width), spec(v_width)] + extra_specs(1)
                 + [pl.BlockSpec(memory_space=pl.ANY)],
        out_specs=tile(out_width),
        input_output_aliases={3 + n_extra: 0},
        compiler_params=_params(("arbitrary",)),
        name=name + "_ctx",
    )(q, k, v, *extra, prev_out)


def _attn_c_kernel(q_ref, k_ref, v_ref, lq1_ref, lk1_ref, lq2_ref, lk2_ref, sn_ref, o_ref, *, lam_init):
    tq = min(DIFF_SUB, q_ref.shape[1])
    lam = (jnp.exp(jnp.sum(lq1_ref[...] * lk1_ref[...], axis=-1, keepdims=True))
           - jnp.exp(jnp.sum(lq2_ref[...] * lk2_ref[...], axis=-1, keepdims=True)) + lam_init)
    lane = lax.broadcasted_iota(jnp.int32, (1, C_HEADS * C_V), 1)
    units = [(sample, sub) for sample in range(q_ref.shape[0]) for sub in range(q_ref.shape[1] // tq)]
    keys_t = {}
    for sample, sub in units:
        if sample not in keys_t:
            keys_t[sample] = _key_rows(k_ref, sample).astype(F32).T.astype(BF16)
        k_t = keys_t[sample]
        v = _key_rows(v_ref, sample)
        rows = slice(sub * tq, (sub + 1) * tq)
        q = q_ref[sample, rows, :]
        outs = []
        for h in range(C_HEADS):
            lo = h * C_V
            q1 = q * _indicator((lane >= lo) & (lane < lo + C_DH))
            q2 = q * _indicator((lane >= lo + C_DH) & (lane < lo + C_V))
            s = _dot(jnp.concatenate([q1, q2], axis=0), k_t)
            pv = _softmax_pv(s, v[:, h * LANES:(h + 1) * LANES], h)
            outs.append(pv[0:tq] - lam * pv[tq:2 * tq])
        acc = _merge_head_pairs(outs)
        sq = acc * acc
        inv = jnp.zeros((tq, C_HEADS * C_V), F32)
        for h in range(C_HEADS):
            in_head = (lane >= h * C_V) & (lane < (h + 1) * C_V)
            ms = jnp.sum(jnp.where(in_head, sq, 0.0), axis=-1, keepdims=True) * (1.0 / C_V)
            inv = inv + jnp.where(in_head, lax.rsqrt(ms + EPS), 0.0)
        o_ref[sample, rows, :] = ((acc * inv) * sn_ref[...] * (1.0 - lam_init)).astype(BF16)


def _outproj_kernel(x_ref, oa_ref, ob_ref, od_ref, w_ref, modb_ref, modc_ref, g2_ref, wr_ref,
                    xo_ref, h2_ref, aff_ref, *, n_chunks):
    w_hi, w_lo = _split_bf16(wr_ref[...])
    w_router = jnp.concatenate([w_hi, w_lo], axis=-1)
    lane = lax.broadcasted_iota(jnp.int32, (1, LANES), 1)
    mixed = _dot(jnp.concatenate([oa_ref[0], ob_ref[0], od_ref[0]], axis=-1), w_ref[0])
    for j in range(n_chunks):
        rows = slice(j * TOK_TILE, (j + 1) * TOK_TILE)
        mod = _chunk_mod(modb_ref, modc_ref, j, n_chunks)
        g1 = mod[:, 2 * D_MODEL:3 * D_MODEL]
        sh2 = mod[:, 3 * D_MODEL:4 * D_MODEL]
        sc2 = mod[:, 4 * D_MODEL:5 * D_MODEL]
        x = x_ref[0, rows, :] + g1 * mixed[rows]
        xo_ref[0, rows, :] = x
        h2 = (_rms(x) * g2_ref[...]) * (1.0 + sc2) + sh2
        h_hi, h_lo = _split_bf16(h2)
        h2_ref[0, rows, :] = h_hi
        prod = _dot(jnp.concatenate([h_hi, h_lo], axis=0), w_router)
        logits = (prod[0:TOK_TILE, 0:LANES] + prod[0:TOK_TILE, LANES:2 * LANES]
                  + prod[TOK_TILE:2 * TOK_TILE, 0:LANES])
        aff_ref[0, rows, :] = _softmax_rows(jnp.where(lane < N_EXPERTS, logits, NEG_INF))


def _outproj(xall, oa, ob, od, w, mod_l, g2, wr, with_ctx, layer):
    n_chunks = STEP_CHUNKS if with_ctx else LAT_STEP_CHUNKS
    n_steps = (N_TILES if with_ctx else N_LAT_TILES) // n_chunks
    full = lambda shape: pl.BlockSpec(shape, lambda b, t: (0,) * len(shape))
    tok = lambda n: pl.BlockSpec((1, n_chunks * TOK_TILE, n), lambda b, t: (b, t, 0))
    return pl.pallas_call(
        functools.partial(_outproj_kernel, n_chunks=n_chunks),
        out_shape=[jax.ShapeDtypeStruct((BATCH, NT_PAD, D_MODEL), F32),
                   jax.ShapeDtypeStruct((BATCH, NT_PAD, D_MODEL), BF16),
                   jax.ShapeDtypeStruct((BATCH, NT_PAD, LANES), F32)],
        grid=(BATCH, n_steps),
        in_specs=[tok(D_MODEL), tok(512), tok(256), tok(256),
                  pl.BlockSpec((1, D_MODEL, D_MODEL), lambda b, t: (layer, 0, 0))] + _mod_specs()
                 + [full((1, D_MODEL)), full((D_MODEL, LANES))],
        out_specs=[tok(D_MODEL), tok(D_MODEL), tok(LANES)],
        compiler_params=_params(("arbitrary", "arbitrary")),
        name="out_projection",
    )(xall, oa, ob, od, w, mod_l, mod_l, g2, wr)


def _prefix_count(flags_f32, tri_ref):
    rows, n = flags_f32.shape
    run = jnp.zeros((rows, 1), F32)
    outs = []
    for c in range(n // LANES):
        blk = flags_f32[:, c * LANES:(c + 1) * LANES]
        outs.append(_dot(blk.astype(BF16), tri_ref[...]) + run)
        run = run + jnp.sum(blk, axis=-1, keepdims=True)
    return jnp.concatenate(outs, axis=-1)


def _select_slots(a, cap, tri_ref):
    bits = pltpu.bitcast(a, jnp.int32)

    def step(i, t):
        cand = t | jnp.left_shift(jnp.int32(1), 30 - i)
        cnt = jnp.sum((bits >= cand).astype(jnp.int32), axis=-1, keepdims=True)
        return jnp.where(cnt >= cap, cand, t)

    thr = lax.fori_loop(0, 31, step, jnp.zeros((a.shape[0], 1), jnp.int32))
    above = bits > thr
    tied = bits == thr
    need = cap - jnp.sum(above.astype(F32), axis=-1, keepdims=True)
    tied_f = tied.astype(F32)
    sel = above | (tied & (_prefix_count(tied_f, tri_ref) < need))
    sel_f = sel.astype(F32)
    return jnp.where(sel, _prefix_count(sel_f, tri_ref), -1.0)


def _route_kernel(aff_ref, tri_ref, row_ref, arow_ref, col_ref, *, with_ctx):
    lat, ctx = [], []
    for b in range(BATCH):
        lat.append(aff_ref[b, 0:SEQ, :].T[0:N_EXPERTS])
        if with_ctx:
            ctx.append(aff_ref[b, SEQ:NT, :].T[0:N_EXPERTS])
    aff_lat = jnp.concatenate(lat, axis=0)
    arow_ref[:, 0:SEQ] = aff_lat
    slot_lat = _select_slots(aff_lat, CAP_LAT, tri_ref)
    if with_ctx:
        aff_ctx = jnp.concatenate(ctx, axis=0)
        arow_ref[:, SEQ:NT] = aff_ctx
        slot_ctx = _select_slots(aff_ctx, CAP_CTX, tri_ref)
        slots = jnp.concatenate([slot_lat, slot_ctx], axis=-1)
    else:
        slots = slot_lat
    n = slots.shape[1]
    row_ref[:, 0:n] = slots
    pad = jnp.full((LANES - N_EXPERTS, n), -1.0, F32)
    for b in range(BATCH):
        blk = jnp.concatenate([slots[b * N_EXPERTS:(b + 1) * N_EXPERTS], pad], axis=0)
        col_ref[b, 0:n, :] = blk.T
        if with_ctx and NT_PAD > NT:
            col_ref[b, NT:NT_PAD, :] = jnp.full((NT_PAD - NT, LANES), -1.0, F32)


def _route(aff, tri, with_ctx):
    vm = pl.BlockSpec(memory_space=pltpu.VMEM)
    return pl.pallas_call(
        functools.partial(_route_kernel, with_ctx=with_ctx),
        out_shape=[jax.ShapeDtypeStruct((BATCH * N_EXPERTS, NT), F32),
                   jax.ShapeDtypeStruct((BATCH * N_EXPERTS, NT), F32),
                   jax.ShapeDtypeStruct((BATCH, NT_PAD, LANES), F32)],
        in_specs=[vm, vm],
        out_specs=[vm, vm, vm],
        compiler_params=pltpu.CompilerParams(vmem_limit_bytes=VMEM_LIMIT),
        name="route_topc",
    )(aff, tri)


GATHER_GROUP = 4


def _gather_kernel(row_ref, arow_ref, h_ref, *out_refs, with_ctx):
    slot = lax.broadcasted_iota(jnp.int32, (CAP_LAT, 1), 0).astype(F32)

    def pick(e0, cols, n_slots, h_rows, x_ref, g_ref):
        hots = []
        for e in range(e0, e0 + GATHER_GROUP):
            hot = jnp.where(row_ref[e:e + 1, cols] == slot[0:n_slots], 1.0, 0.0)
            gate = jnp.sum(hot * arow_ref[e:e + 1, cols], axis=-1, keepdims=True)
            g_ref[e] = jnp.broadcast_to(gate, (n_slots, LANES))
            hots.append(hot.astype(BF16))
        x = _dot(jnp.concatenate(hots, axis=0), h_rows).astype(BF16)
        for i in range(GATHER_GROUP):
            x_ref[e0 + i] = x[i * n_slots:(i + 1) * n_slots]

    for e0 in range(0, N_EXPERTS, GATHER_GROUP):
        pick(e0, slice(0, SEQ), CAP_LAT, h_ref[0, 0:SEQ, :], out_refs[0], out_refs[1])
        if with_ctx:
            pick(e0, slice(SEQ, NT), CAP_CTX, h_ref[0, SEQ:NT, :], out_refs[2], out_refs[3])


def _gather(slot_rows, aff_rows, h2, with_ctx):
    out_shape = [jax.ShapeDtypeStruct((N_EXPERTS, SEQ, D_MODEL), BF16),
                 jax.ShapeDtypeStruct((N_EXPERTS, SEQ, LANES), F32)]
    out_specs = [pl.BlockSpec((N_EXPERTS, CAP_LAT, D_MODEL), lambda b: (0, b, 0)),
                 pl.BlockSpec((N_EXPERTS, CAP_LAT, LANES), lambda b: (0, b, 0))]
    if with_ctx:
        out_shape += [jax.ShapeDtypeStruct((N_EXPERTS, CTX_LEN, D_MODEL), BF16),
                      jax.ShapeDtypeStruct((N_EXPERTS, CTX_LEN, LANES), F32)]
        out_specs += [pl.BlockSpec((N_EXPERTS, CAP_CTX, D_MODEL), lambda b: (0, b, 0)),
                      pl.BlockSpec((N_EXPERTS, CAP_CTX, LANES), lambda b: (0, b, 0))]
    row_spec = pl.BlockSpec((N_EXPERTS, NT), lambda b: (b, 0))
    return pl.pallas_call(
        functools.partial(_gather_kernel, with_ctx=with_ctx),
        out_shape=out_shape,
        grid=(BATCH,),
        in_specs=[row_spec, row_spec, pl.BlockSpec((1, NT_PAD, D_MODEL), lambda b: (b, 0, 0))],
        out_specs=out_specs,
        compiler_params=_params(("arbitrary",)),
        name="expert_gather",
    )(slot_rows, aff_rows, h2)


def _ffn_kernel(*refs, with_ctx):
    if with_ctx:
        xl_ref, gl_ref, xc_ref, gc_ref, wg_ref, wu_ref, wd_ref, y_ref = refs
    else:
        xl_ref, gl_ref, wg_ref, wu_ref, wd_ref, y_ref = refs
    wg = wg_ref[0, 0].astype(BF16)
    wu = wu_ref[0, 0].astype(BF16)
    wd = wd_ref[0, 0].astype(BF16)

    def swiglu(x, slot_gate):
        gate = _dot(x, wg)
        hid = (gate / (1.0 + jnp.exp(-gate))) * _dot(x, wu)
        scale = jnp.concatenate([slot_gate] * (D_MODEL // LANES), axis=-1)
        return (_dot(hid.astype(BF16), wd) * scale).astype(BF16)

    for r in range(FFN_LAT_TILES):
        rows = slice(r * FFN_TILE, (r + 1) * FFN_TILE)
        y_ref[0, rows, :] = swiglu(xl_ref[0, rows, :], gl_ref[0, rows, :])
    if with_ctx:
        y_ref[0, SEQ:NT, :] = swiglu(xc_ref[0], gc_ref[0])


def _ffn(gathered, w_gate, w_up, w_down, layer):
    with_ctx = len(gathered) == 4
    ins = list(gathered) + [w_gate, w_up, w_down]
    in_specs = [pl.BlockSpec((1, SEQ, D_MODEL), lambda e: (e, 0, 0)),
                pl.BlockSpec((1, SEQ, LANES), lambda e: (e, 0, 0))]
    if with_ctx:
        in_specs += [pl.BlockSpec((1, CTX_LEN, D_MODEL), lambda e: (e, 0, 0)),
                     pl.BlockSpec((1, CTX_LEN, LANES), lambda e: (e, 0, 0))]
    in_specs += [pl.BlockSpec((1, 1, D_MODEL, D_EXPERT), lambda e: (layer, e, 0, 0)),
                 pl.BlockSpec((1, 1, D_MODEL, D_EXPERT), lambda e: (layer, e, 0, 0)),
                 pl.BlockSpec((1, 1, D_EXPERT, D_MODEL), lambda e: (layer, e, 0, 0))]
    return pl.pallas_call(
        functools.partial(_ffn_kernel, with_ctx=with_ctx),
        out_shape=jax.ShapeDtypeStruct((N_EXPERTS, NT, D_MODEL), BF16),
        grid=(N_EXPERTS,),
        in_specs=in_specs,
        out_specs=pl.BlockSpec((1, NT, D_MODEL), lambda e: (e, 0, 0)),
        compiler_params=_params(("arbitrary",)),
        name="expert_ffn",
    )(*ins)


COMBINE_ROWS = 1024


def _scatter_add(col, ys, n_slots):
    lane = lax.broadcasted_iota(jnp.int32, (1, n_slots), 1).astype(F32)
    hot = [_indicator(col[:, e:e + 1] == lane) for e in range(N_EXPERTS)]
    return _dot(jnp.concatenate(hot, axis=-1), jnp.concatenate(ys, axis=0))


def _combine_latent_kernel(*refs, final):
    if final:
        x_ref, col_ref, y_ref, mod_ref, fn_ref, o_ref = refs
    else:
        x_ref, col_ref, y_ref, mod_ref, o_ref = refs
    g2 = mod_ref[0][:, 5 * D_MODEL:6 * D_MODEL]
    acc = _scatter_add(col_ref[0], [y_ref[e] for e in range(N_EXPERTS)], CAP_LAT)
    x = x_ref[0] + g2 * acc
    if final:
        x = _rms(x) * fn_ref[...]
    o_ref[0] = x


def _combine_ctx_kernel(x_ref, col_ref, y_ref, mod_ref, prev_ref, o_ref):
    del prev_ref
    g2 = mod_ref[0][:, 5 * D_MODEL:6 * D_MODEL]
    pad = jnp.zeros((LANES - CAP_CTX, D_MODEL), BF16)
    for b in range(BATCH):
        ys = [jnp.concatenate([y_ref[e, b * CAP_CTX:(b + 1) * CAP_CTX, :], pad], axis=0)
              for e in range(N_EXPERTS)]
        o_ref[b] = x_ref[b] + g2 * _scatter_add(col_ref[b], ys, LANES)


def _combine(xall, slot_cols, y, mod_l, with_ctx, final_norm=None):
    final = final_norm is not None
    n_rows = SEQ if final else NT_PAD
    tok = lambda n: pl.BlockSpec((1, COMBINE_ROWS, n), lambda b, t: (b, t, 0))
    ins = [xall, slot_cols, y, mod_l]
    in_specs = [tok(D_MODEL), tok(LANES),
                pl.BlockSpec((N_EXPERTS, CAP_LAT, D_MODEL), lambda b, t: (0, b, 0)),
                pl.BlockSpec((1, 1, 6 * D_MODEL), lambda b, t: (b, 0, 0))]
    if final:
        ins.append(final_norm)
        in_specs.append(pl.BlockSpec((1, D_MODEL), lambda b, t: (0, 0)))
    out = pl.pallas_call(
        functools.partial(_combine_latent_kernel, final=final),
        out_shape=jax.ShapeDtypeStruct((BATCH, n_rows, D_MODEL), F32),
        grid=(BATCH, SEQ // COMBINE_ROWS),
        in_specs=in_specs,
        out_specs=tok(D_MODEL),
        compiler_params=_params(("arbitrary", "arbitrary")),
        name="expert_combine",
    )(*ins)
    if not with_ctx:
        return out
    ctx_blk = SEQ // CTX_LEN
    ctx_rows = lambda n: pl.BlockSpec((BATCH, CTX_LEN, n), lambda i: (0, ctx_blk, 0))
    return pl.pallas_call(
        _combine_ctx_kernel,
        out_shape=jax.ShapeDtypeStruct((BATCH, NT_PAD, D_MODEL), F32),
        grid=(1,),
        in_specs=[ctx_rows(D_MODEL), ctx_rows(LANES),
                  pl.BlockSpec((N_EXPERTS, CTX_LEN, D_MODEL), lambda i: (0, ctx_blk, 0)),
                  pl.BlockSpec((1, 1, 6 * D_MODEL), lambda i: (BATCH, 0, 0)),
                  pl.BlockSpec(memory_space=pl.ANY)],
        out_specs=ctx_rows(D_MODEL),
        input_output_aliases={4: 0},
        compiler_params=_params(("arbitrary",)),
        name="expert_combine_ctx",
    )(xall, slot_cols, y, mod_l, out)


def _rope_tables():
    t = jnp.arange(SEQ)
    pos = jnp.stack([t // GRID_W, t % GRID_W], axis=0)

    def table(n_f, lanes_per_unit):
        inv = ROPE_BASE ** (-jnp.arange(n_f, dtype=F32) / n_f)
        d = np.arange(lanes_per_unit)
        axis, second, f = d // (2 * n_f), (d // n_f) % 2, d % n_f
        ang = pos[axis].T.astype(F32) * inv[f][None, :]
        cos = jnp.cos(ang)
        sin = jnp.sin(ang) * jnp.where(second == 1, 1.0, -1.0)[None, :]
        return cos, sin

    def finish(cos, sin, scale):
        cos = jnp.concatenate([cos, jnp.ones((NT_PAD - SEQ, LANES), F32)], axis=0)
        sin = jnp.concatenate([sin, jnp.zeros((NT_PAD - SEQ, LANES), F32)], axis=0)
        scale = scale * LOG2_E
        return [cos, sin, cos * scale, sin * scale]

    cos_a, sin_a = table(HEAD_DIM // 4, HEAD_DIM)
    tabs_a = finish(jnp.tile(cos_a, (1, 2)), jnp.tile(sin_a, (1, 2)), HEAD_DIM ** -0.5)
    cos_8, sin_8 = table(B_ROPE // 4, B_ROPE)
    tabs_8 = finish(jnp.tile(cos_8, (1, 4)), jnp.tile(sin_8, (1, 4)), C_DH ** -0.5)
    ones = jnp.ones((SEQ, B_NOPE), F32)
    pad1 = jnp.ones((SEQ, LANES - B_NOPE - B_ROPE), F32)
    cos_b = jnp.concatenate([ones, cos_8, pad1], axis=1)
    sin_b = jnp.concatenate([0 * ones, sin_8, 0 * pad1], axis=1)
    tabs_b = finish(cos_b, sin_b, (B_NOPE + B_ROPE) ** -0.5)
    return tabs_a + tabs_8 + tabs_b


def _window_bias():
    qi = np.arange(WIN)[:, None]
    kj = np.arange(A_LOCAL)[None, :]
    out = np.zeros((3, WIN, A_LOCAL + CTX_LEN), np.float32)
    for v in range(3):
        out[v, :, :A_LOCAL] = np.where(np.abs(kj - v * WIN - qi) <= WIN, 0.0, NEG_INF)
    return jnp.asarray(out)


def _prep_weights(w_in, w_uq, w_ukv, w_out, w_router):
    offs = np.cumsum((0, 512, 128, 128, 256, 128, 32, 256, 256, 256))
    seg = lambda i: w_in[:, :, offs[i]:offs[i + 1]]
    wq = seg(0).reshape(DEPTH, D_MODEL, A_KV_HEADS, A_GROUP, HEAD_DIM)
    wq = jnp.swapaxes(wq, 2, 3).reshape(DEPTH, D_MODEL, A_HEADS * HEAD_DIM)
    kr = jnp.pad(seg(5), ((0, 0), (0, 0), (B_NOPE, LANES - B_NOPE - B_ROPE)))
    w = jnp.concatenate([wq, seg(1), seg(2), seg(3), seg(4), kr,
                         seg(6), seg(7), seg(8)], axis=-1).astype(BF16)
    wuq = w_uq.reshape(DEPTH, B_Q_RANK, B_HEADS, B_NOPE + B_ROPE)
    wuq = jnp.pad(wuq, ((0, 0), (0, 0), (0, 0), (0, LANES - B_NOPE - B_ROPE)))
    wuq = wuq.reshape(DEPTH, B_Q_RANK, B_HEADS * LANES).astype(BF16)
    wukv = w_ukv.reshape(DEPTH, B_KV_RANK, B_HEADS, B_NOPE + B_V)
    wk = jnp.pad(wukv[..., :B_NOPE], ((0, 0), (0, 0), (0, 0), (0, LANES - B_NOPE)))
    wukv = jnp.concatenate([wk.reshape(DEPTH, B_KV_RANK, B_HEADS * LANES),
                            wukv[..., B_NOPE:].reshape(DEPTH, B_KV_RANK, B_HEADS * B_V)],
                           axis=-1).astype(BF16)
    wo_a = w_out[:, :A_HEADS * HEAD_DIM].reshape(DEPTH, A_KV_HEADS, A_GROUP, HEAD_DIM, D_MODEL)
    wo_a = jnp.swapaxes(wo_a, 1, 2).reshape(DEPTH, A_HEADS * HEAD_DIM, D_MODEL)
    wo = jnp.concatenate([wo_a, w_out[:, A_HEADS * HEAD_DIM:]], axis=1).astype(BF16)
    wr = jnp.pad(w_router, ((0, 0), (0, 0), (0, LANES - N_EXPERTS)))
    return w, wuq, wukv, wo, wr


def kernel(x, c, ctx, c_ctx, norm1, norm2, w_ada, b_ada, w_in, sink, mla_q_norm, w_uq, mla_kv_norm, w_ukv,
           lam_q1, lam_k1, lam_q2, lam_k2, diff_norm, w_out, w_router, w_gate, w_up, w_down, final_norm):
    xall = jnp.concatenate([x, ctx, jnp.zeros((BATCH, NT_PAD - NT, D_MODEL), F32)], axis=1)
    cc = jnp.concatenate([c, c_ctx[None, :], jnp.zeros((16 - BATCH - 1, D_MODEL), F32)], axis=0)
    mod = _modulation(cc, w_ada, b_ada).reshape(DEPTH, 16, 1, 6 * D_MODEL)
    tabs = _rope_tables()
    bias = _window_bias()
    tri = jnp.asarray(np.triu(np.ones((LANES, LANES), np.float32), k=1), BF16)
    w, wuq, wukv, wo, wr = _prep_weights(w_in, w_uq, w_ukv, w_out, w_router)
    sub_norm = jnp.tile(diff_norm, (1, C_HEADS)).reshape(DEPTH, 1, C_HEADS * C_V)
    row = lambda a, l: a[l].reshape(1, -1)

    for l in range(DEPTH):
        need_ctx = l < DEPTH - 1
        qa, ka, va, qb, kb, vb, qd, kd, vd = _inproj(
            xall, mod[l], row(norm1, l), w, row(mla_q_norm, l), wuq[l], row(mla_kv_norm, l), wukv[l], tabs, l)

        oa = _attn_a_latent(sink[l], qa, ka, va, bias)
        ob = _full_attention(_attn_b_kernel, "attn_mla", MLA_TQ, qb, kb, vb, [], lambda n: [],
                             256, 512, 512, 512)
        lam_init = 0.8 - 0.6 * math.exp(-0.3 * l)
        c_kernel = functools.partial(_attn_c_kernel, lam_init=lam_init)
        c_extra = [row(lam_q1, l), row(lam_k1, l), row(lam_q2, l), row(lam_k2, l), sub_norm[l]]

        def c_specs(n_grid, c_extra=c_extra):
            zmap = (lambda b, t: (0, 0)) if n_grid == 2 else (lambda b: (0, 0))
            return [pl.BlockSpec(a.shape, zmap) for a in c_extra]

        od = _full_attention(c_kernel, "attn_diff", DIFF_TQ, qd, kd, vd, c_extra, c_specs, 256, 256, 256, 512)
        if need_ctx:
            oa = _attn_a_ctx(sink[l], qa, ka, va, oa)
            ob = _full_attention_ctx(_attn_b_kernel, "attn_mla", qb, kb, vb, [], lambda n: [],
                                     256, 512, 512, 512, ob)
            od = _full_attention_ctx(c_kernel, "attn_diff", qd, kd, vd, c_extra, c_specs,
                                     256, 256, 256, 512, od)

        xall, h2, aff = _outproj(xall, oa, ob, od, wo, mod[l], row(norm2, l), wr[l], need_ctx, l)
        slot_rows, aff_rows, slot_cols = _route(aff, tri, need_ctx)
        gathered = _gather(slot_rows, aff_rows, h2, need_ctx)
        y = _ffn(gathered, w_gate, w_up, w_down, l)
        xall = _combine(xall, slot_cols, y, mod[l], need_ctx,
                        final_norm=None if need_ctx else final_norm.reshape(1, D_MODEL))
    return xall
```

```python
import functools
import math

import numpy as np
import jax
import jax.numpy as jnp
from jax import lax
from jax.experimental import pallas as pl
from jax.experimental.pallas import tpu as pltpu

D_MODEL = 1024
BATCH = 8
SEQ = 2048
DEPTH = 4
CTX_LEN = 256
NT = SEQ + CTX_LEN
GRID_W = 64
HEAD_DIM = 64
ROPE_BASE = 10000.0
EPS = 1e-6
NEG_INF = -1e30
LOG2_E = math.log2(math.e)

WIN = 128
A_HEADS = 8
A_KV_HEADS = 2
A_GROUP = A_HEADS // A_KV_HEADS
B_HEADS = 4
B_NOPE = 64
B_ROPE = 32
B_V = 64
B_Q_RANK = 256
B_KV_RANK = 128
C_HEADS = 4
C_DH = 32
C_V = 2 * C_DH
N_EXPERTS = 16
EC_FACTOR = 2
D_EXPERT = 512
CAP_LAT = EC_FACTOR * SEQ // N_EXPERTS
CAP_CTX = EC_FACTOR * CTX_LEN // N_EXPERTS

LANES = 128
TOK_TILE = 256
NT_PAD = SEQ + TOK_TILE
N_TILES = NT_PAD // TOK_TILE
N_LAT_TILES = SEQ // TOK_TILE
STEP_CHUNKS = 3
LAT_STEP_CHUNKS = 2
FFN_TILE = 1024
FFN_LAT_TILES = SEQ // FFN_TILE
VMEM_LIMIT = 56 * 1024 * 1024

F32 = jnp.float32
BF16 = jnp.bfloat16

_NT_DIMS = (((1,), (1,)), ((), ()))


def _params(sem):
    return pltpu.CompilerParams(dimension_semantics=sem, vmem_limit_bytes=VMEM_LIMIT)


def _dot(a, b):
    return jnp.dot(a, b, preferred_element_type=F32)


def _dot_nt(a, b):
    return lax.dot_general(a, b, _NT_DIMS, preferred_element_type=F32)


def _split_bf16(v):
    hi = v.astype(BF16)
    lo = (v - hi.astype(F32)).astype(BF16)
    return hi, lo


def _dot3(a, b):
    a_hi, a_lo = _split_bf16(a)
    b_hi, b_lo = _split_bf16(b)
    return _dot(a_hi, b_hi) + _dot(a_lo, b_hi) + _dot(a_hi, b_lo)


def _indicator(cond):
    return jnp.where(cond, 1.0, 0.0).astype(BF16)


def _rms(v):
    return v * lax.rsqrt(jnp.mean(v * v, axis=-1, keepdims=True) + EPS)


def _softmax_pv(s, v_blk, head):
    m = jnp.max(s, axis=-1, keepdims=True)
    ol = _dot(jnp.exp2(s - m).astype(BF16), v_blk)
    ones_lane = 64 if head % 2 == 0 else 0
    return ol / ol[:, ones_lane:ones_lane + 1]


def _merge_head_pairs(per_head):
    lane = lax.broadcasted_iota(jnp.int32, (1, LANES), 1)
    blocks = [jnp.where(lane < 64, per_head[j], per_head[j + 1]) for j in range(0, len(per_head), 2)]
    return jnp.concatenate(blocks, axis=-1)


def _softmax_rows(s):
    m = jnp.max(s, axis=-1, keepdims=True)
    e = jnp.exp(s - m)
    return e / jnp.sum(e, axis=-1, keepdims=True)


def _mod_kernel(c_ref, w_ref, b_ref, o_ref):
    c = c_ref[...]
    s = c / (1.0 + jnp.exp(-c))
    o_ref[0] = _dot3(s, w_ref[0]) + b_ref[0]


def _modulation(cc, w_ada, b_ada):
    tn = 1536
    return pl.pallas_call(
        _mod_kernel,
        out_shape=jax.ShapeDtypeStruct((DEPTH, 16, 6 * D_MODEL), F32),
        grid=(DEPTH, 6 * D_MODEL // tn),
        in_specs=[
            pl.BlockSpec((16, D_MODEL), lambda l, j: (0, 0)),
            pl.BlockSpec((1, D_MODEL, tn), lambda l, j: (l, 0, j)),
            pl.BlockSpec((1, 1, tn), lambda l, j: (l, 0, j)),
        ],
        out_specs=pl.BlockSpec((1, 16, tn), lambda l, j: (l, 0, j)),
        compiler_params=_params(("arbitrary", "arbitrary")),
        name="adaln_mod",
    )(cc, w_ada, b_ada.reshape(DEPTH, 1, 6 * D_MODEL))


def _rope(p, cos, sin, half):
    lane = lax.broadcasted_iota(jnp.int32, (1, LANES), 1)
    first = (lane // half) % 2 == 0
    outs = []
    for j in range(p.shape[1] // LANES):
        blk = p[:, j * LANES:(j + 1) * LANES]
        partner = jnp.where(first, pltpu.roll(blk, LANES - half, 1), pltpu.roll(blk, half, 1))
        outs.append(blk * cos + partner * sin)
    return outs[0] if len(outs) == 1 else jnp.concatenate(outs, axis=-1)


def _per_head_with_ones(v):
    lane = lax.broadcasted_iota(jnp.int32, (1, LANES), 1)
    outs = []
    for j in range(v.shape[1] // LANES):
        blk = v[:, j * LANES:(j + 1) * LANES]
        outs.append(jnp.where(lane < 64, blk, jnp.where(lane == 64, 1.0, 0.0)))
        outs.append(jnp.where(lane >= 64, blk, jnp.where(lane == 0, 1.0, 0.0)))
    return jnp.concatenate(outs, axis=-1)


def _chunk_mod(modb_ref, modc_ref, j, n_chunks):
    mod = modb_ref[0]
    if n_chunks == STEP_CHUNKS and j == n_chunks - 1:
        mod = jnp.where(pl.program_id(1) == N_TILES // STEP_CHUNKS - 1, modc_ref[0], mod)
    return mod


def _inproj_kernel(x_ref, modb_ref, modc_ref, g1_ref, w_ref, qn_ref, wuq_ref, kvn_ref, wukv_ref,
                   ca_ref, sa_ref, caq_ref, saq_ref, c8_ref, s8_ref, c8q_ref, s8q_ref,
                   cb_ref, sb_ref, cbq_ref, sbq_ref,
                   qa_ref, ka_ref, va_ref, qb_ref, kb_ref, vb_ref, qd_ref, kd_ref, vd_ref):
    for j in range(STEP_CHUNKS):
        rows = slice(j * TOK_TILE, (j + 1) * TOK_TILE)
        x = x_ref[0, rows, :]
        mod = _chunk_mod(modb_ref, modc_ref, j, STEP_CHUNKS)
        sh1 = mod[:, 0:D_MODEL]
        sc1 = mod[:, D_MODEL:2 * D_MODEL]
        h = (_rms(x) * g1_ref[...]) * (1.0 + sc1) + sh1
        p = _dot(h.astype(BF16), w_ref[0])

        qa_ref[0, rows, :] = _rope(p[:, 0:512], caq_ref[rows, :], saq_ref[rows, :], 16).astype(BF16)
        ka_ref[0, rows, :] = _rope(p[:, 512:640], ca_ref[rows, :], sa_ref[rows, :], 16).astype(BF16)
        lane = lax.broadcasted_iota(jnp.int32, (TOK_TILE, LANES), 1)
        va_ref[0, rows, :] = jnp.concatenate(
            [p[:, 640:768], jnp.where(lane == 0, 1.0, 0.0)], axis=-1).astype(BF16)

        cq = (_rms(p[:, 768:1024]) * qn_ref[...]).astype(BF16)
        qb = _dot(cq, wuq_ref[...])
        qb_ref[0, rows, :] = _rope(qb, cbq_ref[rows, :], sbq_ref[rows, :], 8).astype(BF16)
        ckv = (_rms(p[:, 1024:1152]) * kvn_ref[...]).astype(BF16)
        kv = _dot(ckv, wukv_ref[...])
        kr = _rope(p[:, 1152:1280], cb_ref[rows, :], sb_ref[rows, :], 8)
        kb_ref[0, rows, :] = (kv[:, 0:512] + jnp.concatenate([kr] * B_HEADS, axis=-1)).astype(BF16)
        vb_ref[0, rows, :] = _per_head_with_ones(kv[:, 512:768]).astype(BF16)

        qd_ref[0, rows, :] = _rope(p[:, 1280:1536], c8q_ref[rows, :], s8q_ref[rows, :], 8).astype(BF16)
        kd_ref[0, rows, :] = _rope(p[:, 1536:1792], c8_ref[rows, :], s8_ref[rows, :], 8).astype(BF16)
        vd_ref[0, rows, :] = _per_head_with_ones(p[:, 1792:2048]).astype(BF16)


def _mod_specs():
    return [pl.BlockSpec((1, 1, 6 * D_MODEL), lambda b, t: (b, 0, 0)),
            pl.BlockSpec((1, 1, 6 * D_MODEL), lambda b, t: (BATCH, 0, 0))]


def _inproj(xall, mod_l, g1, w, qn, wuq, kvn, wukv, tabs, layer):
    full = lambda shape: pl.BlockSpec(shape, lambda b, t: (0,) * len(shape))
    step_rows = STEP_CHUNKS * TOK_TILE
    tab_spec = pl.BlockSpec((step_rows, LANES), lambda b, t: (t, 0))
    widths = (512, 128, 256, 512, 512, 512, 256, 256, 512)
    return pl.pallas_call(
        _inproj_kernel,
        out_shape=[jax.ShapeDtypeStruct((BATCH, NT_PAD, n), BF16) for n in widths],
        grid=(BATCH, N_TILES // STEP_CHUNKS),
        in_specs=[pl.BlockSpec((1, step_rows, D_MODEL), lambda b, t: (b, t, 0))] + _mod_specs() + [
            full((1, D_MODEL)),
            pl.BlockSpec((1, D_MODEL, 2048), lambda b, t: (layer, 0, 0)),
            full((1, B_Q_RANK)),
            full((B_Q_RANK, 512)),
            full((1, B_KV_RANK)),
            full((B_KV_RANK, 768)),
        ] + [tab_spec] * 12,
        out_specs=[pl.BlockSpec((1, step_rows, n), lambda b, t: (b, t, 0)) for n in widths],
        compiler_params=_params(("arbitrary", "arbitrary")),
        name="in_projection",
    )(xall, mod_l, mod_l, g1, w, qn, wuq, kvn, wukv, *tabs)


def _window_heads(q, kcat, vcat, bias, sink_ref):
    tq = q.shape[0]
    lane = lax.broadcasted_iota(jnp.int32, (1, LANES), 1)
    acc = [jnp.zeros((tq, LANES), F32) for _ in range(A_GROUP)]
    for g in range(A_KV_HEADS):
        in_half = (lane < HEAD_DIM) if g == 0 else (lane >= HEAD_DIM)
        keep = _indicator(in_half)
        qs = jnp.concatenate([q[:, j * LANES:(j + 1) * LANES] * keep for j in range(A_GROUP)], axis=0)
        s = _dot_nt(qs, kcat)
        if bias is not None:
            s = s + jnp.concatenate([bias] * A_GROUP, axis=0)
        sink = jnp.concatenate(
            [jnp.full((tq, 1), sink_ref[g * A_GROUP + j] * LOG2_E, F32) for j in range(A_GROUP)], axis=0)
        m = jnp.maximum(jnp.max(s, axis=-1, keepdims=True), sink)
        ol = _dot(jnp.exp2(s - m).astype(BF16), vcat)
        o = ol[:, 0:LANES] / (ol[:, LANES:LANES + 1] + jnp.exp2(sink - m))
        for j in range(A_GROUP):
            acc[j] = acc[j] + jnp.where(in_half, o[j * tq:(j + 1) * tq], 0.0)
    return jnp.concatenate(acc, axis=-1).astype(BF16)


A_STEP_BLOCKS = 16
A_LOCAL = 3 * WIN


def _attn_a_latent_kernel(sink_ref, q_ref, k_ref, v_ref, bias_ref, o_ref):
    step = pl.program_id(1)
    last = SEQ // WIN - 1
    k_ctx = k_ref[0, SEQ:NT, :]
    v_ctx = v_ref[0, SEQ:NT, :]
    for j in range(A_STEP_BLOCKS):
        n = step * A_STEP_BLOCKS + j
        start = pl.multiple_of(jnp.clip((n - 1) * WIN, 0, SEQ - A_LOCAL), WIN)
        variant = jnp.where(n == 0, 0, jnp.where(n == last, 2, 1))
        kcat = jnp.concatenate([k_ref[0, pl.ds(start, A_LOCAL), :], k_ctx], axis=0)
        vcat = jnp.concatenate([v_ref[0, pl.ds(start, A_LOCAL), :], v_ctx], axis=0)
        rows = slice(j * WIN, (j + 1) * WIN)
        o_ref[0, rows, :] = _window_heads(q_ref[0, rows, :], kcat, vcat, bias_ref[variant], sink_ref)


def _attn_a_latent(sink, qa, ka, va, bias):
    tq = A_STEP_BLOCKS * WIN
    return pl.pallas_call(
        _attn_a_latent_kernel,
        out_shape=jax.ShapeDtypeStruct((BATCH, NT_PAD, 512), BF16),
        grid=(BATCH, SEQ // tq),
        in_specs=[pl.BlockSpec(memory_space=pltpu.SMEM),
                  pl.BlockSpec((1, tq, 512), lambda b, n: (b, n, 0)),
                  pl.BlockSpec((1, NT_PAD, LANES), lambda b, n: (b, 0, 0)),
                  pl.BlockSpec((1, NT_PAD, 2 * LANES), lambda b, n: (b, 0, 0)),
                  pl.BlockSpec((3, WIN, A_LOCAL + CTX_LEN), lambda b, n: (0, 0, 0))],
        out_specs=pl.BlockSpec((1, tq, 512), lambda b, n: (b, n, 0)),
        compiler_params=_params(("arbitrary", "arbitrary")),
        name="attn_window_latent",
    )(sink, qa, ka, va, bias)


def _attn_a_ctx(sink, qa, ka, va, prev_out):
    ctx_blk = SEQ // CTX_LEN
    spec = lambda n: pl.BlockSpec((BATCH, CTX_LEN, n), lambda i: (0, ctx_blk, 0))
    tile = lambda n: pl.BlockSpec((BATCH, TOK_TILE, n), lambda i: (0, N_LAT_TILES, 0))

    def body(sink_ref, q_ref, k_ref, v_ref, prev_ref, o_ref):
        del prev_ref
        for sample in range(BATCH):
            o_ref[sample] = _window_heads(q_ref[sample], k_ref[sample], v_ref[sample], None, sink_ref)

    return pl.pallas_call(
        body,
        out_shape=jax.ShapeDtypeStruct((BATCH, NT_PAD, 512), BF16),
        grid=(1,),
        in_specs=[pl.BlockSpec(memory_space=pltpu.SMEM), tile(512), spec(LANES), spec(2 * LANES),
                  pl.BlockSpec(memory_space=pl.ANY)],
        out_specs=tile(512),
        input_output_aliases={4: 0},
        compiler_params=_params(("arbitrary",)),
        name="attn_window_ctx",
    )(sink, qa, ka, va, prev_out)


def _key_rows(ref, sample):
    return ref[sample, 0:NT, :] if ref.shape[1] == NT_PAD else ref[sample]


MLA_TQ = 2048
MLA_SUB = 512
DIFF_TQ = 2048
DIFF_SUB = 256


def _attn_b_kernel(q_ref, k_ref, v_ref, o_ref):
    for sample in range(q_ref.shape[0]):
        k = _key_rows(k_ref, sample)
        v = _key_rows(v_ref, sample)
        sub_rows = min(MLA_SUB, q_ref.shape[1])
        for sub in range(q_ref.shape[1] // sub_rows):
            rows = slice(sub * sub_rows, (sub + 1) * sub_rows)
            q = q_ref[sample, rows, :]
            outs = []
            for h in range(B_HEADS):
                s = _dot_nt(q[:, h * LANES:(h + 1) * LANES], k[:, h * LANES:(h + 1) * LANES])
                outs.append(_softmax_pv(s, v[:, h * LANES:(h + 1) * LANES], h))
            o_ref[sample, rows, :] = _merge_head_pairs(outs).astype(BF16)


def _full_attention(kernel, name, tq, q, k, v, extra, extra_specs, out_width, q_width, k_width, v_width):
    out = pl.pallas_call(
        kernel,
        out_shape=jax.ShapeDtypeStruct((BATCH, NT_PAD, out_width), BF16),
        grid=(BATCH, SEQ // tq),
        in_specs=[pl.BlockSpec((1, tq, q_width), lambda b, t: (b, t, 0)),
                  pl.BlockSpec((1, NT_PAD, k_width), lambda b, t: (b, 0, 0)),
                  pl.BlockSpec((1, NT_PAD, v_width), lambda b, t: (b, 0, 0))] + extra_specs(2),
        out_specs=pl.BlockSpec((1, tq, out_width), lambda b, t: (b, t, 0)),
        compiler_params=_params(("arbitrary", "arbitrary")),
        name=name + "_latent",
    )(q, k, v, *extra)
    return out


def _full_attention_ctx(kernel, name, q, k, v, extra, extra_specs, out_width, q_width, k_width, v_width,
                        prev_out):
    ctx_blk = SEQ // CTX_LEN
    n_extra = len(extra)

    def body(*refs):
        ins = refs[:3 + n_extra]
        kernel(*ins, refs[-1])

    spec = lambda n: pl.BlockSpec((BATCH, CTX_LEN, n), lambda i: (0, ctx_blk, 0))
    tile = lambda n: pl.BlockSpec((BATCH, TOK_TILE, n), lambda i: (0, N_LAT_TILES, 0))
    return pl.pallas_call(
        body,
        out_shape=jax.ShapeDtypeStruct((BATCH, NT_PAD, out_width), BF16),
        grid=(1,),
        in_specs=[tile(q_width), spec(k_width), spec(v_width)] + extra_specs(1)
                 + [pl.BlockSpec(memory_space=pl.ANY)],
        out_specs=tile(out_width),
        input_output_aliases={3 + n_extra: 0},
        compiler_params=_params(("arbitrary",)),
        name=name + "_ctx",
    )(q, k, v, *extra, prev_out)


def _attn_c_kernel(q_ref, k_ref, v_ref, lq1_ref, lk1_ref, lq2_ref, lk2_ref, sn_ref, o_ref, *, lam_init):
    tq = min(DIFF_SUB, q_ref.shape[1])
    lam = (jnp.exp(jnp.sum(lq1_ref[...] * lk1_ref[...], axis=-1, keepdims=True))
           - jnp.exp(jnp.sum(lq2_ref[...] * lk2_ref[...], axis=-1, keepdims=True)) + lam_init)
    lane = lax.broadcasted_iota(jnp.int32, (1, C_HEADS * C_V), 1)
    units = [(sample, sub) for sample in range(q_ref.shape[0]) for sub in range(q_ref.shape[1] // tq)]
    for sample, sub in units:
        k = _key_rows(k_ref, sample)
        v = _key_rows(v_ref, sample)
        rows = slice(sub * tq, (sub + 1) * tq)
        q = q_ref[sample, rows, :]
        outs = []
        for h in range(C_HEADS):
            lo = h * C_V
            q1 = q * _indicator((lane >= lo) & (lane < lo + C_DH))
            q2 = q * _indicator((lane >= lo + C_DH) & (lane < lo + C_V))
            s = _dot_nt(jnp.concatenate([q1, q2], axis=0), k)
            pv = _softmax_pv(s, v[:, h * LANES:(h + 1) * LANES], h)
            outs.append(pv[0:tq] - lam * pv[tq:2 * tq])
        acc = _merge_head_pairs(outs)
        sq = acc * acc
        inv = jnp.zeros((tq, C_HEADS * C_V), F32)
        for h in range(C_HEADS):
            in_head = (lane >= h * C_V) & (lane < (h + 1) * C_V)
            ms = jnp.sum(jnp.where(in_head, sq, 0.0), axis=-1, keepdims=True) * (1.0 / C_V)
            inv = inv + jnp.where(in_head, lax.rsqrt(ms + EPS), 0.0)
        o_ref[sample, rows, :] = ((acc * inv) * sn_ref[...] * (1.0 - lam_init)).astype(BF16)


def _outproj_kernel(x_ref, oa_ref, ob_ref, od_ref, w_ref, modb_ref, modc_ref, g2_ref, wr_ref,
                    xo_ref, h2_ref, aff_ref, *, n_chunks):
    w_hi, w_lo = _split_bf16(wr_ref[...])
    w_router = jnp.concatenate([w_hi, w_lo], axis=-1)
    lane = lax.broadcasted_iota(jnp.int32, (1, LANES), 1)
    mixed = _dot(jnp.concatenate([oa_ref[0], ob_ref[0], od_ref[0]], axis=-1), w_ref[0])
    for j in range(n_chunks):
        rows = slice(j * TOK_TILE, (j + 1) * TOK_TILE)
        mod = _chunk_mod(modb_ref, modc_ref, j, n_chunks)
        g1 = mod[:, 2 * D_MODEL:3 * D_MODEL]
        sh2 = mod[:, 3 * D_MODEL:4 * D_MODEL]
        sc2 = mod[:, 4 * D_MODEL:5 * D_MODEL]
        x = x_ref[0, rows, :] + g1 * mixed[rows]
        xo_ref[0, rows, :] = x
        h2 = (_rms(x) * g2_ref[...]) * (1.0 + sc2) + sh2
        h_hi, h_lo = _split_bf16(h2)
        h2_ref[0, rows, :] = h_hi
        prod = _dot(jnp.concatenate([h_hi, h_lo], axis=0), w_router)
        logits = (prod[0:TOK_TILE, 0:LANES] + prod[0:TOK_TILE, LANES:2 * LANES]
                  + prod[TOK_TILE:2 * TOK_TILE, 0:LANES])
        aff_ref[0, rows, :] = _softmax_rows(jnp.where(lane < N_EXPERTS, logits, NEG_INF))


def _outproj(xall, oa, ob, od, w, mod_l, g2, wr, with_ctx, layer):
    n_chunks = STEP_CHUNKS if with_ctx else LAT_STEP_CHUNKS
    n_steps = (N_TILES if with_ctx else N_LAT_TILES) // n_chunks
    full = lambda shape: pl.BlockSpec(shape, lambda b, t: (0,) * len(shape))
    tok = lambda n: pl.BlockSpec((1, n_chunks * TOK_TILE, n), lambda b, t: (b, t, 0))
    return pl.pallas_call(
        functools.partial(_outproj_kernel, n_chunks=n_chunks),
        out_shape=[jax.ShapeDtypeStruct((BATCH, NT_PAD, D_MODEL), F32),
                   jax.ShapeDtypeStruct((BATCH, NT_PAD, D_MODEL), BF16),
                   jax.ShapeDtypeStruct((BATCH, NT_PAD, LANES), F32)],
        grid=(BATCH, n_steps),
        in_specs=[tok(D_MODEL), tok(512), tok(256), tok(256),
                  pl.BlockSpec((1, D_MODEL, D_MODEL), lambda b, t: (layer, 0, 0))] + _mod_specs()
                 + [full((1, D_MODEL)), full((D_MODEL, LANES))],
        out_specs=[tok(D_MODEL), tok(D_MODEL), tok(LANES)],
        compiler_params=_params(("arbitrary", "arbitrary")),
        name="out_projection",
    )(xall, oa, ob, od, w, mod_l, mod_l, g2, wr)


def _prefix_count(flags_f32, tri_ref):
    rows, n = flags_f32.shape
    run = jnp.zeros((rows, 1), F32)
    outs = []
    for c in range(n // LANES):
        blk = flags_f32[:, c * LANES:(c + 1) * LANES]
        outs.append(_dot(blk.astype(BF16), tri_ref[...]) + run)
        run = run + jnp.sum(blk, axis=-1, keepdims=True)
    return jnp.concatenate(outs, axis=-1)


def _select_slots(a, cap, tri_ref):
    bits = pltpu.bitcast(a, jnp.int32)

    def step(i, t):
        cand = t | jnp.left_shift(jnp.int32(1), 30 - i)
        cnt = jnp.sum((bits >= cand).astype(jnp.int32), axis=-1, keepdims=True)
        return jnp.where(cnt >= cap, cand, t)

    thr = lax.fori_loop(0, 31, step, jnp.zeros((a.shape[0], 1), jnp.int32))
    above = bits > thr
    tied = bits == thr
    need = cap - jnp.sum(above.astype(F32), axis=-1, keepdims=True)
    tied_f = tied.astype(F32)
    sel = above | (tied & (_prefix_count(tied_f, tri_ref) < need))
    sel_f = sel.astype(F32)
    return jnp.where(sel, _prefix_count(sel_f, tri_ref), -1.0)


def _route_kernel(aff_ref, tri_ref, row_ref, arow_ref, col_ref, *, with_ctx):
    lat, ctx = [], []
    for b in range(BATCH):
        lat.append(aff_ref[b, 0:SEQ, :].T[0:N_EXPERTS])
        if with_ctx:
            ctx.append(aff_ref[b, SEQ:NT, :].T[0:N_EXPERTS])
    aff_lat = jnp.concatenate(lat, axis=0)
    arow_ref[:, 0:SEQ] = aff_lat
    slot_lat = _select_slots(aff_lat, CAP_LAT, tri_ref)
    if with_ctx:
        aff_ctx = jnp.concatenate(ctx, axis=0)
        arow_ref[:, SEQ:NT] = aff_ctx
        slot_ctx = _select_slots(aff_ctx, CAP_CTX, tri_ref)
        slots = jnp.concatenate([slot_lat, slot_ctx], axis=-1)
    else:
        slots = slot_lat
    n = slots.shape[1]
    row_ref[:, 0:n] = slots
    pad = jnp.full((LANES - N_EXPERTS, n), -1.0, F32)
    for b in range(BATCH):
        blk = jnp.concatenate([slots[b * N_EXPERTS:(b + 1) * N_EXPERTS], pad], axis=0)
        col_ref[b, 0:n, :] = blk.T
        if with_ctx and NT_PAD > NT:
            col_ref[b, NT:NT_PAD, :] = jnp.full((NT_PAD - NT, LANES), -1.0, F32)


def _route(aff, tri, with_ctx):
    vm = pl.BlockSpec(memory_space=pltpu.VMEM)
    return pl.pallas_call(
        functools.partial(_route_kernel, with_ctx=with_ctx),
        out_shape=[jax.ShapeDtypeStruct((BATCH * N_EXPERTS, NT), F32),
                   jax.ShapeDtypeStruct((BATCH * N_EXPERTS, NT), F32),
                   jax.ShapeDtypeStruct((BATCH, NT_PAD, LANES), F32)],
        in_specs=[vm, vm],
        out_specs=[vm, vm, vm],
        compiler_params=pltpu.CompilerParams(vmem_limit_bytes=VMEM_LIMIT),
        name="route_topc",
    )(aff, tri)


GATHER_GROUP = 4


def _gather_kernel(row_ref, arow_ref, h_ref, *out_refs, with_ctx):
    slot = lax.broadcasted_iota(jnp.int32, (CAP_LAT, 1), 0).astype(F32)

    def pick(e0, cols, n_slots, h_rows, x_ref, g_ref):
        hots = []
        for e in range(e0, e0 + GATHER_GROUP):
            hot = jnp.where(row_ref[e:e + 1, cols] == slot[0:n_slots], 1.0, 0.0)
            gate = jnp.sum(hot * arow_ref[e:e + 1, cols], axis=-1, keepdims=True)
            g_ref[e] = jnp.broadcast_to(gate, (n_slots, LANES))
            hots.append(hot.astype(BF16))
        x = _dot(jnp.concatenate(hots, axis=0), h_rows).astype(BF16)
        for i in range(GATHER_GROUP):
            x_ref[e0 + i] = x[i * n_slots:(i + 1) * n_slots]

    for e0 in range(0, N_EXPERTS, GATHER_GROUP):
        pick(e0, slice(0, SEQ), CAP_LAT, h_ref[0, 0:SEQ, :], out_refs[0], out_refs[1])
        if with_ctx:
            pick(e0, slice(SEQ, NT), CAP_CTX, h_ref[0, SEQ:NT, :], out_refs[2], out_refs[3])


def _gather(slot_rows, aff_rows, h2, with_ctx):
    out_shape = [jax.ShapeDtypeStruct((N_EXPERTS, SEQ, D_MODEL), BF16),
                 jax.ShapeDtypeStruct((N_EXPERTS, SEQ, LANES), F32)]
    out_specs = [pl.BlockSpec((N_EXPERTS, CAP_LAT, D_MODEL), lambda b: (0, b, 0)),
                 pl.BlockSpec((N_EXPERTS, CAP_LAT, LANES), lambda b: (0, b, 0))]
    if with_ctx:
        out_shape += [jax.ShapeDtypeStruct((N_EXPERTS, CTX_LEN, D_MODEL), BF16),
                      jax.ShapeDtypeStruct((N_EXPERTS, CTX_LEN, LANES), F32)]
        out_specs += [pl.BlockSpec((N_EXPERTS, CAP_CTX, D_MODEL), lambda b: (0, b, 0)),
                      pl.BlockSpec((N_EXPERTS, CAP_CTX, LANES), lambda b: (0, b, 0))]
    row_spec = pl.BlockSpec((N_EXPERTS, NT), lambda b: (b, 0))
    return pl.pallas_call(
        functools.partial(_gather_kernel, with_ctx=with_ctx),
        out_shape=out_shape,
        grid=(BATCH,),
        in_specs=[row_spec, row_spec, pl.BlockSpec((1, NT_PAD, D_MODEL), lambda b: (b, 0, 0))],
        out_specs=out_specs,
        compiler_params=_params(("arbitrary",)),
        name="expert_gather",
    )(slot_rows, aff_rows, h2)


def _ffn_kernel(*refs, with_ctx):
    if with_ctx:
        xl_ref, gl_ref, xc_ref, gc_ref, wg_ref, wu_ref, wd_ref, y_ref = refs
    else:
        xl_ref, gl_ref, wg_ref, wu_ref, wd_ref, y_ref = refs
    wg = wg_ref[0, 0].astype(BF16)
    wu = wu_ref[0, 0].astype(BF16)
    wd = wd_ref[0, 0].astype(BF16)

    def swiglu(x, slot_gate):
        gate = _dot(x, wg)
        hid = (gate / (1.0 + jnp.exp(-gate))) * _dot(x, wu)
        scale = jnp.concatenate([slot_gate] * (D_MODEL // LANES), axis=-1)
        return (_dot(hid.astype(BF16), wd) * scale).astype(BF16)

    for r in range(FFN_LAT_TILES):
        rows = slice(r * FFN_TILE, (r + 1) * FFN_TILE)
        y_ref[0, rows, :] = swiglu(xl_ref[0, rows, :], gl_ref[0, rows, :])
    if with_ctx:
        y_ref[0, SEQ:NT, :] = swiglu(xc_ref[0], gc_ref[0])


def _ffn(gathered, w_gate, w_up, w_down, layer):
    with_ctx = len(gathered) == 4
    ins = list(gathered) + [w_gate, w_up, w_down]
    in_specs = [pl.BlockSpec((1, SEQ, D_MODEL), lambda e: (e, 0, 0)),
                pl.BlockSpec((1, SEQ, LANES), lambda e: (e, 0, 0))]
    if with_ctx:
        in_specs += [pl.BlockSpec((1, CTX_LEN, D_MODEL), lambda e: (e, 0, 0)),
                     pl.BlockSpec((1, CTX_LEN, LANES), lambda e: (e, 0, 0))]
    in_specs += [pl.BlockSpec((1, 1, D_MODEL, D_EXPERT), lambda e: (layer, e, 0, 0)),
                 pl.BlockSpec((1, 1, D_MODEL, D_EXPERT), lambda e: (layer, e, 0, 0)),
                 pl.BlockSpec((1, 1, D_EXPERT, D_MODEL), lambda e: (layer, e, 0, 0))]
    return pl.pallas_call(
        functools.partial(_ffn_kernel, with_ctx=with_ctx),
        out_shape=jax.ShapeDtypeStruct((N_EXPERTS, NT, D_MODEL), BF16),
        grid=(N_EXPERTS,),
        in_specs=in_specs,
        out_specs=pl.BlockSpec((1, NT, D_MODEL), lambda e: (e, 0, 0)),
        compiler_params=_params(("arbitrary",)),
        name="expert_ffn",
    )(*ins)


COMBINE_ROWS = 1024


def _scatter_add(col, ys, n_slots):
    lane = lax.broadcasted_iota(jnp.int32, (1, n_slots), 1).astype(F32)
    hot = [_indicator(col[:, e:e + 1] == lane) for e in range(N_EXPERTS)]
    return _dot(jnp.concatenate(hot, axis=-1), jnp.concatenate(ys, axis=0))


def _combine_latent_kernel(*refs, final):
    if final:
        x_ref, col_ref, y_ref, mod_ref, fn_ref, o_ref = refs
    else:
        x_ref, col_ref, y_ref, mod_ref, o_ref = refs
    g2 = mod_ref[0][:, 5 * D_MODEL:6 * D_MODEL]
    acc = _scatter_add(col_ref[0], [y_ref[e] for e in range(N_EXPERTS)], CAP_LAT)
    x = x_ref[0] + g2 * acc
    if final:
        x = _rms(x) * fn_ref[...]
    o_ref[0] = x


def _combine_ctx_kernel(x_ref, col_ref, y_ref, mod_ref, prev_ref, o_ref):
    del prev_ref
    g2 = mod_ref[0][:, 5 * D_MODEL:6 * D_MODEL]
    pad = jnp.zeros((LANES - CAP_CTX, D_MODEL), BF16)
    for b in range(BATCH):
        ys = [jnp.concatenate([y_ref[e, b * CAP_CTX:(b + 1) * CAP_CTX, :], pad], axis=0)
              for e in range(N_EXPERTS)]
        o_ref[b] = x_ref[b] + g2 * _scatter_add(col_ref[b], ys, LANES)


def _combine(xall, slot_cols, y, mod_l, with_ctx, final_norm=None):
    final = final_norm is not None
    n_rows = SEQ if final else NT_PAD
    tok = lambda n: pl.BlockSpec((1, COMBINE_ROWS, n), lambda b, t: (b, t, 0))
    ins = [xall, slot_cols, y, mod_l]
    in_specs = [tok(D_MODEL), tok(LANES),
                pl.BlockSpec((N_EXPERTS, CAP_LAT, D_MODEL), lambda b, t: (0, b, 0)),
                pl.BlockSpec((1, 1, 6 * D_MODEL), lambda b, t: (b, 0, 0))]
    if final:
        ins.append(final_norm)
        in_specs.append(pl.BlockSpec((1, D_MODEL), lambda b, t: (0, 0)))
    out = pl.pallas_call(
        functools.partial(_combine_latent_kernel, final=final),
        out_shape=jax.ShapeDtypeStruct((BATCH, n_rows, D_MODEL), F32),
        grid=(BATCH, SEQ // COMBINE_ROWS),
        in_specs=in_specs,
        out_specs=tok(D_MODEL),
        compiler_params=_params(("arbitrary", "arbitrary")),
        name="expert_combine",
    )(*ins)
    if not with_ctx:
        return out
    ctx_blk = SEQ // CTX_LEN
    ctx_rows = lambda n: pl.BlockSpec((BATCH, CTX_LEN, n), lambda i: (0, ctx_blk, 0))
    return pl.pallas_call(
        _combine_ctx_kernel,
        out_shape=jax.ShapeDtypeStruct((BATCH, NT_PAD, D_MODEL), F32),
        grid=(1,),
        in_specs=[ctx_rows(D_MODEL), ctx_rows(LANES),
                  pl.BlockSpec((N_EXPERTS, CTX_LEN, D_MODEL), lambda i: (0, ctx_blk, 0)),
                  pl.BlockSpec((1, 1, 6 * D_MODEL), lambda i: (BATCH, 0, 0)),
                  pl.BlockSpec(memory_space=pl.ANY)],
        out_specs=ctx_rows(D_MODEL),
        input_output_aliases={4: 0},
        compiler_params=_params(("arbitrary",)),
        name="expert_combine_ctx",
    )(xall, slot_cols, y, mod_l, out)


def _rope_tables():
    t = jnp.arange(SEQ)
    pos = jnp.stack([t // GRID_W, t % GRID_W], axis=0)

    def table(n_f, lanes_per_unit):
        inv = ROPE_BASE ** (-jnp.arange(n_f, dtype=F32) / n_f)
        d = np.arange(lanes_per_unit)
        axis, second, f = d // (2 * n_f), (d // n_f) % 2, d % n_f
        ang = pos[axis].T.astype(F32) * inv[f][None, :]
        cos = jnp.cos(ang)
        sin = jnp.sin(ang) * jnp.where(second == 1, 1.0, -1.0)[None, :]
        return cos, sin

    def finish(cos, sin, scale):
        cos = jnp.concatenate([cos, jnp.ones((NT_PAD - SEQ, LANES), F32)], axis=0)
        sin = jnp.concatenate([sin, jnp.zeros((NT_PAD - SEQ, LANES), F32)], axis=0)
        scale = scale * LOG2_E
        return [cos, sin, cos * scale, sin * scale]

    cos_a, sin_a = table(HEAD_DIM // 4, HEAD_DIM)
    tabs_a = finish(jnp.tile(cos_a, (1, 2)), jnp.tile(sin_a, (1, 2)), HEAD_DIM ** -0.5)
    cos_8, sin_8 = table(B_ROPE // 4, B_ROPE)
    tabs_8 = finish(jnp.tile(cos_8, (1, 4)), jnp.tile(sin_8, (1, 4)), C_DH ** -0.5)
    ones = jnp.ones((SEQ, B_NOPE), F32)
    pad1 = jnp.ones((SEQ, LANES - B_NOPE - B_ROPE), F32)
    cos_b = jnp.concatenate([ones, cos_8, pad1], axis=1)
    sin_b = jnp.concatenate([0 * ones, sin_8, 0 * pad1], axis=1)
    tabs_b = finish(cos_b, sin_b, (B_NOPE + B_ROPE) ** -0.5)
    return tabs_a + tabs_8 + tabs_b


def _window_bias():
    qi = np.arange(WIN)[:, None]
    kj = np.arange(A_LOCAL)[None, :]
    out = np.zeros((3, WIN, A_LOCAL + CTX_LEN), np.float32)
    for v in range(3):
        out[v, :, :A_LOCAL] = np.where(np.abs(kj - v * WIN - qi) <= WIN, 0.0, NEG_INF)
    return jnp.asarray(out)


def _prep_weights(w_in, w_uq, w_ukv, w_out, w_router):
    offs = np.cumsum((0, 512, 128, 128, 256, 128, 32, 256, 256, 256))
    seg = lambda i: w_in[:, :, offs[i]:offs[i + 1]]
    wq = seg(0).reshape(DEPTH, D_MODEL, A_KV_HEADS, A_GROUP, HEAD_DIM)
    wq = jnp.swapaxes(wq, 2, 3).reshape(DEPTH, D_MODEL, A_HEADS * HEAD_DIM)
    kr = jnp.pad(seg(5), ((0, 0), (0, 0), (B_NOPE, LANES - B_NOPE - B_ROPE)))
    w = jnp.concatenate([wq, seg(1), seg(2), seg(3), seg(4), kr,
                         seg(6), seg(7), seg(8)], axis=-1).astype(BF16)
    wuq = w_uq.reshape(DEPTH, B_Q_RANK, B_HEADS, B_NOPE + B_ROPE)
    wuq = jnp.pad(wuq, ((0, 0), (0, 0), (0, 0), (0, LANES - B_NOPE - B_ROPE)))
    wuq = wuq.reshape(DEPTH, B_Q_RANK, B_HEADS * LANES).astype(BF16)
    wukv = w_ukv.reshape(DEPTH, B_KV_RANK, B_HEADS, B_NOPE + B_V)
    wk = jnp.pad(wukv[..., :B_NOPE], ((0, 0), (0, 0), (0, 0), (0, LANES - B_NOPE)))
    wukv = jnp.concatenate([wk.reshape(DEPTH, B_KV_RANK, B_HEADS * LANES),
                            wukv[..., B_NOPE:].reshape(DEPTH, B_KV_RANK, B_HEADS * B_V)],
                           axis=-1).astype(BF16)
    wo_a = w_out[:, :A_HEADS * HEAD_DIM].reshape(DEPTH, A_KV_HEADS, A_GROUP, HEAD_DIM, D_MODEL)
    wo_a = jnp.swapaxes(wo_a, 1, 2).reshape(DEPTH, A_HEADS * HEAD_DIM, D_MODEL)
    wo = jnp.concatenate([wo_a, w_out[:, A_HEADS * HEAD_DIM:]], axis=1).astype(BF16)
    wr = jnp.pad(w_router, ((0, 0), (0, 0), (0, LANES - N_EXPERTS)))
    return w, wuq, wukv, wo, wr


def kernel(x, c, ctx, c_ctx, norm1, norm2, w_ada, b_ada, w_in, sink, mla_q_norm, w_uq, mla_kv_norm, w_ukv,
           lam_q1, lam_k1, lam_q2, lam_k2, diff_norm, w_out, w_router, w_gate, w_up, w_down, final_norm):
    xall = jnp.concatenate([x, ctx, jnp.zeros((BATCH, NT_PAD - NT, D_MODEL), F32)], axis=1)
    cc = jnp.concatenate([c, c_ctx[None, :], jnp.zeros((16 - BATCH - 1, D_MODEL), F32)], axis=0)
    mod = _modulation(cc, w_ada, b_ada).reshape(DEPTH, 16, 1, 6 * D_MODEL)
    tabs = _rope_tables()
    bias = _window_bias()
    tri = jnp.asarray(np.triu(np.ones((LANES, LANES), np.float32), k=1), BF16)
    w, wuq, wukv, wo, wr = _prep_weights(w_in, w_uq, w_ukv, w_out, w_router)
    sub_norm = jnp.tile(diff_norm, (1, C_HEADS)).reshape(DEPTH, 1, C_HEADS * C_V)
    row = lambda a, l: a[l].reshape(1, -1)

    for l in range(DEPTH):
        need_ctx = l < DEPTH - 1
        qa, ka, va, qb, kb, vb, qd, kd, vd = _inproj(
            xall, mod[l], row(norm1, l), w, row(mla_q_norm, l), wuq[l], row(mla_kv_norm, l), wukv[l], tabs, l)

        oa = _attn_a_latent(sink[l], qa, ka, va, bias)
        ob = _full_attention(_attn_b_kernel, "attn_mla", MLA_TQ, qb, kb, vb, [], lambda n: [],
                             256, 512, 512, 512)
        lam_init = 0.8 - 0.6 * math.exp(-0.3 * l)
        c_kernel = functools.partial(_attn_c_kernel, lam_init=lam_init)
        c_extra = [row(lam_q1, l), row(lam_k1, l), row(lam_q2, l), row(lam_k2, l), sub_norm[l]]

        def c_specs(n_grid, c_extra=c_extra):
            zmap = (lambda b, t: (0, 0)) if n_grid == 2 else (lambda b: (0, 0))
            return [pl.BlockSpec(a.shape, zmap) for a in c_extra]

        od = _full_attention(c_kernel, "attn_diff", DIFF_TQ, qd, kd, vd, c_extra, c_specs, 256, 256, 256, 512)
        if need_ctx:
            oa = _attn_a_ctx(sink[l], qa, ka, va, oa)
            ob = _full_attention_ctx(_attn_b_kernel, "attn_mla", qb, kb, vb, [], lambda n: [],
                                     256, 512, 512, 512, ob)
            od = _full_attention_ctx(c_kernel, "attn_diff", qd, kd, vd, c_extra, c_specs,
                                     256, 256, 256, 512, od)

        xall, h2, aff = _outproj(xall, oa, ob, od, wo, mod[l], row(norm2, l), wr[l], need_ctx, l)
        slot_rows, aff_rows, slot_cols = _route(aff, tri, need_ctx)
        gathered = _gather(slot_rows, aff_rows, h2, need_ctx)
        y = _ffn(gathered, w_gate, w_up, w_down, l)
        xall = _combine(xall, slot_cols, y, mod[l], need_ctx,
                        final_norm=None if need_ctx else final_norm.reshape(1, D_MODEL))
    return xall
```

```python
import functools
import math

import numpy as np
import jax
import jax.numpy as jnp
from jax import lax
from jax.experimental import pallas as pl
from jax.experimental.pallas import tpu as pltpu

D_MODEL = 1024
BATCH = 8
SEQ = 2048
DEPTH = 4
CTX_LEN = 256
NT = SEQ + CTX_LEN
GRID_W = 64
HEAD_DIM = 64
ROPE_BASE = 10000.0
EPS = 1e-6
NEG_INF = -1e30
LOG2_E = math.log2(math.e)

WIN = 128
A_HEADS = 8
A_KV_HEADS = 2
A_GROUP = A_HEADS // A_KV_HEADS
B_HEADS = 4
B_NOPE = 64
B_ROPE = 32
B_V = 64
B_Q_RANK = 256
B_KV_RANK = 128
C_HEADS = 4
C_DH = 32
C_V = 2 * C_DH
N_EXPERTS = 16
EC_FACTOR = 2
D_EXPERT = 512
CAP_LAT = EC_FACTOR * SEQ // N_EXPERTS
CAP_CTX = EC_FACTOR * CTX_LEN // N_EXPERTS

LANES = 128
TOK_TILE = 256
NT_PAD = SEQ + TOK_TILE
N_TILES = NT_PAD // TOK_TILE
N_LAT_TILES = SEQ // TOK_TILE
STEP_CHUNKS = 3
LAT_STEP_CHUNKS = 2
FFN_TILE = 1024
FFN_LAT_TILES = SEQ // FFN_TILE
VMEM_LIMIT = 56 * 1024 * 1024

F32 = jnp.float32
BF16 = jnp.bfloat16

_NT_DIMS = (((1,), (1,)), ((), ()))


def _params(sem):
    return pltpu.CompilerParams(dimension_semantics=sem, vmem_limit_bytes=VMEM_LIMIT)


def _dot(a, b):
    return jnp.dot(a, b, preferred_element_type=F32)


def _dot_nt(a, b):
    return lax.dot_general(a, b, _NT_DIMS, preferred_element_type=F32)


def _split_bf16(v):
    hi = v.astype(BF16)
    lo = (v - hi.astype(F32)).astype(BF16)
    return hi, lo


def _dot3(a, b):
    a_hi, a_lo = _split_bf16(a)
    b_hi, b_lo = _split_bf16(b)
    return _dot(a_hi, b_hi) + _dot(a_lo, b_hi) + _dot(a_hi, b_lo)


def _indicator(cond):
    return jnp.where(cond, 1.0, 0.0).astype(BF16)


def _rms(v):
    return v * lax.rsqrt(jnp.mean(v * v, axis=-1, keepdims=True) + EPS)


def _softmax_pv(s, v_blk, head):
    m = jnp.max(s, axis=-1, keepdims=True)
    ol = _dot(jnp.exp2(s - m).astype(BF16), v_blk)
    ones_lane = 64 if head % 2 == 0 else 0
    return ol / ol[:, ones_lane:ones_lane + 1]


def _merge_head_pairs(per_head):
    lane = lax.broadcasted_iota(jnp.int32, (1, LANES), 1)
    blocks = [jnp.where(lane < 64, per_head[j], per_head[j + 1]) for j in range(0, len(per_head), 2)]
    return jnp.concatenate(blocks, axis=-1)


def _softmax_rows(s):
    m = jnp.max(s, axis=-1, keepdims=True)
    e = jnp.exp(s - m)
    return e / jnp.sum(e, axis=-1, keepdims=True)


def _mod_kernel(c_ref, w_ref, b_ref, o_ref):
    c = c_ref[...]
    s = c / (1.0 + jnp.exp(-c))
    o_ref[0] = _dot3(s, w_ref[0]) + b_ref[0]


def _modulation(cc, w_ada, b_ada):
    tn = 1536
    return pl.pallas_call(
        _mod_kernel,
        out_shape=jax.ShapeDtypeStruct((DEPTH, 16, 6 * D_MODEL), F32),
        grid=(DEPTH, 6 * D_MODEL // tn),
        in_specs=[
            pl.BlockSpec((16, D_MODEL), lambda l, j: (0, 0)),
            pl.BlockSpec((1, D_MODEL, tn), lambda l, j: (l, 0, j)),
            pl.BlockSpec((1, 1, tn), lambda l, j: (l, 0, j)),
        ],
        out_specs=pl.BlockSpec((1, 16, tn), lambda l, j: (l, 0, j)),
        compiler_params=_params(("arbitrary", "arbitrary")),
        name="adaln_mod",
    )(cc, w_ada, b_ada.reshape(DEPTH, 1, 6 * D_MODEL))


def _rope(p, cos, sin, half):
    lane = lax.broadcasted_iota(jnp.int32, (1, LANES), 1)
    first = (lane // half) % 2 == 0
    outs = []
    for j in range(p.shape[1] // LANES):
        blk = p[:, j * LANES:(j + 1) * LANES]
        partner = jnp.where(first, pltpu.roll(blk, LANES - half, 1), pltpu.roll(blk, half, 1))
        outs.append(blk * cos + partner * sin)
    return outs[0] if len(outs) == 1 else jnp.concatenate(outs, axis=-1)


def _per_head_with_ones(v):
    lane = lax.broadcasted_iota(jnp.int32, (1, LANES), 1)
    outs = []
    for j in range(v.shape[1] // LANES):
        blk = v[:, j * LANES:(j + 1) * LANES]
        outs.append(jnp.where(lane < 64, blk, jnp.where(lane == 64, 1.0, 0.0)))
        outs.append(jnp.where(lane >= 64, blk, jnp.where(lane == 0, 1.0, 0.0)))
    return jnp.concatenate(outs, axis=-1)


def _chunk_mod(modb_ref, modc_ref, j, n_chunks):
    mod = modb_ref[0]
    if n_chunks == STEP_CHUNKS and j == n_chunks - 1:
        mod = jnp.where(pl.program_id(1) == N_TILES // STEP_CHUNKS - 1, modc_ref[0], mod)
    return mod


def _inproj_kernel(x_ref, modb_ref, modc_ref, g1_ref, w_ref, qn_ref, wuq_ref, kvn_ref, wukv_ref,
                   ca_ref, sa_ref, caq_ref, saq_ref, c8_ref, s8_ref, c8q_ref, s8q_ref,
                   cb_ref, sb_ref, cbq_ref, sbq_ref,
                   qa_ref, ka_ref, va_ref, qb_ref, kb_ref, vb_ref, qd_ref, kd_ref, vd_ref):
    for j in range(STEP_CHUNKS):
        rows = slice(j * TOK_TILE, (j + 1) * TOK_TILE)
        x = x_ref[0, rows, :]
        mod = _chunk_mod(modb_ref, modc_ref, j, STEP_CHUNKS)
        sh1 = mod[:, 0:D_MODEL]
        sc1 = mod[:, D_MODEL:2 * D_MODEL]
        h = (_rms(x) * g1_ref[...]) * (1.0 + sc1) + sh1
        p = _dot(h.astype(BF16), w_ref[0])

        qa_ref[0, rows, :] = _rope(p[:, 0:512], caq_ref[rows, :], saq_ref[rows, :], 16).astype(BF16)
        ka_ref[0, rows, :] = _rope(p[:, 512:640], ca_ref[rows, :], sa_ref[rows, :], 16).astype(BF16)
        lane = lax.broadcasted_iota(jnp.int32, (TOK_TILE, LANES), 1)
        va_ref[0, rows, :] = jnp.concatenate(
            [p[:, 640:768], jnp.where(lane == 0, 1.0, 0.0)], axis=-1).astype(BF16)

        cq = (_rms(p[:, 768:1024]) * qn_ref[...]).astype(BF16)
        qb = _dot(cq, wuq_ref[...])
        qb_ref[0, rows, :] = _rope(qb, cbq_ref[rows, :], sbq_ref[rows, :], 8).astype(BF16)
        ckv = (_rms(p[:, 1024:1152]) * kvn_ref[...]).astype(BF16)
        kv = _dot(ckv, wukv_ref[...])
        kr = _rope(p[:, 1152:1280], cb_ref[rows, :], sb_ref[rows, :], 8)
        kb_ref[0, rows, :] = (kv[:, 0:512] + jnp.concatenate([kr] * B_HEADS, axis=-1)).astype(BF16)
        vb_ref[0, rows, :] = _per_head_with_ones(kv[:, 512:768]).astype(BF16)

        qd_ref[0, rows, :] = _rope(p[:, 1280:1536], c8q_ref[rows, :], s8q_ref[rows, :], 8).astype(BF16)
        kd_ref[0, rows, :] = _rope(p[:, 1536:1792], c8_ref[rows, :], s8_ref[rows, :], 8).astype(BF16)
        vd_ref[0, rows, :] = _per_head_with_ones(p[:, 1792:2048]).astype(BF16)


def _mod_specs():
    return [pl.BlockSpec((1, 1, 6 * D_MODEL), lambda b, t: (b, 0, 0)),
            pl.BlockSpec((1, 1, 6 * D_MODEL), lambda b, t: (BATCH, 0, 0))]


def _inproj(xall, mod_l, g1, w, qn, wuq, kvn, wukv, tabs, layer):
    full = lambda shape: pl.BlockSpec(shape, lambda b, t: (0,) * len(shape))
    step_rows = STEP_CHUNKS * TOK_TILE
    tab_spec = pl.BlockSpec((step_rows, LANES), lambda b, t: (t, 0))
    widths = (512, 128, 256, 512, 512, 512, 256, 256, 512)
    return pl.pallas_call(
        _inproj_kernel,
        out_shape=[jax.ShapeDtypeStruct((BATCH, NT_PAD, n), BF16) for n in widths],
        grid=(BATCH, N_TILES // STEP_CHUNKS),
        in_specs=[pl.BlockSpec((1, step_rows, D_MODEL), lambda b, t: (b, t, 0))] + _mod_specs() + [
            full((1, D_MODEL)),
            pl.BlockSpec((1, D_MODEL, 2048), lambda b, t: (layer, 0, 0)),
            full((1, B_Q_RANK)),
            full((B_Q_RANK, 512)),
            full((1, B_KV_RANK)),
            full((B_KV_RANK, 768)),
        ] + [tab_spec] * 12,
        out_specs=[pl.BlockSpec((1, step_rows, n), lambda b, t: (b, t, 0)) for n in widths],
        compiler_params=_params(("arbitrary", "arbitrary")),
        name="in_projection",
    )(xall, mod_l, mod_l, g1, w, qn, wuq, kvn, wukv, *tabs)


def _window_heads(q, kcat, vcat, bias, sink_ref):
    tq = q.shape[0]
    lane = lax.broadcasted_iota(jnp.int32, (1, LANES), 1)
    acc = [jnp.zeros((tq, LANES), F32) for _ in range(A_GROUP)]
    for g in range(A_KV_HEADS):
        in_half = (lane < HEAD_DIM) if g == 0 else (lane >= HEAD_DIM)
        keep = _indicator(in_half)
        qs = jnp.concatenate([q[:, j * LANES:(j + 1) * LANES] * keep for j in range(A_GROUP)], axis=0)
        s = _dot_nt(qs, kcat)
        if bias is not None:
            s = s + jnp.concatenate([bias] * A_GROUP, axis=0)
        sink = jnp.concatenate(
            [jnp.full((tq, 1), sink_ref[g * A_GROUP + j] * LOG2_E, F32) for j in range(A_GROUP)], axis=0)
        m = jnp.maximum(jnp.max(s, axis=-1, keepdims=True), sink)
        ol = _dot(jnp.exp2(s - m).astype(BF16), vcat)
        o = ol[:, 0:LANES] / (ol[:, LANES:LANES + 1] + jnp.exp2(sink - m))
        for j in range(A_GROUP):
            acc[j] = acc[j] + jnp.where(in_half, o[j * tq:(j + 1) * tq], 0.0)
    return jnp.concatenate(acc, axis=-1).astype(BF16)


A_STEP_BLOCKS = 8
A_LOCAL = 3 * WIN


def _attn_a_latent_kernel(sink_ref, q_ref, k_ref, v_ref, bias_ref, o_ref):
    step = pl.program_id(1)
    last = SEQ // WIN - 1
    k_ctx = k_ref[0, SEQ:NT, :]
    v_ctx = v_ref[0, SEQ:NT, :]
    for j in range(A_STEP_BLOCKS):
        n = step * A_STEP_BLOCKS + j
        start = pl.multiple_of(jnp.clip((n - 1) * WIN, 0, SEQ - A_LOCAL), WIN)
        variant = jnp.where(n == 0, 0, jnp.where(n == last, 2, 1))
        kcat = jnp.concatenate([k_ref[0, pl.ds(start, A_LOCAL), :], k_ctx], axis=0)
        vcat = jnp.concatenate([v_ref[0, pl.ds(start, A_LOCAL), :], v_ctx], axis=0)
        rows = slice(j * WIN, (j + 1) * WIN)
        o_ref[0, rows, :] = _window_heads(q_ref[0, rows, :], kcat, vcat, bias_ref[variant], sink_ref)


def _attn_a_latent(sink, qa, ka, va, bias):
    tq = A_STEP_BLOCKS * WIN
    return pl.pallas_call(
        _attn_a_latent_kernel,
        out_shape=jax.ShapeDtypeStruct((BATCH, NT_PAD, 512), BF16),
        grid=(BATCH, SEQ // tq),
        in_specs=[pl.BlockSpec(memory_space=pltpu.SMEM),
                  pl.BlockSpec((1, tq, 512), lambda b, n: (b, n, 0)),
                  pl.BlockSpec((1, NT_PAD, LANES), lambda b, n: (b, 0, 0)),
                  pl.BlockSpec((1, NT_PAD, 2 * LANES), lambda b, n: (b, 0, 0)),
                  pl.BlockSpec((3, WIN, A_LOCAL + CTX_LEN), lambda b, n: (0, 0, 0))],
        out_specs=pl.BlockSpec((1, tq, 512), lambda b, n: (b, n, 0)),
        compiler_params=_params(("arbitrary", "arbitrary")),
        name="attn_window_latent",
    )(sink, qa, ka, va, bias)


def _attn_a_ctx(sink, qa, ka, va, prev_out):
    ctx_blk = SEQ // CTX_LEN
    spec = lambda n: pl.BlockSpec((BATCH, CTX_LEN, n), lambda i: (0, ctx_blk, 0))
    tile = lambda n: pl.BlockSpec((BATCH, TOK_TILE, n), lambda i: (0, N_LAT_TILES, 0))

    def body(sink_ref, q_ref, k_ref, v_ref, prev_ref, o_ref):
        del prev_ref
        for sample in range(BATCH):
            o_ref[sample] = _window_heads(q_ref[sample], k_ref[sample], v_ref[sample], None, sink_ref)

    return pl.pallas_call(
        body,
        out_shape=jax.ShapeDtypeStruct((BATCH, NT_PAD, 512), BF16),
        grid=(1,),
        in_specs=[pl.BlockSpec(memory_space=pltpu.SMEM), tile(512), spec(LANES), spec(2 * LANES),
                  pl.BlockSpec(memory_space=pl.ANY)],
        out_specs=tile(512),
        input_output_aliases={4: 0},
        compiler_params=_params(("arbitrary",)),
        name="attn_window_ctx",
    )(sink, qa, ka, va, prev_out)


def _key_rows(ref, sample):
    return ref[sample, 0:NT, :] if ref.shape[1] == NT_PAD else ref[sample]


MLA_TQ = 2048
MLA_SUB = 512
DIFF_TQ = 512
DIFF_SUB = 256


def _attn_b_kernel(q_ref, k_ref, v_ref, o_ref):
    for sample in range(q_ref.shape[0]):
        k = _key_rows(k_ref, sample)
        v = _key_rows(v_ref, sample)
        sub_rows = min(MLA_SUB, q_ref.shape[1])
        for sub in range(q_ref.shape[1] // sub_rows):
            rows = slice(sub * sub_rows, (sub + 1) * sub_rows)
            q = q_ref[sample, rows, :]
            outs = []
            for h in range(B_HEADS):
                s = _dot_nt(q[:, h * LANES:(h + 1) * LANES], k[:, h * LANES:(h + 1) * LANES])
                outs.append(_softmax_pv(s, v[:, h * LANES:(h + 1) * LANES], h))
            o_ref[sample, rows, :] = _merge_head_pairs(outs).astype(BF16)


def _full_attention(kernel, name, tq, q, k, v, extra, extra_specs, out_width, q_width, k_width, v_width):
    out = pl.pallas_call(
        kernel,
        out_shape=jax.ShapeDtypeStruct((BATCH, NT_PAD, out_width), BF16),
        grid=(BATCH, SEQ // tq),
        in_specs=[pl.BlockSpec((1, tq, q_width), lambda b, t: (b, t, 0)),
                  pl.BlockSpec((1, NT_PAD, k_width), lambda b, t: (b, 0, 0)),
                  pl.BlockSpec((1, NT_PAD, v_width), lambda b, t: (b, 0, 0))] + extra_specs(2),
        out_specs=pl.BlockSpec((1, tq, out_width), lambda b, t: (b, t, 0)),
        compiler_params=_params(("arbitrary", "arbitrary")),
        name=name + "_latent",
    )(q, k, v, *extra)
    return out


def _full_attention_ctx(kernel, name, q, k, v, extra, extra_specs, out_width, q_width, k_width, v_width,
                        prev_out):
    ctx_blk = SEQ // CTX_LEN
    n_extra = len(extra)

    def body(*refs):
        ins = refs[:3 + n_extra]
        kernel(*ins, refs[-1])

    spec = lambda n: pl.BlockSpec((BATCH, CTX_LEN, n), lambda i: (0, ctx_blk, 0))
    tile = lambda n: pl.BlockSpec((BATCH, TOK_TILE, n), lambda i: (0, N_LAT_TILES, 0))
    return pl.pallas_call(
        body,
        out_shape=jax.ShapeDtypeStruct((BATCH, NT_PAD, out_width), BF16),
        grid=(1,),
        in_specs=[tile(q_width), spec(k_width), spec(v_width)] + extra_specs(1)
                 + [pl.BlockSpec(memory_space=pl.ANY)],
        out_specs=tile(out_width),
        input_output_aliases={3 + n_extra: 0},
        compiler_params=_params(("arbitrary",)),
        name=name + "_ctx",
    )(q, k, v, *extra, prev_out)


def _attn_c_kernel(q_ref, k_ref, v_ref, lq1_ref, lk1_ref, lq2_ref, lk2_ref, sn_ref, o_ref, *, lam_init):
    tq = min(DIFF_SUB, q_ref.shape[1])
    lam = (jnp.exp(jnp.sum(lq1_ref[...] * lk1_ref[...], axis=-1, keepdims=True))
           - jnp.exp(jnp.sum(lq2_ref[...] * lk2_ref[...], axis=-1, keepdims=True)) + lam_init)
    lane = lax.broadcasted_iota(jnp.int32, (1, C_HEADS * C_V), 1)
    units = [(sample, sub) for sample in range(q_ref.shape[0]) for sub in range(q_ref.shape[1] // tq)]
    for sample, sub in units:
        k = _key_rows(k_ref, sample)
        v = _key_rows(v_ref, sample)
        rows = slice(sub * tq, (sub + 1) * tq)
        q = q_ref[sample, rows, :]
        outs = []
        for h in range(C_HEADS):
            lo = h * C_V
            q1 = q * _indicator((lane >= lo) & (lane < lo + C_DH))
            q2 = q * _indicator((lane >= lo + C_DH) & (lane < lo + C_V))
            s = _dot_nt(jnp.concatenate([q1, q2], axis=0), k)
            pv = _softmax_pv(s, v[:, h * LANES:(h + 1) * LANES], h)
            outs.append(pv[0:tq] - lam * pv[tq:2 * tq])
        acc = _merge_head_pairs(outs)
        sq = acc * acc
        inv = jnp.zeros((tq, C_HEADS * C_V), F32)
        for h in range(C_HEADS):
            in_head = (lane >= h * C_V) & (lane < (h + 1) * C_V)
            ms = jnp.sum(jnp.where(in_head, sq, 0.0), axis=-1, keepdims=True) * (1.0 / C_V)
            inv = inv + jnp.where(in_head, lax.rsqrt(ms + EPS), 0.0)
        o_ref[sample, rows, :] = ((acc * inv) * sn_ref[...] * (1.0 - lam_init)).astype(BF16)


def _outproj_kernel(x_ref, oa_ref, ob_ref, od_ref, w_ref, modb_ref, modc_ref, g2_ref, wr_ref,
                    xo_ref, h2_ref, aff_ref, *, n_chunks):
    w_hi, w_lo = _split_bf16(wr_ref[...])
    w_router = jnp.concatenate([w_hi, w_lo], axis=-1)
    lane = lax.broadcasted_iota(jnp.int32, (1, LANES), 1)
    mixed = _dot(jnp.concatenate([oa_ref[0], ob_ref[0], od_ref[0]], axis=-1), w_ref[0])
    for j in range(n_chunks):
        rows = slice(j * TOK_TILE, (j + 1) * TOK_TILE)
        mod = _chunk_mod(modb_ref, modc_ref, j, n_chunks)
        g1 = mod[:, 2 * D_MODEL:3 * D_MODEL]
        sh2 = mod[:, 3 * D_MODEL:4 * D_MODEL]
        sc2 = mod[:, 4 * D_MODEL:5 * D_MODEL]
        x = x_ref[0, rows, :] + g1 * mixed[rows]
        xo_ref[0, rows, :] = x
        h2 = (_rms(x) * g2_ref[...]) * (1.0 + sc2) + sh2
        h_hi, h_lo = _split_bf16(h2)
        h2_ref[0, rows, :] = h_hi
        prod = _dot(jnp.concatenate([h_hi, h_lo], axis=0), w_router)
        logits = (prod[0:TOK_TILE, 0:LANES] + prod[0:TOK_TILE, LANES:2 * LANES]
                  + prod[TOK_TILE:2 * TOK_TILE, 0:LANES])
        aff_ref[0, rows, :] = _softmax_rows(jnp.where(lane < N_EXPERTS, logits, NEG_INF))


def _outproj(xall, oa, ob, od, w, mod_l, g2, wr, with_ctx, layer):
    n_chunks = STEP_CHUNKS if with_ctx else LAT_STEP_CHUNKS
    n_steps = (N_TILES if with_ctx else N_LAT_TILES) // n_chunks
    full = lambda shape: pl.BlockSpec(shape, lambda b, t: (0,) * len(shape))
    tok = lambda n: pl.BlockSpec((1, n_chunks * TOK_TILE, n), lambda b, t: (b, t, 0))
    return pl.pallas_call(
        functools.partial(_outproj_kernel, n_chunks=n_chunks),
        out_shape=[jax.ShapeDtypeStruct((BATCH, NT_PAD, D_MODEL), F32),
                   jax.ShapeDtypeStruct((BATCH, NT_PAD, D_MODEL), BF16),
                   jax.ShapeDtypeStruct((BATCH, NT_PAD, LANES), F32)],
        grid=(BATCH, n_steps),
        in_specs=[tok(D_MODEL), tok(512), tok(256), tok(256),
                  pl.BlockSpec((1, D_MODEL, D_MODEL), lambda b, t: (layer, 0, 0))] + _mod_specs()
                 + [full((1, D_MODEL)), full((D_MODEL, LANES))],
        out_specs=[tok(D_MODEL), tok(D_MODEL), tok(LANES)],
        compiler_params=_params(("arbitrary", "arbitrary")),
        name="out_projection",
    )(xall, oa, ob, od, w, mod_l, mod_l, g2, wr)


def _prefix_count(flags_f32, tri_ref):
    rows, n = flags_f32.shape
    run = jnp.zeros((rows, 1), F32)
    outs = []
    for c in range(n // LANES):
        blk = flags_f32[:, c * LANES:(c + 1) * LANES]
        outs.append(_dot(blk.astype(BF16), tri_ref[...]) + run)
        run = run + jnp.sum(blk, axis=-1, keepdims=True)
    return jnp.concatenate(outs, axis=-1)


def _select_slots(a, cap, tri_ref):
    bits = pltpu.bitcast(a, jnp.int32)

    def step(i, t):
        cand = t | jnp.left_shift(jnp.int32(1), 30 - i)
        cnt = jnp.sum((bits >= cand).astype(jnp.int32), axis=-1, keepdims=True)
        return jnp.where(cnt >= cap, cand, t)

    thr = lax.fori_loop(0, 31, step, jnp.zeros((a.shape[0], 1), jnp.int32))
    above = bits > thr
    tied = bits == thr
    need = cap - jnp.sum(above.astype(F32), axis=-1, keepdims=True)
    tied_f = tied.astype(F32)
    sel = above | (tied & (_prefix_count(tied_f, tri_ref) < need))
    sel_f = sel.astype(F32)
    return jnp.where(sel, _prefix_count(sel_f, tri_ref), -1.0)


def _route_kernel(aff_ref, tri_ref, row_ref, arow_ref, col_ref, *, with_ctx):
    lat, ctx = [], []
    for b in range(BATCH):
        lat.append(aff_ref[b, 0:SEQ, :].T[0:N_EXPERTS])
        if with_ctx:
            ctx.append(aff_ref[b, SEQ:NT, :].T[0:N_EXPERTS])
    aff_lat = jnp.concatenate(lat, axis=0)
    arow_ref[:, 0:SEQ] = aff_lat
    slot_lat = _select_slots(aff_lat, CAP_LAT, tri_ref)
    if with_ctx:
        aff_ctx = jnp.concatenate(ctx, axis=0)
        arow_ref[:, SEQ:NT] = aff_ctx
        slot_ctx = _select_slots(aff_ctx, CAP_CTX, tri_ref)
        slots = jnp.concatenate([slot_lat, slot_ctx], axis=-1)
    else:
        slots = slot_lat
    n = slots.shape[1]
    row_ref[:, 0:n] = slots
    pad = jnp.full((LANES - N_EXPERTS, n), -1.0, F32)
    for b in range(BATCH):
        blk = jnp.concatenate([slots[b * N_EXPERTS:(b + 1) * N_EXPERTS], pad], axis=0)
        col_ref[b, 0:n, :] = blk.T
        if with_ctx and NT_PAD > NT:
            col_ref[b, NT:NT_PAD, :] = jnp.full((NT_PAD - NT, LANES), -1.0, F32)


def _route(aff, tri, with_ctx):
    vm = pl.BlockSpec(memory_space=pltpu.VMEM)
    return pl.pallas_call(
        functools.partial(_route_kernel, with_ctx=with_ctx),
        out_shape=[jax.ShapeDtypeStruct((BATCH * N_EXPERTS, NT), F32),
                   jax.ShapeDtypeStruct((BATCH * N_EXPERTS, NT), F32),
                   jax.ShapeDtypeStruct((BATCH, NT_PAD, LANES), F32)],
        in_specs=[vm, vm],
        out_specs=[vm, vm, vm],
        compiler_params=pltpu.CompilerParams(vmem_limit_bytes=VMEM_LIMIT),
        name="route_topc",
    )(aff, tri)


GATHER_GROUP = 4


def _gather_kernel(row_ref, arow_ref, h_ref, *out_refs, with_ctx):
    slot = lax.broadcasted_iota(jnp.int32, (CAP_LAT, 1), 0).astype(F32)

    def pick(e0, cols, n_slots, h_rows, x_ref, g_ref):
        hots = []
        for e in range(e0, e0 + GATHER_GROUP):
            hot = jnp.where(row_ref[e:e + 1, cols] == slot[0:n_slots], 1.0, 0.0)
            gate = jnp.sum(hot * arow_ref[e:e + 1, cols], axis=-1, keepdims=True)
            g_ref[e] = jnp.broadcast_to(gate, (n_slots, LANES))
            hots.append(hot.astype(BF16))
        x = _dot(jnp.concatenate(hots, axis=0), h_rows).astype(BF16)
        for i in range(GATHER_GROUP):
            x_ref[e0 + i] = x[i * n_slots:(i + 1) * n_slots]

    for e0 in range(0, N_EXPERTS, GATHER_GROUP):
        pick(e0, slice(0, SEQ), CAP_LAT, h_ref[0, 0:SEQ, :], out_refs[0], out_refs[1])
        if with_ctx:
            pick(e0, slice(SEQ, NT), CAP_CTX, h_ref[0, SEQ:NT, :], out_refs[2], out_refs[3])


def _gather(slot_rows, aff_rows, h2, with_ctx):
    out_shape = [jax.ShapeDtypeStruct((N_EXPERTS, SEQ, D_MODEL), BF16),
                 jax.ShapeDtypeStruct((N_EXPERTS, SEQ, LANES), F32)]
    out_specs = [pl.BlockSpec((N_EXPERTS, CAP_LAT, D_MODEL), lambda b: (0, b, 0)),
                 pl.BlockSpec((N_EXPERTS, CAP_LAT, LANES), lambda b: (0, b, 0))]
    if with_ctx:
        out_shape += [jax.ShapeDtypeStruct((N_EXPERTS, CTX_LEN, D_MODEL), BF16),
                      jax.ShapeDtypeStruct((N_EXPERTS, CTX_LEN, LANES), F32)]
        out_specs += [pl.BlockSpec((N_EXPERTS, CAP_CTX, D_MODEL), lambda b: (0, b, 0)),
                      pl.BlockSpec((N_EXPERTS, CAP_CTX, LANES), lambda b: (0, b, 0))]
    row_spec = pl.BlockSpec((N_EXPERTS, NT), lambda b: (b, 0))
    return pl.pallas_call(
        functools.partial(_gather_kernel, with_ctx=with_ctx),
        out_shape=out_shape,
        grid=(BATCH,),
        in_specs=[row_spec, row_spec, pl.BlockSpec((1, NT_PAD, D_MODEL), lambda b: (b, 0, 0))],
        out_specs=out_specs,
        compiler_params=_params(("arbitrary",)),
        name="expert_gather",
    )(slot_rows, aff_rows, h2)


def _ffn_kernel(*refs, with_ctx):
    if with_ctx:
        xl_ref, gl_ref, xc_ref, gc_ref, wg_ref, wu_ref, wd_ref, y_ref = refs
    else:
        xl_ref, gl_ref, wg_ref, wu_ref, wd_ref, y_ref = refs
    wg = wg_ref[0, 0].astype(BF16)
    wu = wu_ref[0, 0].astype(BF16)
    wd = wd_ref[0, 0].astype(BF16)

    def swiglu(x, slot_gate):
        gate = _dot(x, wg)
        hid = (gate / (1.0 + jnp.exp(-gate))) * _dot(x, wu)
        scale = jnp.concatenate([slot_gate] * (D_MODEL // LANES), axis=-1)
        return (_dot(hid.astype(BF16), wd) * scale).astype(BF16)

    for r in range(FFN_LAT_TILES):
        rows = slice(r * FFN_TILE, (r + 1) * FFN_TILE)
        y_ref[0, rows, :] = swiglu(xl_ref[0, rows, :], gl_ref[0, rows, :])
    if with_ctx:
        y_ref[0, SEQ:NT, :] = swiglu(xc_ref[0], gc_ref[0])


def _ffn(gathered, w_gate, w_up, w_down, layer):
    with_ctx = len(gathered) == 4
    ins = list(gathered) + [w_gate, w_up, w_down]
    in_specs = [pl.BlockSpec((1, SEQ, D_MODEL), lambda e: (e, 0, 0)),
                pl.BlockSpec((1, SEQ, LANES), lambda e: (e, 0, 0))]
    if with_ctx:
        in_specs += [pl.BlockSpec((1, CTX_LEN, D_MODEL), lambda e: (e, 0, 0)),
                     pl.BlockSpec((1, CTX_LEN, LANES), lambda e: (e, 0, 0))]
    in_specs += [pl.BlockSpec((1, 1, D_MODEL, D_EXPERT), lambda e: (layer, e, 0, 0)),
                 pl.BlockSpec((1, 1, D_MODEL, D_EXPERT), lambda e: (layer, e, 0, 0)),
                 pl.BlockSpec((1, 1, D_EXPERT, D_MODEL), lambda e: (layer, e, 0, 0))]
    return pl.pallas_call(
        functools.partial(_ffn_kernel, with_ctx=with_ctx),
        out_shape=jax.ShapeDtypeStruct((N_EXPERTS, NT, D_MODEL), BF16),
        grid=(N_EXPERTS,),
        in_specs=in_specs,
        out_specs=pl.BlockSpec((1, NT, D_MODEL), lambda e: (e, 0, 0)),
        compiler_params=_params(("arbitrary",)),
        name="expert_ffn",
    )(*ins)


COMBINE_ROWS = 1024


def _scatter_add(col, ys, n_slots):
    lane = lax.broadcasted_iota(jnp.int32, (1, n_slots), 1).astype(F32)
    hot = [_indicator(col[:, e:e + 1] == lane) for e in range(N_EXPERTS)]
    return _dot(jnp.concatenate(hot, axis=-1), jnp.concatenate(ys, axis=0))


def _combine_latent_kernel(*refs, final):
    if final:
        x_ref, col_ref, y_ref, mod_ref, fn_ref, o_ref = refs
    else:
        x_ref, col_ref, y_ref, mod_ref, o_ref = refs
    g2 = mod_ref[0][:, 5 * D_MODEL:6 * D_MODEL]
    acc = _scatter_add(col_ref[0], [y_ref[e] for e in range(N_EXPERTS)], CAP_LAT)
    x = x_ref[0] + g2 * acc
    if final:
        x = _rms(x) * fn_ref[...]
    o_ref[0] = x


def _combine_ctx_kernel(x_ref, col_ref, y_ref, mod_ref, prev_ref, o_ref):
    del prev_ref
    g2 = mod_ref[0][:, 5 * D_MODEL:6 * D_MODEL]
    pad = jnp.zeros((LANES - CAP_CTX, D_MODEL), BF16)
    for b in range(BATCH):
        ys = [jnp.concatenate([y_ref[e, b * CAP_CTX:(b + 1) * CAP_CTX, :], pad], axis=0)
              for e in range(N_EXPERTS)]
        o_ref[b] = x_ref[b] + g2 * _scatter_add(col_ref[b], ys, LANES)


def _combine(xall, slot_cols, y, mod_l, with_ctx, final_norm=None):
    final = final_norm is not None
    n_rows = SEQ if final else NT_PAD
    tok = lambda n: pl.BlockSpec((1, COMBINE_ROWS, n), lambda b, t: (b, t, 0))
    ins = [xall, slot_cols, y, mod_l]
    in_specs = [tok(D_MODEL), tok(LANES),
                pl.BlockSpec((N_EXPERTS, CAP_LAT, D_MODEL), lambda b, t: (0, b, 0)),
                pl.BlockSpec((1, 1, 6 * D_MODEL), lambda b, t: (b, 0, 0))]
    if final:
        ins.append(final_norm)
        in_specs.append(pl.BlockSpec((1, D_MODEL), lambda b, t: (0, 0)))
    out = pl.pallas_call(
        functools.partial(_combine_latent_kernel, final=final),
        out_shape=jax.ShapeDtypeStruct((BATCH, n_rows, D_MODEL), F32),
        grid=(BATCH, SEQ // COMBINE_ROWS),
        in_specs=in_specs,
        out_specs=tok(D_MODEL),
        compiler_params=_params(("arbitrary", "arbitrary")),
        name="expert_combine",
    )(*ins)
    if not with_ctx:
        return out
    ctx_blk = SEQ // CTX_LEN
    ctx_rows = lambda n: pl.BlockSpec((BATCH, CTX_LEN, n), lambda i: (0, ctx_blk, 0))
    return pl.pallas_call(
        _combine_ctx_kernel,
        out_shape=jax.ShapeDtypeStruct((BATCH, NT_PAD, D_MODEL), F32),
        grid=(1,),
        in_specs=[ctx_rows(D_MODEL), ctx_rows(LANES),
                  pl.BlockSpec((N_EXPERTS, CTX_LEN, D_MODEL), lambda i: (0, ctx_blk, 0)),
                  pl.BlockSpec((1, 1, 6 * D_MODEL), lambda i: (BATCH, 0, 0)),
                  pl.BlockSpec(memory_space=pl.ANY)],
        out_specs=ctx_rows(D_MODEL),
        input_output_aliases={4: 0},
        compiler_params=_params(("arbitrary",)),
        name="expert_combine_ctx",
    )(xall, slot_cols, y, mod_l, out)


def _rope_tables():
    t = jnp.arange(SEQ)
    pos = jnp.stack([t // GRID_W, t % GRID_W], axis=0)

    def table(n_f, lanes_per_unit):
        inv = ROPE_BASE ** (-jnp.arange(n_f, dtype=F32) / n_f)
        d = np.arange(lanes_per_unit)
        axis, second, f = d // (2 * n_f), (d // n_f) % 2, d % n_f
        ang = pos[axis].T.astype(F32) * inv[f][None, :]
        cos = jnp.cos(ang)
        sin = jnp.sin(ang) * jnp.where(second == 1, 1.0, -1.0)[None, :]
        return cos, sin

    def finish(cos, sin, scale):
        cos = jnp.concatenate([cos, jnp.ones((NT_PAD - SEQ, LANES), F32)], axis=0)
        sin = jnp.concatenate([sin, jnp.zeros((NT_PAD - SEQ, LANES), F32)], axis=0)
        scale = scale * LOG2_E
        return [cos, sin, cos * scale, sin * scale]

    cos_a, sin_a = table(HEAD_DIM // 4, HEAD_DIM)
    tabs_a = finish(jnp.tile(cos_a, (1, 2)), jnp.tile(sin_a, (1, 2)), HEAD_DIM ** -0.5)
    cos_8, sin_8 = table(B_ROPE // 4, B_ROPE)
    tabs_8 = finish(jnp.tile(cos_8, (1, 4)), jnp.tile(sin_8, (1, 4)), C_DH ** -0.5)
    ones = jnp.ones((SEQ, B_NOPE), F32)
    pad1 = jnp.ones((SEQ, LANES - B_NOPE - B_ROPE), F32)
    cos_b = jnp.concatenate([ones, cos_8, pad1], axis=1)
    sin_b = jnp.concatenate([0 * ones, sin_8, 0 * pad1], axis=1)
    tabs_b = finish(cos_b, sin_b, (B_NOPE + B_ROPE) ** -0.5)
    return tabs_a + tabs_8 + tabs_b


def _window_bias():
    qi = np.arange(WIN)[:, None]
    kj = np.arange(A_LOCAL)[None, :]
    out = np.zeros((3, WIN, A_LOCAL + CTX_LEN), np.float32)
    for v in range(3):
        out[v, :, :A_LOCAL] = np.where(np.abs(kj - v * WIN - qi) <= WIN, 0.0, NEG_INF)
    return jnp.asarray(out)


def _prep_weights(w_in, w_uq, w_ukv, w_out, w_router):
    offs = np.cumsum((0, 512, 128, 128, 256, 128, 32, 256, 256, 256))
    seg = lambda i: w_in[:, :, offs[i]:offs[i + 1]]
    wq = seg(0).reshape(DEPTH, D_MODEL, A_KV_HEADS, A_GROUP, HEAD_DIM)
    wq = jnp.swapaxes(wq, 2, 3).reshape(DEPTH, D_MODEL, A_HEADS * HEAD_DIM)
    kr = jnp.pad(seg(5), ((0, 0), (0, 0), (B_NOPE, LANES - B_NOPE - B_ROPE)))
    w = jnp.concatenate([wq, seg(1), seg(2), seg(3), seg(4), kr,
                         seg(6), seg(7), seg(8)], axis=-1).astype(BF16)
    wuq = w_uq.reshape(DEPTH, B_Q_RANK, B_HEADS, B_NOPE + B_ROPE)
    wuq = jnp.pad(wuq, ((0, 0), (0, 0), (0, 0), (0, LANES - B_NOPE - B_ROPE)))
    wuq = wuq.reshape(DEPTH, B_Q_RANK, B_HEADS * LANES).astype(BF16)
    wukv = w_ukv.reshape(DEPTH, B_KV_RANK, B_HEADS, B_NOPE + B_V)
    wk = jnp.pad(wukv[..., :B_NOPE], ((0, 0), (0, 0), (0, 0), (0, LANES - B_NOPE)))
    wukv = jnp.concatenate([wk.reshape(DEPTH, B_KV_RANK, B_HEADS * LANES),
                            wukv[..., B_NOPE:].reshape(DEPTH, B_KV_RANK, B_HEADS * B_V)],
                           axis=-1).astype(BF16)
    wo_a = w_out[:, :A_HEADS * HEAD_DIM].reshape(DEPTH, A_KV_HEADS, A_GROUP, HEAD_DIM, D_MODEL)
    wo_a = jnp.swapaxes(wo_a, 1, 2).reshape(DEPTH, A_HEADS * HEAD_DIM, D_MODEL)
    wo = jnp.concatenate([wo_a, w_out[:, A_HEADS * HEAD_DIM:]], axis=1).astype(BF16)
    wr = jnp.pad(w_router, ((0, 0), (0, 0), (0, LANES - N_EXPERTS)))
    return w, wuq, wukv, wo, wr


def kernel(x, c, ctx, c_ctx, norm1, norm2, w_ada, b_ada, w_in, sink, mla_q_norm, w_uq, mla_kv_norm, w_ukv,
           lam_q1, lam_k1, lam_q2, lam_k2, diff_norm, w_out, w_router, w_gate, w_up, w_down, final_norm):
    xall = jnp.concatenate([x, ctx, jnp.zeros((BATCH, NT_PAD - NT, D_MODEL), F32)], axis=1)
    cc = jnp.concatenate([c, c_ctx[None, :], jnp.zeros((16 - BATCH - 1, D_MODEL), F32)], axis=0)
    mod = _modulation(cc, w_ada, b_ada).reshape(DEPTH, 16, 1, 6 * D_MODEL)
    tabs = _rope_tables()
    bias = _window_bias()
    tri = jnp.asarray(np.triu(np.ones((LANES, LANES), np.float32), k=1), BF16)
    w, wuq, wukv, wo, wr = _prep_weights(w_in, w_uq, w_ukv, w_out, w_router)
    sub_norm = jnp.tile(diff_norm, (1, C_HEADS)).reshape(DEPTH, 1, C_HEADS * C_V)
    row = lambda a, l: a[l].reshape(1, -1)

    for l in range(DEPTH):
        need_ctx = l < DEPTH - 1
        qa, ka, va, qb, kb, vb, qd, kd, vd = _inproj(
            xall, mod[l], row(norm1, l), w, row(mla_q_norm, l), wuq[l], row(mla_kv_norm, l), wukv[l], tabs, l)

        oa = _attn_a_latent(sink[l], qa, ka, va, bias)
        ob = _full_attention(_attn_b_kernel, "attn_mla", MLA_TQ, qb, kb, vb, [], lambda n: [],
                             256, 512, 512, 512)
        lam_init = 0.8 - 0.6 * math.exp(-0.3 * l)
        c_kernel = functools.partial(_attn_c_kernel, lam_init=lam_init)
        c_extra = [row(lam_q1, l), row(lam_k1, l), row(lam_q2, l), row(lam_k2, l), sub_norm[l]]

        def c_specs(n_grid, c_extra=c_extra):
            zmap = (lambda b, t: (0, 0)) if n_grid == 2 else (lambda b: (0, 0))
            return [pl.BlockSpec(a.shape, zmap) for a in c_extra]

        od = _full_attention(c_kernel, "attn_diff", DIFF_TQ, qd, kd, vd, c_extra, c_specs, 256, 256, 256, 512)
        if need_ctx:
            oa = _attn_a_ctx(sink[l], qa, ka, va, oa)
            ob = _full_attention_ctx(_attn_b_kernel, "attn_mla", qb, kb, vb, [], lambda n: [],
                                     256, 512, 512, 512, ob)
            od = _full_attention_ctx(c_kernel, "attn_diff", qd, kd, vd, c_extra, c_specs,
                                     256, 256, 256, 512, od)

        xall, h2, aff = _outproj(xall, oa, ob, od, wo, mod[l], row(norm2, l), wr[l], need_ctx, l)
        slot_rows, aff_rows, slot_cols = _route(aff, tri, need_ctx)
        gathered = _gather(slot_rows, aff_rows, h2, need_ctx)
        y = _ffn(gathered, w_gate, w_up, w_down, l)
        xall = _combine(xall, slot_cols, y, mod[l], need_ctx,
                        final_norm=None if need_ctx else final_norm.reshape(1, D_MODEL))
    return xall
```

```python
import functools
import math

import numpy as np
import jax
import jax.numpy as jnp
from jax import lax
from jax.experimental import pallas as pl
from jax.experimental.pallas import tpu as pltpu

D_MODEL = 1024
BATCH = 8
SEQ = 2048
DEPTH = 4
CTX_LEN = 256
NT = SEQ + CTX_LEN
GRID_W = 64
HEAD_DIM = 64
ROPE_BASE = 10000.0
EPS = 1e-6
NEG_INF = -1e30
LOG2_E = math.log2(math.e)

WIN = 128
A_HEADS = 8
A_KV_HEADS = 2
A_GROUP = A_HEADS // A_KV_HEADS
B_HEADS = 4
B_NOPE = 64
B_ROPE = 32
B_V = 64
B_Q_RANK = 256
B_KV_RANK = 128
C_HEADS = 4
C_DH = 32
C_V = 2 * C_DH
N_EXPERTS = 16
EC_FACTOR = 2
D_EXPERT = 512
CAP_LAT = EC_FACTOR * SEQ // N_EXPERTS
CAP_CTX = EC_FACTOR * CTX_LEN // N_EXPERTS

LANES = 128
TOK_TILE = 256
NT_PAD = SEQ + TOK_TILE
N_TILES = NT_PAD // TOK_TILE
N_LAT_TILES = SEQ // TOK_TILE
STEP_CHUNKS = 3
LAT_STEP_CHUNKS = 2
FFN_TILE = 1024
FFN_LAT_TILES = SEQ // FFN_TILE
VMEM_LIMIT = 56 * 1024 * 1024

F32 = jnp.float32
BF16 = jnp.bfloat16

_NT_DIMS = (((1,), (1,)), ((), ()))


def _params(sem):
    return pltpu.CompilerParams(dimension_semantics=sem, vmem_limit_bytes=VMEM_LIMIT)


def _dot(a, b):
    return jnp.dot(a, b, preferred_element_type=F32)


def _dot_nt(a, b):
    return lax.dot_general(a, b, _NT_DIMS, preferred_element_type=F32)


def _split_bf16(v):
    hi = v.astype(BF16)
    lo = (v - hi.astype(F32)).astype(BF16)
    return hi, lo


def _dot3(a, b):
    a_hi, a_lo = _split_bf16(a)
    b_hi, b_lo = _split_bf16(b)
    return _dot(a_hi, b_hi) + _dot(a_lo, b_hi) + _dot(a_hi, b_lo)


def _indicator(cond):
    return jnp.where(cond, 1.0, 0.0).astype(BF16)


def _rms(v):
    return v * lax.rsqrt(jnp.mean(v * v, axis=-1, keepdims=True) + EPS)


def _softmax_pv(s, v_blk, head):
    m = jnp.max(s, axis=-1, keepdims=True)
    ol = _dot(jnp.exp2(s - m).astype(BF16), v_blk)
    ones_lane = 64 if head % 2 == 0 else 0
    return ol / ol[:, ones_lane:ones_lane + 1]


def _merge_head_pairs(per_head):
    lane = lax.broadcasted_iota(jnp.int32, (1, LANES), 1)
    blocks = [jnp.where(lane < 64, per_head[j], per_head[j + 1]) for j in range(0, len(per_head), 2)]
    return jnp.concatenate(blocks, axis=-1)


def _softmax_rows(s):
    m = jnp.max(s, axis=-1, keepdims=True)
    e = jnp.exp(s - m)
    return e / jnp.sum(e, axis=-1, keepdims=True)


def _mod_kernel(c_ref, w_ref, b_ref, o_ref):
    c = c_ref[...]
    s = c / (1.0 + jnp.exp(-c))
    o_ref[0] = _dot3(s, w_ref[0]) + b_ref[0]


def _modulation(cc, w_ada, b_ada):
    tn = 1536
    return pl.pallas_call(
        _mod_kernel,
        out_shape=jax.ShapeDtypeStruct((DEPTH, 16, 6 * D_MODEL), F32),
        grid=(DEPTH, 6 * D_MODEL // tn),
        in_specs=[
            pl.BlockSpec((16, D_MODEL), lambda l, j: (0, 0)),
            pl.BlockSpec((1, D_MODEL, tn), lambda l, j: (l, 0, j)),
            pl.BlockSpec((1, 1, tn), lambda l, j: (l, 0, j)),
        ],
        out_specs=pl.BlockSpec((1, 16, tn), lambda l, j: (l, 0, j)),
        compiler_params=_params(("arbitrary", "arbitrary")),
        name="adaln_mod",
    )(cc, w_ada, b_ada.reshape(DEPTH, 1, 6 * D_MODEL))


def _rope(p, cos, sin, half):
    lane = lax.broadcasted_iota(jnp.int32, (1, LANES), 1)
    first = (lane // half) % 2 == 0
    outs = []
    for j in range(p.shape[1] // LANES):
        blk = p[:, j * LANES:(j + 1) * LANES]
        partner = jnp.where(first, pltpu.roll(blk, LANES - half, 1), pltpu.roll(blk, half, 1))
        outs.append(blk * cos + partner * sin)
    return outs[0] if len(outs) == 1 else jnp.concatenate(outs, axis=-1)


def _per_head_with_ones(v):
    lane = lax.broadcasted_iota(jnp.int32, (1, LANES), 1)
    outs = []
    for j in range(v.shape[1] // LANES):
        blk = v[:, j * LANES:(j + 1) * LANES]
        outs.append(jnp.where(lane < 64, blk, jnp.where(lane == 64, 1.0, 0.0)))
        outs.append(jnp.where(lane >= 64, blk, jnp.where(lane == 0, 1.0, 0.0)))
    return jnp.concatenate(outs, axis=-1)


def _chunk_mod(modb_ref, modc_ref, j, n_chunks):
    mod = modb_ref[0]
    if n_chunks == STEP_CHUNKS and j == n_chunks - 1:
        mod = jnp.where(pl.program_id(1) == N_TILES // STEP_CHUNKS - 1, modc_ref[0], mod)
    return mod


def _inproj_kernel(x_ref, modb_ref, modc_ref, g1_ref, w_ref, qn_ref, wuq_ref, kvn_ref, wukv_ref,
                   ca_ref, sa_ref, caq_ref, saq_ref, c8_ref, s8_ref, c8q_ref, s8q_ref,
                   cb_ref, sb_ref, cbq_ref, sbq_ref,
                   qa_ref, ka_ref, va_ref, qb_ref, kb_ref, vb_ref, qd_ref, kd_ref, vd_ref):
    for j in range(STEP_CHUNKS):
        rows = slice(j * TOK_TILE, (j + 1) * TOK_TILE)
        x = x_ref[0, rows, :]
        mod = _chunk_mod(modb_ref, modc_ref, j, STEP_CHUNKS)
        sh1 = mod[:, 0:D_MODEL]
        sc1 = mod[:, D_MODEL:2 * D_MODEL]
        h = (_rms(x) * g1_ref[...]) * (1.0 + sc1) + sh1
        p = _dot(h.astype(BF16), w_ref[0])

        qa_ref[0, rows, :] = _rope(p[:, 0:512], caq_ref[rows, :], saq_ref[rows, :], 16).astype(BF16)
        ka_ref[0, rows, :] = _rope(p[:, 512:640], ca_ref[rows, :], sa_ref[rows, :], 16).astype(BF16)
        lane = lax.broadcasted_iota(jnp.int32, (TOK_TILE, LANES), 1)
        va_ref[0, rows, :] = jnp.concatenate(
            [p[:, 640:768], jnp.where(lane == 0, 1.0, 0.0)], axis=-1).astype(BF16)

        cq = (_rms(p[:, 768:1024]) * qn_ref[...]).astype(BF16)
        qb = _dot(cq, wuq_ref[...])
        qb_ref[0, rows, :] = _rope(qb, cbq_ref[rows, :], sbq_ref[rows, :], 8).astype(BF16)
        ckv = (_rms(p[:, 1024:1152]) * kvn_ref[...]).astype(BF16)
        kv = _dot(ckv, wukv_ref[...])
        kr = _rope(p[:, 1152:1280], cb_ref[rows, :], sb_ref[rows, :], 8)
        kb_ref[0, rows, :] = (kv[:, 0:512] + jnp.concatenate([kr] * B_HEADS, axis=-1)).astype(BF16)
        vb_ref[0, rows, :] = _per_head_with_ones(kv[:, 512:768]).astype(BF16)

        qd_ref[0, rows, :] = _rope(p[:, 1280:1536], c8q_ref[rows, :], s8q_ref[rows, :], 8).astype(BF16)
        kd_ref[0, rows, :] = _rope(p[:, 1536:1792], c8_ref[rows, :], s8_ref[rows, :], 8).astype(BF16)
        vd_ref[0, rows, :] = _per_head_with_ones(p[:, 1792:2048]).astype(BF16)


def _mod_specs():
    return [pl.BlockSpec((1, 1, 6 * D_MODEL), lambda b, t: (b, 0, 0)),
            pl.BlockSpec((1, 1, 6 * D_MODEL), lambda b, t: (BATCH, 0, 0))]


def _inproj(xall, mod_l, g1, w, qn, wuq, kvn, wukv, tabs, layer):
    full = lambda shape: pl.BlockSpec(shape, lambda b, t: (0,) * len(shape))
    step_rows = STEP_CHUNKS * TOK_TILE
    tab_spec = pl.BlockSpec((step_rows, LANES), lambda b, t: (t, 0))
    widths = (512, 128, 256, 512, 512, 512, 256, 256, 512)
    return pl.pallas_call(
        _inproj_kernel,
        out_shape=[jax.ShapeDtypeStruct((BATCH, NT_PAD, n), BF16) for n in widths],
        grid=(BATCH, N_TILES // STEP_CHUNKS),
        in_specs=[pl.BlockSpec((1, step_rows, D_MODEL), lambda b, t: (b, t, 0))] + _mod_specs() + [
            full((1, D_MODEL)),
            pl.BlockSpec((1, D_MODEL, 2048), lambda b, t: (layer, 0, 0)),
            full((1, B_Q_RANK)),
            full((B_Q_RANK, 512)),
            full((1, B_KV_RANK)),
            full((B_KV_RANK, 768)),
        ] + [tab_spec] * 12,
        out_specs=[pl.BlockSpec((1, step_rows, n), lambda b, t: (b, t, 0)) for n in widths],
        compiler_params=_params(("arbitrary", "arbitrary")),
        name="in_projection",
    )(xall, mod_l, mod_l, g1, w, qn, wuq, kvn, wukv, *tabs)


def _window_heads(q, kcat, vcat, bias, sink_ref):
    tq = q.shape[0]
    lane = lax.broadcasted_iota(jnp.int32, (1, LANES), 1)
    acc = [jnp.zeros((tq, LANES), F32) for _ in range(A_GROUP)]
    for g in range(A_KV_HEADS):
        in_half = (lane < HEAD_DIM) if g == 0 else (lane >= HEAD_DIM)
        keep = _indicator(in_half)
        qs = jnp.concatenate([q[:, j * LANES:(j + 1) * LANES] * keep for j in range(A_GROUP)], axis=0)
        s = _dot_nt(qs, kcat)
        if bias is not None:
            s = s + jnp.concatenate([bias] * A_GROUP, axis=0)
        sink = jnp.concatenate(
            [jnp.full((tq, 1), sink_ref[g * A_GROUP + j] * LOG2_E, F32) for j in range(A_GROUP)], axis=0)
        m = jnp.maximum(jnp.max(s, axis=-1, keepdims=True), sink)
        ol = _dot(jnp.exp2(s - m).astype(BF16), vcat)
        o = ol[:, 0:LANES] / (ol[:, LANES:LANES + 1] + jnp.exp2(sink - m))
        for j in range(A_GROUP):
            acc[j] = acc[j] + jnp.where(in_half, o[j * tq:(j + 1) * tq], 0.0)
    return jnp.concatenate(acc, axis=-1).astype(BF16)


A_STEP_BLOCKS = 8
A_LOCAL = 3 * WIN


def _attn_a_latent_kernel(sink_ref, q_ref, k_ref, v_ref, bias_ref, o_ref):
    step = pl.program_id(1)
    last = SEQ // WIN - 1
    k_ctx = k_ref[0, SEQ:NT, :]
    v_ctx = v_ref[0, SEQ:NT, :]
    for j in range(A_STEP_BLOCKS):
        n = step * A_STEP_BLOCKS + j
        start = pl.multiple_of(jnp.clip((n - 1) * WIN, 0, SEQ - A_LOCAL), WIN)
        variant = jnp.where(n == 0, 0, jnp.where(n == last, 2, 1))
        kcat = jnp.concatenate([k_ref[0, pl.ds(start, A_LOCAL), :], k_ctx], axis=0)
        vcat = jnp.concatenate([v_ref[0, pl.ds(start, A_LOCAL), :], v_ctx], axis=0)
        rows = slice(j * WIN, (j + 1) * WIN)
        o_ref[0, rows, :] = _window_heads(q_ref[0, rows, :], kcat, vcat, bias_ref[variant], sink_ref)


def _attn_a_latent(sink, qa, ka, va, bias):
    tq = A_STEP_BLOCKS * WIN
    return pl.pallas_call(
        _attn_a_latent_kernel,
        out_shape=jax.ShapeDtypeStruct((BATCH, NT_PAD, 512), BF16),
        grid=(BATCH, SEQ // tq),
        in_specs=[pl.BlockSpec(memory_space=pltpu.SMEM),
                  pl.BlockSpec((1, tq, 512), lambda b, n: (b, n, 0)),
                  pl.BlockSpec((1, NT_PAD, LANES), lambda b, n: (b, 0, 0)),
                  pl.BlockSpec((1, NT_PAD, 2 * LANES), lambda b, n: (b, 0, 0)),
                  pl.BlockSpec((3, WIN, A_LOCAL + CTX_LEN), lambda b, n: (0, 0, 0))],
        out_specs=pl.BlockSpec((1, tq, 512), lambda b, n: (b, n, 0)),
        compiler_params=_params(("arbitrary", "arbitrary")),
        name="attn_window_latent",
    )(sink, qa, ka, va, bias)


def _attn_a_ctx(sink, qa, ka, va, prev_out):
    ctx_blk = SEQ // CTX_LEN
    spec = lambda n: pl.BlockSpec((BATCH, CTX_LEN, n), lambda i: (0, ctx_blk, 0))
    tile = lambda n: pl.BlockSpec((BATCH, TOK_TILE, n), lambda i: (0, N_LAT_TILES, 0))

    def body(sink_ref, q_ref, k_ref, v_ref, prev_ref, o_ref):
        del prev_ref
        for sample in range(BATCH):
            o_ref[sample] = _window_heads(q_ref[sample], k_ref[sample], v_ref[sample], None, sink_ref)

    return pl.pallas_call(
        body,
        out_shape=jax.ShapeDtypeStruct((BATCH, NT_PAD, 512), BF16),
        grid=(1,),
        in_specs=[pl.BlockSpec(memory_space=pltpu.SMEM), tile(512), spec(LANES), spec(2 * LANES),
                  pl.BlockSpec(memory_space=pl.ANY)],
        out_specs=tile(512),
        input_output_aliases={4: 0},
        compiler_params=_params(("arbitrary",)),
        name="attn_window_ctx",
    )(sink, qa, ka, va, prev_out)


def _key_rows(ref, sample):
    return ref[sample, 0:NT, :] if ref.shape[1] == NT_PAD else ref[sample]


MLA_TQ = 2048
MLA_SUB = 512
DIFF_TQ = 1024
DIFF_SUB = 256


def _attn_b_kernel(q_ref, k_ref, v_ref, o_ref):
    for sample in range(q_ref.shape[0]):
        k = _key_rows(k_ref, sample)
        v = _key_rows(v_ref, sample)
        sub_rows = min(MLA_SUB, q_ref.shape[1])
        for sub in range(q_ref.shape[1] // sub_rows):
            rows = slice(sub * sub_rows, (sub + 1) * sub_rows)
            q = q_ref[sample, rows, :]
            outs = []
            for h in range(B_HEADS):
                s = _dot_nt(q[:, h * LANES:(h + 1) * LANES], k[:, h * LANES:(h + 1) * LANES])
                outs.append(_softmax_pv(s, v[:, h * LANES:(h + 1) * LANES], h))
            o_ref[sample, rows, :] = _merge_head_pairs(outs).astype(BF16)


def _full_attention(kernel, name, tq, q, k, v, extra, extra_specs, out_width, q_width, k_width, v_width):
    out = pl.pallas_call(
        kernel,
        out_shape=jax.ShapeDtypeStruct((BATCH, NT_PAD, out_width), BF16),
        grid=(BATCH, SEQ // tq),
        in_specs=[pl.BlockSpec((1, tq, q_width), lambda b, t: (b, t, 0)),
                  pl.BlockSpec((1, NT_PAD, k_width), lambda b, t: (b, 0, 0)),
                  pl.BlockSpec((1, NT_PAD, v_width), lambda b, t: (b, 0, 0))] + extra_specs(2),
        out_specs=pl.BlockSpec((1, tq, out_width), lambda b, t: (b, t, 0)),
        compiler_params=_params(("arbitrary", "arbitrary")),
        name=name + "_latent",
    )(q, k, v, *extra)
    return out


def _full_attention_ctx(kernel, name, q, k, v, extra, extra_specs, out_width, q_width, k_width, v_width,
                        prev_out):
    ctx_blk = SEQ // CTX_LEN
    n_extra = len(extra)

    def body(*refs):
        ins = refs[:3 + n_extra]
        kernel(*ins, refs[-1])

    spec = lambda n: pl.BlockSpec((BATCH, CTX_LEN, n), lambda i: (0, ctx_blk, 0))
    tile = lambda n: pl.BlockSpec((BATCH, TOK_TILE, n), lambda i: (0, N_LAT_TILES, 0))
    return pl.pallas_call(
        body,
        out_shape=jax.ShapeDtypeStruct((BATCH, NT_PAD, out_width), BF16),
        grid=(1,),
        in_specs=[tile(q_width), spec(k_width), spec(v_width)] + extra_specs(1)
                 + [pl.BlockSpec(memory_space=pl.ANY)],
        out_specs=tile(out_width),
        input_output_aliases={3 + n_extra: 0},
        compiler_params=_params(("arbitrary",)),
        name=name + "_ctx",
    )(q, k, v, *extra, prev_out)


def _attn_c_kernel(q_ref, k_ref, v_ref, lq1_ref, lk1_ref, lq2_ref, lk2_ref, sn_ref, o_ref, *, lam_init):
    tq = min(DIFF_SUB, q_ref.shape[1])
    lam = (jnp.exp(jnp.sum(lq1_ref[...] * lk1_ref[...], axis=-1, keepdims=True))
           - jnp.exp(jnp.sum(lq2_ref[...] * lk2_ref[...], axis=-1, keepdims=True)) + lam_init)
    lane = lax.broadcasted_iota(jnp.int32, (1, C_HEADS * C_V), 1)
    units = [(sample, sub) for sample in range(q_ref.shape[0]) for sub in range(q_ref.shape[1] // tq)]
    for sample, sub in units:
        k = _key_rows(k_ref, sample)
        v = _key_rows(v_ref, sample)
        rows = slice(sub * tq, (sub + 1) * tq)
        q = q_ref[sample, rows, :]
        outs = []
        for h in range(C_HEADS):
            lo = h * C_V
            q1 = q * _indicator((lane >= lo) & (lane < lo + C_DH))
            q2 = q * _indicator((lane >= lo + C_DH) & (lane < lo + C_V))
            s = _dot_nt(jnp.concatenate([q1, q2], axis=0), k)
            pv = _softmax_pv(s, v[:, h * LANES:(h + 1) * LANES], h)
            outs.append(pv[0:tq] - lam * pv[tq:2 * tq])
        acc = _merge_head_pairs(outs)
        sq = acc * acc
        inv = jnp.zeros((tq, C_HEADS * C_V), F32)
        for h in range(C_HEADS):
            in_head = (lane >= h * C_V) & (lane < (h + 1) * C_V)
            ms = jnp.sum(jnp.where(in_head, sq, 0.0), axis=-1, keepdims=True) * (1.0 / C_V)
            inv = inv + jnp.where(in_head, lax.rsqrt(ms + EPS), 0.0)
        o_ref[sample, rows, :] = ((acc * inv) * sn_ref[...] * (1.0 - lam_init)).astype(BF16)


def _outproj_kernel(x_ref, oa_ref, ob_ref, od_ref, w_ref, modb_ref, modc_ref, g2_ref, wr_ref,
                    xo_ref, h2_ref, aff_ref, *, n_chunks):
    w_hi, w_lo = _split_bf16(wr_ref[...])
    w_router = jnp.concatenate([w_hi, w_lo], axis=-1)
    lane = lax.broadcasted_iota(jnp.int32, (1, LANES), 1)
    mixed = _dot(jnp.concatenate([oa_ref[0], ob_ref[0], od_ref[0]], axis=-1), w_ref[0])
    for j in range(n_chunks):
        rows = slice(j * TOK_TILE, (j + 1) * TOK_TILE)
        mod = _chunk_mod(modb_ref, modc_ref, j, n_chunks)
        g1 = mod[:, 2 * D_MODEL:3 * D_MODEL]
        sh2 = mod[:, 3 * D_MODEL:4 * D_MODEL]
        sc2 = mod[:, 4 * D_MODEL:5 * D_MODEL]
        x = x_ref[0, rows, :] + g1 * mixed[rows]
        xo_ref[0, rows, :] = x
        h2 = (_rms(x) * g2_ref[...]) * (1.0 + sc2) + sh2
        h_hi, h_lo = _split_bf16(h2)
        h2_ref[0, rows, :] = h_hi
        prod = _dot(jnp.concatenate([h_hi, h_lo], axis=0), w_router)
        logits = (prod[0:TOK_TILE, 0:LANES] + prod[0:TOK_TILE, LANES:2 * LANES]
                  + prod[TOK_TILE:2 * TOK_TILE, 0:LANES])
        aff_ref[0, rows, :] = _softmax_rows(jnp.where(lane < N_EXPERTS, logits, NEG_INF))


def _outproj(xall, oa, ob, od, w, mod_l, g2, wr, with_ctx, layer):
    n_chunks = STEP_CHUNKS if with_ctx else LAT_STEP_CHUNKS
    n_steps = (N_TILES if with_ctx else N_LAT_TILES) // n_chunks
    full = lambda shape: pl.BlockSpec(shape, lambda b, t: (0,) * len(shape))
    tok = lambda n: pl.BlockSpec((1, n_chunks * TOK_TILE, n), lambda b, t: (b, t, 0))
    return pl.pallas_call(
        functools.partial(_outproj_kernel, n_chunks=n_chunks),
        out_shape=[jax.ShapeDtypeStruct((BATCH, NT_PAD, D_MODEL), F32),
                   jax.ShapeDtypeStruct((BATCH, NT_PAD, D_MODEL), BF16),
                   jax.ShapeDtypeStruct((BATCH, NT_PAD, LANES), F32)],
        grid=(BATCH, n_steps),
        in_specs=[tok(D_MODEL), tok(512), tok(256), tok(256),
                  pl.BlockSpec((1, D_MODEL, D_MODEL), lambda b, t: (layer, 0, 0))] + _mod_specs()
                 + [full((1, D_MODEL)), full((D_MODEL, LANES))],
        out_specs=[tok(D_MODEL), tok(D_MODEL), tok(LANES)],
        compiler_params=_params(("arbitrary", "arbitrary")),
        name="out_projection",
    )(xall, oa, ob, od, w, mod_l, mod_l, g2, wr)


def _prefix_count(flags_f32, tri_ref):
    rows, n = flags_f32.shape
    run = jnp.zeros((rows, 1), F32)
    outs = []
    for c in range(n // LANES):
        blk = flags_f32[:, c * LANES:(c + 1) * LANES]
        outs.append(_dot(blk.astype(BF16), tri_ref[...]) + run)
        run = run + jnp.sum(blk, axis=-1, keepdims=True)
    return jnp.concatenate(outs, axis=-1)


def _select_slots(a, cap, tri_ref):
    bits = pltpu.bitcast(a, jnp.int32)

    def step(i, t):
        cand = t | jnp.left_shift(jnp.int32(1), 30 - i)
        cnt = jnp.sum((bits >= cand).astype(jnp.int32), axis=-1, keepdims=True)
        return jnp.where(cnt >= cap, cand, t)

    thr = lax.fori_loop(0, 31, step, jnp.zeros((a.shape[0], 1), jnp.int32))
    above = bits > thr
    tied = bits == thr
    need = cap - jnp.sum(above.astype(F32), axis=-1, keepdims=True)
    tied_f = tied.astype(F32)
    sel = above | (tied & (_prefix_count(tied_f, tri_ref) < need))
    sel_f = sel.astype(F32)
    return jnp.where(sel, _prefix_count(sel_f, tri_ref), -1.0)


def _route_kernel(aff_ref, tri_ref, row_ref, arow_ref, col_ref, *, with_ctx):
    lat, ctx = [], []
    for b in range(BATCH):
        lat.append(aff_ref[b, 0:SEQ, :].T[0:N_EXPERTS])
        if with_ctx:
            ctx.append(aff_ref[b, SEQ:NT, :].T[0:N_EXPERTS])
    aff_lat = jnp.concatenate(lat, axis=0)
    arow_ref[:, 0:SEQ] = aff_lat
    slot_lat = _select_slots(aff_lat, CAP_LAT, tri_ref)
    if with_ctx:
        aff_ctx = jnp.concatenate(ctx, axis=0)
        arow_ref[:, SEQ:NT] = aff_ctx
        slot_ctx = _select_slots(aff_ctx, CAP_CTX, tri_ref)
        slots = jnp.concatenate([slot_lat, slot_ctx], axis=-1)
    else:
        slots = slot_lat
    n = slots.shape[1]
    row_ref[:, 0:n] = slots
    pad = jnp.full((LANES - N_EXPERTS, n), -1.0, F32)
    for b in range(BATCH):
        blk = jnp.concatenate([slots[b * N_EXPERTS:(b + 1) * N_EXPERTS], pad], axis=0)
        col_ref[b, 0:n, :] = blk.T
        if with_ctx and NT_PAD > NT:
            col_ref[b, NT:NT_PAD, :] = jnp.full((NT_PAD - NT, LANES), -1.0, F32)


def _route(aff, tri, with_ctx):
    vm = pl.BlockSpec(memory_space=pltpu.VMEM)
    return pl.pallas_call(
        functools.partial(_route_kernel, with_ctx=with_ctx),
        out_shape=[jax.ShapeDtypeStruct((BATCH * N_EXPERTS, NT), F32),
                   jax.ShapeDtypeStruct((BATCH * N_EXPERTS, NT), F32),
                   jax.ShapeDtypeStruct((BATCH, NT_PAD, LANES), F32)],
        in_specs=[vm, vm],
        out_specs=[vm, vm, vm],
        compiler_params=pltpu.CompilerParams(vmem_limit_bytes=VMEM_LIMIT),
        name="route_topc",
    )(aff, tri)


GATHER_GROUP = 4


def _gather_kernel(row_ref, arow_ref, h_ref, *out_refs, with_ctx):
    slot = lax.broadcasted_iota(jnp.int32, (CAP_LAT, 1), 0).astype(F32)

    def pick(e0, cols, n_slots, h_rows, x_ref, g_ref):
        hots = []
        for e in range(e0, e0 + GATHER_GROUP):
            hot = jnp.where(row_ref[e:e + 1, cols] == slot[0:n_slots], 1.0, 0.0)
            gate = jnp.sum(hot * arow_ref[e:e + 1, cols], axis=-1, keepdims=True)
            g_ref[e] = jnp.broadcast_to(gate, (n_slots, LANES))
            hots.append(hot.astype(BF16))
        x = _dot(jnp.concatenate(hots, axis=0), h_rows).astype(BF16)
        for i in range(GATHER_GROUP):
            x_ref[e0 + i] = x[i * n_slots:(i + 1) * n_slots]

    for e0 in range(0, N_EXPERTS, GATHER_GROUP):
        pick(e0, slice(0, SEQ), CAP_LAT, h_ref[0, 0:SEQ, :], out_refs[0], out_refs[1])
        if with_ctx:
            pick(e0, slice(SEQ, NT), CAP_CTX, h_ref[0, SEQ:NT, :], out_refs[2], out_refs[3])


def _gather(slot_rows, aff_rows, h2, with_ctx):
    out_shape = [jax.ShapeDtypeStruct((N_EXPERTS, SEQ, D_MODEL), BF16),
                 jax.ShapeDtypeStruct((N_EXPERTS, SEQ, LANES), F32)]
    out_specs = [pl.BlockSpec((N_EXPERTS, CAP_LAT, D_MODEL), lambda b: (0, b, 0)),
                 pl.BlockSpec((N_EXPERTS, CAP_LAT, LANES), lambda b: (0, b, 0))]
    if with_ctx:
        out_shape += [jax.ShapeDtypeStruct((N_EXPERTS, CTX_LEN, D_MODEL), BF16),
                      jax.ShapeDtypeStruct((N_EXPERTS, CTX_LEN, LANES), F32)]
        out_specs += [pl.BlockSpec((N_EXPERTS, CAP_CTX, D_MODEL), lambda b: (0, b, 0)),
                      pl.BlockSpec((N_EXPERTS, CAP_CTX, LANES), lambda b: (0, b, 0))]
    row_spec = pl.BlockSpec((N_EXPERTS, NT), lambda b: (b, 0))
    return pl.pallas_call(
        functools.partial(_gather_kernel, with_ctx=with_ctx),
        out_shape=out_shape,
        grid=(BATCH,),
        in_specs=[row_spec, row_spec, pl.BlockSpec((1, NT_PAD, D_MODEL), lambda b: (b, 0, 0))],
        out_specs=out_specs,
        compiler_params=_params(("arbitrary",)),
        name="expert_gather",
    )(slot_rows, aff_rows, h2)


def _ffn_kernel(*refs, with_ctx):
    if with_ctx:
        xl_ref, gl_ref, xc_ref, gc_ref, wg_ref, wu_ref, wd_ref, y_ref = refs
    else:
        xl_ref, gl_ref, wg_ref, wu_ref, wd_ref, y_ref = refs
    wg = wg_ref[0, 0].astype(BF16)
    wu = wu_ref[0, 0].astype(BF16)
    wd = wd_ref[0, 0].astype(BF16)

    def swiglu(x, slot_gate):
        gate = _dot(x, wg)
        hid = (gate / (1.0 + jnp.exp(-gate))) * _dot(x, wu)
        scale = jnp.concatenate([slot_gate] * (D_MODEL // LANES), axis=-1)
        return (_dot(hid.astype(BF16), wd) * scale).astype(BF16)

    for r in range(FFN_LAT_TILES):
        rows = slice(r * FFN_TILE, (r + 1) * FFN_TILE)
        y_ref[0, rows, :] = swiglu(xl_ref[0, rows, :], gl_ref[0, rows, :])
    if with_ctx:
        y_ref[0, SEQ:NT, :] = swiglu(xc_ref[0], gc_ref[0])


def _ffn(gathered, w_gate, w_up, w_down, layer):
    with_ctx = len(gathered) == 4
    ins = list(gathered) + [w_gate, w_up, w_down]
    in_specs = [pl.BlockSpec((1, SEQ, D_MODEL), lambda e: (e, 0, 0)),
                pl.BlockSpec((1, SEQ, LANES), lambda e: (e, 0, 0))]
    if with_ctx:
        in_specs += [pl.BlockSpec((1, CTX_LEN, D_MODEL), lambda e: (e, 0, 0)),
                     pl.BlockSpec((1, CTX_LEN, LANES), lambda e: (e, 0, 0))]
    in_specs += [pl.BlockSpec((1, 1, D_MODEL, D_EXPERT), lambda e: (layer, e, 0, 0)),
                 pl.BlockSpec((1, 1, D_MODEL, D_EXPERT), lambda e: (layer, e, 0, 0)),
                 pl.BlockSpec((1, 1, D_EXPERT, D_MODEL), lambda e: (layer, e, 0, 0))]
    return pl.pallas_call(
        functools.partial(_ffn_kernel, with_ctx=with_ctx),
        out_shape=jax.ShapeDtypeStruct((N_EXPERTS, NT, D_MODEL), BF16),
        grid=(N_EXPERTS,),
        in_specs=in_specs,
        out_specs=pl.BlockSpec((1, NT, D_MODEL), lambda e: (e, 0, 0)),
        compiler_params=_params(("arbitrary",)),
        name="expert_ffn",
    )(*ins)


COMBINE_ROWS = 1024


def _scatter_add(col, ys, n_slots):
    lane = lax.broadcasted_iota(jnp.int32, (1, n_slots), 1).astype(F32)
    hot = [_indicator(col[:, e:e + 1] == lane) for e in range(N_EXPERTS)]
    return _dot(jnp.concatenate(hot, axis=-1), jnp.concatenate(ys, axis=0))


def _combine_latent_kernel(*refs, final):
    if final:
        x_ref, col_ref, y_ref, mod_ref, fn_ref, o_ref = refs
    else:
        x_ref, col_ref, y_ref, mod_ref, o_ref = refs
    g2 = mod_ref[0][:, 5 * D_MODEL:6 * D_MODEL]
    acc = _scatter_add(col_ref[0], [y_ref[e] for e in range(N_EXPERTS)], CAP_LAT)
    x = x_ref[0] + g2 * acc
    if final:
        x = _rms(x) * fn_ref[...]
    o_ref[0] = x


def _combine_ctx_kernel(x_ref, col_ref, y_ref, mod_ref, prev_ref, o_ref):
    del prev_ref
    g2 = mod_ref[0][:, 5 * D_MODEL:6 * D_MODEL]
    pad = jnp.zeros((LANES - CAP_CTX, D_MODEL), BF16)
    for b in range(BATCH):
        ys = [jnp.concatenate([y_ref[e, b * CAP_CTX:(b + 1) * CAP_CTX, :], pad], axis=0)
              for e in range(N_EXPERTS)]
        o_ref[b] = x_ref[b] + g2 * _scatter_add(col_ref[b], ys, LANES)


def _combine(xall, slot_cols, y, mod_l, with_ctx, final_norm=None):
    final = final_norm is not None
    n_rows = SEQ if final else NT_PAD
    tok = lambda n: pl.BlockSpec((1, COMBINE_ROWS, n), lambda b, t: (b, t, 0))
    ins = [xall, slot_cols, y, mod_l]
    in_specs = [tok(D_MODEL), tok(LANES),
                pl.BlockSpec((N_EXPERTS, CAP_LAT, D_MODEL), lambda b, t: (0, b, 0)),
                pl.BlockSpec((1, 1, 6 * D_MODEL), lambda b, t: (b, 0, 0))]
    if final:
        ins.append(final_norm)
        in_specs.append(pl.BlockSpec((1, D_MODEL), lambda b, t: (0, 0)))
    out = pl.pallas_call(
        functools.partial(_combine_latent_kernel, final=final),
        out_shape=jax.ShapeDtypeStruct((BATCH, n_rows, D_MODEL), F32),
        grid=(BATCH, SEQ // COMBINE_ROWS),
        in_specs=in_specs,
        out_specs=tok(D_MODEL),
        compiler_params=_params(("arbitrary", "arbitrary")),
        name="expert_combine",
    )(*ins)
    if not with_ctx:
        return out
    ctx_blk = SEQ // CTX_LEN
    ctx_rows = lambda n: pl.BlockSpec((BATCH, CTX_LEN, n), lambda i: (0, ctx_blk, 0))
    return pl.pallas_call(
        _combine_ctx_kernel,
        out_shape=jax.ShapeDtypeStruct((BATCH, NT_PAD, D_MODEL), F32),
        grid=(1,),
        in_specs=[ctx_rows(D_MODEL), ctx_rows(LANES),
                  pl.BlockSpec((N_EXPERTS, CTX_LEN, D_MODEL), lambda i: (0, ctx_blk, 0)),
                  pl.BlockSpec((1, 1, 6 * D_MODEL), lambda i: (BATCH, 0, 0)),
                  pl.BlockSpec(memory_space=pl.ANY)],
        out_specs=ctx_rows(D_MODEL),
        input_output_aliases={4: 0},
        compiler_params=_params(("arbitrary",)),
        name="expert_combine_ctx",
    )(xall, slot_cols, y, mod_l, out)


def _rope_tables():
    t = jnp.arange(SEQ)
    pos = jnp.stack([t // GRID_W, t % GRID_W], axis=0)

    def table(n_f, lanes_per_unit):
        inv = ROPE_BASE ** (-jnp.arange(n_f, dtype=F32) / n_f)
        d = np.arange(lanes_per_unit)
        axis, second, f = d // (2 * n_f), (d // n_f) % 2, d % n_f
        ang = pos[axis].T.astype(F32) * inv[f][None, :]
        cos = jnp.cos(ang)
        sin = jnp.sin(ang) * jnp.where(second == 1, 1.0, -1.0)[None, :]
        return cos, sin

    def finish(cos, sin, scale):
        cos = jnp.concatenate([cos, jnp.ones((NT_PAD - SEQ, LANES), F32)], axis=0)
        sin = jnp.concatenate([sin, jnp.zeros((NT_PAD - SEQ, LANES), F32)], axis=0)
        scale = scale * LOG2_E
        return [cos, sin, cos * scale, sin * scale]

    cos_a, sin_a = table(HEAD_DIM // 4, HEAD_DIM)
    tabs_a = finish(jnp.tile(cos_a, (1, 2)), jnp.tile(sin_a, (1, 2)), HEAD_DIM ** -0.5)
    cos_8, sin_8 = table(B_ROPE // 4, B_ROPE)
    tabs_8 = finish(jnp.tile(cos_8, (1, 4)), jnp.tile(sin_8, (1, 4)), C_DH ** -0.5)
    ones = jnp.ones((SEQ, B_NOPE), F32)
    pad1 = jnp.ones((SEQ, LANES - B_NOPE - B_ROPE), F32)
    cos_b = jnp.concatenate([ones, cos_8, pad1], axis=1)
    sin_b = jnp.concatenate([0 * ones, sin_8, 0 * pad1], axis=1)
    tabs_b = finish(cos_b, sin_b, (B_NOPE + B_ROPE) ** -0.5)
    return tabs_a + tabs_8 + tabs_b


def _window_bias():
    qi = np.arange(WIN)[:, None]
    kj = np.arange(A_LOCAL)[None, :]
    out = np.zeros((3, WIN, A_LOCAL + CTX_LEN), np.float32)
    for v in range(3):
        out[v, :, :A_LOCAL] = np.where(np.abs(kj - v * WIN - qi) <= WIN, 0.0, NEG_INF)
    return jnp.asarray(out)


def _prep_weights(w_in, w_uq, w_ukv, w_out, w_router):
    offs = np.cumsum((0, 512, 128, 128, 256, 128, 32, 256, 256, 256))
    seg = lambda i: w_in[:, :, offs[i]:offs[i + 1]]
    wq = seg(0).reshape(DEPTH, D_MODEL, A_KV_HEADS, A_GROUP, HEAD_DIM)
    wq = jnp.swapaxes(wq, 2, 3).reshape(DEPTH, D_MODEL, A_HEADS * HEAD_DIM)
    kr = jnp.pad(seg(5), ((0, 0), (0, 0), (B_NOPE, LANES - B_NOPE - B_ROPE)))
    w = jnp.concatenate([wq, seg(1), seg(2), seg(3), seg(4), kr,
                         seg(6), seg(7), seg(8)], axis=-1).astype(BF16)
    wuq = w_uq.reshape(DEPTH, B_Q_RANK, B_HEADS, B_NOPE + B_ROPE)
    wuq = jnp.pad(wuq, ((0, 0), (0, 0), (0, 0), (0, LANES - B_NOPE - B_ROPE)))
    wuq = wuq.reshape(DEPTH, B_Q_RANK, B_HEADS * LANES).astype(BF16)
    wukv = w_ukv.reshape(DEPTH, B_KV_RANK, B_HEADS, B_NOPE + B_V)
    wk = jnp.pad(wukv[..., :B_NOPE], ((0, 0), (0, 0), (0, 0), (0, LANES - B_NOPE)))
    wukv = jnp.concatenate([wk.reshape(DEPTH, B_KV_RANK, B_HEADS * LANES),
                            wukv[..., B_NOPE:].reshape(DEPTH, B_KV_RANK, B_HEADS * B_V)],
                           axis=-1).astype(BF16)
    wo_a = w_out[:, :A_HEADS * HEAD_DIM].reshape(DEPTH, A_KV_HEADS, A_GROUP, HEAD_DIM, D_MODEL)
    wo_a = jnp.swapaxes(wo_a, 1, 2).reshape(DEPTH, A_HEADS * HEAD_DIM, D_MODEL)
    wo = jnp.concatenate([wo_a, w_out[:, A_HEADS * HEAD_DIM:]], axis=1).astype(BF16)
    wr = jnp.pad(w_router, ((0, 0), (0, 0), (0, LANES - N_EXPERTS)))
    return w, wuq, wukv, wo, wr


def kernel(x, c, ctx, c_ctx, norm1, norm2, w_ada, b_ada, w_in, sink, mla_q_norm, w_uq, mla_kv_norm, w_ukv,
           lam_q1, lam_k1, lam_q2, lam_k2, diff_norm, w_out, w_router, w_gate, w_up, w_down, final_norm):
    xall = jnp.concatenate([x, ctx, jnp.zeros((BATCH, NT_PAD - NT, D_MODEL), F32)], axis=1)
    cc = jnp.concatenate([c, c_ctx[None, :], jnp.zeros((16 - BATCH - 1, D_MODEL), F32)], axis=0)
    mod = _modulation(cc, w_ada, b_ada).reshape(DEPTH, 16, 1, 6 * D_MODEL)
    tabs = _rope_tables()
    bias = _window_bias()
    tri = jnp.asarray(np.triu(np.ones((LANES, LANES), np.float32), k=1), BF16)
    w, wuq, wukv, wo, wr = _prep_weights(w_in, w_uq, w_ukv, w_out, w_router)
    sub_norm = jnp.tile(diff_norm, (1, C_HEADS)).reshape(DEPTH, 1, C_HEADS * C_V)
    row = lambda a, l: a[l].reshape(1, -1)

    for l in range(DEPTH):
        need_ctx = l < DEPTH - 1
        qa, ka, va, qb, kb, vb, qd, kd, vd = _inproj(
            xall, mod[l], row(norm1, l), w, row(mla_q_norm, l), wuq[l], row(mla_kv_norm, l), wukv[l], tabs, l)

        oa = _attn_a_latent(sink[l], qa, ka, va, bias)
        ob = _full_attention(_attn_b_kernel, "attn_mla", MLA_TQ, qb, kb, vb, [], lambda n: [],
                             256, 512, 512, 512)
        lam_init = 0.8 - 0.6 * math.exp(-0.3 * l)
        c_kernel = functools.partial(_attn_c_kernel, lam_init=lam_init)
        c_extra = [row(lam_q1, l), row(lam_k1, l), row(lam_q2, l), row(lam_k2, l), sub_norm[l]]

        def c_specs(n_grid, c_extra=c_extra):
            zmap = (lambda b, t: (0, 0)) if n_grid == 2 else (lambda b: (0, 0))
            return [pl.BlockSpec(a.shape, zmap) for a in c_extra]

        od = _full_attention(c_kernel, "attn_diff", DIFF_TQ, qd, kd, vd, c_extra, c_specs, 256, 256, 256, 512)
        if need_ctx:
            oa = _attn_a_ctx(sink[l], qa, ka, va, oa)
            ob = _full_attention_ctx(_attn_b_kernel, "attn_mla", qb, kb, vb, [], lambda n: [],
                                     256, 512, 512, 512, ob)
            od = _full_attention_ctx(c_kernel, "attn_diff", qd, kd, vd, c_extra, c_specs,
                                     256, 256, 256, 512, od)

        xall, h2, aff = _outproj(xall, oa, ob, od, wo, mod[l], row(norm2, l), wr[l], need_ctx, l)
        slot_rows, aff_rows, slot_cols = _route(aff, tri, need_ctx)
        gathered = _gather(slot_rows, aff_rows, h2, need_ctx)
        y = _ffn(gathered, w_gate, w_up, w_down, l)
        xall = _combine(xall, slot_cols, y, mod[l], need_ctx,
                        final_norm=None if need_ctx else final_norm.reshape(1, D_MODEL))
    return xall
```

```python
import functools
import math

import numpy as np
import jax
import jax.numpy as jnp
from jax import lax
from jax.experimental import pallas as pl
from jax.experimental.pallas import tpu as pltpu

D_MODEL = 1024
BATCH = 8
SEQ = 2048
DEPTH = 4
CTX_LEN = 256
NT = SEQ + CTX_LEN
GRID_W = 64
HEAD_DIM = 64
ROPE_BASE = 10000.0
EPS = 1e-6
NEG_INF = -1e30
LOG2_E = math.log2(math.e)

WIN = 128
A_HEADS = 8
A_KV_HEADS = 2
A_GROUP = A_HEADS // A_KV_HEADS
B_HEADS = 4
B_NOPE = 64
B_ROPE = 32
B_V = 64
B_Q_RANK = 256
B_KV_RANK = 128
C_HEADS = 4
C_DH = 32
C_V = 2 * C_DH
N_EXPERTS = 16
EC_FACTOR = 2
D_EXPERT = 512
CAP_LAT = EC_FACTOR * SEQ // N_EXPERTS
CAP_CTX = EC_FACTOR * CTX_LEN // N_EXPERTS

LANES = 128
TOK_TILE = 256
NT_PAD = SEQ + TOK_TILE
N_TILES = NT_PAD // TOK_TILE
N_LAT_TILES = SEQ // TOK_TILE
STEP_CHUNKS = 3
LAT_STEP_CHUNKS = 2
FFN_TILE = 1024
FFN_LAT_TILES = SEQ // FFN_TILE
VMEM_LIMIT = 56 * 1024 * 1024

F32 = jnp.float32
BF16 = jnp.bfloat16

_NT_DIMS = (((1,), (1,)), ((), ()))


def _params(sem):
    return pltpu.CompilerParams(dimension_semantics=sem, vmem_limit_bytes=VMEM_LIMIT)


def _dot(a, b):
    return jnp.dot(a, b, preferred_element_type=F32)


def _dot_nt(a, b):
    return lax.dot_general(a, b, _NT_DIMS, preferred_element_type=F32)


def _split_bf16(v):
    hi = v.astype(BF16)
    lo = (v - hi.astype(F32)).astype(BF16)
    return hi, lo


def _dot3(a, b):
    a_hi, a_lo = _split_bf16(a)
    b_hi, b_lo = _split_bf16(b)
    return _dot(a_hi, b_hi) + _dot(a_lo, b_hi) + _dot(a_hi, b_lo)


def _indicator(cond):
    return jnp.where(cond, 1.0, 0.0).astype(BF16)


def _rms(v):
    return v * lax.rsqrt(jnp.mean(v * v, axis=-1, keepdims=True) + EPS)


def _softmax_pv(s, v_blk, head):
    m = jnp.max(s, axis=-1, keepdims=True)
    ol = _dot(jnp.exp2(s - m).astype(BF16), v_blk)
    ones_lane = 64 if head % 2 == 0 else 0
    return ol / ol[:, ones_lane:ones_lane + 1]


def _merge_head_pairs(per_head):
    lane = lax.broadcasted_iota(jnp.int32, (1, LANES), 1)
    blocks = [jnp.where(lane < 64, per_head[j], per_head[j + 1]) for j in range(0, len(per_head), 2)]
    return jnp.concatenate(blocks, axis=-1)


def _softmax_rows(s):
    m = jnp.max(s, axis=-1, keepdims=True)
    e = jnp.exp(s - m)
    return e / jnp.sum(e, axis=-1, keepdims=True)


def _mod_kernel(c_ref, w_ref, b_ref, o_ref):
    c = c_ref[...]
    s = c / (1.0 + jnp.exp(-c))
    o_ref[0] = _dot3(s, w_ref[0]) + b_ref[0]


def _modulation(cc, w_ada, b_ada):
    tn = 1536
    return pl.pallas_call(
        _mod_kernel,
        out_shape=jax.ShapeDtypeStruct((DEPTH, 16, 6 * D_MODEL), F32),
        grid=(DEPTH, 6 * D_MODEL // tn),
        in_specs=[
            pl.BlockSpec((16, D_MODEL), lambda l, j: (0, 0)),
            pl.BlockSpec((1, D_MODEL, tn), lambda l, j: (l, 0, j)),
            pl.BlockSpec((1, 1, tn), lambda l, j: (l, 0, j)),
        ],
        out_specs=pl.BlockSpec((1, 16, tn), lambda l, j: (l, 0, j)),
        compiler_params=_params(("arbitrary", "arbitrary")),
        name="adaln_mod",
    )(cc, w_ada, b_ada.reshape(DEPTH, 1, 6 * D_MODEL))


def _rope(p, cos, sin, half):
    lane = lax.broadcasted_iota(jnp.int32, (1, LANES), 1)
    first = (lane // half) % 2 == 0
    outs = []
    for j in range(p.shape[1] // LANES):
        blk = p[:, j * LANES:(j + 1) * LANES]
        partner = jnp.where(first, pltpu.roll(blk, LANES - half, 1), pltpu.roll(blk, half, 1))
        outs.append(blk * cos + partner * sin)
    return outs[0] if len(outs) == 1 else jnp.concatenate(outs, axis=-1)


def _per_head_with_ones(v):
    lane = lax.broadcasted_iota(jnp.int32, (1, LANES), 1)
    outs = []
    for j in range(v.shape[1] // LANES):
        blk = v[:, j * LANES:(j + 1) * LANES]
        outs.append(jnp.where(lane < 64, blk, jnp.where(lane == 64, 1.0, 0.0)))
        outs.append(jnp.where(lane >= 64, blk, jnp.where(lane == 0, 1.0, 0.0)))
    return jnp.concatenate(outs, axis=-1)


def _chunk_mod(modb_ref, modc_ref, j, n_chunks):
    mod = modb_ref[0]
    if n_chunks == STEP_CHUNKS and j == n_chunks - 1:
        mod = jnp.where(pl.program_id(1) == N_TILES // STEP_CHUNKS - 1, modc_ref[0], mod)
    return mod


def _inproj_kernel(x_ref, modb_ref, modc_ref, g1_ref, w_ref, qn_ref, wuq_ref, kvn_ref, wukv_ref,
                   ca_ref, sa_ref, caq_ref, saq_ref, c8_ref, s8_ref, c8q_ref, s8q_ref,
                   cb_ref, sb_ref, cbq_ref, sbq_ref,
                   qa_ref, ka_ref, va_ref, qb_ref, kb_ref, vb_ref, qd_ref, kd_ref, vd_ref):
    for j in range(STEP_CHUNKS):
        rows = slice(j * TOK_TILE, (j + 1) * TOK_TILE)
        x = x_ref[0, rows, :]
        mod = _chunk_mod(modb_ref, modc_ref, j, STEP_CHUNKS)
        sh1 = mod[:, 0:D_MODEL]
        sc1 = mod[:, D_MODEL:2 * D_MODEL]
        h = (_rms(x) * g1_ref[...]) * (1.0 + sc1) + sh1
        p = _dot(h.astype(BF16), w_ref[0])

        qa_ref[0, rows, :] = _rope(p[:, 0:512], caq_ref[rows, :], saq_ref[rows, :], 16).astype(BF16)
        ka_ref[0, rows, :] = _rope(p[:, 512:640], ca_ref[rows, :], sa_ref[rows, :], 16).astype(BF16)
        lane = lax.broadcasted_iota(jnp.int32, (TOK_TILE, LANES), 1)
        va_ref[0, rows, :] = jnp.concatenate(
            [p[:, 640:768], jnp.where(lane == 0, 1.0, 0.0)], axis=-1).astype(BF16)

        cq = (_rms(p[:, 768:1024]) * qn_ref[...]).astype(BF16)
        qb = _dot(cq, wuq_ref[...])
        qb_ref[0, rows, :] = _rope(qb, cbq_ref[rows, :], sbq_ref[rows, :], 8).astype(BF16)
        ckv = (_rms(p[:, 1024:1152]) * kvn_ref[...]).astype(BF16)
        kv = _dot(ckv, wukv_ref[...])
        kr = _rope(p[:, 1152:1280], cb_ref[rows, :], sb_ref[rows, :], 8)
        kb_ref[0, rows, :] = (kv[:, 0:512] + jnp.concatenate([kr] * B_HEADS, axis=-1)).astype(BF16)
        vb_ref[0, rows, :] = _per_head_with_ones(kv[:, 512:768]).astype(BF16)

        qd_ref[0, rows, :] = _rope(p[:, 1280:1536], c8q_ref[rows, :], s8q_ref[rows, :], 8).astype(BF16)
        kd_ref[0, rows, :] = _rope(p[:, 1536:1792], c8_ref[rows, :], s8_ref[rows, :], 8).astype(BF16)
        vd_ref[0, rows, :] = _per_head_with_ones(p[:, 1792:2048]).astype(BF16)


def _mod_specs():
    return [pl.BlockSpec((1, 1, 6 * D_MODEL), lambda b, t: (b, 0, 0)),
            pl.BlockSpec((1, 1, 6 * D_MODEL), lambda b, t: (BATCH, 0, 0))]


def _inproj(xall, mod_l, g1, w, qn, wuq, kvn, wukv, tabs, layer):
    full = lambda shape: pl.BlockSpec(shape, lambda b, t: (0,) * len(shape))
    step_rows = STEP_CHUNKS * TOK_TILE
    tab_spec = pl.BlockSpec((step_rows, LANES), lambda b, t: (t, 0))
    widths = (512, 128, 256, 512, 512, 512, 256, 256, 512)
    return pl.pallas_call(
        _inproj_kernel,
        out_shape=[jax.ShapeDtypeStruct((BATCH, NT_PAD, n), BF16) for n in widths],
        grid=(BATCH, N_TILES // STEP_CHUNKS),
        in_specs=[pl.BlockSpec((1, step_rows, D_MODEL), lambda b, t: (b, t, 0))] + _mod_specs() + [
            full((1, D_MODEL)),
            pl.BlockSpec((1, D_MODEL, 2048), lambda b, t: (layer, 0, 0)),
            full((1, B_Q_RANK)),
            full((B_Q_RANK, 512)),
            full((1, B_KV_RANK)),
            full((B_KV_RANK, 768)),
        ] + [tab_spec] * 12,
        out_specs=[pl.BlockSpec((1, step_rows, n), lambda b, t: (b, t, 0)) for n in widths],
        compiler_params=_params(("arbitrary", "arbitrary")),
        name="in_projection",
    )(xall, mod_l, mod_l, g1, w, qn, wuq, kvn, wukv, *tabs)


def _window_heads(q, kcat, vcat, bias, sink_ref):
    tq = q.shape[0]
    lane = lax.broadcasted_iota(jnp.int32, (1, LANES), 1)
    acc = [jnp.zeros((tq, LANES), F32) for _ in range(A_GROUP)]
    for g in range(A_KV_HEADS):
        in_half = (lane < HEAD_DIM) if g == 0 else (lane >= HEAD_DIM)
        keep = _indicator(in_half)
        qs = jnp.concatenate([q[:, j * LANES:(j + 1) * LANES] * keep for j in range(A_GROUP)], axis=0)
        s = _dot_nt(qs, kcat)
        if bias is not None:
            s = s + jnp.concatenate([bias] * A_GROUP, axis=0)
        sink = jnp.concatenate(
            [jnp.full((tq, 1), sink_ref[g * A_GROUP + j] * LOG2_E, F32) for j in range(A_GROUP)], axis=0)
        m = jnp.maximum(jnp.max(s, axis=-1, keepdims=True), sink)
        ol = _dot(jnp.exp2(s - m).astype(BF16), vcat)
        o = ol[:, 0:LANES] / (ol[:, LANES:LANES + 1] + jnp.exp2(sink - m))
        for j in range(A_GROUP):
            acc[j] = acc[j] + jnp.where(in_half, o[j * tq:(j + 1) * tq], 0.0)
    return jnp.concatenate(acc, axis=-1).astype(BF16)


A_STEP_BLOCKS = 8
A_LOCAL = 3 * WIN


def _attn_a_latent_kernel(sink_ref, q_ref, k_ref, v_ref, bias_ref, o_ref):
    step = pl.program_id(1)
    last = SEQ // WIN - 1
    k_ctx = k_ref[0, SEQ:NT, :]
    v_ctx = v_ref[0, SEQ:NT, :]
    for j in range(A_STEP_BLOCKS):
        n = step * A_STEP_BLOCKS + j
        start = pl.multiple_of(jnp.clip((n - 1) * WIN, 0, SEQ - A_LOCAL), WIN)
        variant = jnp.where(n == 0, 0, jnp.where(n == last, 2, 1))
        kcat = jnp.concatenate([k_ref[0, pl.ds(start, A_LOCAL), :], k_ctx], axis=0)
        vcat = jnp.concatenate([v_ref[0, pl.ds(start, A_LOCAL), :], v_ctx], axis=0)
        rows = slice(j * WIN, (j + 1) * WIN)
        o_ref[0, rows, :] = _window_heads(q_ref[0, rows, :], kcat, vcat, bias_ref[variant], sink_ref)


def _attn_a_latent(sink, qa, ka, va, bias):
    tq = A_STEP_BLOCKS * WIN
    return pl.pallas_call(
        _attn_a_latent_kernel,
        out_shape=jax.ShapeDtypeStruct((BATCH, NT_PAD, 512), BF16),
        grid=(BATCH, SEQ // tq),
        in_specs=[pl.BlockSpec(memory_space=pltpu.SMEM),
                  pl.BlockSpec((1, tq, 512), lambda b, n: (b, n, 0)),
                  pl.BlockSpec((1, NT_PAD, LANES), lambda b, n: (b, 0, 0)),
                  pl.BlockSpec((1, NT_PAD, 2 * LANES), lambda b, n: (b, 0, 0)),
                  pl.BlockSpec((3, WIN, A_LOCAL + CTX_LEN), lambda b, n: (0, 0, 0))],
        out_specs=pl.BlockSpec((1, tq, 512), lambda b, n: (b, n, 0)),
        compiler_params=_params(("arbitrary", "arbitrary")),
        name="attn_window_latent",
    )(sink, qa, ka, va, bias)


def _attn_a_ctx(sink, qa, ka, va, prev_out):
    ctx_blk = SEQ // CTX_LEN
    spec = lambda n: pl.BlockSpec((BATCH, CTX_LEN, n), lambda i: (0, ctx_blk, 0))
    tile = lambda n: pl.BlockSpec((BATCH, TOK_TILE, n), lambda i: (0, N_LAT_TILES, 0))

    def body(sink_ref, q_ref, k_ref, v_ref, prev_ref, o_ref):
        del prev_ref
        for sample in range(BATCH):
            o_ref[sample] = _window_heads(q_ref[sample], k_ref[sample], v_ref[sample], None, sink_ref)

    return pl.pallas_call(
        body,
        out_shape=jax.ShapeDtypeStruct((BATCH, NT_PAD, 512), BF16),
        grid=(1,),
        in_specs=[pl.BlockSpec(memory_space=pltpu.SMEM), tile(512), spec(LANES), spec(2 * LANES),
                  pl.BlockSpec(memory_space=pl.ANY)],
        out_specs=tile(512),
        input_output_aliases={4: 0},
        compiler_params=_params(("arbitrary",)),
        name="attn_window_ctx",
    )(sink, qa, ka, va, prev_out)


def _key_rows(ref, sample):
    return ref[sample, 0:NT, :] if ref.shape[1] == NT_PAD else ref[sample]


MLA_TQ = 2048
MLA_SUB = 512
DIFF_TQ = 1024
DIFF_SUB = 256


def _attn_b_kernel(q_ref, k_ref, v_ref, o_ref):
    for sample in range(q_ref.shape[0]):
        k = _key_rows(k_ref, sample)
        v = _key_rows(v_ref, sample)
        sub_rows = min(MLA_SUB, q_ref.shape[1])
        for sub in range(q_ref.shape[1] // sub_rows):
            rows = slice(sub * sub_rows, (sub + 1) * sub_rows)
            q = q_ref[sample, rows, :]
            outs = []
            for h in range(B_HEADS):
                s = _dot_nt(q[:, h * LANES:(h + 1) * LANES], k[:, h * LANES:(h + 1) * LANES])
                outs.append(_softmax_pv(s, v[:, h * LANES:(h + 1) * LANES], h))
            o_ref[sample, rows, :] = _merge_head_pairs(outs).astype(BF16)


def _full_attention(kernel, name, tq, q, k, v, extra, extra_specs, out_width, q_width, k_width, v_width):
    out = pl.pallas_call(
        kernel,
        out_shape=jax.ShapeDtypeStruct((BATCH, NT_PAD, out_width), BF16),
        grid=(BATCH, SEQ // tq),
        in_specs=[pl.BlockSpec((1, tq, q_width), lambda b, t: (b, t, 0)),
                  pl.BlockSpec((1, NT_PAD, k_width), lambda b, t: (b, 0, 0)),
                  pl.BlockSpec((1, NT_PAD, v_width), lambda b, t: (b, 0, 0))] + extra_specs(2),
        out_specs=pl.BlockSpec((1, tq, out_width), lambda b, t: (b, t, 0)),
        compiler_params=_params(("arbitrary", "arbitrary")),
        name=name + "_latent",
    )(q, k, v, *extra)
    return out


def _full_attention_ctx(kernel, name, q, k, v, extra, extra_specs, out_width, q_width, k_width, v_width,
                        prev_out):
    ctx_blk = SEQ // CTX_LEN
    n_extra = len(extra)

    def body(*refs):
        ins = refs[:3 + n_extra]
        kernel(*ins, refs[-1])

    spec = lambda n: pl.BlockSpec((BATCH, CTX_LEN, n), lambda i: (0, ctx_blk, 0))
    tile = lambda n: pl.BlockSpec((BATCH, TOK_TILE, n), lambda i: (0, N_LAT_TILES, 0))
    return pl.pallas_call(
        body,
        out_shape=jax.ShapeDtypeStruct((BATCH, NT_PAD, out_width), BF16),
        grid=(1,),
        in_specs=[tile(q_width), spec(k_width), spec(v_width)] + extra_specs(1)
                 + [pl.BlockSpec(memory_space=pl.ANY)],
        out_specs=tile(out_width),
        input_output_aliases={3 + n_extra: 0},
        compiler_params=_params(("arbitrary",)),
        name=name + "_ctx",
    )(q, k, v, *extra, prev_out)


def _attn_c_kernel(q_ref, k_ref, v_ref, lq1_ref, lk1_ref, lq2_ref, lk2_ref, sn_ref, o_ref, *, lam_init):
    tq = min(DIFF_SUB, q_ref.shape[1])
    lam = (jnp.exp(jnp.sum(lq1_ref[...] * lk1_ref[...], axis=-1, keepdims=True))
           - jnp.exp(jnp.sum(lq2_ref[...] * lk2_ref[...], axis=-1, keepdims=True)) + lam_init)
    lane = lax.broadcasted_iota(jnp.int32, (1, C_HEADS * C_V), 1)
    units = [(sample, sub) for sample in range(q_ref.shape[0]) for sub in range(q_ref.shape[1] // tq)]
    for sample, sub in units:
        k = _key_rows(k_ref, sample)
        v = _key_rows(v_ref, sample)
        rows = slice(sub * tq, (sub + 1) * tq)
        q = q_ref[sample, rows, :]
        outs = []
        for h in range(C_HEADS):
            lo = h * C_V
            q1 = q * _indicator((lane >= lo) & (lane < lo + C_DH))
            q2 = q * _indicator((lane >= lo + C_DH) & (lane < lo + C_V))
            s = _dot_nt(jnp.concatenate([q1, q2], axis=0), k)
            pv = _softmax_pv(s, v[:, h * LANES:(h + 1) * LANES], h)
            outs.append(pv[0:tq] - lam * pv[tq:2 * tq])
        acc = _merge_head_pairs(outs)
        sq = acc * acc
        inv = jnp.zeros((tq, C_HEADS * C_V), F32)
        for h in range(C_HEADS):
            in_head = (lane >= h * C_V) & (lane < (h + 1) * C_V)
            ms = jnp.sum(jnp.where(in_head, sq, 0.0), axis=-1, keepdims=True) * (1.0 / C_V)
            inv = inv + jnp.where(in_head, lax.rsqrt(ms + EPS), 0.0)
        o_ref[sample, rows, :] = ((acc * inv) * sn_ref[...] * (1.0 - lam_init)).astype(BF16)


def _outproj_kernel(x_ref, oa_ref, ob_ref, od_ref, w_ref, modb_ref, modc_ref, g2_ref, wr_ref,
                    xo_ref, h2_ref, aff_ref, *, n_chunks):
    w_hi, w_lo = _split_bf16(wr_ref[...])
    w_router = jnp.concatenate([w_hi, w_lo], axis=-1)
    lane = lax.broadcasted_iota(jnp.int32, (1, LANES), 1)
    mixed = _dot(jnp.concatenate([oa_ref[0], ob_ref[0], od_ref[0]], axis=-1), w_ref[0])
    for j in range(n_chunks):
        rows = slice(j * TOK_TILE, (j + 1) * TOK_TILE)
        mod = _chunk_mod(modb_ref, modc_ref, j, n_chunks)
        g1 = mod[:, 2 * D_MODEL:3 * D_MODEL]
        sh2 = mod[:, 3 * D_MODEL:4 * D_MODEL]
        sc2 = mod[:, 4 * D_MODEL:5 * D_MODEL]
        x = x_ref[0, rows, :] + g1 * mixed[rows]
        xo_ref[0, rows, :] = x
        h2 = (_rms(x) * g2_ref[...]) * (1.0 + sc2) + sh2
        h_hi, h_lo = _split_bf16(h2)
        h2_ref[0, rows, :] = h_hi
        prod = _dot(jnp.concatenate([h_hi, h_lo], axis=0), w_router)
        logits = (prod[0:TOK_TILE, 0:LANES] + prod[0:TOK_TILE, LANES:2 * LANES]
                  + prod[TOK_TILE:2 * TOK_TILE, 0:LANES])
        aff_ref[0, rows, :] = _softmax_rows(jnp.where(lane < N_EXPERTS, logits, NEG_INF))


def _outproj(xall, oa, ob, od, w, mod_l, g2, wr, with_ctx, layer):
    n_chunks = STEP_CHUNKS if with_ctx else LAT_STEP_CHUNKS
    n_steps = (N_TILES if with_ctx else N_LAT_TILES) // n_chunks
    full = lambda shape: pl.BlockSpec(shape, lambda b, t: (0,) * len(shape))
    tok = lambda n: pl.BlockSpec((1, n_chunks * TOK_TILE, n), lambda b, t: (b, t, 0))
    return pl.pallas_call(
        functools.partial(_outproj_kernel, n_chunks=n_chunks),
        out_shape=[jax.ShapeDtypeStruct((BATCH, NT_PAD, D_MODEL), F32),
                   jax.ShapeDtypeStruct((BATCH, NT_PAD, D_MODEL), BF16),
                   jax.ShapeDtypeStruct((BATCH, NT_PAD, LANES), F32)],
        grid=(BATCH, n_steps),
        in_specs=[tok(D_MODEL), tok(512), tok(256), tok(256),
                  pl.BlockSpec((1, D_MODEL, D_MODEL), lambda b, t: (layer, 0, 0))] + _mod_specs()
                 + [full((1, D_MODEL)), full((D_MODEL, LANES))],
        out_specs=[tok(D_MODEL), tok(D_MODEL), tok(LANES)],
        compiler_params=_params(("arbitrary", "arbitrary")),
        name="out_projection",
    )(xall, oa, ob, od, w, mod_l, mod_l, g2, wr)


def _prefix_count(flags_f32, tri_ref):
    rows, n = flags_f32.shape
    run = jnp.zeros((rows, 1), F32)
    outs = []
    for c in range(n // LANES):
        blk = flags_f32[:, c * LANES:(c + 1) * LANES]
        outs.append(_dot(blk.astype(BF16), tri_ref[...]) + run)
        run = run + jnp.sum(blk, axis=-1, keepdims=True)
    return jnp.concatenate(outs, axis=-1)


def _select_slots(a, cap, tri_ref):
    bits = pltpu.bitcast(a, jnp.int32)

    def step(i, t):
        cand = t | jnp.left_shift(jnp.int32(1), 30 - i)
        cnt = jnp.sum((bits >= cand).astype(jnp.int32), axis=-1, keepdims=True)
        return jnp.where(cnt >= cap, cand, t)

    thr = lax.fori_loop(0, 31, step, jnp.zeros((a.shape[0], 1), jnp.int32))
    above = bits > thr
    tied = bits == thr
    need = cap - jnp.sum(above.astype(F32), axis=-1, keepdims=True)
    tied_f = tied.astype(F32)
    sel = above | (tied & (_prefix_count(tied_f, tri_ref) < need))
    sel_f = sel.astype(F32)
    return jnp.where(sel, _prefix_count(sel_f, tri_ref), -1.0)


def _route_kernel(aff_ref, tri_ref, row_ref, arow_ref, col_ref, *, with_ctx):
    lat, ctx = [], []
    for b in range(BATCH):
        lat.append(aff_ref[b, 0:SEQ, :].T[0:N_EXPERTS])
        if with_ctx:
            ctx.append(aff_ref[b, SEQ:NT, :].T[0:N_EXPERTS])
    aff_lat = jnp.concatenate(lat, axis=0)
    arow_ref[:, 0:SEQ] = aff_lat
    slot_lat = _select_slots(aff_lat, CAP_LAT, tri_ref)
    if with_ctx:
        aff_ctx = jnp.concatenate(ctx, axis=0)
        arow_ref[:, SEQ:NT] = aff_ctx
        slot_ctx = _select_slots(aff_ctx, CAP_CTX, tri_ref)
        slots = jnp.concatenate([slot_lat, slot_ctx], axis=-1)
    else:
        slots = slot_lat
    n = slots.shape[1]
    row_ref[:, 0:n] = slots
    pad = jnp.full((LANES - N_EXPERTS, n), -1.0, F32)
    for b in range(BATCH):
        blk = jnp.concatenate([slots[b * N_EXPERTS:(b + 1) * N_EXPERTS], pad], axis=0)
        col_ref[b, 0:n, :] = blk.T
        if with_ctx and NT_PAD > NT:
            col_ref[b, NT:NT_PAD, :] = jnp.full((NT_PAD - NT, LANES), -1.0, F32)


def _route(aff, tri, with_ctx):
    vm = pl.BlockSpec(memory_space=pltpu.VMEM)
    return pl.pallas_call(
        functools.partial(_route_kernel, with_ctx=with_ctx),
        out_shape=[jax.ShapeDtypeStruct((BATCH * N_EXPERTS, NT), F32),
                   jax.ShapeDtypeStruct((BATCH * N_EXPERTS, NT), F32),
                   jax.ShapeDtypeStruct((BATCH, NT_PAD, LANES), F32)],
        in_specs=[vm, vm],
        out_specs=[vm, vm, vm],
        compiler_params=pltpu.CompilerParams(vmem_limit_bytes=VMEM_LIMIT),
        name="route_topc",
    )(aff, tri)


GATHER_GROUP = 4


def _gather_kernel(row_ref, arow_ref, h_ref, *out_refs, with_ctx):
    slot = lax.broadcasted_iota(jnp.int32, (CAP_LAT, 1), 0).astype(F32)

    def pick(e0, cols, n_slots, h_rows, x_ref, g_ref):
        hots = []
        for e in range(e0, e0 + GATHER_GROUP):
            hot = jnp.where(row_ref[e:e + 1, cols] == slot[0:n_slots], 1.0, 0.0)
            gate = jnp.sum(hot * arow_ref[e:e + 1, cols], axis=-1, keepdims=True)
            g_ref[e] = jnp.broadcast_to(gate, (n_slots, LANES))
            hots.append(hot.astype(BF16))
        x = _dot(jnp.concatenate(hots, axis=0), h_rows).astype(BF16)
        for i in range(GATHER_GROUP):
            x_ref[e0 + i] = x[i * n_slots:(i + 1) * n_slots]

    for e0 in range(0, N_EXPERTS, GATHER_GROUP):
        pick(e0, slice(0, SEQ), CAP_LAT, h_ref[0, 0:SEQ, :], out_refs[0], out_refs[1])
        if with_ctx:
            pick(e0, slice(SEQ, NT), CAP_CTX, h_ref[0, SEQ:NT, :], out_refs[2], out_refs[3])


def _gather(slot_rows, aff_rows, h2, with_ctx):
    out_shape = [jax.ShapeDtypeStruct((N_EXPERTS, SEQ, D_MODEL), BF16),
                 jax.ShapeDtypeStruct((N_EXPERTS, SEQ, LANES), F32)]
    out_specs = [pl.BlockSpec((N_EXPERTS, CAP_LAT, D_MODEL), lambda b: (0, b, 0)),
                 pl.BlockSpec((N_EXPERTS, CAP_LAT, LANES), lambda b: (0, b, 0))]
    if with_ctx:
        out_shape += [jax.ShapeDtypeStruct((N_EXPERTS, CTX_LEN, D_MODEL), BF16),
                      jax.ShapeDtypeStruct((N_EXPERTS, CTX_LEN, LANES), F32)]
        out_specs += [pl.BlockSpec((N_EXPERTS, CAP_CTX, D_MODEL), lambda b: (0, b, 0)),
                      pl.BlockSpec((N_EXPERTS, CAP_CTX, LANES), lambda b: (0, b, 0))]
    row_spec = pl.BlockSpec((N_EXPERTS, NT), lambda b: (b, 0))
    return pl.pallas_call(
        functools.partial(_gather_kernel, with_ctx=with_ctx),
        out_shape=out_shape,
        grid=(BATCH,),
        in_specs=[row_spec, row_spec, pl.BlockSpec((1, NT_PAD, D_MODEL), lambda b: (b, 0, 0))],
        out_specs=out_specs,
        compiler_params=_params(("arbitrary",)),
        name="expert_gather",
    )(slot_rows, aff_rows, h2)


def _ffn_kernel(*refs, with_ctx):
    if with_ctx:
        xl_ref, gl_ref, xc_ref, gc_ref, wg_ref, wu_ref, wd_ref, y_ref = refs
    else:
        xl_ref, gl_ref, wg_ref, wu_ref, wd_ref, y_ref = refs
    wg = wg_ref[0, 0].astype(BF16)
    wu = wu_ref[0, 0].astype(BF16)
    wd = wd_ref[0, 0].astype(BF16)

    def swiglu(x, slot_gate):
        gate = _dot(x, wg)
        hid = (gate / (1.0 + jnp.exp(-gate))) * _dot(x, wu)
        scale = jnp.concatenate([slot_gate] * (D_MODEL // LANES), axis=-1)
        return (_dot(hid.astype(BF16), wd) * scale).astype(BF16)

    for r in range(FFN_LAT_TILES):
        rows = slice(r * FFN_TILE, (r + 1) * FFN_TILE)
        y_ref[0, rows, :] = swiglu(xl_ref[0, rows, :], gl_ref[0, rows, :])
    if with_ctx:
        y_ref[0, SEQ:NT, :] = swiglu(xc_ref[0], gc_ref[0])


def _ffn(gathered, w_gate, w_up, w_down, layer):
    with_ctx = len(gathered) == 4
    ins = list(gathered) + [w_gate, w_up, w_down]
    in_specs = [pl.BlockSpec((1, SEQ, D_MODEL), lambda e: (e, 0, 0)),
                pl.BlockSpec((1, SEQ, LANES), lambda e: (e, 0, 0))]
    if with_ctx:
        in_specs += [pl.BlockSpec((1, CTX_LEN, D_MODEL), lambda e: (e, 0, 0)),
                     pl.BlockSpec((1, CTX_LEN, LANES), lambda e: (e, 0, 0))]
    in_specs += [pl.BlockSpec((1, 1, D_MODEL, D_EXPERT), lambda e: (layer, e, 0, 0)),
                 pl.BlockSpec((1, 1, D_MODEL, D_EXPERT), lambda e: (layer, e, 0, 0)),
                 pl.BlockSpec((1, 1, D_EXPERT, D_MODEL), lambda e: (layer, e, 0, 0))]
    return pl.pallas_call(
        functools.partial(_ffn_kernel, with_ctx=with_ctx),
        out_shape=jax.ShapeDtypeStruct((N_EXPERTS, NT, D_MODEL), BF16),
        grid=(N_EXPERTS,),
        in_specs=in_specs,
        out_specs=pl.BlockSpec((1, NT, D_MODEL), lambda e: (e, 0, 0)),
        compiler_params=_params(("arbitrary",)),
        name="expert_ffn",
    )(*ins)


COMBINE_ROWS = 1024


def _scatter_add(col, ys, n_slots):
    lane = lax.broadcasted_iota(jnp.int32, (1, n_slots), 1).astype(F32)
    hot = [_indicator(col[:, e:e + 1] == lane) for e in range(N_EXPERTS)]
    return _dot(jnp.concatenate(hot, axis=-1), jnp.concatenate(ys, axis=0))


def _combine_latent_kernel(*refs, final):
    if final:
        x_ref, col_ref, y_ref, mod_ref, fn_ref, o_ref = refs
    else:
        x_ref, col_ref, y_ref, mod_ref, o_ref = refs
    g2 = mod_ref[0][:, 5 * D_MODEL:6 * D_MODEL]
    acc = _scatter_add(col_ref[0], [y_ref[e] for e in range(N_EXPERTS)], CAP_LAT)
    x = x_ref[0] + g2 * acc
    if final:
        x = _rms(x) * fn_ref[...]
    o_ref[0] = x


def _combine_ctx_kernel(x_ref, col_ref, y_ref, mod_ref, prev_ref, o_ref):
    del prev_ref
    g2 = mod_ref[0][:, 5 * D_MODEL:6 * D_MODEL]
    n_pairs = N_EXPERTS * CAP_CTX
    lane = lax.broadcasted_iota(jnp.int32, (1, n_pairs), 1)
    slot_of_lane = (lane % CAP_CTX).astype(F32)
    spread = _indicator(lax.broadcasted_iota(jnp.int32, (LANES, n_pairs), 0)
                        == lax.broadcasted_iota(jnp.int32, (LANES, n_pairs), 1) // CAP_CTX)
    for b in range(BATCH):
        slot_on_lane = _dot(col_ref[b].astype(BF16), spread)
        hot = _indicator(slot_on_lane == slot_of_lane)
        ys = jnp.concatenate([y_ref[e, b * CAP_CTX:(b + 1) * CAP_CTX, :] for e in range(N_EXPERTS)], axis=0)
        o_ref[b] = x_ref[b] + g2 * _dot(hot, ys)


def _combine(xall, slot_cols, y, mod_l, with_ctx, final_norm=None):
    final = final_norm is not None
    n_rows = SEQ if final else NT_PAD
    tok = lambda n: pl.BlockSpec((1, COMBINE_ROWS, n), lambda b, t: (b, t, 0))
    ins = [xall, slot_cols, y, mod_l]
    in_specs = [tok(D_MODEL), tok(LANES),
                pl.BlockSpec((N_EXPERTS, CAP_LAT, D_MODEL), lambda b, t: (0, b, 0)),
                pl.BlockSpec((1, 1, 6 * D_MODEL), lambda b, t: (b, 0, 0))]
    if final:
        ins.append(final_norm)
        in_specs.append(pl.BlockSpec((1, D_MODEL), lambda b, t: (0, 0)))
    out = pl.pallas_call(
        functools.partial(_combine_latent_kernel, final=final),
        out_shape=jax.ShapeDtypeStruct((BATCH, n_rows, D_MODEL), F32),
        grid=(BATCH, SEQ // COMBINE_ROWS),
        in_specs=in_specs,
        out_specs=tok(D_MODEL),
        compiler_params=_params(("arbitrary", "arbitrary")),
        name="expert_combine",
    )(*ins)
    if not with_ctx:
        return out
    ctx_blk = SEQ // CTX_LEN
    ctx_rows = lambda n: pl.BlockSpec((BATCH, CTX_LEN, n), lambda i: (0, ctx_blk, 0))
    return pl.pallas_call(
        _combine_ctx_kernel,
        out_shape=jax.ShapeDtypeStruct((BATCH, NT_PAD, D_MODEL), F32),
        grid=(1,),
        in_specs=[ctx_rows(D_MODEL), ctx_rows(LANES),
                  pl.BlockSpec((N_EXPERTS, CTX_LEN, D_MODEL), lambda i: (0, ctx_blk, 0)),
                  pl.BlockSpec((1, 1, 6 * D_MODEL), lambda i: (BATCH, 0, 0)),
                  pl.BlockSpec(memory_space=pl.ANY)],
        out_specs=ctx_rows(D_MODEL),
        input_output_aliases={4: 0},
        compiler_params=_params(("arbitrary",)),
        name="expert_combine_ctx",
    )(xall, slot_cols, y, mod_l, out)


def _rope_tables():
    t = jnp.arange(SEQ)
    pos = jnp.stack([t // GRID_W, t % GRID_W], axis=0)

    def table(n_f, lanes_per_unit):
        inv = ROPE_BASE ** (-jnp.arange(n_f, dtype=F32) / n_f)
        d = np.arange(lanes_per_unit)
        axis, second, f = d // (2 * n_f), (d // n_f) % 2, d % n_f
        ang = pos[axis].T.astype(F32) * inv[f][None, :]
        cos = jnp.cos(ang)
        sin = jnp.sin(ang) * jnp.where(second == 1, 1.0, -1.0)[None, :]
        return cos, sin

    def finish(cos, sin, scale):
        cos = jnp.concatenate([cos, jnp.ones((NT_PAD - SEQ, LANES), F32)], axis=0)
        sin = jnp.concatenate([sin, jnp.zeros((NT_PAD - SEQ, LANES), F32)], axis=0)
        scale = scale * LOG2_E
        return [cos, sin, cos * scale, sin * scale]

    cos_a, sin_a = table(HEAD_DIM // 4, HEAD_DIM)
    tabs_a = finish(jnp.tile(cos_a, (1, 2)), jnp.tile(sin_a, (1, 2)), HEAD_DIM ** -0.5)
    cos_8, sin_8 = table(B_ROPE // 4, B_ROPE)
    tabs_8 = finish(jnp.tile(cos_8, (1, 4)), jnp.tile(sin_8, (1, 4)), C_DH ** -0.5)
    ones = jnp.ones((SEQ, B_NOPE), F32)
    pad1 = jnp.ones((SEQ, LANES - B_NOPE - B_ROPE), F32)
    cos_b = jnp.concatenate([ones, cos_8, pad1], axis=1)
    sin_b = jnp.concatenate([0 * ones, sin_8, 0 * pad1], axis=1)
    tabs_b = finish(cos_b, sin_b, (B_NOPE + B_ROPE) ** -0.5)
    return tabs_a + tabs_8 + tabs_b


def _window_bias():
    qi = np.arange(WIN)[:, None]
    kj = np.arange(A_LOCAL)[None, :]
    out = np.zeros((3, WIN, A_LOCAL + CTX_LEN), np.float32)
    for v in range(3):
        out[v, :, :A_LOCAL] = np.where(np.abs(kj - v * WIN - qi) <= WIN, 0.0, NEG_INF)
    return jnp.asarray(out)


def _prep_weights(w_in, w_uq, w_ukv, w_out, w_router):
    offs = np.cumsum((0, 512, 128, 128, 256, 128, 32, 256, 256, 256))
    seg = lambda i: w_in[:, :, offs[i]:offs[i + 1]]
    wq = seg(0).reshape(DEPTH, D_MODEL, A_KV_HEADS, A_GROUP, HEAD_DIM)
    wq = jnp.swapaxes(wq, 2, 3).reshape(DEPTH, D_MODEL, A_HEADS * HEAD_DIM)
    kr = jnp.pad(seg(5), ((0, 0), (0, 0), (B_NOPE, LANES - B_NOPE - B_ROPE)))
    w = jnp.concatenate([wq, seg(1), seg(2), seg(3), seg(4), kr,
                         seg(6), seg(7), seg(8)], axis=-1).astype(BF16)
    wuq = w_uq.reshape(DEPTH, B_Q_RANK, B_HEADS, B_NOPE + B_ROPE)
    wuq = jnp.pad(wuq, ((0, 0), (0, 0), (0, 0), (0, LANES - B_NOPE - B_ROPE)))
    wuq = wuq.reshape(DEPTH, B_Q_RANK, B_HEADS * LANES).astype(BF16)
    wukv = w_ukv.reshape(DEPTH, B_KV_RANK, B_HEADS, B_NOPE + B_V)
    wk = jnp.pad(wukv[..., :B_NOPE], ((0, 0), (0, 0), (0, 0), (0, LANES - B_NOPE)))
    wukv = jnp.concatenate([wk.reshape(DEPTH, B_KV_RANK, B_HEADS * LANES),
                            wukv[..., B_NOPE:].reshape(DEPTH, B_KV_RANK, B_HEADS * B_V)],
                           axis=-1).astype(BF16)
    wo_a = w_out[:, :A_HEADS * HEAD_DIM].reshape(DEPTH, A_KV_HEADS, A_GROUP, HEAD_DIM, D_MODEL)
    wo_a = jnp.swapaxes(wo_a, 1, 2).reshape(DEPTH, A_HEADS * HEAD_DIM, D_MODEL)
    wo = jnp.concatenate([wo_a, w_out[:, A_HEADS * HEAD_DIM:]], axis=1).astype(BF16)
    wr = jnp.pad(w_router, ((0, 0), (0, 0), (0, LANES - N_EXPERTS)))
    return w, wuq, wukv, wo, wr


def kernel(x, c, ctx, c_ctx, norm1, norm2, w_ada, b_ada, w_in, sink, mla_q_norm, w_uq, mla_kv_norm, w_ukv,
           lam_q1, lam_k1, lam_q2, lam_k2, diff_norm, w_out, w_router, w_gate, w_up, w_down, final_norm):
    xall = jnp.concatenate([x, ctx, jnp.zeros((BATCH, NT_PAD - NT, D_MODEL), F32)], axis=1)
    cc = jnp.concatenate([c, c_ctx[None, :], jnp.zeros((16 - BATCH - 1, D_MODEL), F32)], axis=0)
    mod = _modulation(cc, w_ada, b_ada).reshape(DEPTH, 16, 1, 6 * D_MODEL)
    tabs = _rope_tables()
    bias = _window_bias()
    tri = jnp.asarray(np.triu(np.ones((LANES, LANES), np.float32), k=1), BF16)
    w, wuq, wukv, wo, wr = _prep_weights(w_in, w_uq, w_ukv, w_out, w_router)
    sub_norm = jnp.tile(diff_norm, (1, C_HEADS)).reshape(DEPTH, 1, C_HEADS * C_V)
    row = lambda a, l: a[l].reshape(1, -1)

    for l in range(DEPTH):
        need_ctx = l < DEPTH - 1
        qa, ka, va, qb, kb, vb, qd, kd, vd = _inproj(
            xall, mod[l], row(norm1, l), w, row(mla_q_norm, l), wuq[l], row(mla_kv_norm, l), wukv[l], tabs, l)

        oa = _attn_a_latent(sink[l], qa, ka, va, bias)
        ob = _full_attention(_attn_b_kernel, "attn_mla", MLA_TQ, qb, kb, vb, [], lambda n: [],
                             256, 512, 512, 512)
        lam_init = 0.8 - 0.6 * math.exp(-0.3 * l)
        c_kernel = functools.partial(_attn_c_kernel, lam_init=lam_init)
        c_extra = [row(lam_q1, l), row(lam_k1, l), row(lam_q2, l), row(lam_k2, l), sub_norm[l]]

        def c_specs(n_grid, c_extra=c_extra):
            zmap = (lambda b, t: (0, 0)) if n_grid == 2 else (lambda b: (0, 0))
            return [pl.BlockSpec(a.shape, zmap) for a in c_extra]

        od = _full_attention(c_kernel, "attn_diff", DIFF_TQ, qd, kd, vd, c_extra, c_specs, 256, 256, 256, 512)
        if need_ctx:
            oa = _attn_a_ctx(sink[l], qa, ka, va, oa)
            ob = _full_attention_ctx(_attn_b_kernel, "attn_mla", qb, kb, vb, [], lambda n: [],
                                     256, 512, 512, 512, ob)
            od = _full_attention_ctx(c_kernel, "attn_diff", qd, kd, vd, c_extra, c_specs,
                                     256, 256, 256, 512, od)

        xall, h2, aff = _outproj(xall, oa, ob, od, wo, mod[l], row(norm2, l), wr[l], need_ctx, l)
        slot_rows, aff_rows, slot_cols = _route(aff, tri, need_ctx)
        gathered = _gather(slot_rows, aff_rows, h2, need_ctx)
        y = _ffn(gathered, w_gate, w_up, w_down, l)
        xall = _combine(xall, slot_cols, y, mod[l], need_ctx,
                        final_norm=None if need_ctx else final_norm.reshape(1, D_MODEL))
    return xall
```

```python
import functools
import math

import numpy as np
import jax
import jax.numpy as jnp
from jax import lax
from jax.experimental import pallas as pl
from jax.experimental.pallas import tpu as pltpu

D_MODEL = 1024
BATCH = 8
SEQ = 2048
DEPTH = 4
CTX_LEN = 256
NT = SEQ + CTX_LEN
GRID_W = 64
HEAD_DIM = 64
ROPE_BASE = 10000.0
EPS = 1e-6
NEG_INF = -1e30
LOG2_E = math.log2(math.e)

WIN = 128
A_HEADS = 8
A_KV_HEADS = 2
A_GROUP = A_HEADS // A_KV_HEADS
B_HEADS = 4
B_NOPE = 64
B_ROPE = 32
B_V = 64
B_Q_RANK = 256
B_KV_RANK = 128
C_HEADS = 4
C_DH = 32
C_V = 2 * C_DH
N_EXPERTS = 16
EC_FACTOR = 2
D_EXPERT = 512
CAP_LAT = EC_FACTOR * SEQ // N_EXPERTS
CAP_CTX = EC_FACTOR * CTX_LEN // N_EXPERTS

LANES = 128
TOK_TILE = 256
NT_PAD = SEQ + TOK_TILE
N_TILES = NT_PAD // TOK_TILE
N_LAT_TILES = SEQ // TOK_TILE
STEP_CHUNKS = 3
LAT_STEP_CHUNKS = 2
FFN_TILE = 1024
FFN_LAT_TILES = SEQ // FFN_TILE
VMEM_LIMIT = 56 * 1024 * 1024

F32 = jnp.float32
BF16 = jnp.bfloat16

_NT_DIMS = (((1,), (1,)), ((), ()))


def _params(sem):
    return pltpu.CompilerParams(dimension_semantics=sem, vmem_limit_bytes=VMEM_LIMIT)


def _dot(a, b):
    return jnp.dot(a, b, preferred_element_type=F32)


def _dot_nt(a, b):
    return lax.dot_general(a, b, _NT_DIMS, preferred_element_type=F32)


def _split_bf16(v):
    hi = v.astype(BF16)
    lo = (v - hi.astype(F32)).astype(BF16)
    return hi, lo


def _dot3(a, b):
    a_hi, a_lo = _split_bf16(a)
    b_hi, b_lo = _split_bf16(b)
    return _dot(a_hi, b_hi) + _dot(a_lo, b_hi) + _dot(a_hi, b_lo)


def _indicator(cond):
    return jnp.where(cond, 1.0, 0.0).astype(BF16)


def _rms(v):
    return v * lax.rsqrt(jnp.mean(v * v, axis=-1, keepdims=True) + EPS)


def _softmax_pv(s, v_blk, head):
    m = jnp.max(s, axis=-1, keepdims=True)
    ol = _dot(jnp.exp2(s - m).astype(BF16), v_blk)
    ones_lane = 64 if head % 2 == 0 else 0
    return ol / ol[:, ones_lane:ones_lane + 1]


def _merge_head_pairs(per_head):
    lane = lax.broadcasted_iota(jnp.int32, (1, LANES), 1)
    blocks = [jnp.where(lane < 64, per_head[j], per_head[j + 1]) for j in range(0, len(per_head), 2)]
    return jnp.concatenate(blocks, axis=-1)


def _softmax_rows(s):
    m = jnp.max(s, axis=-1, keepdims=True)
    e = jnp.exp(s - m)
    return e / jnp.sum(e, axis=-1, keepdims=True)


def _mod_kernel(c_ref, w_ref, b_ref, o_ref):
    c = c_ref[...]
    s = c / (1.0 + jnp.exp(-c))
    o_ref[0] = _dot3(s, w_ref[0]) + b_ref[0]


def _modulation(cc, w_ada, b_ada):
    tn = 1536
    return pl.pallas_call(
        _mod_kernel,
        out_shape=jax.ShapeDtypeStruct((DEPTH, 16, 6 * D_MODEL), F32),
        grid=(DEPTH, 6 * D_MODEL // tn),
        in_specs=[
            pl.BlockSpec((16, D_MODEL), lambda l, j: (0, 0)),
            pl.BlockSpec((1, D_MODEL, tn), lambda l, j: (l, 0, j)),
            pl.BlockSpec((1, 1, tn), lambda l, j: (l, 0, j)),
        ],
        out_specs=pl.BlockSpec((1, 16, tn), lambda l, j: (l, 0, j)),
        compiler_params=_params(("arbitrary", "arbitrary")),
        name="adaln_mod",
    )(cc, w_ada, b_ada.reshape(DEPTH, 1, 6 * D_MODEL))


def _rope(p, cos, sin, half):
    lane = lax.broadcasted_iota(jnp.int32, (1, LANES), 1)
    first = (lane // half) % 2 == 0
    outs = []
    for j in range(p.shape[1] // LANES):
        blk = p[:, j * LANES:(j + 1) * LANES]
        partner = jnp.where(first, pltpu.roll(blk, LANES - half, 1), pltpu.roll(blk, half, 1))
        outs.append(blk * cos + partner * sin)
    return outs[0] if len(outs) == 1 else jnp.concatenate(outs, axis=-1)


def _per_head_with_ones(v):
    lane = lax.broadcasted_iota(jnp.int32, (1, LANES), 1)
    outs = []
    for j in range(v.shape[1] // LANES):
        blk = v[:, j * LANES:(j + 1) * LANES]
        outs.append(jnp.where(lane < 64, blk, jnp.where(lane == 64, 1.0, 0.0)))
        outs.append(jnp.where(lane >= 64, blk, jnp.where(lane == 0, 1.0, 0.0)))
    return jnp.concatenate(outs, axis=-1)


def _chunk_mod(modb_ref, modc_ref, j, n_chunks):
    mod = modb_ref[0]
    if n_chunks == STEP_CHUNKS and j == n_chunks - 1:
        mod = jnp.where(pl.program_id(1) == N_TILES // STEP_CHUNKS - 1, modc_ref[0], mod)
    return mod


def _inproj_kernel(x_ref, modb_ref, modc_ref, g1_ref, w_ref, qn_ref, wuq_ref, kvn_ref, wukv_ref,
                   ca_ref, sa_ref, caq_ref, saq_ref, c8_ref, s8_ref, c8q_ref, s8q_ref,
                   cb_ref, sb_ref, cbq_ref, sbq_ref,
                   qa_ref, ka_ref, va_ref, qb_ref, kb_ref, vb_ref, qd_ref, kd_ref, vd_ref):
    for j in range(STEP_CHUNKS):
        rows = slice(j * TOK_TILE, (j + 1) * TOK_TILE)
        x = x_ref[0, rows, :]
        mod = _chunk_mod(modb_ref, modc_ref, j, STEP_CHUNKS)
        sh1 = mod[:, 0:D_MODEL]
        sc1 = mod[:, D_MODEL:2 * D_MODEL]
        h = (_rms(x) * g1_ref[...]) * (1.0 + sc1) + sh1
        p = _dot(h.astype(BF16), w_ref[0])

        qa_ref[0, rows, :] = _rope(p[:, 0:512], caq_ref[rows, :], saq_ref[rows, :], 16).astype(BF16)
        ka_ref[0, rows, :] = _rope(p[:, 512:640], ca_ref[rows, :], sa_ref[rows, :], 16).astype(BF16)
        lane = lax.broadcasted_iota(jnp.int32, (TOK_TILE, LANES), 1)
        va_ref[0, rows, :] = jnp.concatenate(
            [p[:, 640:768], jnp.where(lane == 0, 1.0, 0.0)], axis=-1).astype(BF16)

        cq = (_rms(p[:, 768:1024]) * qn_ref[...]).astype(BF16)
        qb = _dot(cq, wuq_ref[...])
        qb_ref[0, rows, :] = _rope(qb, cbq_ref[rows, :], sbq_ref[rows, :], 8).astype(BF16)
        ckv = (_rms(p[:, 1024:1152]) * kvn_ref[...]).astype(BF16)
        kv = _dot(ckv, wukv_ref[...])
        kr = _rope(p[:, 1152:1280], cb_ref[rows, :], sb_ref[rows, :], 8)
        kb_ref[0, rows, :] = (kv[:, 0:512] + jnp.concatenate([kr] * B_HEADS, axis=-1)).astype(BF16)
        vb_ref[0, rows, :] = _per_head_with_ones(kv[:, 512:768]).astype(BF16)

        qd_ref[0, rows, :] = _rope(p[:, 1280:1536], c8q_ref[rows, :], s8q_ref[rows, :], 8).astype(BF16)
        kd_ref[0, rows, :] = _rope(p[:, 1536:1792], c8_ref[rows, :], s8_ref[rows, :], 8).astype(BF16)
        vd_ref[0, rows, :] = _per_head_with_ones(p[:, 1792:2048]).astype(BF16)


def _mod_specs():
    return [pl.BlockSpec((1, 1, 6 * D_MODEL), lambda b, t: (b, 0, 0)),
            pl.BlockSpec((1, 1, 6 * D_MODEL), lambda b, t: (BATCH, 0, 0))]


def _inproj(xall, mod_l, g1, w, qn, wuq, kvn, wukv, tabs, layer):
    full = lambda shape: pl.BlockSpec(shape, lambda b, t: (0,) * len(shape))
    step_rows = STEP_CHUNKS * TOK_TILE
    tab_spec = pl.BlockSpec((step_rows, LANES), lambda b, t: (t, 0))
    widths = (512, 128, 256, 512, 512, 512, 256, 256, 512)
    return pl.pallas_call(
        _inproj_kernel,
        out_shape=[jax.ShapeDtypeStruct((BATCH, NT_PAD, n), BF16) for n in widths],
        grid=(BATCH, N_TILES // STEP_CHUNKS),
        in_specs=[pl.BlockSpec((1, step_rows, D_MODEL), lambda b, t: (b, t, 0))] + _mod_specs() + [
            full((1, D_MODEL)),
            pl.BlockSpec((1, D_MODEL, 2048), lambda b, t: (layer, 0, 0)),
            full((1, B_Q_RANK)),
            full((B_Q_RANK, 512)),
            full((1, B_KV_RANK)),
            full((B_KV_RANK, 768)),
        ] + [tab_spec] * 12,
        out_specs=[pl.BlockSpec((1, step_rows, n), lambda b, t: (b, t, 0)) for n in widths],
        compiler_params=_params(("arbitrary", "arbitrary")),
        name="in_projection",
    )(xall, mod_l, mod_l, g1, w, qn, wuq, kvn, wukv, *tabs)


def _window_heads(q, kcat, vcat, bias, sink_ref):
    tq = q.shape[0]
    lane = lax.broadcasted_iota(jnp.int32, (1, LANES), 1)
    first_half = lane < HEAD_DIM
    keep = [_indicator(first_half), _indicator(jnp.logical_not(first_half))]
    qs = jnp.concatenate([q[:, j * LANES:(j + 1) * LANES] * keep[g]
                          for g in range(A_KV_HEADS) for j in range(A_GROUP)], axis=0)
    s = _dot_nt(qs, kcat)
    if bias is not None:
        s = s + jnp.concatenate([bias] * A_HEADS, axis=0)
    sink = jnp.concatenate([jnp.full((tq, 1), sink_ref[h] * LOG2_E, F32) for h in range(A_HEADS)], axis=0)
    m = jnp.maximum(jnp.max(s, axis=-1, keepdims=True), sink)
    ol = _dot(jnp.exp2(s - m).astype(BF16), vcat)
    o = ol[:, 0:LANES] / (ol[:, LANES:LANES + 1] + jnp.exp2(sink - m))
    blocks = [jnp.where(first_half, o[j * tq:(j + 1) * tq], o[(A_GROUP + j) * tq:(A_GROUP + j + 1) * tq])
              for j in range(A_GROUP)]
    return jnp.concatenate(blocks, axis=-1).astype(BF16)


A_STEP_BLOCKS = 8
A_LOCAL = 3 * WIN


def _attn_a_latent_kernel(sink_ref, q_ref, k_ref, v_ref, bias_ref, o_ref):
    step = pl.program_id(1)
    last = SEQ // WIN - 1
    k_ctx = k_ref[0, SEQ:NT, :]
    v_ctx = v_ref[0, SEQ:NT, :]
    for j in range(A_STEP_BLOCKS):
        n = step * A_STEP_BLOCKS + j
        start = pl.multiple_of(jnp.clip((n - 1) * WIN, 0, SEQ - A_LOCAL), WIN)
        variant = jnp.where(n == 0, 0, jnp.where(n == last, 2, 1))
        kcat = jnp.concatenate([k_ref[0, pl.ds(start, A_LOCAL), :], k_ctx], axis=0)
        vcat = jnp.concatenate([v_ref[0, pl.ds(start, A_LOCAL), :], v_ctx], axis=0)
        rows = slice(j * WIN, (j + 1) * WIN)
        o_ref[0, rows, :] = _window_heads(q_ref[0, rows, :], kcat, vcat, bias_ref[variant], sink_ref)


def _attn_a_latent(sink, qa, ka, va, bias):
    tq = A_STEP_BLOCKS * WIN
    return pl.pallas_call(
        _attn_a_latent_kernel,
        out_shape=jax.ShapeDtypeStruct((BATCH, NT_PAD, 512), BF16),
        grid=(BATCH, SEQ // tq),
        in_specs=[pl.BlockSpec(memory_space=pltpu.SMEM),
                  pl.BlockSpec((1, tq, 512), lambda b, n: (b, n, 0)),
                  pl.BlockSpec((1, NT_PAD, LANES), lambda b, n: (b, 0, 0)),
                  pl.BlockSpec((1, NT_PAD, 2 * LANES), lambda b, n: (b, 0, 0)),
                  pl.BlockSpec((3, WIN, A_LOCAL + CTX_LEN), lambda b, n: (0, 0, 0))],
        out_specs=pl.BlockSpec((1, tq, 512), lambda b, n: (b, n, 0)),
        compiler_params=_params(("arbitrary", "arbitrary")),
        name="attn_window_latent",
    )(sink, qa, ka, va, bias)


def _attn_a_ctx(sink, qa, ka, va, prev_out):
    ctx_blk = SEQ // CTX_LEN
    spec = lambda n: pl.BlockSpec((BATCH, CTX_LEN, n), lambda i: (0, ctx_blk, 0))
    tile = lambda n: pl.BlockSpec((BATCH, TOK_TILE, n), lambda i: (0, N_LAT_TILES, 0))

    def body(sink_ref, q_ref, k_ref, v_ref, prev_ref, o_ref):
        del prev_ref
        for sample in range(BATCH):
            o_ref[sample] = _window_heads(q_ref[sample], k_ref[sample], v_ref[sample], None, sink_ref)

    return pl.pallas_call(
        body,
        out_shape=jax.ShapeDtypeStruct((BATCH, NT_PAD, 512), BF16),
        grid=(1,),
        in_specs=[pl.BlockSpec(memory_space=pltpu.SMEM), tile(512), spec(LANES), spec(2 * LANES),
                  pl.BlockSpec(memory_space=pl.ANY)],
        out_specs=tile(512),
        input_output_aliases={4: 0},
        compiler_params=_params(("arbitrary",)),
        name="attn_window_ctx",
    )(sink, qa, ka, va, prev_out)


def _key_rows(ref, sample):
    return ref[sample, 0:NT, :] if ref.shape[1] == NT_PAD else ref[sample]


MLA_TQ = 2048
MLA_SUB = 512
DIFF_TQ = 1024
DIFF_SUB = 256


def _attn_b_kernel(q_ref, k_ref, v_ref, o_ref):
    for sample in range(q_ref.shape[0]):
        k = _key_rows(k_ref, sample)
        v = _key_rows(v_ref, sample)
        sub_rows = min(MLA_SUB, q_ref.shape[1])
        for sub in range(q_ref.shape[1] // sub_rows):
            rows = slice(sub * sub_rows, (sub + 1) * sub_rows)
            q = q_ref[sample, rows, :]
            outs = []
            for h in range(B_HEADS):
                s = _dot_nt(q[:, h * LANES:(h + 1) * LANES], k[:, h * LANES:(h + 1) * LANES])
                outs.append(_softmax_pv(s, v[:, h * LANES:(h + 1) * LANES], h))
            o_ref[sample, rows, :] = _merge_head_pairs(outs).astype(BF16)


def _full_attention(kernel, name, tq, q, k, v, extra, extra_specs, out_width, q_width, k_width, v_width):
    out = pl.pallas_call(
        kernel,
        out_shape=jax.ShapeDtypeStruct((BATCH, NT_PAD, out_width), BF16),
        grid=(BATCH, SEQ // tq),
        in_specs=[pl.BlockSpec((1, tq, q_width), lambda b, t: (b, t, 0)),
                  pl.BlockSpec((1, NT_PAD, k_width), lambda b, t: (b, 0, 0)),
                  pl.BlockSpec((1, NT_PAD, v_width), lambda b, t: (b, 0, 0))] + extra_specs(2),
        out_specs=pl.BlockSpec((1, tq, out_width), lambda b, t: (b, t, 0)),
        compiler_params=_params(("arbitrary", "arbitrary")),
        name=name + "_latent",
    )(q, k, v, *extra)
    return out


def _full_attention_ctx(kernel, name, q, k, v, extra, extra_specs, out_width, q_width, k_width, v_width,
                        prev_out):
    ctx_blk = SEQ // CTX_LEN
    n_extra = len(extra)

    def body(*refs):
        ins = refs[:3 + n_extra]
        kernel(*ins, refs[-1])

    spec = lambda n: pl.BlockSpec((BATCH, CTX_LEN, n), lambda i: (0, ctx_blk, 0))
    tile = lambda n: pl.BlockSpec((BATCH, TOK_TILE, n), lambda i: (0, N_LAT_TILES, 0))
    return pl.pallas_call(
        body,
        out_shape=jax.ShapeDtypeStruct((BATCH, NT_PAD, out_width), BF16),
        grid=(1,),
        in_specs=[tile(q_width), spec(k_width), spec(v_width)] + extra_specs(1)
                 + [pl.BlockSpec(memory_space=pl.ANY)],
        out_specs=tile(out_width),
        input_output_aliases={3 + n_extra: 0},
        compiler_params=_params(("arbitrary",)),
        name=name + "_ctx",
    )(q, k, v, *extra, prev_out)


def _attn_c_kernel(q_ref, k_ref, v_ref, lq1_ref, lk1_ref, lq2_ref, lk2_ref, sn_ref, o_ref, *, lam_init):
    tq = min(DIFF_SUB, q_ref.shape[1])
    lam = (jnp.exp(jnp.sum(lq1_ref[...] * lk1_ref[...], axis=-1, keepdims=True))
           - jnp.exp(jnp.sum(lq2_ref[...] * lk2_ref[...], axis=-1, keepdims=True)) + lam_init)
    lane = lax.broadcasted_iota(jnp.int32, (1, C_HEADS * C_V), 1)
    units = [(sample, sub) for sample in range(q_ref.shape[0]) for sub in range(q_ref.shape[1] // tq)]
    for sample, sub in units:
        k = _key_rows(k_ref, sample)
        v = _key_rows(v_ref, sample)
        rows = slice(sub * tq, (sub + 1) * tq)
        q = q_ref[sample, rows, :]
        outs = []
        for h in range(C_HEADS):
            lo = h * C_V
            q1 = q * _indicator((lane >= lo) & (lane < lo + C_DH))
            q2 = q * _indicator((lane >= lo + C_DH) & (lane < lo + C_V))
            s = _dot_nt(jnp.concatenate([q1, q2], axis=0), k)
            pv = _softmax_pv(s, v[:, h * LANES:(h + 1) * LANES], h)
            outs.append(pv[0:tq] - lam * pv[tq:2 * tq])
        acc = _merge_head_pairs(outs)
        sq = acc * acc
        inv = jnp.zeros((tq, C_HEADS * C_V), F32)
        for h in range(C_HEADS):
            in_head = (lane >= h * C_V) & (lane < (h + 1) * C_V)
            ms = jnp.sum(jnp.where(in_head, sq, 0.0), axis=-1, keepdims=True) * (1.0 / C_V)
            inv = inv + jnp.where(in_head, lax.rsqrt(ms + EPS), 0.0)
        o_ref[sample, rows, :] = ((acc * inv) * sn_ref[...] * (1.0 - lam_init)).astype(BF16)


def _outproj_kernel(x_ref, oa_ref, ob_ref, od_ref, w_ref, modb_ref, modc_ref, g2_ref, wr_ref,
                    xo_ref, h2_ref, aff_ref, *, n_chunks):
    w_hi, w_lo = _split_bf16(wr_ref[...])
    w_router = jnp.concatenate([w_hi, w_lo], axis=-1)
    lane = lax.broadcasted_iota(jnp.int32, (1, LANES), 1)
    mixed = _dot(jnp.concatenate([oa_ref[0], ob_ref[0], od_ref[0]], axis=-1), w_ref[0])
    for j in range(n_chunks):
        rows = slice(j * TOK_TILE, (j + 1) * TOK_TILE)
        mod = _chunk_mod(modb_ref, modc_ref, j, n_chunks)
        g1 = mod[:, 2 * D_MODEL:3 * D_MODEL]
        sh2 = mod[:, 3 * D_MODEL:4 * D_MODEL]
        sc2 = mod[:, 4 * D_MODEL:5 * D_MODEL]
        x = x_ref[0, rows, :] + g1 * mixed[rows]
        xo_ref[0, rows, :] = x
        h2 = (_rms(x) * g2_ref[...]) * (1.0 + sc2) + sh2
        h_hi, h_lo = _split_bf16(h2)
        h2_ref[0, rows, :] = h_hi
        prod = _dot(jnp.concatenate([h_hi, h_lo], axis=0), w_router)
        logits = (prod[0:TOK_TILE, 0:LANES] + prod[0:TOK_TILE, LANES:2 * LANES]
                  + prod[TOK_TILE:2 * TOK_TILE, 0:LANES])
        aff_ref[0, rows, :] = _softmax_rows(jnp.where(lane < N_EXPERTS, logits, NEG_INF))


def _outproj(xall, oa, ob, od, w, mod_l, g2, wr, with_ctx, layer):
    n_chunks = STEP_CHUNKS if with_ctx else LAT_STEP_CHUNKS
    n_steps = (N_TILES if with_ctx else N_LAT_TILES) // n_chunks
    full = lambda shape: pl.BlockSpec(shape, lambda b, t: (0,) * len(shape))
    tok = lambda n: pl.BlockSpec((1, n_chunks * TOK_TILE, n), lambda b, t: (b, t, 0))
    return pl.pallas_call(
        functools.partial(_outproj_kernel, n_chunks=n_chunks),
        out_shape=[jax.ShapeDtypeStruct((BATCH, NT_PAD, D_MODEL), F32),
                   jax.ShapeDtypeStruct((BATCH, NT_PAD, D_MODEL), BF16),
                   jax.ShapeDtypeStruct((BATCH, NT_PAD, LANES), F32)],
        grid=(BATCH, n_steps),
        in_specs=[tok(D_MODEL), tok(512), tok(256), tok(256),
                  pl.BlockSpec((1, D_MODEL, D_MODEL), lambda b, t: (layer, 0, 0))] + _mod_specs()
                 + [full((1, D_MODEL)), full((D_MODEL, LANES))],
        out_specs=[tok(D_MODEL), tok(D_MODEL), tok(LANES)],
        compiler_params=_params(("arbitrary", "arbitrary")),
        name="out_projection",
    )(xall, oa, ob, od, w, mod_l, mod_l, g2, wr)


def _prefix_count(flags_f32, tri_ref):
    rows, n = flags_f32.shape
    run = jnp.zeros((rows, 1), F32)
    outs = []
    for c in range(n // LANES):
        blk = flags_f32[:, c * LANES:(c + 1) * LANES]
        outs.append(_dot(blk.astype(BF16), tri_ref[...]) + run)
        run = run + jnp.sum(blk, axis=-1, keepdims=True)
    return jnp.concatenate(outs, axis=-1)


def _select_slots(a, cap, tri_ref):
    bits = pltpu.bitcast(a, jnp.int32)

    def step(i, t):
        cand = t | jnp.left_shift(jnp.int32(1), 30 - i)
        cnt = jnp.sum((bits >= cand).astype(jnp.int32), axis=-1, keepdims=True)
        return jnp.where(cnt >= cap, cand, t)

    thr = lax.fori_loop(0, 31, step, jnp.zeros((a.shape[0], 1), jnp.int32))
    above = bits > thr
    tied = bits == thr
    need = cap - jnp.sum(above.astype(F32), axis=-1, keepdims=True)
    tied_f = tied.astype(F32)
    sel = above | (tied & (_prefix_count(tied_f, tri_ref) < need))
    sel_f = sel.astype(F32)
    return jnp.where(sel, _prefix_count(sel_f, tri_ref), -1.0)


def _route_kernel(aff_ref, tri_ref, row_ref, arow_ref, col_ref, *, with_ctx):
    lat, ctx = [], []
    for b in range(BATCH):
        lat.append(aff_ref[b, 0:SEQ, :].T[0:N_EXPERTS])
        if with_ctx:
            ctx.append(aff_ref[b, SEQ:NT, :].T[0:N_EXPERTS])
    aff_lat = jnp.concatenate(lat, axis=0)
    arow_ref[:, 0:SEQ] = aff_lat
    slot_lat = _select_slots(aff_lat, CAP_LAT, tri_ref)
    if with_ctx:
        aff_ctx = jnp.concatenate(ctx, axis=0)
        arow_ref[:, SEQ:NT] = aff_ctx
        slot_ctx = _select_slots(aff_ctx, CAP_CTX, tri_ref)
        slots = jnp.concatenate([slot_lat, slot_ctx], axis=-1)
    else:
        slots = slot_lat
    n = slots.shape[1]
    row_ref[:, 0:n] = slots
    pad = jnp.full((LANES - N_EXPERTS, n), -1.0, F32)
    for b in range(BATCH):
        blk = jnp.concatenate([slots[b * N_EXPERTS:(b + 1) * N_EXPERTS], pad], axis=0)
        col_ref[b, 0:n, :] = blk.T
        if with_ctx and NT_PAD > NT:
            col_ref[b, NT:NT_PAD, :] = jnp.full((NT_PAD - NT, LANES), -1.0, F32)


def _route(aff, tri, with_ctx):
    vm = pl.BlockSpec(memory_space=pltpu.VMEM)
    return pl.pallas_call(
        functools.partial(_route_kernel, with_ctx=with_ctx),
        out_shape=[jax.ShapeDtypeStruct((BATCH * N_EXPERTS, NT), F32),
                   jax.ShapeDtypeStruct((BATCH * N_EXPERTS, NT), F32),
                   jax.ShapeDtypeStruct((BATCH, NT_PAD, LANES), F32)],
        in_specs=[vm, vm],
        out_specs=[vm, vm, vm],
        compiler_params=pltpu.CompilerParams(vmem_limit_bytes=VMEM_LIMIT),
        name="route_topc",
    )(aff, tri)


GATHER_GROUP = 4


def _gather_kernel(row_ref, arow_ref, h_ref, *out_refs, with_ctx):
    slot = lax.broadcasted_iota(jnp.int32, (CAP_LAT, 1), 0).astype(F32)

    def pick(e0, cols, n_slots, h_rows, x_ref, g_ref):
        hots = []
        for e in range(e0, e0 + GATHER_GROUP):
            hot = jnp.where(row_ref[e:e + 1, cols] == slot[0:n_slots], 1.0, 0.0)
            gate = jnp.sum(hot * arow_ref[e:e + 1, cols], axis=-1, keepdims=True)
            g_ref[e] = jnp.broadcast_to(gate, (n_slots, LANES))
            hots.append(hot.astype(BF16))
        x = _dot(jnp.concatenate(hots, axis=0), h_rows).astype(BF16)
        for i in range(GATHER_GROUP):
            x_ref[e0 + i] = x[i * n_slots:(i + 1) * n_slots]

    for e0 in range(0, N_EXPERTS, GATHER_GROUP):
        pick(e0, slice(0, SEQ), CAP_LAT, h_ref[0, 0:SEQ, :], out_refs[0], out_refs[1])
        if with_ctx:
            pick(e0, slice(SEQ, NT), CAP_CTX, h_ref[0, SEQ:NT, :], out_refs[2], out_refs[3])


def _gather(slot_rows, aff_rows, h2, with_ctx):
    out_shape = [jax.ShapeDtypeStruct((N_EXPERTS, SEQ, D_MODEL), BF16),
                 jax.ShapeDtypeStruct((N_EXPERTS, SEQ, LANES), F32)]
    out_specs = [pl.BlockSpec((N_EXPERTS, CAP_LAT, D_MODEL), lambda b: (0, b, 0)),
                 pl.BlockSpec((N_EXPERTS, CAP_LAT, LANES), lambda b: (0, b, 0))]
    if with_ctx:
        out_shape += [jax.ShapeDtypeStruct((N_EXPERTS, CTX_LEN, D_MODEL), BF16),
                      jax.ShapeDtypeStruct((N_EXPERTS, CTX_LEN, LANES), F32)]
        out_specs += [pl.BlockSpec((N_EXPERTS, CAP_CTX, D_MODEL), lambda b: (0, b, 0)),
                      pl.BlockSpec((N_EXPERTS, CAP_CTX, LANES), lambda b: (0, b, 0))]
    row_spec = pl.BlockSpec((N_EXPERTS, NT), lambda b: (b, 0))
    return pl.pallas_call(
        functools.partial(_gather_kernel, with_ctx=with_ctx),
        out_shape=out_shape,
        grid=(BATCH,),
        in_specs=[row_spec, row_spec, pl.BlockSpec((1, NT_PAD, D_MODEL), lambda b: (b, 0, 0))],
        out_specs=out_specs,
        compiler_params=_params(("arbitrary",)),
        name="expert_gather",
    )(slot_rows, aff_rows, h2)


def _ffn_kernel(*refs, with_ctx):
    if with_ctx:
        xl_ref, gl_ref, xc_ref, gc_ref, wg_ref, wu_ref, wd_ref, y_ref = refs
    else:
        xl_ref, gl_ref, wg_ref, wu_ref, wd_ref, y_ref = refs
    wg = wg_ref[0, 0].astype(BF16)
    wu = wu_ref[0, 0].astype(BF16)
    wd = wd_ref[0, 0].astype(BF16)

    def swiglu(x, slot_gate):
        gate = _dot(x, wg)
        hid = (gate / (1.0 + jnp.exp(-gate))) * _dot(x, wu)
        scale = jnp.concatenate([slot_gate] * (D_MODEL // LANES), axis=-1)
        return (_dot(hid.astype(BF16), wd) * scale).astype(BF16)

    for r in range(FFN_LAT_TILES):
        rows = slice(r * FFN_TILE, (r + 1) * FFN_TILE)
        y_ref[0, rows, :] = swiglu(xl_ref[0, rows, :], gl_ref[0, rows, :])
    if with_ctx:
        y_ref[0, SEQ:NT, :] = swiglu(xc_ref[0], gc_ref[0])


def _ffn(gathered, w_gate, w_up, w_down, layer):
    with_ctx = len(gathered) == 4
    ins = list(gathered) + [w_gate, w_up, w_down]
    in_specs = [pl.BlockSpec((1, SEQ, D_MODEL), lambda e: (e, 0, 0)),
                pl.BlockSpec((1, SEQ, LANES), lambda e: (e, 0, 0))]
    if with_ctx:
        in_specs += [pl.BlockSpec((1, CTX_LEN, D_MODEL), lambda e: (e, 0, 0)),
                     pl.BlockSpec((1, CTX_LEN, LANES), lambda e: (e, 0, 0))]
    in_specs += [pl.BlockSpec((1, 1, D_MODEL, D_EXPERT), lambda e: (layer, e, 0, 0)),
                 pl.BlockSpec((1, 1, D_MODEL, D_EXPERT), lambda e: (layer, e, 0, 0)),
                 pl.BlockSpec((1, 1, D_EXPERT, D_MODEL), lambda e: (layer, e, 0, 0))]
    return pl.pallas_call(
        functools.partial(_ffn_kernel, with_ctx=with_ctx),
        out_shape=jax.ShapeDtypeStruct((N_EXPERTS, NT, D_MODEL), BF16),
        grid=(N_EXPERTS,),
        in_specs=in_specs,
        out_specs=pl.BlockSpec((1, NT, D_MODEL), lambda e: (e, 0, 0)),
        compiler_params=_params(("arbitrary",)),
        name="expert_ffn",
    )(*ins)


COMBINE_ROWS = 1024


def _scatter_add(col, ys, n_slots):
    lane = lax.broadcasted_iota(jnp.int32, (1, n_slots), 1).astype(F32)
    hot = [_indicator(col[:, e:e + 1] == lane) for e in range(N_EXPERTS)]
    return _dot(jnp.concatenate(hot, axis=-1), jnp.concatenate(ys, axis=0))


def _combine_latent_kernel(*refs, final):
    if final:
        x_ref, col_ref, y_ref, mod_ref, fn_ref, o_ref = refs
    else:
        x_ref, col_ref, y_ref, mod_ref, o_ref = refs
    g2 = mod_ref[0][:, 5 * D_MODEL:6 * D_MODEL]
    acc = _scatter_add(col_ref[0], [y_ref[e] for e in range(N_EXPERTS)], CAP_LAT)
    x = x_ref[0] + g2 * acc
    if final:
        x = _rms(x) * fn_ref[...]
    o_ref[0] = x


def _combine_ctx_kernel(x_ref, col_ref, y_ref, mod_ref, prev_ref, o_ref):
    del prev_ref
    g2 = mod_ref[0][:, 5 * D_MODEL:6 * D_MODEL]
    n_pairs = N_EXPERTS * CAP_CTX
    lane = lax.broadcasted_iota(jnp.int32, (1, n_pairs), 1)
    slot_of_lane = (lane % CAP_CTX).astype(F32)
    spread = _indicator(lax.broadcasted_iota(jnp.int32, (LANES, n_pairs), 0)
                        == lax.broadcasted_iota(jnp.int32, (LANES, n_pairs), 1) // CAP_CTX)
    for b in range(BATCH):
        slot_on_lane = _dot(col_ref[b].astype(BF16), spread)
        hot = _indicator(slot_on_lane == slot_of_lane)
        ys = jnp.concatenate([y_ref[e, b * CAP_CTX:(b + 1) * CAP_CTX, :] for e in range(N_EXPERTS)], axis=0)
        o_ref[b] = x_ref[b] + g2 * _dot(hot, ys)


def _combine(xall, slot_cols, y, mod_l, with_ctx, final_norm=None):
    final = final_norm is not None
    n_rows = SEQ if final else NT_PAD
    tok = lambda n: pl.BlockSpec((1, COMBINE_ROWS, n), lambda b, t: (b, t, 0))
    ins = [xall, slot_cols, y, mod_l]
    in_specs = [tok(D_MODEL), tok(LANES),
                pl.BlockSpec((N_EXPERTS, CAP_LAT, D_MODEL), lambda b, t: (0, b, 0)),
                pl.BlockSpec((1, 1, 6 * D_MODEL), lambda b, t: (b, 0, 0))]
    if final:
        ins.append(final_norm)
        in_specs.append(pl.BlockSpec((1, D_MODEL), lambda b, t: (0, 0)))
    out = pl.pallas_call(
        functools.partial(_combine_latent_kernel, final=final),
        out_shape=jax.ShapeDtypeStruct((BATCH, n_rows, D_MODEL), F32),
        grid=(BATCH, SEQ // COMBINE_ROWS),
        in_specs=in_specs,
        out_specs=tok(D_MODEL),
        compiler_params=_params(("arbitrary", "arbitrary")),
        name="expert_combine",
    )(*ins)
    if not with_ctx:
        return out
    ctx_blk = SEQ // CTX_LEN
    ctx_rows = lambda n: pl.BlockSpec((BATCH, CTX_LEN, n), lambda i: (0, ctx_blk, 0))
    return pl.pallas_call(
        _combine_ctx_kernel,
        out_shape=jax.ShapeDtypeStruct((BATCH, NT_PAD, D_MODEL), F32),
        grid=(1,),
        in_specs=[ctx_rows(D_MODEL), ctx_rows(LANES),
                  pl.BlockSpec((N_EXPERTS, CTX_LEN, D_MODEL), lambda i: (0, ctx_blk, 0)),
                  pl.BlockSpec((1, 1, 6 * D_MODEL), lambda i: (BATCH, 0, 0)),
                  pl.BlockSpec(memory_space=pl.ANY)],
        out_specs=ctx_rows(D_MODEL),
        input_output_aliases={4: 0},
        compiler_params=_params(("arbitrary",)),
        name="expert_combine_ctx",
    )(xall, slot_cols, y, mod_l, out)


def _rope_tables():
    t = jnp.arange(SEQ)
    pos = jnp.stack([t // GRID_W, t % GRID_W], axis=0)

    def table(n_f, lanes_per_unit):
        inv = ROPE_BASE ** (-jnp.arange(n_f, dtype=F32) / n_f)
        d = np.arange(lanes_per_unit)
        axis, second, f = d // (2 * n_f), (d // n_f) % 2, d % n_f
        ang = pos[axis].T.astype(F32) * inv[f][None, :]
        cos = jnp.cos(ang)
        sin = jnp.sin(ang) * jnp.where(second == 1, 1.0, -1.0)[None, :]
        return cos, sin

    def finish(cos, sin, scale):
        cos = jnp.concatenate([cos, jnp.ones((NT_PAD - SEQ, LANES), F32)], axis=0)
        sin = jnp.concatenate([sin, jnp.zeros((NT_PAD - SEQ, LANES), F32)], axis=0)
        scale = scale * LOG2_E
        return [cos, sin, cos * scale, sin * scale]

    cos_a, sin_a = table(HEAD_DIM // 4, HEAD_DIM)
    tabs_a = finish(jnp.tile(cos_a, (1, 2)), jnp.tile(sin_a, (1, 2)), HEAD_DIM ** -0.5)
    cos_8, sin_8 = table(B_ROPE // 4, B_ROPE)
    tabs_8 = finish(jnp.tile(cos_8, (1, 4)), jnp.tile(sin_8, (1, 4)), C_DH ** -0.5)
    ones = jnp.ones((SEQ, B_NOPE), F32)
    pad1 = jnp.ones((SEQ, LANES - B_NOPE - B_ROPE), F32)
    cos_b = jnp.concatenate([ones, cos_8, pad1], axis=1)
    sin_b = jnp.concatenate([0 * ones, sin_8, 0 * pad1], axis=1)
    tabs_b = finish(cos_b, sin_b, (B_NOPE + B_ROPE) ** -0.5)
    return tabs_a + tabs_8 + tabs_b


def _window_bias():
    qi = np.arange(WIN)[:, None]
    kj = np.arange(A_LOCAL)[None, :]
    out = np.zeros((3, WIN, A_LOCAL + CTX_LEN), np.float32)
    for v in range(3):
        out[v, :, :A_LOCAL] = np.where(np.abs(kj - v * WIN - qi) <= WIN, 0.0, NEG_INF)
    return jnp.asarray(out)


def _prep_weights(w_in, w_uq, w_ukv, w_out, w_router):
    offs = np.cumsum((0, 512, 128, 128, 256, 128, 32, 256, 256, 256))
    seg = lambda i: w_in[:, :, offs[i]:offs[i + 1]]
    wq = seg(0).reshape(DEPTH, D_MODEL, A_KV_HEADS, A_GROUP, HEAD_DIM)
    wq = jnp.swapaxes(wq, 2, 3).reshape(DEPTH, D_MODEL, A_HEADS * HEAD_DIM)
    kr = jnp.pad(seg(5), ((0, 0), (0, 0), (B_NOPE, LANES - B_NOPE - B_ROPE)))
    w = jnp.concatenate([wq, seg(1), seg(2), seg(3), seg(4), kr,
                         seg(6), seg(7), seg(8)], axis=-1).astype(BF16)
    wuq = w_uq.reshape(DEPTH, B_Q_RANK, B_HEADS, B_NOPE + B_ROPE)
    wuq = jnp.pad(wuq, ((0, 0), (0, 0), (0, 0), (0, LANES - B_NOPE - B_ROPE)))
    wuq = wuq.reshape(DEPTH, B_Q_RANK, B_HEADS * LANES).astype(BF16)
    wukv = w_ukv.reshape(DEPTH, B_KV_RANK, B_HEADS, B_NOPE + B_V)
    wk = jnp.pad(wukv[..., :B_NOPE], ((0, 0), (0, 0), (0, 0), (0, LANES - B_NOPE)))
    wukv = jnp.concatenate([wk.reshape(DEPTH, B_KV_RANK, B_HEADS * LANES),
                            wukv[..., B_NOPE:].reshape(DEPTH, B_KV_RANK, B_HEADS * B_V)],
                           axis=-1).astype(BF16)
    wo_a = w_out[:, :A_HEADS * HEAD_DIM].reshape(DEPTH, A_KV_HEADS, A_GROUP, HEAD_DIM, D_MODEL)
    wo_a = jnp.swapaxes(wo_a, 1, 2).reshape(DEPTH, A_HEADS * HEAD_DIM, D_MODEL)
    wo = jnp.concatenate([wo_a, w_out[:, A_HEADS * HEAD_DIM:]], axis=1).astype(BF16)
    wr = jnp.pad(w_router, ((0, 0), (0, 0), (0, LANES - N_EXPERTS)))
    return w, wuq, wukv, wo, wr


def kernel(x, c, ctx, c_ctx, norm1, norm2, w_ada, b_ada, w_in, sink, mla_q_norm, w_uq, mla_kv_norm, w_ukv,
           lam_q1, lam_k1, lam_q2, lam_k2, diff_norm, w_out, w_router, w_gate, w_up, w_down, final_norm):
    xall = jnp.concatenate([x, ctx, jnp.zeros((BATCH, NT_PAD - NT, D_MODEL), F32)], axis=1)
    cc = jnp.concatenate([c, c_ctx[None, :], jnp.zeros((16 - BATCH - 1, D_MODEL), F32)], axis=0)
    mod = _modulation(cc, w_ada, b_ada).reshape(DEPTH, 16, 1, 6 * D_MODEL)
    tabs = _rope_tables()
    bias = _window_bias()
    tri = jnp.asarray(np.triu(np.ones((LANES, LANES), np.float32), k=1), BF16)
    w, wuq, wukv, wo, wr = _prep_weights(w_in, w_uq, w_ukv, w_out, w_router)
    sub_norm = jnp.tile(diff_norm, (1, C_HEADS)).reshape(DEPTH, 1, C_HEADS * C_V)
    row = lambda a, l: a[l].reshape(1, -1)

    for l in range(DEPTH):
        need_ctx = l < DEPTH - 1
        qa, ka, va, qb, kb, vb, qd, kd, vd = _inproj(
            xall, mod[l], row(norm1, l), w, row(mla_q_norm, l), wuq[l], row(mla_kv_norm, l), wukv[l], tabs, l)

        oa = _attn_a_latent(sink[l], qa, ka, va, bias)
        ob = _full_attention(_attn_b_kernel, "attn_mla", MLA_TQ, qb, kb, vb, [], lambda n: [],
                             256, 512, 512, 512)
        lam_init = 0.8 - 0.6 * math.exp(-0.3 * l)
        c_kernel = functools.partial(_attn_c_kernel, lam_init=lam_init)
        c_extra = [row(lam_q1, l), row(lam_k1, l), row(lam_q2, l), row(lam_k2, l), sub_norm[l]]

        def c_specs(n_grid, c_extra=c_extra):
            zmap = (lambda b, t: (0, 0)) if n_grid == 2 else (lambda b: (0, 0))
            return [pl.BlockSpec(a.shape, zmap) for a in c_extra]

        od = _full_attention(c_kernel, "attn_diff", DIFF_TQ, qd, kd, vd, c_extra, c_specs, 256, 256, 256, 512)
        if need_ctx:
            oa = _attn_a_ctx(sink[l], qa, ka, va, oa)
            ob = _full_attention_ctx(_attn_b_kernel, "attn_mla", qb, kb, vb, [], lambda n: [],
                                     256, 512, 512, 512, ob)
            od = _full_attention_ctx(c_kernel, "attn_diff", qd, kd, vd, c_extra, c_specs,
                                     256, 256, 256, 512, od)

        xall, h2, aff = _outproj(xall, oa, ob, od, wo, mod[l], row(norm2, l), wr[l], need_ctx, l)
        slot_rows, aff_rows, slot_cols = _route(aff, tri, need_ctx)
        gathered = _gather(slot_rows, aff_rows, h2, need_ctx)
        y = _ffn(gathered, w_gate, w_up, w_down, l)
        xall = _combine(xall, slot_cols, y, mod[l], need_ctx,
                        final_norm=None if need_ctx else final_norm.reshape(1, D_MODEL))
    return xall
```

```python
import functools
import math

import numpy as np
import jax
import jax.numpy as jnp
from jax import lax
from jax.experimental import pallas as pl
from jax.experimental.pallas import tpu as pltpu

D_MODEL = 1024
BATCH = 8
SEQ = 2048
DEPTH = 4
CTX_LEN = 256
NT = SEQ + CTX_LEN
GRID_W = 64
HEAD_DIM = 64
ROPE_BASE = 10000.0
EPS = 1e-6
NEG_INF = -1e30
LOG2_E = math.log2(math.e)

WIN = 128
A_HEADS = 8
A_KV_HEADS = 2
A_GROUP = A_HEADS // A_KV_HEADS
B_HEADS = 4
B_NOPE = 64
B_ROPE = 32
B_V = 64
B_Q_RANK = 256
B_KV_RANK = 128
C_HEADS = 4
C_DH = 32
C_V = 2 * C_DH
N_EXPERTS = 16
EC_FACTOR = 2
D_EXPERT = 512
CAP_LAT = EC_FACTOR * SEQ // N_EXPERTS
CAP_CTX = EC_FACTOR * CTX_LEN // N_EXPERTS

LANES = 128
TOK_TILE = 256
NT_PAD = SEQ + TOK_TILE
N_TILES = NT_PAD // TOK_TILE
N_LAT_TILES = SEQ // TOK_TILE
STEP_CHUNKS = 3
LAT_STEP_CHUNKS = 2
FFN_TILE = 1024
FFN_LAT_TILES = SEQ // FFN_TILE
VMEM_LIMIT = 56 * 1024 * 1024

F32 = jnp.float32
BF16 = jnp.bfloat16

_NT_DIMS = (((1,), (1,)), ((), ()))


def _params(sem):
    return pltpu.CompilerParams(dimension_semantics=sem, vmem_limit_bytes=VMEM_LIMIT)


def _dot(a, b):
    return jnp.dot(a, b, preferred_element_type=F32)


def _dot_nt(a, b):
    return lax.dot_general(a, b, _NT_DIMS, preferred_element_type=F32)


def _split_bf16(v):
    hi = v.astype(BF16)
    lo = (v - hi.astype(F32)).astype(BF16)
    return hi, lo


def _dot3(a, b):
    a_hi, a_lo = _split_bf16(a)
    b_hi, b_lo = _split_bf16(b)
    return _dot(a_hi, b_hi) + _dot(a_lo, b_hi) + _dot(a_hi, b_lo)


def _indicator(cond):
    return jnp.where(cond, 1.0, 0.0).astype(BF16)


def _rms(v):
    return v * lax.rsqrt(jnp.mean(v * v, axis=-1, keepdims=True) + EPS)


def _softmax_pv(s, v_blk, head):
    m = jnp.max(s, axis=-1, keepdims=True)
    ol = _dot(jnp.exp2(s - m).astype(BF16), v_blk)
    ones_lane = 64 if head % 2 == 0 else 0
    return ol / ol[:, ones_lane:ones_lane + 1]


def _merge_head_pairs(per_head):
    lane = lax.broadcasted_iota(jnp.int32, (1, LANES), 1)
    blocks = [jnp.where(lane < 64, per_head[j], per_head[j + 1]) for j in range(0, len(per_head), 2)]
    return jnp.concatenate(blocks, axis=-1)


def _softmax_rows(s):
    m = jnp.max(s, axis=-1, keepdims=True)
    e = jnp.exp(s - m)
    return e / jnp.sum(e, axis=-1, keepdims=True)


def _mod_kernel(c_ref, w_ref, b_ref, o_ref):
    c = c_ref[...]
    s = c / (1.0 + jnp.exp(-c))
    o_ref[0] = _dot3(s, w_ref[0]) + b_ref[0]


def _modulation(cc, w_ada, b_ada):
    tn = 1536
    return pl.pallas_call(
        _mod_kernel,
        out_shape=jax.ShapeDtypeStruct((DEPTH, 16, 6 * D_MODEL), F32),
        grid=(DEPTH, 6 * D_MODEL // tn),
        in_specs=[
            pl.BlockSpec((16, D_MODEL), lambda l, j: (0, 0)),
            pl.BlockSpec((1, D_MODEL, tn), lambda l, j: (l, 0, j)),
            pl.BlockSpec((1, 1, tn), lambda l, j: (l, 0, j)),
        ],
        out_specs=pl.BlockSpec((1, 16, tn), lambda l, j: (l, 0, j)),
        compiler_params=_params(("arbitrary", "arbitrary")),
        name="adaln_mod",
    )(cc, w_ada, b_ada.reshape(DEPTH, 1, 6 * D_MODEL))


def _rope(p, cos, sin, half):
    lane = lax.broadcasted_iota(jnp.int32, (1, LANES), 1)
    first = (lane // half) % 2 == 0
    outs = []
    for j in range(p.shape[1] // LANES):
        blk = p[:, j * LANES:(j + 1) * LANES]
        partner = jnp.where(first, pltpu.roll(blk, LANES - half, 1), pltpu.roll(blk, half, 1))
        outs.append(blk * cos + partner * sin)
    return outs[0] if len(outs) == 1 else jnp.concatenate(outs, axis=-1)


def _per_head_with_ones(v):
    lane = lax.broadcasted_iota(jnp.int32, (1, LANES), 1)
    outs = []
    for j in range(v.shape[1] // LANES):
        blk = v[:, j * LANES:(j + 1) * LANES]
        outs.append(jnp.where(lane < 64, blk, jnp.where(lane == 64, 1.0, 0.0)))
        outs.append(jnp.where(lane >= 64, blk, jnp.where(lane == 0, 1.0, 0.0)))
    return jnp.concatenate(outs, axis=-1)


def _chunk_mod(modb_ref, modc_ref, j, n_chunks):
    mod = modb_ref[0]
    if n_chunks == STEP_CHUNKS and j == n_chunks - 1:
        mod = jnp.where(pl.program_id(1) == N_TILES // STEP_CHUNKS - 1, modc_ref[0], mod)
    return mod


def _stream_specs(split):
    if not split:
        return [pl.BlockSpec((1, STEP_CHUNKS * TOK_TILE, D_MODEL), lambda b, t: (b, t, 0))]
    last = N_LAT_TILES - 1
    lat = lambda j: pl.BlockSpec((1, TOK_TILE, D_MODEL),
                                 lambda b, t: (b, jnp.minimum(STEP_CHUNKS * t + j, last), 0))
    return [lat(0), lat(1), lat(2), pl.BlockSpec((1, CTX_LEN, D_MODEL), lambda b, t: (b, 0, 0))]


def _chunk_rows(x_refs, j):
    if len(x_refs) == 1:
        return x_refs[0][0, j * TOK_TILE:(j + 1) * TOK_TILE, :]
    if j < STEP_CHUNKS - 1:
        return x_refs[j][0]
    is_ctx = pl.program_id(1) == N_TILES // STEP_CHUNKS - 1
    return jnp.where(is_ctx, x_refs[3][0], x_refs[2][0])


def _inproj_kernel(*refs, n_x):
    x_refs = refs[:n_x]
    (modb_ref, modc_ref, g1_ref, w_ref, qn_ref, wuq_ref, kvn_ref, wukv_ref,
     ca_ref, sa_ref, caq_ref, saq_ref, c8_ref, s8_ref, c8q_ref, s8q_ref,
     cb_ref, sb_ref, cbq_ref, sbq_ref,
     qa_ref, ka_ref, va_ref, qb_ref, kb_ref, vb_ref, qd_ref, kd_ref, vd_ref) = refs[n_x:]
    for j in range(STEP_CHUNKS):
        rows = slice(j * TOK_TILE, (j + 1) * TOK_TILE)
        x = _chunk_rows(x_refs, j)
        mod = _chunk_mod(modb_ref, modc_ref, j, STEP_CHUNKS)
        sh1 = mod[:, 0:D_MODEL]
        sc1 = mod[:, D_MODEL:2 * D_MODEL]
        h = (_rms(x) * g1_ref[...]) * (1.0 + sc1) + sh1
        p = _dot(h.astype(BF16), w_ref[0])

        qa_ref[0, rows, :] = _rope(p[:, 0:512], caq_ref[rows, :], saq_ref[rows, :], 16).astype(BF16)
        ka_ref[0, rows, :] = _rope(p[:, 512:640], ca_ref[rows, :], sa_ref[rows, :], 16).astype(BF16)
        lane = lax.broadcasted_iota(jnp.int32, (TOK_TILE, LANES), 1)
        va_ref[0, rows, :] = jnp.concatenate(
            [p[:, 640:768], jnp.where(lane == 0, 1.0, 0.0)], axis=-1).astype(BF16)

        cq = (_rms(p[:, 768:1024]) * qn_ref[...]).astype(BF16)
        qb = _dot(cq, wuq_ref[...])
        qb_ref[0, rows, :] = _rope(qb, cbq_ref[rows, :], sbq_ref[rows, :], 8).astype(BF16)
        ckv = (_rms(p[:, 1024:1152]) * kvn_ref[...]).astype(BF16)
        kv = _dot(ckv, wukv_ref[...])
        kr = _rope(p[:, 1152:1280], cb_ref[rows, :], sb_ref[rows, :], 8)
        kb_ref[0, rows, :] = (kv[:, 0:512] + jnp.concatenate([kr] * B_HEADS, axis=-1)).astype(BF16)
        vb_ref[0, rows, :] = _per_head_with_ones(kv[:, 512:768]).astype(BF16)

        qd_ref[0, rows, :] = _rope(p[:, 1280:1536], c8q_ref[rows, :], s8q_ref[rows, :], 8).astype(BF16)
        kd_ref[0, rows, :] = _rope(p[:, 1536:1792], c8_ref[rows, :], s8_ref[rows, :], 8).astype(BF16)
        vd_ref[0, rows, :] = _per_head_with_ones(p[:, 1792:2048]).astype(BF16)


def _mod_specs():
    return [pl.BlockSpec((1, 1, 6 * D_MODEL), lambda b, t: (b, 0, 0)),
            pl.BlockSpec((1, 1, 6 * D_MODEL), lambda b, t: (BATCH, 0, 0))]


def _stream_operands(stream):
    return [stream[0]] * 3 + [stream[1]] if isinstance(stream, tuple) else [stream]


def _inproj(stream, mod_l, g1, w, qn, wuq, kvn, wukv, tabs, layer):
    full = lambda shape: pl.BlockSpec(shape, lambda b, t: (0,) * len(shape))
    step_rows = STEP_CHUNKS * TOK_TILE
    tab_spec = pl.BlockSpec((step_rows, LANES), lambda b, t: (t, 0))
    widths = (512, 128, 256, 512, 512, 512, 256, 256, 512)
    xs = _stream_operands(stream)
    return pl.pallas_call(
        functools.partial(_inproj_kernel, n_x=len(xs)),
        out_shape=[jax.ShapeDtypeStruct((BATCH, NT_PAD, n), BF16) for n in widths],
        grid=(BATCH, N_TILES // STEP_CHUNKS),
        in_specs=_stream_specs(len(xs) > 1) + _mod_specs() + [
            full((1, D_MODEL)),
            pl.BlockSpec((1, D_MODEL, 2048), lambda b, t: (layer, 0, 0)),
            full((1, B_Q_RANK)),
            full((B_Q_RANK, 512)),
            full((1, B_KV_RANK)),
            full((B_KV_RANK, 768)),
        ] + [tab_spec] * 12,
        out_specs=[pl.BlockSpec((1, step_rows, n), lambda b, t: (b, t, 0)) for n in widths],
        compiler_params=_params(("arbitrary", "arbitrary")),
        name="in_projection",
    )(*xs, mod_l, mod_l, g1, w, qn, wuq, kvn, wukv, *tabs)


def _window_heads(q, kcat, vcat, bias, sink_ref):
    tq = q.shape[0]
    lane = lax.broadcasted_iota(jnp.int32, (1, LANES), 1)
    acc = [jnp.zeros((tq, LANES), F32) for _ in range(A_GROUP)]
    for g in range(A_KV_HEADS):
        in_half = (lane < HEAD_DIM) if g == 0 else (lane >= HEAD_DIM)
        keep = _indicator(in_half)
        qs = jnp.concatenate([q[:, j * LANES:(j + 1) * LANES] * keep for j in range(A_GROUP)], axis=0)
        s = _dot_nt(qs, kcat)
        if bias is not None:
            s = s + jnp.concatenate([bias] * A_GROUP, axis=0)
        sink = jnp.concatenate(
            [jnp.full((tq, 1), sink_ref[g * A_GROUP + j] * LOG2_E, F32) for j in range(A_GROUP)], axis=0)
        m = jnp.maximum(jnp.max(s, axis=-1, keepdims=True), sink)
        ol = _dot(jnp.exp2(s - m).astype(BF16), vcat)
        o = ol[:, 0:LANES] / (ol[:, LANES:LANES + 1] + jnp.exp2(sink - m))
        for j in range(A_GROUP):
            acc[j] = acc[j] + jnp.where(in_half, o[j * tq:(j + 1) * tq], 0.0)
    return jnp.concatenate(acc, axis=-1).astype(BF16)


A_STEP_BLOCKS = 8
A_LOCAL = 3 * WIN


def _attn_a_latent_kernel(sink_ref, q_ref, k_ref, v_ref, bias_ref, o_ref):
    step = pl.program_id(1)
    last = SEQ // WIN - 1
    k_ctx = k_ref[0, SEQ:NT, :]
    v_ctx = v_ref[0, SEQ:NT, :]
    for j in range(A_STEP_BLOCKS):
        n = step * A_STEP_BLOCKS + j
        start = pl.multiple_of(jnp.clip((n - 1) * WIN, 0, SEQ - A_LOCAL), WIN)
        variant = jnp.where(n == 0, 0, jnp.where(n == last, 2, 1))
        kcat = jnp.concatenate([k_ref[0, pl.ds(start, A_LOCAL), :], k_ctx], axis=0)
        vcat = jnp.concatenate([v_ref[0, pl.ds(start, A_LOCAL), :], v_ctx], axis=0)
        rows = slice(j * WIN, (j + 1) * WIN)
        o_ref[0, rows, :] = _window_heads(q_ref[0, rows, :], kcat, vcat, bias_ref[variant], sink_ref)


def _attn_a_latent(sink, qa, ka, va, bias):
    tq = A_STEP_BLOCKS * WIN
    return pl.pallas_call(
        _attn_a_latent_kernel,
        out_shape=jax.ShapeDtypeStruct((BATCH, NT_PAD, 512), BF16),
        grid=(BATCH, SEQ // tq),
        in_specs=[pl.BlockSpec(memory_space=pltpu.SMEM),
                  pl.BlockSpec((1, tq, 512), lambda b, n: (b, n, 0)),
                  pl.BlockSpec((1, NT_PAD, LANES), lambda b, n: (b, 0, 0)),
                  pl.BlockSpec((1, NT_PAD, 2 * LANES), lambda b, n: (b, 0, 0)),
                  pl.BlockSpec((3, WIN, A_LOCAL + CTX_LEN), lambda b, n: (0, 0, 0))],
        out_specs=pl.BlockSpec((1, tq, 512), lambda b, n: (b, n, 0)),
        compiler_params=_params(("arbitrary", "arbitrary")),
        name="attn_window_latent",
    )(sink, qa, ka, va, bias)


def _attn_a_ctx(sink, qa, ka, va, prev_out):
    ctx_blk = SEQ // CTX_LEN
    spec = lambda n: pl.BlockSpec((BATCH, CTX_LEN, n), lambda i: (0, ctx_blk, 0))
    tile = lambda n: pl.BlockSpec((BATCH, TOK_TILE, n), lambda i: (0, N_LAT_TILES, 0))

    def body(sink_ref, q_ref, k_ref, v_ref, prev_ref, o_ref):
        del prev_ref
        for sample in range(BATCH):
            o_ref[sample] = _window_heads(q_ref[sample], k_ref[sample], v_ref[sample], None, sink_ref)

    return pl.pallas_call(
        body,
        out_shape=jax.ShapeDtypeStruct((BATCH, NT_PAD, 512), BF16),
        grid=(1,),
        in_specs=[pl.BlockSpec(memory_space=pltpu.SMEM), tile(512), spec(LANES), spec(2 * LANES),
                  pl.BlockSpec(memory_space=pl.ANY)],
        out_specs=tile(512),
        input_output_aliases={4: 0},
        compiler_params=_params(("arbitrary",)),
        name="attn_window_ctx",
    )(sink, qa, ka, va, prev_out)


def _key_rows(ref, sample):
    return ref[sample, 0:NT, :] if ref.shape[1] == NT_PAD else ref[sample]


MLA_TQ = 2048
MLA_SUB = 512
DIFF_TQ = 1024
DIFF_SUB = 256


def _attn_b_kernel(q_ref, k_ref, v_ref, o_ref):
    for sample in range(q_ref.shape[0]):
        k = _key_rows(k_ref, sample)
        v = _key_rows(v_ref, sample)
        sub_rows = min(MLA_SUB, q_ref.shape[1])
        for sub in range(q_ref.shape[1] // sub_rows):
            rows = slice(sub * sub_rows, (sub + 1) * sub_rows)
            q = q_ref[sample, rows, :]
            outs = []
            for h in range(B_HEADS):
                s = _dot_nt(q[:, h * LANES:(h + 1) * LANES], k[:, h * LANES:(h + 1) * LANES])
                outs.append(_softmax_pv(s, v[:, h * LANES:(h + 1) * LANES], h))
            o_ref[sample, rows, :] = _merge_head_pairs(outs).astype(BF16)


def _full_attention(kernel, name, tq, q, k, v, extra, extra_specs, out_width, q_width, k_width, v_width):
    out = pl.pallas_call(
        kernel,
        out_shape=jax.ShapeDtypeStruct((BATCH, NT_PAD, out_width), BF16),
        grid=(BATCH, SEQ // tq),
        in_specs=[pl.BlockSpec((1, tq, q_width), lambda b, t: (b, t, 0)),
                  pl.BlockSpec((1, NT_PAD, k_width), lambda b, t: (b, 0, 0)),
                  pl.BlockSpec((1, NT_PAD, v_width), lambda b, t: (b, 0, 0))] + extra_specs(2),
        out_specs=pl.BlockSpec((1, tq, out_width), lambda b, t: (b, t, 0)),
        compiler_params=_params(("arbitrary", "arbitrary")),
        name=name + "_latent",
    )(q, k, v, *extra)
    return out


def _full_attention_ctx(kernel, name, q, k, v, extra, extra_specs, out_width, q_width, k_width, v_width,
                        prev_out):
    ctx_blk = SEQ // CTX_LEN
    n_extra = len(extra)

    def body(*refs):
        ins = refs[:3 + n_extra]
        kernel(*ins, refs[-1])

    spec = lambda n: pl.BlockSpec((BATCH, CTX_LEN, n), lambda i: (0, ctx_blk, 0))
    tile = lambda n: pl.BlockSpec((BATCH, TOK_TILE, n), lambda i: (0, N_LAT_TILES, 0))
    return pl.pallas_call(
        body,
        out_shape=jax.ShapeDtypeStruct((BATCH, NT_PAD, out_width), BF16),
        grid=(1,),
        in_specs=[tile(q_width), spec(k_width), spec(v_width)] + extra_specs(1)
                 + [pl.BlockSpec(memory_space=pl.ANY)],
        out_specs=tile(out_width),
        input_output_aliases={3 + n_extra: 0},
        compiler_params=_params(("arbitrary",)),
        name=name + "_ctx",
    )(q, k, v, *extra, prev_out)


def _attn_c_kernel(q_ref, k_ref, v_ref, lq1_ref, lk1_ref, lq2_ref, lk2_ref, sn_ref, o_ref, *, lam_init):
    tq = min(DIFF_SUB, q_ref.shape[1])
    lam = (jnp.exp(jnp.sum(lq1_ref[...] * lk1_ref[...], axis=-1, keepdims=True))
           - jnp.exp(jnp.sum(lq2_ref[...] * lk2_ref[...], axis=-1, keepdims=True)) + lam_init)
    lane = lax.broadcasted_iota(jnp.int32, (1, C_HEADS * C_V), 1)
    units = [(sample, sub) for sample in range(q_ref.shape[0]) for sub in range(q_ref.shape[1] // tq)]
    for sample, sub in units:
        k = _key_rows(k_ref, sample)
        v = _key_rows(v_ref, sample)
        rows = slice(sub * tq, (sub + 1) * tq)
        q = q_ref[sample, rows, :]
        outs = []
        for h in range(C_HEADS):
            lo = h * C_V
            q1 = q * _indicator((lane >= lo) & (lane < lo + C_DH))
            q2 = q * _indicator((lane >= lo + C_DH) & (lane < lo + C_V))
            s = _dot_nt(jnp.concatenate([q1, q2], axis=0), k)
            pv = _softmax_pv(s, v[:, h * LANES:(h + 1) * LANES], h)
            outs.append(pv[0:tq] - lam * pv[tq:2 * tq])
        acc = _merge_head_pairs(outs)
        sq = acc * acc
        inv = jnp.zeros((tq, C_HEADS * C_V), F32)
        for h in range(C_HEADS):
            in_head = (lane >= h * C_V) & (lane < (h + 1) * C_V)
            ms = jnp.sum(jnp.where(in_head, sq, 0.0), axis=-1, keepdims=True) * (1.0 / C_V)
            inv = inv + jnp.where(in_head, lax.rsqrt(ms + EPS), 0.0)
        o_ref[sample, rows, :] = ((acc * inv) * sn_ref[...] * (1.0 - lam_init)).astype(BF16)


def _outproj_kernel(*refs, n_chunks, n_x):
    x_refs = refs[:n_x]
    oa_ref, ob_ref, od_ref, w_ref, modb_ref, modc_ref, g2_ref, wr_ref, xo_ref, h2_ref, aff_ref = refs[n_x:]
    w_hi, w_lo = _split_bf16(wr_ref[...])
    w_router = jnp.concatenate([w_hi, w_lo], axis=-1)
    lane = lax.broadcasted_iota(jnp.int32, (1, LANES), 1)
    mixed = _dot(jnp.concatenate([oa_ref[0], ob_ref[0], od_ref[0]], axis=-1), w_ref[0])
    for j in range(n_chunks):
        rows = slice(j * TOK_TILE, (j + 1) * TOK_TILE)
        mod = _chunk_mod(modb_ref, modc_ref, j, n_chunks)
        g1 = mod[:, 2 * D_MODEL:3 * D_MODEL]
        sh2 = mod[:, 3 * D_MODEL:4 * D_MODEL]
        sc2 = mod[:, 4 * D_MODEL:5 * D_MODEL]
        x = _chunk_rows(x_refs, j) + g1 * mixed[rows]
        xo_ref[0, rows, :] = x
        h2 = (_rms(x) * g2_ref[...]) * (1.0 + sc2) + sh2
        h_hi, h_lo = _split_bf16(h2)
        h2_ref[0, rows, :] = h_hi
        prod = _dot(jnp.concatenate([h_hi, h_lo], axis=0), w_router)
        logits = (prod[0:TOK_TILE, 0:LANES] + prod[0:TOK_TILE, LANES:2 * LANES]
                  + prod[TOK_TILE:2 * TOK_TILE, 0:LANES])
        aff_ref[0, rows, :] = _softmax_rows(jnp.where(lane < N_EXPERTS, logits, NEG_INF))


def _outproj(stream, oa, ob, od, w, mod_l, g2, wr, with_ctx, layer):
    n_chunks = STEP_CHUNKS if with_ctx else LAT_STEP_CHUNKS
    n_steps = (N_TILES if with_ctx else N_LAT_TILES) // n_chunks
    full = lambda shape: pl.BlockSpec(shape, lambda b, t: (0,) * len(shape))
    tok = lambda n: pl.BlockSpec((1, n_chunks * TOK_TILE, n), lambda b, t: (b, t, 0))
    xs = _stream_operands(stream)
    x_specs = _stream_specs(True) if len(xs) > 1 else [tok(D_MODEL)]
    return pl.pallas_call(
        functools.partial(_outproj_kernel, n_chunks=n_chunks, n_x=len(xs)),
        out_shape=[jax.ShapeDtypeStruct((BATCH, NT_PAD, D_MODEL), F32),
                   jax.ShapeDtypeStruct((BATCH, NT_PAD, D_MODEL), BF16),
                   jax.ShapeDtypeStruct((BATCH, NT_PAD, LANES), F32)],
        grid=(BATCH, n_steps),
        in_specs=x_specs + [tok(512), tok(256), tok(256),
                            pl.BlockSpec((1, D_MODEL, D_MODEL), lambda b, t: (layer, 0, 0))] + _mod_specs()
                 + [full((1, D_MODEL)), full((D_MODEL, LANES))],
        out_specs=[tok(D_MODEL), tok(D_MODEL), tok(LANES)],
        compiler_params=_params(("arbitrary", "arbitrary")),
        name="out_projection",
    )(*xs, oa, ob, od, w, mod_l, mod_l, g2, wr)


def _prefix_count(flags_f32, tri_ref):
    rows, n = flags_f32.shape
    run = jnp.zeros((rows, 1), F32)
    outs = []
    for c in range(n // LANES):
        blk = flags_f32[:, c * LANES:(c + 1) * LANES]
        outs.append(_dot(blk.astype(BF16), tri_ref[...]) + run)
        run = run + jnp.sum(blk, axis=-1, keepdims=True)
    return jnp.concatenate(outs, axis=-1)


def _select_slots(a, cap, tri_ref):
    bits = pltpu.bitcast(a, jnp.int32)

    def step(i, t):
        cand = t | jnp.left_shift(jnp.int32(1), 30 - i)
        cnt = jnp.sum((bits >= cand).astype(jnp.int32), axis=-1, keepdims=True)
        return jnp.where(cnt >= cap, cand, t)

    thr = lax.fori_loop(0, 31, step, jnp.zeros((a.shape[0], 1), jnp.int32))
    above = bits > thr
    tied = bits == thr
    need = cap - jnp.sum(above.astype(F32), axis=-1, keepdims=True)
    tied_f = tied.astype(F32)
    sel = above | (tied & (_prefix_count(tied_f, tri_ref) < need))
    sel_f = sel.astype(F32)
    return jnp.where(sel, _prefix_count(sel_f, tri_ref), -1.0)


def _route_kernel(aff_ref, tri_ref, row_ref, arow_ref, col_ref, *, with_ctx):
    lat, ctx = [], []
    for b in range(BATCH):
        lat.append(aff_ref[b, 0:SEQ, :].T[0:N_EXPERTS])
        if with_ctx:
            ctx.append(aff_ref[b, SEQ:NT, :].T[0:N_EXPERTS])
    aff_lat = jnp.concatenate(lat, axis=0)
    arow_ref[:, 0:SEQ] = aff_lat
    slot_lat = _select_slots(aff_lat, CAP_LAT, tri_ref)
    if with_ctx:
        aff_ctx = jnp.concatenate(ctx, axis=0)
        arow_ref[:, SEQ:NT] = aff_ctx
        slot_ctx = _select_slots(aff_ctx, CAP_CTX, tri_ref)
        slots = jnp.concatenate([slot_lat, slot_ctx], axis=-1)
    else:
        slots = slot_lat
    n = slots.shape[1]
    row_ref[:, 0:n] = slots
    pad = jnp.full((LANES - N_EXPERTS, n), -1.0, F32)
    for b in range(BATCH):
        blk = jnp.concatenate([slots[b * N_EXPERTS:(b + 1) * N_EXPERTS], pad], axis=0)
        col_ref[b, 0:n, :] = blk.T
        if with_ctx and NT_PAD > NT:
            col_ref[b, NT:NT_PAD, :] = jnp.full((NT_PAD - NT, LANES), -1.0, F32)


def _route(aff, tri, with_ctx):
    vm = pl.BlockSpec(memory_space=pltpu.VMEM)
    return pl.pallas_call(
        functools.partial(_route_kernel, with_ctx=with_ctx),
        out_shape=[jax.ShapeDtypeStruct((BATCH * N_EXPERTS, NT), F32),
                   jax.ShapeDtypeStruct((BATCH * N_EXPERTS, NT), F32),
                   jax.ShapeDtypeStruct((BATCH, NT_PAD, LANES), F32)],
        in_specs=[vm, vm],
        out_specs=[vm, vm, vm],
        compiler_params=pltpu.CompilerParams(vmem_limit_bytes=VMEM_LIMIT),
        name="route_topc",
    )(aff, tri)


GATHER_GROUP = 4


def _gather_kernel(row_ref, arow_ref, h_ref, *out_refs, with_ctx):
    slot = lax.broadcasted_iota(jnp.int32, (CAP_LAT, 1), 0).astype(F32)

    def pick(e0, cols, n_slots, h_rows, x_ref, g_ref):
        hots = []
        for e in range(e0, e0 + GATHER_GROUP):
            hot = jnp.where(row_ref[e:e + 1, cols] == slot[0:n_slots], 1.0, 0.0)
            gate = jnp.sum(hot * arow_ref[e:e + 1, cols], axis=-1, keepdims=True)
            g_ref[e] = jnp.broadcast_to(gate, (n_slots, LANES))
            hots.append(hot.astype(BF16))
        x = _dot(jnp.concatenate(hots, axis=0), h_rows).astype(BF16)
        for i in range(GATHER_GROUP):
            x_ref[e0 + i] = x[i * n_slots:(i + 1) * n_slots]

    for e0 in range(0, N_EXPERTS, GATHER_GROUP):
        pick(e0, slice(0, SEQ), CAP_LAT, h_ref[0, 0:SEQ, :], out_refs[0], out_refs[1])
        if with_ctx:
            pick(e0, slice(SEQ, NT), CAP_CTX, h_ref[0, SEQ:NT, :], out_refs[2], out_refs[3])


def _gather(slot_rows, aff_rows, h2, with_ctx):
    out_shape = [jax.ShapeDtypeStruct((N_EXPERTS, SEQ, D_MODEL), BF16),
                 jax.ShapeDtypeStruct((N_EXPERTS, SEQ, LANES), F32)]
    out_specs = [pl.BlockSpec((N_EXPERTS, CAP_LAT, D_MODEL), lambda b: (0, b, 0)),
                 pl.BlockSpec((N_EXPERTS, CAP_LAT, LANES), lambda b: (0, b, 0))]
    if with_ctx:
        out_shape += [jax.ShapeDtypeStruct((N_EXPERTS, CTX_LEN, D_MODEL), BF16),
                      jax.ShapeDtypeStruct((N_EXPERTS, CTX_LEN, LANES), F32)]
        out_specs += [pl.BlockSpec((N_EXPERTS, CAP_CTX, D_MODEL), lambda b: (0, b, 0)),
                      pl.BlockSpec((N_EXPERTS, CAP_CTX, LANES), lambda b: (0, b, 0))]
    row_spec = pl.BlockSpec((N_EXPERTS, NT), lambda b: (b, 0))
    return pl.pallas_call(
        functools.partial(_gather_kernel, with_ctx=with_ctx),
        out_shape=out_shape,
        grid=(BATCH,),
        in_specs=[row_spec, row_spec, pl.BlockSpec((1, NT_PAD, D_MODEL), lambda b: (b, 0, 0))],
        out_specs=out_specs,
        compiler_params=_params(("arbitrary",)),
        name="expert_gather",
    )(slot_rows, aff_rows, h2)


def _ffn_kernel(*refs, with_ctx):
    if with_ctx:
        xl_ref, gl_ref, xc_ref, gc_ref, wg_ref, wu_ref, wd_ref, y_ref = refs
    else:
        xl_ref, gl_ref, wg_ref, wu_ref, wd_ref, y_ref = refs
    wg = wg_ref[0, 0].astype(BF16)
    wu = wu_ref[0, 0].astype(BF16)
    wd = wd_ref[0, 0].astype(BF16)

    def swiglu(x, slot_gate):
        gate = _dot(x, wg)
        hid = (gate / (1.0 + jnp.exp(-gate))) * _dot(x, wu)
        scale = jnp.concatenate([slot_gate] * (D_MODEL // LANES), axis=-1)
        return (_dot(hid.astype(BF16), wd) * scale).astype(BF16)

    for r in range(FFN_LAT_TILES):
        rows = slice(r * FFN_TILE, (r + 1) * FFN_TILE)
        y_ref[0, rows, :] = swiglu(xl_ref[0, rows, :], gl_ref[0, rows, :])
    if with_ctx:
        y_ref[0, SEQ:NT, :] = swiglu(xc_ref[0], gc_ref[0])


def _ffn(gathered, w_gate, w_up, w_down, layer):
    with_ctx = len(gathered) == 4
    ins = list(gathered) + [w_gate, w_up, w_down]
    in_specs = [pl.BlockSpec((1, SEQ, D_MODEL), lambda e: (e, 0, 0)),
                pl.BlockSpec((1, SEQ, LANES), lambda e: (e, 0, 0))]
    if with_ctx:
        in_specs += [pl.BlockSpec((1, CTX_LEN, D_MODEL), lambda e: (e, 0, 0)),
                     pl.BlockSpec((1, CTX_LEN, LANES), lambda e: (e, 0, 0))]
    in_specs += [pl.BlockSpec((1, 1, D_MODEL, D_EXPERT), lambda e: (layer, e, 0, 0)),
                 pl.BlockSpec((1, 1, D_MODEL, D_EXPERT), lambda e: (layer, e, 0, 0)),
                 pl.BlockSpec((1, 1, D_EXPERT, D_MODEL), lambda e: (layer, e, 0, 0))]
    return pl.pallas_call(
        functools.partial(_ffn_kernel, with_ctx=with_ctx),
        out_shape=jax.ShapeDtypeStruct((N_EXPERTS, NT, D_MODEL), BF16),
        grid=(N_EXPERTS,),
        in_specs=in_specs,
        out_specs=pl.BlockSpec((1, NT, D_MODEL), lambda e: (e, 0, 0)),
        compiler_params=_params(("arbitrary",)),
        name="expert_ffn",
    )(*ins)


COMBINE_ROWS = 1024


def _scatter_add(col, ys, n_slots):
    lane = lax.broadcasted_iota(jnp.int32, (1, n_slots), 1).astype(F32)
    hot = [_indicator(col[:, e:e + 1] == lane) for e in range(N_EXPERTS)]
    return _dot(jnp.concatenate(hot, axis=-1), jnp.concatenate(ys, axis=0))


def _combine_latent_kernel(*refs, final):
    if final:
        x_ref, col_ref, y_ref, mod_ref, fn_ref, o_ref = refs
    else:
        x_ref, col_ref, y_ref, mod_ref, o_ref = refs
    g2 = mod_ref[0][:, 5 * D_MODEL:6 * D_MODEL]
    acc = _scatter_add(col_ref[0], [y_ref[e] for e in range(N_EXPERTS)], CAP_LAT)
    x = x_ref[0] + g2 * acc
    if final:
        x = _rms(x) * fn_ref[...]
    o_ref[0] = x


def _combine_ctx_kernel(x_ref, col_ref, y_ref, mod_ref, prev_ref, o_ref):
    del prev_ref
    g2 = mod_ref[0][:, 5 * D_MODEL:6 * D_MODEL]
    n_pairs = N_EXPERTS * CAP_CTX
    lane = lax.broadcasted_iota(jnp.int32, (1, n_pairs), 1)
    slot_of_lane = (lane % CAP_CTX).astype(F32)
    spread = _indicator(lax.broadcasted_iota(jnp.int32, (LANES, n_pairs), 0)
                        == lax.broadcasted_iota(jnp.int32, (LANES, n_pairs), 1) // CAP_CTX)
    for b in range(BATCH):
        slot_on_lane = _dot(col_ref[b].astype(BF16), spread)
        hot = _indicator(slot_on_lane == slot_of_lane)
        ys = jnp.concatenate([y_ref[e, b * CAP_CTX:(b + 1) * CAP_CTX, :] for e in range(N_EXPERTS)], axis=0)
        o_ref[b] = x_ref[b] + g2 * _dot(hot, ys)


def _combine(xall, slot_cols, y, mod_l, with_ctx, final_norm=None):
    final = final_norm is not None
    n_rows = SEQ if final else NT_PAD
    tok = lambda n: pl.BlockSpec((1, COMBINE_ROWS, n), lambda b, t: (b, t, 0))
    ins = [xall, slot_cols, y, mod_l]
    in_specs = [tok(D_MODEL), tok(LANES),
                pl.BlockSpec((N_EXPERTS, CAP_LAT, D_MODEL), lambda b, t: (0, b, 0)),
                pl.BlockSpec((1, 1, 6 * D_MODEL), lambda b, t: (b, 0, 0))]
    if final:
        ins.append(final_norm)
        in_specs.append(pl.BlockSpec((1, D_MODEL), lambda b, t: (0, 0)))
    out = pl.pallas_call(
        functools.partial(_combine_latent_kernel, final=final),
        out_shape=jax.ShapeDtypeStruct((BATCH, n_rows, D_MODEL), F32),
        grid=(BATCH, SEQ // COMBINE_ROWS),
        in_specs=in_specs,
        out_specs=tok(D_MODEL),
        compiler_params=_params(("arbitrary", "arbitrary")),
        name="expert_combine",
    )(*ins)
    if not with_ctx:
        return out
    ctx_blk = SEQ // CTX_LEN
    ctx_rows = lambda n: pl.BlockSpec((BATCH, CTX_LEN, n), lambda i: (0, ctx_blk, 0))
    return pl.pallas_call(
        _combine_ctx_kernel,
        out_shape=jax.ShapeDtypeStruct((BATCH, NT_PAD, D_MODEL), F32),
        grid=(1,),
        in_specs=[ctx_rows(D_MODEL), ctx_rows(LANES),
                  pl.BlockSpec((N_EXPERTS, CTX_LEN, D_MODEL), lambda i: (0, ctx_blk, 0)),
                  pl.BlockSpec((1, 1, 6 * D_MODEL), lambda i: (BATCH, 0, 0)),
                  pl.BlockSpec(memory_space=pl.ANY)],
        out_specs=ctx_rows(D_MODEL),
        input_output_aliases={4: 0},
        compiler_params=_params(("arbitrary",)),
        name="expert_combine_ctx",
    )(xall, slot_cols, y, mod_l, out)


def _rope_tables():
    t = jnp.arange(SEQ)
    pos = jnp.stack([t // GRID_W, t % GRID_W], axis=0)

    def table(n_f, lanes_per_unit):
        inv = ROPE_BASE ** (-jnp.arange(n_f, dtype=F32) / n_f)
        d = np.arange(lanes_per_unit)
        axis, second, f = d // (2 * n_f), (d // n_f) % 2, d % n_f
        ang = pos[axis].T.astype(F32) * inv[f][None, :]
        cos = jnp.cos(ang)
        sin = jnp.sin(ang) * jnp.where(second == 1, 1.0, -1.0)[None, :]
        return cos, sin

    def finish(cos, sin, scale):
        cos = jnp.concatenate([cos, jnp.ones((NT_PAD - SEQ, LANES), F32)], axis=0)
        sin = jnp.concatenate([sin, jnp.zeros((NT_PAD - SEQ, LANES), F32)], axis=0)
        scale = scale * LOG2_E
        return [cos, sin, cos * scale, sin * scale]

    cos_a, sin_a = table(HEAD_DIM // 4, HEAD_DIM)
    tabs_a = finish(jnp.tile(cos_a, (1, 2)), jnp.tile(sin_a, (1, 2)), HEAD_DIM ** -0.5)
    cos_8, sin_8 = table(B_ROPE // 4, B_ROPE)
    tabs_8 = finish(jnp.tile(cos_8, (1, 4)), jnp.tile(sin_8, (1, 4)), C_DH ** -0.5)
    ones = jnp.ones((SEQ, B_NOPE), F32)
    pad1 = jnp.ones((SEQ, LANES - B_NOPE - B_ROPE), F32)
    cos_b = jnp.concatenate([ones, cos_8, pad1], axis=1)
    sin_b = jnp.concatenate([0 * ones, sin_8, 0 * pad1], axis=1)
    tabs_b = finish(cos_b, sin_b, (B_NOPE + B_ROPE) ** -0.5)
    return tabs_a + tabs_8 + tabs_b


def _window_bias():
    qi = np.arange(WIN)[:, None]
    kj = np.arange(A_LOCAL)[None, :]
    out = np.zeros((3, WIN, A_LOCAL + CTX_LEN), np.float32)
    for v in range(3):
        out[v, :, :A_LOCAL] = np.where(np.abs(kj - v * WIN - qi) <= WIN, 0.0, NEG_INF)
    return jnp.asarray(out)


def _prep_weights(w_in, w_uq, w_ukv, w_out, w_router):
    offs = np.cumsum((0, 512, 128, 128, 256, 128, 32, 256, 256, 256))
    seg = lambda i: w_in[:, :, offs[i]:offs[i + 1]]
    wq = seg(0).reshape(DEPTH, D_MODEL, A_KV_HEADS, A_GROUP, HEAD_DIM)
    wq = jnp.swapaxes(wq, 2, 3).reshape(DEPTH, D_MODEL, A_HEADS * HEAD_DIM)
    kr = jnp.pad(seg(5), ((0, 0), (0, 0), (B_NOPE, LANES - B_NOPE - B_ROPE)))
    w = jnp.concatenate([wq, seg(1), seg(2), seg(3), seg(4), kr,
                         seg(6), seg(7), seg(8)], axis=-1).astype(BF16)
    wuq = w_uq.reshape(DEPTH, B_Q_RANK, B_HEADS, B_NOPE + B_ROPE)
    wuq = jnp.pad(wuq, ((0, 0), (0, 0), (0, 0), (0, LANES - B_NOPE - B_ROPE)))
    wuq = wuq.reshape(DEPTH, B_Q_RANK, B_HEADS * LANES).astype(BF16)
    wukv = w_ukv.reshape(DEPTH, B_KV_RANK, B_HEADS, B_NOPE + B_V)
    wk = jnp.pad(wukv[..., :B_NOPE], ((0, 0), (0, 0), (0, 0), (0, LANES - B_NOPE)))
    wukv = jnp.concatenate([wk.reshape(DEPTH, B_KV_RANK, B_HEADS * LANES),
                            wukv[..., B_NOPE:].reshape(DEPTH, B_KV_RANK, B_HEADS * B_V)],
                           axis=-1).astype(BF16)
    wo_a = w_out[:, :A_HEADS * HEAD_DIM].reshape(DEPTH, A_KV_HEADS, A_GROUP, HEAD_DIM, D_MODEL)
    wo_a = jnp.swapaxes(wo_a, 1, 2).reshape(DEPTH, A_HEADS * HEAD_DIM, D_MODEL)
    wo = jnp.concatenate([wo_a, w_out[:, A_HEADS * HEAD_DIM:]], axis=1).astype(BF16)
    wr = jnp.pad(w_router, ((0, 0), (0, 0), (0, LANES - N_EXPERTS)))
    return w, wuq, wukv, wo, wr


def kernel(x, c, ctx, c_ctx, norm1, norm2, w_ada, b_ada, w_in, sink, mla_q_norm, w_uq, mla_kv_norm, w_ukv,
           lam_q1, lam_k1, lam_q2, lam_k2, diff_norm, w_out, w_router, w_gate, w_up, w_down, final_norm):
    assert NT_PAD == NT
    xall = (x, ctx)
    cc = jnp.concatenate([c, c_ctx[None, :], jnp.zeros((16 - BATCH - 1, D_MODEL), F32)], axis=0)
    mod = _modulation(cc, w_ada, b_ada).reshape(DEPTH, 16, 1, 6 * D_MODEL)
    tabs = _rope_tables()
    bias = _window_bias()
    tri = jnp.asarray(np.triu(np.ones((LANES, LANES), np.float32), k=1), BF16)
    w, wuq, wukv, wo, wr = _prep_weights(w_in, w_uq, w_ukv, w_out, w_router)
    sub_norm = jnp.tile(diff_norm, (1, C_HEADS)).reshape(DEPTH, 1, C_HEADS * C_V)
    row = lambda a, l: a[l].reshape(1, -1)

    for l in range(DEPTH):
        need_ctx = l < DEPTH - 1
        qa, ka, va, qb, kb, vb, qd, kd, vd = _inproj(
            xall, mod[l], row(norm1, l), w, row(mla_q_norm, l), wuq[l], row(mla_kv_norm, l), wukv[l], tabs, l)

        oa = _attn_a_latent(sink[l], qa, ka, va, bias)
        ob = _full_attention(_attn_b_kernel, "attn_mla", MLA_TQ, qb, kb, vb, [], lambda n: [],
                             256, 512, 512, 512)
        lam_init = 0.8 - 0.6 * math.exp(-0.3 * l)
        c_kernel = functools.partial(_attn_c_kernel, lam_init=lam_init)
        c_extra = [row(lam_q1, l), row(lam_k1, l), row(lam_q2, l), row(lam_k2, l), sub_norm[l]]

        def c_specs(n_grid, c_extra=c_extra):
            zmap = (lambda b, t: (0, 0)) if n_grid == 2 else (lambda b: (0, 0))
            return [pl.BlockSpec(a.shape, zmap) for a in c_extra]

        od = _full_attention(c_kernel, "attn_diff", DIFF_TQ, qd, kd, vd, c_extra, c_specs, 256, 256, 256, 512)
        if need_ctx:
            oa = _attn_a_ctx(sink[l], qa, ka, va, oa)
            ob = _full_attention_ctx(_attn_b_kernel, "attn_mla", qb, kb, vb, [], lambda n: [],
                                     256, 512, 512, 512, ob)
            od = _full_attention_ctx(c_kernel, "attn_diff", qd, kd, vd, c_extra, c_specs,
                                     256, 256, 256, 512, od)

        xall, h2, aff = _outproj(xall, oa, ob, od, wo, mod[l], row(norm2, l), wr[l], need_ctx, l)
        slot_rows, aff_rows, slot_cols = _route(aff, tri, need_ctx)
        gathered = _gather(slot_rows, aff_rows, h2, need_ctx)
        y = _ffn(gathered, w_gate, w_up, w_down, l)
        xall = _combine(xall, slot_cols, y, mod[l], need_ctx,
                        final_norm=None if need_ctx else final_norm.reshape(1, D_MODEL))
    return xall
```

```python
import functools
import math

import numpy as np
import jax
import jax.numpy as jnp
from jax import lax
from jax.experimental import pallas as pl
from jax.experimental.pallas import tpu as pltpu

D_MODEL = 1024
BATCH = 8
SEQ = 2048
DEPTH = 4
CTX_LEN = 256
NT = SEQ + CTX_LEN
GRID_W = 64
HEAD_DIM = 64
ROPE_BASE = 10000.0
EPS = 1e-6
NEG_INF = -1e30
LOG2_E = math.log2(math.e)

WIN = 128
A_HEADS = 8
A_KV_HEADS = 2
A_GROUP = A_HEADS // A_KV_HEADS
B_HEADS = 4
B_NOPE = 64
B_ROPE = 32
B_V = 64
B_Q_RANK = 256
B_KV_RANK = 128
C_HEADS = 4
C_DH = 32
C_V = 2 * C_DH
N_EXPERTS = 16
EC_FACTOR = 2
D_EXPERT = 512
CAP_LAT = EC_FACTOR * SEQ // N_EXPERTS
CAP_CTX = EC_FACTOR * CTX_LEN // N_EXPERTS

LANES = 128
TOK_TILE = 256
NT_PAD = SEQ + TOK_TILE
N_TILES = NT_PAD // TOK_TILE
N_LAT_TILES = SEQ // TOK_TILE
STEP_CHUNKS = 3
LAT_STEP_CHUNKS = 2
FFN_TILE = 1024
FFN_LAT_TILES = SEQ // FFN_TILE
VMEM_LIMIT = 56 * 1024 * 1024

F32 = jnp.float32
BF16 = jnp.bfloat16

_NT_DIMS = (((1,), (1,)), ((), ()))


def _params(sem):
    return pltpu.CompilerParams(dimension_semantics=sem, vmem_limit_bytes=VMEM_LIMIT)


def _dot(a, b):
    return jnp.dot(a, b, preferred_element_type=F32)


def _dot_nt(a, b):
    return lax.dot_general(a, b, _NT_DIMS, preferred_element_type=F32)


def _split_bf16(v):
    hi = v.astype(BF16)
    lo = (v - hi.astype(F32)).astype(BF16)
    return hi, lo


def _dot3(a, b):
    a_hi, a_lo = _split_bf16(a)
    b_hi, b_lo = _split_bf16(b)
    return _dot(a_hi, b_hi) + _dot(a_lo, b_hi) + _dot(a_hi, b_lo)


def _indicator(cond):
    return jnp.where(cond, 1.0, 0.0).astype(BF16)


def _rms(v):
    return v * lax.rsqrt(jnp.mean(v * v, axis=-1, keepdims=True) + EPS)


def _softmax_pv(s, v_blk, head):
    m = jnp.max(s, axis=-1, keepdims=True)
    ol = _dot(jnp.exp2(s - m).astype(BF16), v_blk)
    ones_lane = 64 if head % 2 == 0 else 0
    return ol / ol[:, ones_lane:ones_lane + 1]


def _merge_head_pairs(per_head):
    lane = lax.broadcasted_iota(jnp.int32, (1, LANES), 1)
    blocks = [jnp.where(lane < 64, per_head[j], per_head[j + 1]) for j in range(0, len(per_head), 2)]
    return jnp.concatenate(blocks, axis=-1)


def _softmax_rows(s):
    m = jnp.max(s, axis=-1, keepdims=True)
    e = jnp.exp(s - m)
    return e / jnp.sum(e, axis=-1, keepdims=True)


def _mod_kernel(c_ref, w_ref, b_ref, o_ref):
    c = c_ref[...]
    s = c / (1.0 + jnp.exp(-c))
    o_ref[0] = _dot3(s, w_ref[0]) + b_ref[0]


def _modulation(cc, w_ada, b_ada):
    tn = 1536
    return pl.pallas_call(
        _mod_kernel,
        out_shape=jax.ShapeDtypeStruct((DEPTH, 16, 6 * D_MODEL), F32),
        grid=(DEPTH, 6 * D_MODEL // tn),
        in_specs=[
            pl.BlockSpec((16, D_MODEL), lambda l, j: (0, 0)),
            pl.BlockSpec((1, D_MODEL, tn), lambda l, j: (l, 0, j)),
            pl.BlockSpec((1, 1, tn), lambda l, j: (l, 0, j)),
        ],
        out_specs=pl.BlockSpec((1, 16, tn), lambda l, j: (l, 0, j)),
        compiler_params=_params(("arbitrary", "arbitrary")),
        name="adaln_mod",
    )(cc, w_ada, b_ada.reshape(DEPTH, 1, 6 * D_MODEL))


def _rope(p, cos, sin, half):
    lane = lax.broadcasted_iota(jnp.int32, (1, LANES), 1)
    first = (lane // half) % 2 == 0
    outs = []
    for j in range(p.shape[1] // LANES):
        blk = p[:, j * LANES:(j + 1) * LANES]
        partner = jnp.where(first, pltpu.roll(blk, LANES - half, 1), pltpu.roll(blk, half, 1))
        outs.append(blk * cos + partner * sin)
    return outs[0] if len(outs) == 1 else jnp.concatenate(outs, axis=-1)


def _per_head_with_ones(v):
    lane = lax.broadcasted_iota(jnp.int32, (1, LANES), 1)
    outs = []
    for j in range(v.shape[1] // LANES):
        blk = v[:, j * LANES:(j + 1) * LANES]
        outs.append(jnp.where(lane < 64, blk, jnp.where(lane == 64, 1.0, 0.0)))
        outs.append(jnp.where(lane >= 64, blk, jnp.where(lane == 0, 1.0, 0.0)))
    return jnp.concatenate(outs, axis=-1)


def _chunk_mod(modb_ref, modc_ref, j, n_chunks):
    mod = modb_ref[0]
    if n_chunks == STEP_CHUNKS and j == n_chunks - 1:
        mod = jnp.where(pl.program_id(1) == N_TILES // STEP_CHUNKS - 1, modc_ref[0], mod)
    return mod


def _stream_specs(split):
    if not split:
        return [pl.BlockSpec((1, STEP_CHUNKS * TOK_TILE, D_MODEL), lambda b, t: (b, t, 0))]
    last = N_LAT_TILES - 1
    lat = lambda j: pl.BlockSpec((1, TOK_TILE, D_MODEL),
                                 lambda b, t: (b, jnp.minimum(STEP_CHUNKS * t + j, last), 0))
    return [lat(0), lat(1), lat(2), pl.BlockSpec((1, CTX_LEN, D_MODEL), lambda b, t: (b, 0, 0))]


def _chunk_rows(x_refs, j):
    if len(x_refs) == 1:
        return x_refs[0][0, j * TOK_TILE:(j + 1) * TOK_TILE, :]
    if j < STEP_CHUNKS - 1:
        return x_refs[j][0]
    is_ctx = pl.program_id(1) == N_TILES // STEP_CHUNKS - 1
    return jnp.where(is_ctx, x_refs[3][0], x_refs[2][0])


def _inproj_kernel(*refs, n_x):
    x_refs = refs[:n_x]
    (modb_ref, modc_ref, g1_ref, w_ref, qn_ref, wuq_ref, kvn_ref, wukv_ref,
     ca_ref, sa_ref, caq_ref, saq_ref, c8_ref, s8_ref, c8q_ref, s8q_ref,
     cb_ref, sb_ref, cbq_ref, sbq_ref,
     qa_ref, ka_ref, va_ref, qb_ref, kb_ref, vb_ref, qd_ref, kd_ref, vd_ref) = refs[n_x:]
    for j in range(STEP_CHUNKS):
        rows = slice(j * TOK_TILE, (j + 1) * TOK_TILE)
        x = _chunk_rows(x_refs, j)
        mod = _chunk_mod(modb_ref, modc_ref, j, STEP_CHUNKS)
        sh1 = mod[:, 0:D_MODEL]
        sc1 = mod[:, D_MODEL:2 * D_MODEL]
        h = (_rms(x) * g1_ref[...]) * (1.0 + sc1) + sh1
        p = _dot(h.astype(BF16), w_ref[0])

        qa_ref[0, rows, :] = _rope(p[:, 0:512], caq_ref[rows, :], saq_ref[rows, :], 16).astype(BF16)
        ka_ref[0, rows, :] = _rope(p[:, 512:640], ca_ref[rows, :], sa_ref[rows, :], 16).astype(BF16)
        lane = lax.broadcasted_iota(jnp.int32, (TOK_TILE, LANES), 1)
        va_ref[0, rows, :] = jnp.concatenate(
            [p[:, 640:768], jnp.where(lane == 0, 1.0, 0.0)], axis=-1).astype(BF16)

        cq = (_rms(p[:, 768:1024]) * qn_ref[...]).astype(BF16)
        qb = _dot(cq, wuq_ref[...])
        qb_ref[0, rows, :] = _rope(qb, cbq_ref[rows, :], sbq_ref[rows, :], 8).astype(BF16)
        ckv = (_rms(p[:, 1024:1152]) * kvn_ref[...]).astype(BF16)
        kv = _dot(ckv, wukv_ref[...])
        kr = _rope(p[:, 1152:1280], cb_ref[rows, :], sb_ref[rows, :], 8)
        kb_ref[0, rows, :] = (kv[:, 0:512] + jnp.concatenate([kr] * B_HEADS, axis=-1)).astype(BF16)
        vb_ref[0, rows, :] = _per_head_with_ones(kv[:, 512:768]).astype(BF16)

        qd_ref[0, rows, :] = _rope(p[:, 1280:1536], c8q_ref[rows, :], s8q_ref[rows, :], 8).astype(BF16)
        kd_ref[0, rows, :] = _rope(p[:, 1536:1792], c8_ref[rows, :], s8_ref[rows, :], 8).astype(BF16)
        vd_ref[0, rows, :] = _per_head_with_ones(p[:, 1792:2048]).astype(BF16)


def _mod_specs():
    return [pl.BlockSpec((1, 1, 6 * D_MODEL), lambda b, t: (b, 0, 0)),
            pl.BlockSpec((1, 1, 6 * D_MODEL), lambda b, t: (BATCH, 0, 0))]


def _stream_operands(stream):
    return [stream[0]] * 3 + [stream[1]] if isinstance(stream, tuple) else [stream]


def _inproj(stream, mod_l, g1, w, qn, wuq, kvn, wukv, tabs, layer):
    full = lambda shape: pl.BlockSpec(shape, lambda b, t: (0,) * len(shape))
    step_rows = STEP_CHUNKS * TOK_TILE
    tab_spec = pl.BlockSpec((step_rows, LANES), lambda b, t: (t, 0))
    widths = (512, 128, 256, 512, 512, 512, 256, 256, 512)
    xs = _stream_operands(stream)
    return pl.pallas_call(
        functools.partial(_inproj_kernel, n_x=len(xs)),
        out_shape=[jax.ShapeDtypeStruct((BATCH, NT_PAD, n), BF16) for n in widths],
        grid=(BATCH, N_TILES // STEP_CHUNKS),
        in_specs=_stream_specs(len(xs) > 1) + _mod_specs() + [
            full((1, D_MODEL)),
            pl.BlockSpec((1, D_MODEL, 2048), lambda b, t: (layer, 0, 0)),
            full((1, B_Q_RANK)),
            full((B_Q_RANK, 512)),
            full((1, B_KV_RANK)),
            full((B_KV_RANK, 768)),
        ] + [tab_spec] * 12,
        out_specs=[pl.BlockSpec((1, step_rows, n), lambda b, t: (b, t, 0)) for n in widths],
        compiler_params=_params(("arbitrary", "arbitrary")),
        name="in_projection",
    )(*xs, mod_l, mod_l, g1, w, qn, wuq, kvn, wukv, *tabs)


def _window_heads(q, kcat, vcat, bias, sink_ref):
    tq = q.shape[0]
    lane = lax.broadcasted_iota(jnp.int32, (1, LANES), 1)
    acc = [jnp.zeros((tq, LANES), F32) for _ in range(A_GROUP)]
    for g in range(A_KV_HEADS):
        in_half = (lane < HEAD_DIM) if g == 0 else (lane >= HEAD_DIM)
        keep = _indicator(in_half)
        qs = jnp.concatenate([q[:, j * LANES:(j + 1) * LANES] * keep for j in range(A_GROUP)], axis=0)
        s = _dot_nt(qs, kcat)
        if bias is not None:
            s = s + jnp.concatenate([bias] * A_GROUP, axis=0)
        sink = jnp.concatenate(
            [jnp.full((tq, 1), sink_ref[g * A_GROUP + j] * LOG2_E, F32) for j in range(A_GROUP)], axis=0)
        m = jnp.maximum(jnp.max(s, axis=-1, keepdims=True), sink)
        ol = _dot(jnp.exp2(s - m).astype(BF16), vcat)
        o = ol[:, 0:LANES] / (ol[:, LANES:LANES + 1] + jnp.exp2(sink - m))
        for j in range(A_GROUP):
            acc[j] = acc[j] + jnp.where(in_half, o[j * tq:(j + 1) * tq], 0.0)
    return jnp.concatenate(acc, axis=-1).astype(BF16)


A_STEP_BLOCKS = 8
A_LOCAL = 3 * WIN


def _attn_a_latent_kernel(sink_ref, q_ref, k_ref, v_ref, bias_ref, o_ref):
    step = pl.program_id(1)
    last = SEQ // WIN - 1
    k_ctx = k_ref[0, SEQ:NT, :]
    v_ctx = v_ref[0, SEQ:NT, :]
    for j in range(A_STEP_BLOCKS):
        n = step * A_STEP_BLOCKS + j
        start = pl.multiple_of(jnp.clip((n - 1) * WIN, 0, SEQ - A_LOCAL), WIN)
        variant = jnp.where(n == 0, 0, jnp.where(n == last, 2, 1))
        kcat = jnp.concatenate([k_ref[0, pl.ds(start, A_LOCAL), :], k_ctx], axis=0)
        vcat = jnp.concatenate([v_ref[0, pl.ds(start, A_LOCAL), :], v_ctx], axis=0)
        rows = slice(j * WIN, (j + 1) * WIN)
        o_ref[0, rows, :] = _window_heads(q_ref[0, rows, :], kcat, vcat, bias_ref[variant], sink_ref)


def _attn_a_latent(sink, qa, ka, va, bias):
    tq = A_STEP_BLOCKS * WIN
    return pl.pallas_call(
        _attn_a_latent_kernel,
        out_shape=jax.ShapeDtypeStruct((BATCH, NT_PAD, 512), BF16),
        grid=(BATCH, SEQ // tq),
        in_specs=[pl.BlockSpec(memory_space=pltpu.SMEM),
                  pl.BlockSpec((1, tq, 512), lambda b, n: (b, n, 0)),
                  pl.BlockSpec((1, NT_PAD, LANES), lambda b, n: (b, 0, 0)),
                  pl.BlockSpec((1, NT_PAD, 2 * LANES), lambda b, n: (b, 0, 0)),
                  pl.BlockSpec((3, WIN, A_LOCAL + CTX_LEN), lambda b, n: (0, 0, 0))],
        out_specs=pl.BlockSpec((1, tq, 512), lambda b, n: (b, n, 0)),
        compiler_params=_params(("arbitrary", "arbitrary")),
        name="attn_window_latent",
    )(sink, qa, ka, va, bias)


def _attn_a_ctx(sink, qa, ka, va, prev_out):
    ctx_blk = SEQ // CTX_LEN
    spec = lambda n: pl.BlockSpec((BATCH, CTX_LEN, n), lambda i: (0, ctx_blk, 0))
    tile = lambda n: pl.BlockSpec((BATCH, TOK_TILE, n), lambda i: (0, N_LAT_TILES, 0))

    def body(sink_ref, q_ref, k_ref, v_ref, prev_ref, o_ref):
        del prev_ref
        for sample in range(BATCH):
            o_ref[sample] = _window_heads(q_ref[sample], k_ref[sample], v_ref[sample], None, sink_ref)

    return pl.pallas_call(
        body,
        out_shape=jax.ShapeDtypeStruct((BATCH, NT_PAD, 512), BF16),
        grid=(1,),
        in_specs=[pl.BlockSpec(memory_space=pltpu.SMEM), tile(512), spec(LANES), spec(2 * LANES),
                  pl.BlockSpec(memory_space=pl.ANY)],
        out_specs=tile(512),
        input_output_aliases={4: 0},
        compiler_params=_params(("arbitrary",)),
        name="attn_window_ctx",
    )(sink, qa, ka, va, prev_out)


def _key_rows(ref, sample):
    return ref[sample, 0:NT, :] if ref.shape[1] == NT_PAD else ref[sample]


MLA_TQ = 2048
MLA_SUB = 1024
DIFF_TQ = 1024
DIFF_SUB = 256


def _attn_b_kernel(q_ref, k_ref, v_ref, o_ref):
    for sample in range(q_ref.shape[0]):
        k = _key_rows(k_ref, sample)
        v = _key_rows(v_ref, sample)
        sub_rows = min(MLA_SUB, q_ref.shape[1])
        for sub in range(q_ref.shape[1] // sub_rows):
            rows = slice(sub * sub_rows, (sub + 1) * sub_rows)
            q = q_ref[sample, rows, :]
            outs = []
            for h in range(B_HEADS):
                s = _dot_nt(q[:, h * LANES:(h + 1) * LANES], k[:, h * LANES:(h + 1) * LANES])
                outs.append(_softmax_pv(s, v[:, h * LANES:(h + 1) * LANES], h))
            o_ref[sample, rows, :] = _merge_head_pairs(outs).astype(BF16)


def _full_attention(kernel, name, tq, q, k, v, extra, extra_specs, out_width, q_width, k_width, v_width):
    out = pl.pallas_call(
        kernel,
        out_shape=jax.ShapeDtypeStruct((BATCH, NT_PAD, out_width), BF16),
        grid=(BATCH, SEQ // tq),
        in_specs=[pl.BlockSpec((1, tq, q_width), lambda b, t: (b, t, 0)),
                  pl.BlockSpec((1, NT_PAD, k_width), lambda b, t: (b, 0, 0)),
                  pl.BlockSpec((1, NT_PAD, v_width), lambda b, t: (b, 0, 0))] + extra_specs(2),
        out_specs=pl.BlockSpec((1, tq, out_width), lambda b, t: (b, t, 0)),
        compiler_params=_params(("arbitrary", "arbitrary")),
        name=name + "_latent",
    )(q, k, v, *extra)
    return out


def _full_attention_ctx(kernel, name, q, k, v, extra, extra_specs, out_width, q_width, k_width, v_width,
                        prev_out):
    ctx_blk = SEQ // CTX_LEN
    n_extra = len(extra)

    def body(*refs):
        ins = refs[:3 + n_extra]
        kernel(*ins, refs[-1])

    spec = lambda n: pl.BlockSpec((BATCH, CTX_LEN, n), lambda i: (0, ctx_blk, 0))
    tile = lambda n: pl.BlockSpec((BATCH, TOK_TILE, n), lambda i: (0, N_LAT_TILES, 0))
    return pl.pallas_call(
        body,
        out_shape=jax.ShapeDtypeStruct((BATCH, NT_PAD, out_width), BF16),
        grid=(1,),
        in_specs=[tile(q_width), spec(k_width), spec(v_width)] + extra_specs(1)
                 + [pl.BlockSpec(memory_space=pl.ANY)],
        out_specs=tile(out_width),
        input_output_aliases={3 + n_extra: 0},
        compiler_params=_params(("arbitrary",)),
        name=name + "_ctx",
    )(q, k, v, *extra, prev_out)


def _attn_c_kernel(q_ref, k_ref, v_ref, lq1_ref, lk1_ref, lq2_ref, lk2_ref, sn_ref, o_ref, *, lam_init):
    tq = min(DIFF_SUB, q_ref.shape[1])
    lam = (jnp.exp(jnp.sum(lq1_ref[...] * lk1_ref[...], axis=-1, keepdims=True))
           - jnp.exp(jnp.sum(lq2_ref[...] * lk2_ref[...], axis=-1, keepdims=True)) + lam_init)
    lane = lax.broadcasted_iota(jnp.int32, (1, C_HEADS * C_V), 1)
    units = [(sample, sub) for sample in range(q_ref.shape[0]) for sub in range(q_ref.shape[1] // tq)]
    for sample, sub in units:
        k = _key_rows(k_ref, sample)
        v = _key_rows(v_ref, sample)
        rows = slice(sub * tq, (sub + 1) * tq)
        q = q_ref[sample, rows, :]
        outs = []
        for h in range(C_HEADS):
            lo = h * C_V
            q1 = q * _indicator((lane >= lo) & (lane < lo + C_DH))
            q2 = q * _indicator((lane >= lo + C_DH) & (lane < lo + C_V))
            s = _dot_nt(jnp.concatenate([q1, q2], axis=0), k)
            pv = _softmax_pv(s, v[:, h * LANES:(h + 1) * LANES], h)
            outs.append(pv[0:tq] - lam * pv[tq:2 * tq])
        acc = _merge_head_pairs(outs)
        sq = acc * acc
        inv = jnp.zeros((tq, C_HEADS * C_V), F32)
        for h in range(C_HEADS):
            in_head = (lane >= h * C_V) & (lane < (h + 1) * C_V)
            ms = jnp.sum(jnp.where(in_head, sq, 0.0), axis=-1, keepdims=True) * (1.0 / C_V)
            inv = inv + jnp.where(in_head, lax.rsqrt(ms + EPS), 0.0)
        o_ref[sample, rows, :] = ((acc * inv) * sn_ref[...] * (1.0 - lam_init)).astype(BF16)


def _outproj_kernel(*refs, n_chunks, n_x):
    x_refs = refs[:n_x]
    oa_ref, ob_ref, od_ref, w_ref, modb_ref, modc_ref, g2_ref, wr_ref, xo_ref, h2_ref, aff_ref = refs[n_x:]
    w_hi, w_lo = _split_bf16(wr_ref[...])
    w_router = jnp.concatenate([w_hi, w_lo], axis=-1)
    lane = lax.broadcasted_iota(jnp.int32, (1, LANES), 1)
    mixed = _dot(jnp.concatenate([oa_ref[0], ob_ref[0], od_ref[0]], axis=-1), w_ref[0])
    for j in range(n_chunks):
        rows = slice(j * TOK_TILE, (j + 1) * TOK_TILE)
        mod = _chunk_mod(modb_ref, modc_ref, j, n_chunks)
        g1 = mod[:, 2 * D_MODEL:3 * D_MODEL]
        sh2 = mod[:, 3 * D_MODEL:4 * D_MODEL]
        sc2 = mod[:, 4 * D_MODEL:5 * D_MODEL]
        x = _chunk_rows(x_refs, j) + g1 * mixed[rows]
        xo_ref[0, rows, :] = x
        h2 = (_rms(x) * g2_ref[...]) * (1.0 + sc2) + sh2
        h_hi, h_lo = _split_bf16(h2)
        h2_ref[0, rows, :] = h_hi
        prod = _dot(jnp.concatenate([h_hi, h_lo], axis=0), w_router)
        logits = (prod[0:TOK_TILE, 0:LANES] + prod[0:TOK_TILE, LANES:2 * LANES]
                  + prod[TOK_TILE:2 * TOK_TILE, 0:LANES])
        aff_ref[0, rows, :] = _softmax_rows(jnp.where(lane < N_EXPERTS, logits, NEG_INF))


def _outproj(stream, oa, ob, od, w, mod_l, g2, wr, with_ctx, layer):
    n_chunks = STEP_CHUNKS if with_ctx else LAT_STEP_CHUNKS
    n_steps = (N_TILES if with_ctx else N_LAT_TILES) // n_chunks
    full = lambda shape: pl.BlockSpec(shape, lambda b, t: (0,) * len(shape))
    tok = lambda n: pl.BlockSpec((1, n_chunks * TOK_TILE, n), lambda b, t: (b, t, 0))
    xs = _stream_operands(stream)
    x_specs = _stream_specs(True) if len(xs) > 1 else [tok(D_MODEL)]
    return pl.pallas_call(
        functools.partial(_outproj_kernel, n_chunks=n_chunks, n_x=len(xs)),
        out_shape=[jax.ShapeDtypeStruct((BATCH, NT_PAD, D_MODEL), F32),
                   jax.ShapeDtypeStruct((BATCH, NT_PAD, D_MODEL), BF16),
                   jax.ShapeDtypeStruct((BATCH, NT_PAD, LANES), F32)],
        grid=(BATCH, n_steps),
        in_specs=x_specs + [tok(512), tok(256), tok(256),
                            pl.BlockSpec((1, D_MODEL, D_MODEL), lambda b, t: (layer, 0, 0))] + _mod_specs()
                 + [full((1, D_MODEL)), full((D_MODEL, LANES))],
        out_specs=[tok(D_MODEL), tok(D_MODEL), tok(LANES)],
        compiler_params=_params(("arbitrary", "arbitrary")),
        name="out_projection",
    )(*xs, oa, ob, od, w, mod_l, mod_l, g2, wr)


def _prefix_count(flags_f32, tri_ref):
    rows, n = flags_f32.shape
    run = jnp.zeros((rows, 1), F32)
    outs = []
    for c in range(n // LANES):
        blk = flags_f32[:, c * LANES:(c + 1) * LANES]
        outs.append(_dot(blk.astype(BF16), tri_ref[...]) + run)
        run = run + jnp.sum(blk, axis=-1, keepdims=True)
    return jnp.concatenate(outs, axis=-1)


def _select_slots(a, cap, tri_ref):
    bits = pltpu.bitcast(a, jnp.int32)

    def step(i, t):
        cand = t | jnp.left_shift(jnp.int32(1), 30 - i)
        cnt = jnp.sum((bits >= cand).astype(jnp.int32), axis=-1, keepdims=True)
        return jnp.where(cnt >= cap, cand, t)

    thr = lax.fori_loop(0, 31, step, jnp.zeros((a.shape[0], 1), jnp.int32))
    above = bits > thr
    tied = bits == thr
    need = cap - jnp.sum(above.astype(F32), axis=-1, keepdims=True)
    tied_f = tied.astype(F32)
    sel = above | (tied & (_prefix_count(tied_f, tri_ref) < need))
    sel_f = sel.astype(F32)
    return jnp.where(sel, _prefix_count(sel_f, tri_ref), -1.0)


def _route_kernel(aff_ref, tri_ref, row_ref, arow_ref, col_ref, *, with_ctx):
    lat, ctx = [], []
    for b in range(BATCH):
        lat.append(aff_ref[b, 0:SEQ, :].T[0:N_EXPERTS])
        if with_ctx:
            ctx.append(aff_ref[b, SEQ:NT, :].T[0:N_EXPERTS])
    aff_lat = jnp.concatenate(lat, axis=0)
    arow_ref[:, 0:SEQ] = aff_lat
    slot_lat = _select_slots(aff_lat, CAP_LAT, tri_ref)
    if with_ctx:
        aff_ctx = jnp.concatenate(ctx, axis=0)
        arow_ref[:, SEQ:NT] = aff_ctx
        slot_ctx = _select_slots(aff_ctx, CAP_CTX, tri_ref)
        slots = jnp.concatenate([slot_lat, slot_ctx], axis=-1)
    else:
        slots = slot_lat
    n = slots.shape[1]
    row_ref[:, 0:n] = slots
    pad = jnp.full((LANES - N_EXPERTS, n), -1.0, F32)
    for b in range(BATCH):
        blk = jnp.concatenate([slots[b * N_EXPERTS:(b + 1) * N_EXPERTS], pad], axis=0)
        col_ref[b, 0:n, :] = blk.T
        if with_ctx and NT_PAD > NT:
            col_ref[b, NT:NT_PAD, :] = jnp.full((NT_PAD - NT, LANES), -1.0, F32)


def _route(aff, tri, with_ctx):
    vm = pl.BlockSpec(memory_space=pltpu.VMEM)
    return pl.pallas_call(
        functools.partial(_route_kernel, with_ctx=with_ctx),
        out_shape=[jax.ShapeDtypeStruct((BATCH * N_EXPERTS, NT), F32),
                   jax.ShapeDtypeStruct((BATCH * N_EXPERTS, NT), F32),
                   jax.ShapeDtypeStruct((BATCH, NT_PAD, LANES), F32)],
        in_specs=[vm, vm],
        out_specs=[vm, vm, vm],
        compiler_params=pltpu.CompilerParams(vmem_limit_bytes=VMEM_LIMIT),
        name="route_topc",
    )(aff, tri)


GATHER_GROUP = 4


def _gather_kernel(row_ref, arow_ref, h_ref, *out_refs, with_ctx):
    slot = lax.broadcasted_iota(jnp.int32, (CAP_LAT, 1), 0).astype(F32)

    def pick(e0, cols, n_slots, h_rows, x_ref, g_ref):
        hots = []
        for e in range(e0, e0 + GATHER_GROUP):
            hot = jnp.where(row_ref[e:e + 1, cols] == slot[0:n_slots], 1.0, 0.0)
            gate = jnp.sum(hot * arow_ref[e:e + 1, cols], axis=-1, keepdims=True)
            g_ref[e] = jnp.broadcast_to(gate, (n_slots, LANES))
            hots.append(hot.astype(BF16))
        x = _dot(jnp.concatenate(hots, axis=0), h_rows).astype(BF16)
        for i in range(GATHER_GROUP):
            x_ref[e0 + i] = x[i * n_slots:(i + 1) * n_slots]

    for e0 in range(0, N_EXPERTS, GATHER_GROUP):
        pick(e0, slice(0, SEQ), CAP_LAT, h_ref[0, 0:SEQ, :], out_refs[0], out_refs[1])
        if with_ctx:
            pick(e0, slice(SEQ, NT), CAP_CTX, h_ref[0, SEQ:NT, :], out_refs[2], out_refs[3])


def _gather(slot_rows, aff_rows, h2, with_ctx):
    out_shape = [jax.ShapeDtypeStruct((N_EXPERTS, SEQ, D_MODEL), BF16),
                 jax.ShapeDtypeStruct((N_EXPERTS, SEQ, LANES), F32)]
    out_specs = [pl.BlockSpec((N_EXPERTS, CAP_LAT, D_MODEL), lambda b: (0, b, 0)),
                 pl.BlockSpec((N_EXPERTS, CAP_LAT, LANES), lambda b: (0, b, 0))]
    if with_ctx:
        out_shape += [jax.ShapeDtypeStruct((N_EXPERTS, CTX_LEN, D_MODEL), BF16),
                      jax.ShapeDtypeStruct((N_EXPERTS, CTX_LEN, LANES), F32)]
        out_specs += [pl.BlockSpec((N_EXPERTS, CAP_CTX, D_MODEL), lambda b: (0, b, 0)),
                      pl.BlockSpec((N_EXPERTS, CAP_CTX, LANES), lambda b: (0, b, 0))]
    row_spec = pl.BlockSpec((N_EXPERTS, NT), lambda b: (b, 0))
    return pl.pallas_call(
        functools.partial(_gather_kernel, with_ctx=with_ctx),
        out_shape=out_shape,
        grid=(BATCH,),
        in_specs=[row_spec, row_spec, pl.BlockSpec((1, NT_PAD, D_MODEL), lambda b: (b, 0, 0))],
        out_specs=out_specs,
        compiler_params=_params(("arbitrary",)),
        name="expert_gather",
    )(slot_rows, aff_rows, h2)


def _ffn_kernel(*refs, with_ctx):
    if with_ctx:
        xl_ref, gl_ref, xc_ref, gc_ref, wg_ref, wu_ref, wd_ref, y_ref = refs
    else:
        xl_ref, gl_ref, wg_ref, wu_ref, wd_ref, y_ref = refs
    wg = wg_ref[0, 0].astype(BF16)
    wu = wu_ref[0, 0].astype(BF16)
    wd = wd_ref[0, 0].astype(BF16)

    def swiglu(x, slot_gate):
        gate = _dot(x, wg)
        hid = (gate / (1.0 + jnp.exp(-gate))) * _dot(x, wu)
        scale = jnp.concatenate([slot_gate] * (D_MODEL // LANES), axis=-1)
        return (_dot(hid.astype(BF16), wd) * scale).astype(BF16)

    for r in range(FFN_LAT_TILES):
        rows = slice(r * FFN_TILE, (r + 1) * FFN_TILE)
        y_ref[0, rows, :] = swiglu(xl_ref[0, rows, :], gl_ref[0, rows, :])
    if with_ctx:
        y_ref[0, SEQ:NT, :] = swiglu(xc_ref[0], gc_ref[0])


def _ffn(gathered, w_gate, w_up, w_down, layer):
    with_ctx = len(gathered) == 4
    ins = list(gathered) + [w_gate, w_up, w_down]
    in_specs = [pl.BlockSpec((1, SEQ, D_MODEL), lambda e: (e, 0, 0)),
                pl.BlockSpec((1, SEQ, LANES), lambda e: (e, 0, 0))]
    if with_ctx:
        in_specs += [pl.BlockSpec((1, CTX_LEN, D_MODEL), lambda e: (e, 0, 0)),
                     pl.BlockSpec((1, CTX_LEN, LANES), lambda e: (e, 0, 0))]
    in_specs += [pl.BlockSpec((1, 1, D_MODEL, D_EXPERT), lambda e: (layer, e, 0, 0)),
                 pl.BlockSpec((1, 1, D_MODEL, D_EXPERT), lambda e: (layer, e, 0, 0)),
                 pl.BlockSpec((1, 1, D_EXPERT, D_MODEL), lambda e: (layer, e, 0, 0))]
    return pl.pallas_call(
        functools.partial(_ffn_kernel, with_ctx=with_ctx),
        out_shape=jax.ShapeDtypeStruct((N_EXPERTS, NT, D_MODEL), BF16),
        grid=(N_EXPERTS,),
        in_specs=in_specs,
        out_specs=pl.BlockSpec((1, NT, D_MODEL), lambda e: (e, 0, 0)),
        compiler_params=_params(("arbitrary",)),
        name="expert_ffn",
    )(*ins)


COMBINE_ROWS = 1024


def _scatter_add(col, ys, n_slots):
    lane = lax.broadcasted_iota(jnp.int32, (1, n_slots), 1).astype(F32)
    hot = [_indicator(col[:, e:e + 1] == lane) for e in range(N_EXPERTS)]
    return _dot(jnp.concatenate(hot, axis=-1), jnp.concatenate(ys, axis=0))


def _combine_latent_kernel(*refs, final):
    if final:
        x_ref, col_ref, y_ref, mod_ref, fn_ref, o_ref = refs
    else:
        x_ref, col_ref, y_ref, mod_ref, o_ref = refs
    g2 = mod_ref[0][:, 5 * D_MODEL:6 * D_MODEL]
    acc = _scatter_add(col_ref[0], [y_ref[e] for e in range(N_EXPERTS)], CAP_LAT)
    x = x_ref[0] + g2 * acc
    if final:
        x = _rms(x) * fn_ref[...]
    o_ref[0] = x


def _combine_ctx_kernel(x_ref, col_ref, y_ref, mod_ref, prev_ref, o_ref):
    del prev_ref
    g2 = mod_ref[0][:, 5 * D_MODEL:6 * D_MODEL]
    n_pairs = N_EXPERTS * CAP_CTX
    lane = lax.broadcasted_iota(jnp.int32, (1, n_pairs), 1)
    slot_of_lane = (lane % CAP_CTX).astype(F32)
    spread = _indicator(lax.broadcasted_iota(jnp.int32, (LANES, n_pairs), 0)
                        == lax.broadcasted_iota(jnp.int32, (LANES, n_pairs), 1) // CAP_CTX)
    for b in range(BATCH):
        slot_on_lane = _dot(col_ref[b].astype(BF16), spread)
        hot = _indicator(slot_on_lane == slot_of_lane)
        ys = jnp.concatenate([y_ref[e, b * CAP_CTX:(b + 1) * CAP_CTX, :] for e in range(N_EXPERTS)], axis=0)
        o_ref[b] = x_ref[b] + g2 * _dot(hot, ys)


def _combine(xall, slot_cols, y, mod_l, with_ctx, final_norm=None):
    final = final_norm is not None
    n_rows = SEQ if final else NT_PAD
    tok = lambda n: pl.BlockSpec((1, COMBINE_ROWS, n), lambda b, t: (b, t, 0))
    ins = [xall, slot_cols, y, mod_l]
    in_specs = [tok(D_MODEL), tok(LANES),
                pl.BlockSpec((N_EXPERTS, CAP_LAT, D_MODEL), lambda b, t: (0, b, 0)),
                pl.BlockSpec((1, 1, 6 * D_MODEL), lambda b, t: (b, 0, 0))]
    if final:
        ins.append(final_norm)
        in_specs.append(pl.BlockSpec((1, D_MODEL), lambda b, t: (0, 0)))
    out = pl.pallas_call(
        functools.partial(_combine_latent_kernel, final=final),
        out_shape=jax.ShapeDtypeStruct((BATCH, n_rows, D_MODEL), F32),
        grid=(BATCH, SEQ // COMBINE_ROWS),
        in_specs=in_specs,
        out_specs=tok(D_MODEL),
        compiler_params=_params(("arbitrary", "arbitrary")),
        name="expert_combine",
    )(*ins)
    if not with_ctx:
        return out
    ctx_blk = SEQ // CTX_LEN
    ctx_rows = lambda n: pl.BlockSpec((BATCH, CTX_LEN, n), lambda i: (0, ctx_blk, 0))
    return pl.pallas_call(
        _combine_ctx_kernel,
        out_shape=jax.ShapeDtypeStruct((BATCH, NT_PAD, D_MODEL), F32),
        grid=(1,),
        in_specs=[ctx_rows(D_MODEL), ctx_rows(LANES),
                  pl.BlockSpec((N_EXPERTS, CTX_LEN, D_MODEL), lambda i: (0, ctx_blk, 0)),
                  pl.BlockSpec((1, 1, 6 * D_MODEL), lambda i: (BATCH, 0, 0)),
                  pl.BlockSpec(memory_space=pl.ANY)],
        out_specs=ctx_rows(D_MODEL),
        input_output_aliases={4: 0},
        compiler_params=_params(("arbitrary",)),
        name="expert_combine_ctx",
    )(xall, slot_cols, y, mod_l, out)


def _rope_tables():
    t = jnp.arange(SEQ)
    pos = jnp.stack([t // GRID_W, t % GRID_W], axis=0)

    def table(n_f, lanes_per_unit):
        inv = ROPE_BASE ** (-jnp.arange(n_f, dtype=F32) / n_f)
        d = np.arange(lanes_per_unit)
        axis, second, f = d // (2 * n_f), (d // n_f) % 2, d % n_f
        ang = pos[axis].T.astype(F32) * inv[f][None, :]
        cos = jnp.cos(ang)
        sin = jnp.sin(ang) * jnp.where(second == 1, 1.0, -1.0)[None, :]
        return cos, sin

    def finish(cos, sin, scale):
        cos = jnp.concatenate([cos, jnp.ones((NT_PAD - SEQ, LANES), F32)], axis=0)
        sin = jnp.concatenate([sin, jnp.zeros((NT_PAD - SEQ, LANES), F32)], axis=0)
        scale = scale * LOG2_E
        return [cos, sin, cos * scale, sin * scale]

    cos_a, sin_a = table(HEAD_DIM // 4, HEAD_DIM)
    tabs_a = finish(jnp.tile(cos_a, (1, 2)), jnp.tile(sin_a, (1, 2)), HEAD_DIM ** -0.5)
    cos_8, sin_8 = table(B_ROPE // 4, B_ROPE)
    tabs_8 = finish(jnp.tile(cos_8, (1, 4)), jnp.tile(sin_8, (1, 4)), C_DH ** -0.5)
    ones = jnp.ones((SEQ, B_NOPE), F32)
    pad1 = jnp.ones((SEQ, LANES - B_NOPE - B_ROPE), F32)
    cos_b = jnp.concatenate([ones, cos_8, pad1], axis=1)
    sin_b = jnp.concatenate([0 * ones, sin_8, 0 * pad1], axis=1)
    tabs_b = finish(cos_b, sin_b, (B_NOPE + B_ROPE) ** -0.5)
    return tabs_a + tabs_8 + tabs_b


def _window_bias():
    qi = np.arange(WIN)[:, None]
    kj = np.arange(A_LOCAL)[None, :]
    out = np.zeros((3, WIN, A_LOCAL + CTX_LEN), np.float32)
    for v in range(3):
        out[v, :, :A_LOCAL] = np.where(np.abs(kj - v * WIN - qi) <= WIN, 0.0, NEG_INF)
    return jnp.asarray(out)


def _prep_weights(w_in, w_uq, w_ukv, w_out, w_router):
    offs = np.cumsum((0, 512, 128, 128, 256, 128, 32, 256, 256, 256))
    seg = lambda i: w_in[:, :, offs[i]:offs[i + 1]]
    wq = seg(0).reshape(DEPTH, D_MODEL, A_KV_HEADS, A_GROUP, HEAD_DIM)
    wq = jnp.swapaxes(wq, 2, 3).reshape(DEPTH, D_MODEL, A_HEADS * HEAD_DIM)
    kr = jnp.pad(seg(5), ((0, 0), (0, 0), (B_NOPE, LANES - B_NOPE - B_ROPE)))
    w = jnp.concatenate([wq, seg(1), seg(2), seg(3), seg(4), kr,
                         seg(6), seg(7), seg(8)], axis=-1).astype(BF16)
    wuq = w_uq.reshape(DEPTH, B_Q_RANK, B_HEADS, B_NOPE + B_ROPE)
    wuq = jnp.pad(wuq, ((0, 0), (0, 0), (0, 0), (0, LANES - B_NOPE - B_ROPE)))
    wuq = wuq.reshape(DEPTH, B_Q_RANK, B_HEADS * LANES).astype(BF16)
    wukv = w_ukv.reshape(DEPTH, B_KV_RANK, B_HEADS, B_NOPE + B_V)
    wk = jnp.pad(wukv[..., :B_NOPE], ((0, 0), (0, 0), (0, 0), (0, LANES - B_NOPE)))
    wukv = jnp.concatenate([wk.reshape(DEPTH, B_KV_RANK, B_HEADS * LANES),
                            wukv[..., B_NOPE:].reshape(DEPTH, B_KV_RANK, B_HEADS * B_V)],
                           axis=-1).astype(BF16)
    wo_a = w_out[:, :A_HEADS * HEAD_DIM].reshape(DEPTH, A_KV_HEADS, A_GROUP, HEAD_DIM, D_MODEL)
    wo_a = jnp.swapaxes(wo_a, 1, 2).reshape(DEPTH, A_HEADS * HEAD_DIM, D_MODEL)
    wo = jnp.concatenate([wo_a, w_out[:, A_HEADS * HEAD_DIM:]], axis=1).astype(BF16)
    wr = jnp.pad(w_router, ((0, 0), (0, 0), (0, LANES - N_EXPERTS)))
    return w, wuq, wukv, wo, wr


def kernel(x, c, ctx, c_ctx, norm1, norm2, w_ada, b_ada, w_in, sink, mla_q_norm, w_uq, mla_kv_norm, w_ukv,
           lam_q1, lam_k1, lam_q2, lam_k2, diff_norm, w_out, w_router, w_gate, w_up, w_down, final_norm):
    assert NT_PAD == NT
    xall = (x, ctx)
    cc = jnp.concatenate([c, c_ctx[None, :], jnp.zeros((16 - BATCH - 1, D_MODEL), F32)], axis=0)
    mod = _modulation(cc, w_ada, b_ada).reshape(DEPTH, 16, 1, 6 * D_MODEL)
    tabs = _rope_tables()
    bias = _window_bias()
    tri = jnp.asarray(np.triu(np.ones((LANES, LANES), np.float32), k=1), BF16)
    w, wuq, wukv, wo, wr = _prep_weights(w_in, w_uq, w_ukv, w_out, w_router)
    sub_norm = jnp.tile(diff_norm, (1, C_HEADS)).reshape(DEPTH, 1, C_HEADS * C_V)
    row = lambda a, l: a[l].reshape(1, -1)

    for l in range(DEPTH):
        need_ctx = l < DEPTH - 1
        qa, ka, va, qb, kb, vb, qd, kd, vd = _inproj(
            xall, mod[l], row(norm1, l), w, row(mla_q_norm, l), wuq[l], row(mla_kv_norm, l), wukv[l], tabs, l)

        oa = _attn_a_latent(sink[l], qa, ka, va, bias)
        ob = _full_attention(_attn_b_kernel, "attn_mla", MLA_TQ, qb, kb, vb, [], lambda n: [],
                             256, 512, 512, 512)
        lam_init = 0.8 - 0.6 * math.exp(-0.3 * l)
        c_kernel = functools.partial(_attn_c_kernel, lam_init=lam_init)
        c_extra = [row(lam_q1, l), row(lam_k1, l), row(lam_q2, l), row(lam_k2, l), sub_norm[l]]

        def c_specs(n_grid, c_extra=c_extra):
            zmap = (lambda b, t: (0, 0)) if n_grid == 2 else (lambda b: (0, 0))
            return [pl.BlockSpec(a.shape, zmap) for a in c_extra]

        od = _full_attention(c_kernel, "attn_diff", DIFF_TQ, qd, kd, vd, c_extra, c_specs, 256, 256, 256, 512)
        if need_ctx:
            oa = _attn_a_ctx(sink[l], qa, ka, va, oa)
            ob = _full_attention_ctx(_attn_b_kernel, "attn_mla", qb, kb, vb, [], lambda n: [],
                                     256, 512, 512, 512, ob)
            od = _full_attention_ctx(c_kernel, "attn_diff", qd, kd, vd, c_extra, c_specs,
                                     256, 256, 256, 512, od)

        xall, h2, aff = _outproj(xall, oa, ob, od, wo, mod[l], row(norm2, l), wr[l], need_ctx, l)
        slot_rows, aff_rows, slot_cols = _route(aff, tri, need_ctx)
        gathered = _gather(slot_rows, aff_rows, h2, need_ctx)
        y = _ffn(gathered, w_gate, w_up, w_down, l)
        xall = _combine(xall, slot_cols, y, mod[l], need_ctx,
                        final_norm=None if need_ctx else final_norm.reshape(1, D_MODEL))
    return xall
```
